```python
import math
import jax, jax.numpy as jnp
from jax import lax
import numpy as np

D_MODEL = 1024
BATCH = 16
SEQ = 2048
DEPTH = 2

CHUNK = 64
N_MIXERS = 2
D_FF = 2816
PLE_DIM = 256
S5_GROUP = 16
S5_GROUPS = D_MODEL // S5_GROUP
S5_STATE = 64
SB_HEAD_DIM = 64
SB_HEADS = D_MODEL // SB_HEAD_DIM
Q_BLOCK = 128
N_A = (DEPTH + 1) // 2
N_B = DEPTH // 2
EPS = 1e-6
DT_MIN = 1e-3
DT_MAX = 1e-1

kernel_name = 'hybrid_s5_stickbreaking_macaron'


def rmsnorm(x, g):
    xf = x.astype(jnp.float32)
    y = xf * lax.rsqrt(jnp.mean(xf * xf, axis=-1, keepdims=True) + EPS)
    return (y * g.astype(jnp.float32)).astype(x.dtype)


def swiglu(x, w1, w3, w2):
    return (jax.nn.silu(x @ w1) * (x @ w3)) @ w2


def _cmul(ar, ai, br, bi):
    return ar * br - ai * bi, ar * bi + ai * br


def _s5_combine(e1, e2):
    a1r, a1i, b1r, b1i = e1
    a2r, a2i, b2r, b2i = e2
    ar, ai = _cmul(a2r, a2i, a1r, a1i)
    cr, ci = _cmul(a2r, a2i, b1r, b1i)
    return ar, ai, cr + b2r, ci + b2i


def s5_mixer(h, w_in, a_re, a_im, log_dt, b_re, b_im, c_re, c_im, d_skip, w_glu):
    bsz, seq_len, _ = h.shape
    f32 = jnp.float32
    u = h @ w_in
    ug = u.astype(f32).reshape(bsz, seq_len, S5_GROUPS, S5_GROUP)
    lam_re = jnp.minimum(a_re.astype(f32), -1e-4)
    lam_im = a_im.astype(f32)
    dt = jnp.exp(log_dt.astype(f32))[:, None]
    mag = jnp.exp(lam_re * dt)
    abar_re = mag * jnp.cos(lam_im * dt)
    abar_im = mag * jnp.sin(lam_im * dt)
    den = lam_re * lam_re + lam_im * lam_im
    nr = abar_re - 1.0
    ni = abar_im
    fr = (nr * lam_re + ni * lam_im) / den
    fi = (ni * lam_re - nr * lam_im) / den
    bre = b_re.astype(f32)
    bim = b_im.astype(f32)
    bbar_re = fr[..., None] * bre - fi[..., None] * bim
    bbar_im = fr[..., None] * bim + fi[..., None] * bre
    bu_re = jnp.einsum('blgh,gph->blgp', ug, bbar_re)
    bu_im = jnp.einsum('blgh,gph->blgp', ug, bbar_im)
    a_seq_re = jnp.broadcast_to(abar_re, (1, seq_len, S5_GROUPS, S5_STATE))
    a_seq_im = jnp.broadcast_to(abar_im, (1, seq_len, S5_GROUPS, S5_STATE))
    _, _, s_re, s_im = lax.associative_scan(
        _s5_combine, (a_seq_re, a_seq_im, bu_re, bu_im), axis=1)
    y = (jnp.einsum('blgp,ghp->blgh', s_re, c_re.astype(f32))
         - jnp.einsum('blgp,ghp->blgh', s_im, c_im.astype(f32)))
    y = y.reshape(bsz, seq_len, D_MODEL) + d_skip.astype(f32) * u.astype(f32)
    z = jax.nn.gelu(y).astype(h.dtype)
    z_out, z_gate = jnp.split(z @ w_glu, 2, axis=-1)
    return z_out * jax.nn.sigmoid(z_gate)


def stick_breaking_mixer(h, w_qkv, w_o):
    bsz, seq_len, _ = h.shape
    qkv = (h @ w_qkv).reshape(bsz, seq_len, 3, SB_HEADS, SB_HEAD_DIM)
    q = qkv[:, :, 0].transpose(0, 2, 1, 3)
    k = qkv[:, :, 1].transpose(0, 2, 1, 3)
    v = qkv[:, :, 2].transpose(0, 2, 1, 3)
    scale = SB_HEAD_DIM ** -0.5
    outs = []
    for blk in range(seq_len // Q_BLOCK):
        start = blk * Q_BLOCK
        end = start + Q_BLOCK
        qb = q[:, :, start:end]
        kb = k[:, :, :end]
        vb = v[:, :, :end]
        z = jnp.einsum('bhqd,bhkd->bhqk', qb, kb).astype(jnp.float32) * scale
        t_pos = start + jnp.arange(Q_BLOCK)[:, None]
        s_pos = jnp.arange(end)[None, :]
        strict = s_pos < t_pos
        log_keep = jnp.where(strict, jax.nn.log_sigmoid(-z), 0.0)
        later = lax.cumsum(log_keep, axis=3, reverse=True) - log_keep
        att = jnp.where(strict, jnp.exp(jax.nn.log_sigmoid(z) + later), 0.0)
        outs.append(jnp.einsum('bhqk,bhkd->bhqd', att.astype(vb.dtype), vb))
    o = jnp.concatenate(outs, axis=2).transpose(0, 2, 1, 3).reshape(bsz, seq_len, D_MODEL)
    return o @ w_o


def _fwd_setup_inputs(seed: int = 0) -> dict:
    key = jax.random.key(seed)
    ks = jax.random.split(key, 32)
    f32 = jnp.float32
    nrm = lambda k, shape, s: jax.random.normal(k, shape, f32) * s
    gain = lambda k, shape: 1.0 + 0.02 * jax.random.normal(k, shape, f32)
    d_in = D_MODEL ** -0.5
    a_im0 = math.pi * jnp.arange(S5_STATE, dtype=f32)
    return {
        'x': jax.random.normal(ks[0], (BATCH, SEQ, D_MODEL), f32),
        'p': jax.random.normal(ks[1], (DEPTH, BATCH, SEQ, PLE_DIM), f32),
        'ffn1_norm': gain(ks[2], (DEPTH, D_MODEL)),
        'ffn1_w1': nrm(ks[3], (DEPTH, D_MODEL, D_FF), d_in),
        'ffn1_w3': nrm(ks[4], (DEPTH, D_MODEL, D_FF), d_in),
        'ffn1_w2': nrm(ks[5], (DEPTH, D_FF, D_MODEL), D_FF ** -0.5),
        'mix_norm': gain(ks[6], (DEPTH, D_MODEL)),
        'ffn2_norm': gain(ks[7], (DEPTH, D_MODEL)),
        'ffn2_w1': nrm(ks[8], (DEPTH, D_MODEL, D_FF), d_in),
        'ffn2_w3': nrm(ks[9], (DEPTH, D_MODEL, D_FF), d_in),
        'ffn2_w2': nrm(ks[10], (DEPTH, D_FF, D_MODEL), D_FF ** -0.5),
        'ple_norm': gain(ks[11], (DEPTH, D_MODEL)),
        'ple_proj': nrm(ks[12], (DEPTH, PLE_DIM, D_MODEL), PLE_DIM ** -0.5),
        'ple_gate': nrm(ks[13], (DEPTH, D_MODEL, D_MODEL), d_in),
        's5_w_in': nrm(ks[14], (N_A, D_MODEL, D_MODEL), d_in),
        's5_a_re': -0.5 + 0.01 * jax.random.normal(ks[15], (N_A, S5_GROUPS, S5_STATE), f32),
        's5_a_im': a_im0 + 0.01 * jax.random.normal(ks[16], (N_A, S5_GROUPS, S5_STATE), f32),
        's5_log_dt': jax.random.uniform(ks[17], (N_A, S5_GROUPS), f32,
                                        math.log(DT_MIN), math.log(DT_MAX)),
        's5_b_re': nrm(ks[18], (N_A, S5_GROUPS, S5_STATE, S5_GROUP), (2 * S5_GROUP) ** -0.5),
        's5_b_im': nrm(ks[19], (N_A, S5_GROUPS, S5_STATE, S5_GROUP), (2 * S5_GROUP) ** -0.5),
        's5_c_re': nrm(ks[20], (N_A, S5_GROUPS, S5_GROUP, S5_STATE), S5_STATE ** -0.5),
        's5_c_im': nrm(ks[21], (N_A, S5_GROUPS, S5_GROUP, S5_STATE), S5_STATE ** -0.5),
        's5_d': jax.random.normal(ks[22], (N_A, D_MODEL), f32),
        's5_w_glu': nrm(ks[23], (N_A, D_MODEL, 2 * D_MODEL), d_in),
        'sb_w_qkv': nrm(ks[24], (N_B, D_MODEL, 3 * D_MODEL), d_in),
        'sb_w_o': nrm(ks[25], (N_B, D_MODEL, D_MODEL), d_in),
        'final_norm': gain(ks[26], (D_MODEL,)),
    }


def _fwd_reference(x, p, ffn1_norm, ffn1_w1, ffn1_w3, ffn1_w2, mix_norm, ffn2_norm,
              ffn2_w1, ffn2_w3, ffn2_w2, ple_norm, ple_proj, ple_gate,
              s5_w_in, s5_a_re, s5_a_im, s5_log_dt, s5_b_re, s5_b_im,
              s5_c_re, s5_c_im, s5_d, s5_w_glu, sb_w_qkv, sb_w_o, final_norm):
    h = x
    for i in range(DEPTH):
        h = h + 0.5 * swiglu(rmsnorm(h, ffn1_norm[i]), ffn1_w1[i], ffn1_w3[i], ffn1_w2[i])
        hn = rmsnorm(h, mix_norm[i])
        j = i // N_MIXERS
        if i % N_MIXERS == 0:
            m = s5_mixer(hn, s5_w_in[j], s5_a_re[j], s5_a_im[j], s5_log_dt[j],
                         s5_b_re[j], s5_b_im[j], s5_c_re[j], s5_c_im[j],
                         s5_d[j], s5_w_glu[j])
        else:
            m = stick_breaking_mixer(hn, sb_w_qkv[j], sb_w_o[j])
        h = h + m
        h = h + 0.5 * swiglu(rmsnorm(h, ffn2_norm[i]), ffn2_w1[i], ffn2_w3[i], ffn2_w2[i])
        gate = jax.nn.sigmoid(rmsnorm(h, ple_norm[i]) @ ple_gate[i])
        h = h + (p[i].astype(h.dtype) @ ple_proj[i]) * gate
    return rmsnorm(h, final_norm)


import jax as _jax
import jax.numpy as _jnp

TWIN_FORMAT = 'train_step'
FWD_PARAMS = ['x', 'p', 'ffn1_norm', 'ffn1_w1', 'ffn1_w3', 'ffn1_w2', 'mix_norm', 'ffn2_norm', 'ffn2_w1', 'ffn2_w3', 'ffn2_w2', 'ple_norm', 'ple_proj', 'ple_gate', 's5_w_in', 's5_a_re', 's5_a_im', 's5_log_dt', 's5_b_re', 's5_b_im', 's5_c_re', 's5_c_im', 's5_d', 's5_w_glu', 'sb_w_qkv', 'sb_w_o', 'final_norm']
TWIN_WEIGHTS = ['ffn1_norm', 'ffn1_w1', 'ffn1_w3', 'ffn1_w2', 'mix_norm', 'ffn2_norm', 'ffn2_w1', 'ffn2_w3', 'ffn2_w2', 'ple_norm', 'ple_proj', 'ple_gate', 's5_w_in', 's5_a_re', 's5_a_im', 's5_log_dt', 's5_b_re', 's5_b_im', 's5_c_re', 's5_c_im', 's5_d', 's5_w_glu', 'sb_w_qkv', 'sb_w_o', 'final_norm']
TWIN_DIFF_INPUT = 'x'
TWIN_INPUTS = ['x', 'p', 'ffn1_norm', 'ffn1_w1', 'ffn1_w3', 'ffn1_w2', 'mix_norm', 'ffn2_norm', 'ffn2_w1', 'ffn2_w3', 'ffn2_w2', 'ple_norm', 'ple_proj', 'ple_gate', 's5_w_in', 's5_a_re', 's5_a_im', 's5_log_dt', 's5_b_re', 's5_b_im', 's5_c_re', 's5_c_im', 's5_d', 's5_w_glu', 'sb_w_qkv', 'sb_w_o', 'final_norm', 'loss_target', 'm_ffn1_norm', 'm_ffn1_w1', 'm_ffn1_w3', 'm_ffn1_w2', 'm_mix_norm', 'm_ffn2_norm', 'm_ffn2_w1', 'm_ffn2_w3', 'm_ffn2_w2', 'm_ple_norm', 'm_ple_proj', 'm_ple_gate', 'm_s5_w_in', 'm_s5_a_re', 'm_s5_a_im', 'm_s5_log_dt', 'm_s5_b_re', 'm_s5_b_im', 'm_s5_c_re', 'm_s5_c_im', 'm_s5_d', 'm_s5_w_glu', 'm_sb_w_qkv', 'm_sb_w_o', 'm_final_norm', 'v_ffn1_norm', 'v_ffn1_w1', 'v_ffn1_w3', 'v_ffn1_w2', 'v_mix_norm', 'v_ffn2_norm', 'v_ffn2_w1', 'v_ffn2_w3', 'v_ffn2_w2', 'v_ple_norm', 'v_ple_proj', 'v_ple_gate', 'v_s5_w_in', 'v_s5_a_re', 'v_s5_a_im', 'v_s5_log_dt', 'v_s5_b_re', 'v_s5_b_im', 'v_s5_c_re', 'v_s5_c_im', 'v_s5_d', 'v_s5_w_glu', 'v_sb_w_qkv', 'v_sb_w_o', 'v_final_norm']
TWIN_OUTPUTS = ['loss', 'grad_x', 'grad_ffn1_norm', 'grad_ffn1_w1', 'grad_ffn1_w3', 'grad_ffn1_w2', 'grad_mix_norm', 'grad_ffn2_norm', 'grad_ffn2_w1', 'grad_ffn2_w3', 'grad_ffn2_w2', 'grad_ple_norm', 'grad_ple_proj', 'grad_ple_gate', 'grad_s5_w_in', 'grad_s5_a_re', 'grad_s5_a_im', 'grad_s5_log_dt', 'grad_s5_b_re', 'grad_s5_b_im', 'grad_s5_c_re', 'grad_s5_c_im', 'grad_s5_d', 'grad_s5_w_glu', 'grad_sb_w_qkv', 'grad_sb_w_o', 'grad_final_norm', 'delta_ffn1_norm', 'delta_ffn1_w1', 'delta_ffn1_w3', 'delta_ffn1_w2', 'delta_mix_norm', 'delta_ffn2_norm', 'delta_ffn2_w1', 'delta_ffn2_w3', 'delta_ffn2_w2', 'delta_ple_norm', 'delta_ple_proj', 'delta_ple_gate', 'delta_s5_w_in', 'delta_s5_a_re', 'delta_s5_a_im', 'delta_s5_log_dt', 'delta_s5_b_re', 'delta_s5_b_im', 'delta_s5_c_re', 'delta_s5_c_im', 'delta_s5_d', 'delta_s5_w_glu', 'delta_sb_w_qkv', 'delta_sb_w_o', 'delta_final_norm', 'new_m_ffn1_norm', 'new_m_ffn1_w1', 'new_m_ffn1_w3', 'new_m_ffn1_w2', 'new_m_mix_norm', 'new_m_ffn2_norm', 'new_m_ffn2_w1', 'new_m_ffn2_w3', 'new_m_ffn2_w2', 'new_m_ple_norm', 'new_m_ple_proj', 'new_m_ple_gate', 'new_m_s5_w_in', 'new_m_s5_a_re', 'new_m_s5_a_im', 'new_m_s5_log_dt', 'new_m_s5_b_re', 'new_m_s5_b_im', 'new_m_s5_c_re', 'new_m_s5_c_im', 'new_m_s5_d', 'new_m_s5_w_glu', 'new_m_sb_w_qkv', 'new_m_sb_w_o', 'new_m_final_norm', 'new_v_ffn1_norm', 'new_v_ffn1_w1', 'new_v_ffn1_w3', 'new_v_ffn1_w2', 'new_v_mix_norm', 'new_v_ffn2_norm', 'new_v_ffn2_w1', 'new_v_ffn2_w3', 'new_v_ffn2_w2', 'new_v_ple_norm', 'new_v_ple_proj', 'new_v_ple_gate', 'new_v_s5_w_in', 'new_v_s5_a_re', 'new_v_s5_a_im', 'new_v_s5_log_dt', 'new_v_s5_b_re', 'new_v_s5_b_im', 'new_v_s5_c_re', 'new_v_s5_c_im', 'new_v_s5_d', 'new_v_s5_w_glu', 'new_v_sb_w_qkv', 'new_v_sb_w_o', 'new_v_final_norm']
TWIN_LEAF_KINDS = {'loss': 'loss', 'grad_x': 'grad_x', 'grad_ffn1_norm': 'grad_w', 'grad_ffn1_w1': 'grad_w', 'grad_ffn1_w3': 'grad_w', 'grad_ffn1_w2': 'grad_w', 'grad_mix_norm': 'grad_w', 'grad_ffn2_norm': 'grad_w', 'grad_ffn2_w1': 'grad_w', 'grad_ffn2_w3': 'grad_w', 'grad_ffn2_w2': 'grad_w', 'grad_ple_norm': 'grad_w', 'grad_ple_proj': 'grad_w', 'grad_ple_gate': 'grad_w', 'grad_s5_w_in': 'grad_w', 'grad_s5_a_re': 'grad_w', 'grad_s5_a_im': 'grad_w', 'grad_s5_log_dt': 'grad_w', 'grad_s5_b_re': 'grad_w', 'grad_s5_b_im': 'grad_w', 'grad_s5_c_re': 'grad_w', 'grad_s5_c_im': 'grad_w', 'grad_s5_d': 'grad_w', 'grad_s5_w_glu': 'grad_w', 'grad_sb_w_qkv': 'grad_w', 'grad_sb_w_o': 'grad_w', 'grad_final_norm': 'grad_w', 'delta_ffn1_norm': 'delta_w', 'delta_ffn1_w1': 'delta_w', 'delta_ffn1_w3': 'delta_w', 'delta_ffn1_w2': 'delta_w', 'delta_mix_norm': 'delta_w', 'delta_ffn2_norm': 'delta_w', 'delta_ffn2_w1': 'delta_w', 'delta_ffn2_w3': 'delta_w', 'delta_ffn2_w2': 'delta_w', 'delta_ple_norm': 'delta_w', 'delta_ple_proj': 'delta_w', 'delta_ple_gate': 'delta_w', 'delta_s5_w_in': 'delta_w', 'delta_s5_a_re': 'delta_w', 'delta_s5_a_im': 'delta_w', 'delta_s5_log_dt': 'delta_w', 'delta_s5_b_re': 'delta_w', 'delta_s5_b_im': 'delta_w', 'delta_s5_c_re': 'delta_w', 'delta_s5_c_im': 'delta_w', 'delta_s5_d': 'delta_w', 'delta_s5_w_glu': 'delta_w', 'delta_sb_w_qkv': 'delta_w', 'delta_sb_w_o': 'delta_w', 'delta_final_norm': 'delta_w', 'new_m_ffn1_norm': 'new_m', 'new_m_ffn1_w1': 'new_m', 'new_m_ffn1_w3': 'new_m', 'new_m_ffn1_w2': 'new_m', 'new_m_mix_norm': 'new_m', 'new_m_ffn2_norm': 'new_m', 'new_m_ffn2_w1': 'new_m', 'new_m_ffn2_w3': 'new_m', 'new_m_ffn2_w2': 'new_m', 'new_m_ple_norm': 'new_m', 'new_m_ple_proj': 'new_m', 'new_m_ple_gate': 'new_m', 'new_m_s5_w_in': 'new_m', 'new_m_s5_a_re': 'new_m', 'new_m_s5_a_im': 'new_m', 'new_m_s5_log_dt': 'new_m', 'new_m_s5_b_re': 'new_m', 'new_m_s5_b_im': 'new_m', 'new_m_s5_c_re': 'new_m', 'new_m_s5_c_im': 'new_m', 'new_m_s5_d': 'new_m', 'new_m_s5_w_glu': 'new_m', 'new_m_sb_w_qkv': 'new_m', 'new_m_sb_w_o': 'new_m', 'new_m_final_norm': 'new_m', 'new_v_ffn1_norm': 'new_v', 'new_v_ffn1_w1': 'new_v', 'new_v_ffn1_w3': 'new_v', 'new_v_ffn1_w2': 'new_v', 'new_v_mix_norm': 'new_v', 'new_v_ffn2_norm': 'new_v', 'new_v_ffn2_w1': 'new_v', 'new_v_ffn2_w3': 'new_v', 'new_v_ffn2_w2': 'new_v', 'new_v_ple_norm': 'new_v', 'new_v_ple_proj': 'new_v', 'new_v_ple_gate': 'new_v', 'new_v_s5_w_in': 'new_v', 'new_v_s5_a_re': 'new_v', 'new_v_s5_a_im': 'new_v', 'new_v_s5_log_dt': 'new_v', 'new_v_s5_b_re': 'new_v', 'new_v_s5_b_im': 'new_v', 'new_v_s5_c_re': 'new_v', 'new_v_s5_c_im': 'new_v', 'new_v_s5_d': 'new_v', 'new_v_s5_w_glu': 'new_v', 'new_v_sb_w_qkv': 'new_v', 'new_v_sb_w_o': 'new_v', 'new_v_final_norm': 'new_v'}


def _forward(args):
    return _fwd_reference(*[args[k] for k in FWD_PARAMS])


def _output_shape():
    out = _jax.eval_shape(lambda: _forward(_fwd_setup_inputs(0)))
    return out.shape, out.dtype

N_MICROBATCH = 1
ADAM_LR = 0.001
ADAM_B1 = 0.9
ADAM_B2 = 0.999
ADAM_EPS = 1e-08
ADAM_WD = 0.01
ADAM_STEP = 10
PER_EXAMPLE_BATCH_AXIS = {'x': 0, 'p': 1, 'loss_target': 0}
SHARED_INPUTS = []
_WEIGHT_DTYPES = {'ffn1_norm': _jnp.float32, 'ffn1_w1': _jnp.float32, 'ffn1_w3': _jnp.float32, 'ffn1_w2': _jnp.float32, 'mix_norm': _jnp.float32, 'ffn2_norm': _jnp.float32, 'ffn2_w1': _jnp.float32, 'ffn2_w3': _jnp.float32, 'ffn2_w2': _jnp.float32, 'ple_norm': _jnp.float32, 'ple_proj': _jnp.float32, 'ple_gate': _jnp.float32, 's5_w_in': _jnp.float32, 's5_a_re': _jnp.float32, 's5_a_im': _jnp.float32, 's5_log_dt': _jnp.float32, 's5_b_re': _jnp.float32, 's5_b_im': _jnp.float32, 's5_c_re': _jnp.float32, 's5_c_im': _jnp.float32, 's5_d': _jnp.float32, 's5_w_glu': _jnp.float32, 'sb_w_qkv': _jnp.float32, 'sb_w_o': _jnp.float32, 'final_norm': _jnp.float32}
MOMENT_SCALE = {'ffn1_norm': 6.638060e-02, 'ffn1_w1': 2.856737e-02, 'ffn1_w3': 2.764563e-02, 'ffn1_w2': 4.577595e-02, 'mix_norm': 7.747178e-02, 'ffn2_norm': 5.739851e-02, 'ffn2_w1': 2.443562e-02, 'ffn2_w3': 2.364571e-02, 'ffn2_w2': 3.922102e-02, 'ple_norm': 2.776171e-02, 'ple_proj': 7.044114e-02, 'ple_gate': 2.763294e-02, 's5_w_in': 5.974369e-02, 's5_a_re': 4.081517e-03, 's5_a_im': 4.564438e-03, 's5_log_dt': 2.049246e+00, 's5_b_re': 2.776796e-03, 's5_b_im': 2.801044e-03, 's5_c_re': 3.975006e-03, 's5_c_im': 3.974279e-03, 's5_d': 6.383632e-02, 's5_w_glu': 4.238852e-02, 'sb_w_qkv': 5.349031e-02, 'sb_w_o': 7.793581e-02, 'final_norm': 3.195667e+01}


def _to_microbatches(a, axis):
    t = _jnp.moveaxis(a, axis, 0)
    t = t.reshape((N_MICROBATCH, t.shape[0] // N_MICROBATCH) + t.shape[1:])
    return _jnp.moveaxis(t, 1, axis + 1)


def setup_inputs(seed: int = 0) -> dict:
    inp = _fwd_setup_inputs(seed)
    key = _jax.random.fold_in(_jax.random.key(seed), 7919)
    shape, _ = _output_shape()
    out = dict(inp)
    out["loss_target"] = _jax.random.normal(_jax.random.fold_in(key, 0), shape, _jnp.float32)
    for i, name in enumerate(TWIN_WEIGHTS):
        w = inp[name].astype(_jnp.float32)
        if MOMENT_SCALE is None:
            s = _jnp.sqrt(_jnp.mean(_jnp.square(w)) + 1e-30)
        else:
            s = MOMENT_SCALE[name]
        km, kv = _jax.random.split(_jax.random.fold_in(key, i + 1))
        out[name] = w
        out["m_" + name] = s * _jax.random.normal(km, w.shape, _jnp.float32)
        out["v_" + name] = (s * s) * _jax.random.uniform(kv, w.shape, _jnp.float32, 0.5, 1.5)
    if N_MICROBATCH > 1:
        for name, axis in PER_EXAMPLE_BATCH_AXIS.items():
            out[name] = _to_microbatches(out[name], axis)
    return {'x': out['x'], 'p': out['p'], 'ffn1_norm': out['ffn1_norm'], 'ffn1_w1': out['ffn1_w1'], 'ffn1_w3': out['ffn1_w3'], 'ffn1_w2': out['ffn1_w2'], 'mix_norm': out['mix_norm'], 'ffn2_norm': out['ffn2_norm'], 'ffn2_w1': out['ffn2_w1'], 'ffn2_w3': out['ffn2_w3'], 'ffn2_w2': out['ffn2_w2'], 'ple_norm': out['ple_norm'], 'ple_proj': out['ple_proj'], 'ple_gate': out['ple_gate'], 's5_w_in': out['s5_w_in'], 's5_a_re': out['s5_a_re'], 's5_a_im': out['s5_a_im'], 's5_log_dt': out['s5_log_dt'], 's5_b_re': out['s5_b_re'], 's5_b_im': out['s5_b_im'], 's5_c_re': out['s5_c_re'], 's5_c_im': out['s5_c_im'], 's5_d': out['s5_d'], 's5_w_glu': out['s5_w_glu'], 'sb_w_qkv': out['sb_w_qkv'], 'sb_w_o': out['sb_w_o'], 'final_norm': out['final_norm'], 'loss_target': out['loss_target'], 'm_ffn1_norm': out['m_ffn1_norm'], 'm_ffn1_w1': out['m_ffn1_w1'], 'm_ffn1_w3': out['m_ffn1_w3'], 'm_ffn1_w2': out['m_ffn1_w2'], 'm_mix_norm': out['m_mix_norm'], 'm_ffn2_norm': out['m_ffn2_norm'], 'm_ffn2_w1': out['m_ffn2_w1'], 'm_ffn2_w3': out['m_ffn2_w3'], 'm_ffn2_w2': out['m_ffn2_w2'], 'm_ple_norm': out['m_ple_norm'], 'm_ple_proj': out['m_ple_proj'], 'm_ple_gate': out['m_ple_gate'], 'm_s5_w_in': out['m_s5_w_in'], 'm_s5_a_re': out['m_s5_a_re'], 'm_s5_a_im': out['m_s5_a_im'], 'm_s5_log_dt': out['m_s5_log_dt'], 'm_s5_b_re': out['m_s5_b_re'], 'm_s5_b_im': out['m_s5_b_im'], 'm_s5_c_re': out['m_s5_c_re'], 'm_s5_c_im': out['m_s5_c_im'], 'm_s5_d': out['m_s5_d'], 'm_s5_w_glu': out['m_s5_w_glu'], 'm_sb_w_qkv': out['m_sb_w_qkv'], 'm_sb_w_o': out['m_sb_w_o'], 'm_final_norm': out['m_final_norm'], 'v_ffn1_norm': out['v_ffn1_norm'], 'v_ffn1_w1': out['v_ffn1_w1'], 'v_ffn1_w3': out['v_ffn1_w3'], 'v_ffn1_w2': out['v_ffn1_w2'], 'v_mix_norm': out['v_mix_norm'], 'v_ffn2_norm': out['v_ffn2_norm'], 'v_ffn2_w1': out['v_ffn2_w1'], 'v_ffn2_w3': out['v_ffn2_w3'], 'v_ffn2_w2': out['v_ffn2_w2'], 'v_ple_norm': out['v_ple_norm'], 'v_ple_proj': out['v_ple_proj'], 'v_ple_gate': out['v_ple_gate'], 'v_s5_w_in': out['v_s5_w_in'], 'v_s5_a_re': out['v_s5_a_re'], 'v_s5_a_im': out['v_s5_a_im'], 'v_s5_log_dt': out['v_s5_log_dt'], 'v_s5_b_re': out['v_s5_b_re'], 'v_s5_b_im': out['v_s5_b_im'], 'v_s5_c_re': out['v_s5_c_re'], 'v_s5_c_im': out['v_s5_c_im'], 'v_s5_d': out['v_s5_d'], 'v_s5_w_glu': out['v_s5_w_glu'], 'v_sb_w_qkv': out['v_sb_w_qkv'], 'v_sb_w_o': out['v_sb_w_o'], 'v_final_norm': out['v_final_norm']}


def _loss(weights, diff, rest, loss_target):
    with _jax.named_scope("forward"):
        args = {**rest, TWIN_DIFF_INPUT: diff, **{k: w.astype(_WEIGHT_DTYPES[k]) for k, w in weights.items()}}
        y = _forward(args)
    with _jax.named_scope("loss_head"):
        err = _jnp.square(y.astype(_jnp.float32) - loss_target)
        return 0.5 * _jnp.sum(_jnp.mean(err, axis=-1)) if err.ndim else 0.5 * err


def _adamw(w, g, m, v):
    m = ADAM_B1 * m + (1.0 - ADAM_B1) * g
    v = ADAM_B2 * v + (1.0 - ADAM_B2) * _jnp.square(g)
    m_hat = m / (1.0 - ADAM_B1 ** ADAM_STEP)
    v_hat = v / (1.0 - ADAM_B2 ** ADAM_STEP)
    delta = -ADAM_LR * (m_hat / (_jnp.sqrt(v_hat) + ADAM_EPS) + ADAM_WD * w)
    return delta, m, v


def reference(x, p, ffn1_norm, ffn1_w1, ffn1_w3, ffn1_w2, mix_norm, ffn2_norm, ffn2_w1, ffn2_w3, ffn2_w2, ple_norm, ple_proj, ple_gate, s5_w_in, s5_a_re, s5_a_im, s5_log_dt, s5_b_re, s5_b_im, s5_c_re, s5_c_im, s5_d, s5_w_glu, sb_w_qkv, sb_w_o, final_norm, loss_target, m_ffn1_norm, m_ffn1_w1, m_ffn1_w3, m_ffn1_w2, m_mix_norm, m_ffn2_norm, m_ffn2_w1, m_ffn2_w3, m_ffn2_w2, m_ple_norm, m_ple_proj, m_ple_gate, m_s5_w_in, m_s5_a_re, m_s5_a_im, m_s5_log_dt, m_s5_b_re, m_s5_b_im, m_s5_c_re, m_s5_c_im, m_s5_d, m_s5_w_glu, m_sb_w_qkv, m_sb_w_o, m_final_norm, v_ffn1_norm, v_ffn1_w1, v_ffn1_w3, v_ffn1_w2, v_mix_norm, v_ffn2_norm, v_ffn2_w1, v_ffn2_w3, v_ffn2_w2, v_ple_norm, v_ple_proj, v_ple_gate, v_s5_w_in, v_s5_a_re, v_s5_a_im, v_s5_log_dt, v_s5_b_re, v_s5_b_im, v_s5_c_re, v_s5_c_im, v_s5_d, v_s5_w_glu, v_sb_w_qkv, v_sb_w_o, v_final_norm):
    given = dict(x=x, p=p, ffn1_norm=ffn1_norm, ffn1_w1=ffn1_w1, ffn1_w3=ffn1_w3, ffn1_w2=ffn1_w2, mix_norm=mix_norm, ffn2_norm=ffn2_norm, ffn2_w1=ffn2_w1, ffn2_w3=ffn2_w3, ffn2_w2=ffn2_w2, ple_norm=ple_norm, ple_proj=ple_proj, ple_gate=ple_gate, s5_w_in=s5_w_in, s5_a_re=s5_a_re, s5_a_im=s5_a_im, s5_log_dt=s5_log_dt, s5_b_re=s5_b_re, s5_b_im=s5_b_im, s5_c_re=s5_c_re, s5_c_im=s5_c_im, s5_d=s5_d, s5_w_glu=s5_w_glu, sb_w_qkv=sb_w_qkv, sb_w_o=sb_w_o, final_norm=final_norm, loss_target=loss_target, m_ffn1_norm=m_ffn1_norm, m_ffn1_w1=m_ffn1_w1, m_ffn1_w3=m_ffn1_w3, m_ffn1_w2=m_ffn1_w2, m_mix_norm=m_mix_norm, m_ffn2_norm=m_ffn2_norm, m_ffn2_w1=m_ffn2_w1, m_ffn2_w3=m_ffn2_w3, m_ffn2_w2=m_ffn2_w2, m_ple_norm=m_ple_norm, m_ple_proj=m_ple_proj, m_ple_gate=m_ple_gate, m_s5_w_in=m_s5_w_in, m_s5_a_re=m_s5_a_re, m_s5_a_im=m_s5_a_im, m_s5_log_dt=m_s5_log_dt, m_s5_b_re=m_s5_b_re, m_s5_b_im=m_s5_b_im, m_s5_c_re=m_s5_c_re, m_s5_c_im=m_s5_c_im, m_s5_d=m_s5_d, m_s5_w_glu=m_s5_w_glu, m_sb_w_qkv=m_sb_w_qkv, m_sb_w_o=m_sb_w_o, m_final_norm=m_final_norm, v_ffn1_norm=v_ffn1_norm, v_ffn1_w1=v_ffn1_w1, v_ffn1_w3=v_ffn1_w3, v_ffn1_w2=v_ffn1_w2, v_mix_norm=v_mix_norm, v_ffn2_norm=v_ffn2_norm, v_ffn2_w1=v_ffn2_w1, v_ffn2_w3=v_ffn2_w3, v_ffn2_w2=v_ffn2_w2, v_ple_norm=v_ple_norm, v_ple_proj=v_ple_proj, v_ple_gate=v_ple_gate, v_s5_w_in=v_s5_w_in, v_s5_a_re=v_s5_a_re, v_s5_a_im=v_s5_a_im, v_s5_log_dt=v_s5_log_dt, v_s5_b_re=v_s5_b_re, v_s5_b_im=v_s5_b_im, v_s5_c_re=v_s5_c_re, v_s5_c_im=v_s5_c_im, v_s5_d=v_s5_d, v_s5_w_glu=v_s5_w_glu, v_sb_w_qkv=v_sb_w_qkv, v_sb_w_o=v_sb_w_o, v_final_norm=v_final_norm)
    weights = {n: given[n] for n in TWIN_WEIGHTS}
    shared = {n: given[n] for n in SHARED_INPUTS}
    per_example = {n: given[n] for n in ['x', 'p']}
    grad_fn = _jax.value_and_grad(_loss, argnums=(0, 1))

    def one_microbatch(ex, loss_target):
        ex = dict(ex)
        diff = ex.pop(TWIN_DIFF_INPUT)
        return grad_fn(weights, diff, {**shared, **ex}, loss_target)

    if N_MICROBATCH == 1:
        loss, (grad_w, grad_x) = one_microbatch(per_example, given["loss_target"])
    else:
        def body(carry, xs):
            loss_sum, grad_sum = carry
            l_k, (gw_k, gx_k) = one_microbatch(xs[0], xs[1])
            with _jax.named_scope("update"):
                return (loss_sum + l_k, _jax.tree.map(_jnp.add, grad_sum, gw_k)), gx_k

        init = (_jnp.zeros((), _jnp.float32), _jax.tree.map(_jnp.zeros_like, weights))
        (loss, grad_w), grad_x = _jax.lax.scan(body, init, (per_example, given["loss_target"]))
    with _jax.named_scope("update"):
        delta_w, new_m, new_v = {}, {}, {}
        for n in TWIN_WEIGHTS:
            delta_w[n], new_m[n], new_v[n] = _adamw(weights[n], grad_w[n], given["m_" + n], given["v_" + n])
    return (loss, grad_x, *[grad_w[n] for n in TWIN_WEIGHTS], *[delta_w[n] for n in TWIN_WEIGHTS],
            *[new_m[n] for n in TWIN_WEIGHTS], *[new_v[n] for n in TWIN_WEIGHTS])
```

```python
import functools
import math

import jax
import jax.numpy as jnp
from jax import lax
from jax.experimental import pallas as pl
from jax.experimental.pallas import tpu as pltpu

F32 = jnp.float32
BF16 = jnp.bfloat16
MESH = pl.DeviceIdType.MESH

N_DEV = 8
N_CHIP = 4
EPS = 1e-6
S5_GROUP = 16
S5_STATE = 64
S5_BLOCK_GROUPS = 16
HEAD_DIM = 64
Q_BLOCK = 128
SCAN_CHUNK = 128
SUBLANES = 8
FLAT_W = 1024
ADAM_LR, ADAM_B1, ADAM_B2, ADAM_EPS, ADAM_WD, ADAM_STEP = 1e-3, 0.9, 0.999, 1e-8, 0.01, 10
V7X_VMEM_LIMIT = 56 * 1024 * 1024
ROW_TILE = 256
WGRAD_ROW_TILE = 512
WGRAD_OUT_BYTES = 6 * 1024 * 1024

SHARDED = ("ffn1_w1", "ffn1_w3", "ffn1_w2", "ffn2_w1", "ffn2_w3", "ffn2_w2", "ple_proj", "ple_gate",
           "s5_w_in", "s5_w_glu", "sb_w_qkv", "sb_w_o")
COL_SHARDED = ("ffn1_w1", "ffn1_w3", "ffn2_w1", "ffn2_w3", "ple_proj", "s5_w_glu", "sb_w_qkv")
REPLICATED = ("ffn1_norm", "mix_norm", "ffn2_norm", "ple_norm", "s5_a_re", "s5_a_im", "s5_log_dt",
              "s5_b_re", "s5_b_im", "s5_c_re", "s5_c_im", "s5_d", "final_norm")
WEIGHTS = ("ffn1_norm", "ffn1_w1", "ffn1_w3", "ffn1_w2", "mix_norm", "ffn2_norm", "ffn2_w1", "ffn2_w3", "ffn2_w2",
           "ple_norm", "ple_proj", "ple_gate", "s5_w_in", "s5_a_re", "s5_a_im", "s5_log_dt", "s5_b_re", "s5_b_im",
           "s5_c_re", "s5_c_im", "s5_d", "s5_w_glu", "sb_w_qkv", "sb_w_o", "final_norm")


def _dot(a, b):
    return jnp.dot(a, b, preferred_element_type=F32)


def _dot_nt(a, b):
    return lax.dot_general(a, b, (((1,), (1,)), ((), ())), preferred_element_type=F32)


def _dot_tn(a, b):
    return lax.dot_general(a, b, (((0,), (0,)), ((), ())), preferred_element_type=F32)


def _dot3(x, m):
    hi = x.astype(BF16)
    r = x - hi.astype(F32)
    mid = r.astype(BF16)
    lo = (r - mid.astype(F32)).astype(BF16)
    return _dot(hi, m) + _dot(mid, m) + _dot(lo, m)


def _rms(x):
    r = lax.rsqrt(jnp.mean(x * x, axis=-1, keepdims=True) + EPS)
    return x * r, r


def _rms_bwd(dn, xh, r, g):
    gy = dn * g
    return r * (gy - xh * jnp.mean(gy * xh, axis=-1, keepdims=True))


_GELU_C = math.sqrt(2.0 / math.pi)


def _gelu(x):
    return 0.5 * x * (1.0 + jnp.tanh(_GELU_C * (x + 0.044715 * x * x * x)))


def _gelu_grad(x):
    th = jnp.tanh(_GELU_C * (x + 0.044715 * x * x * x))
    return 0.5 * (1.0 + th) + 0.5 * x * (1.0 - th * th) * _GELU_C * (1.0 + 3.0 * 0.044715 * x * x)


def _neg_softplus(z):
    return -(jnp.maximum(z, 0.0) + jnp.log(1.0 + jnp.exp(-jnp.abs(z))))


def _tile(n, want, mult=SUBLANES):
    for t in range(min(want, n), 0, -1):
        if n % t == 0 and t % mult == 0:
            return t
    return n


def _rows(tm, c):
    return pl.BlockSpec((tm, c), lambda i: (i, 0))


def _whole(a):
    nd = a.ndim
    return pl.BlockSpec(a.shape, lambda i: (0,) * nd)


def _row_params(sem="parallel"):
    return pltpu.CompilerParams(dimension_semantics=(sem,), vmem_limit_bytes=V7X_VMEM_LIMIT)


def _ffn_fwd(h, g, w1, w3, w2):
    t, d = h.shape
    f = w1.shape[1]
    tm = _tile(t, ROW_TILE)

    def body(h_ref, g_ref, w1_ref, w3_ref, w2_ref, ho_ref, n_ref, a_ref, b_ref):
        x = h_ref[...]
        xh, _ = _rms(x)
        n = (xh * g_ref[...]).astype(BF16)
        a = _dot(n, w1_ref[...])
        b = _dot(n, w3_ref[...])
        s = (a * jax.nn.sigmoid(a) * b).astype(BF16)
        ho_ref[...] = x + 0.5 * _dot(s, w2_ref[...])
        n_ref[...] = n
        a_ref[...] = a.astype(BF16)
        b_ref[...] = b.astype(BF16)

    return pl.pallas_call(
        body, name="ffn_fwd", grid=(t // tm,),
        in_specs=[_rows(tm, d), _whole(g), _whole(w1), _whole(w3), _whole(w2)],
        out_specs=[_rows(tm, d), _rows(tm, d), _rows(tm, f), _rows(tm, f)],
        out_shape=[jax.ShapeDtypeStruct((t, d), F32), jax.ShapeDtypeStruct((t, d), BF16),
                   jax.ShapeDtypeStruct((t, f), BF16), jax.ShapeDtypeStruct((t, f), BF16)],
        compiler_params=_row_params(),
    )(h, g, w1, w3, w2)


def _ffn_bwd_down(dh, a, b, w2):
    t, d = dh.shape
    f = a.shape[1]
    tm = _tile(t, ROW_TILE)

    def body(dh_ref, a_ref, b_ref, w2_ref, da_ref, db_ref, s_ref):
        ds = _dot_nt((0.5 * dh_ref[...]).astype(BF16), w2_ref[...])
        a32 = a_ref[...].astype(F32)
        b32 = b_ref[...].astype(F32)
        sig = jax.nn.sigmoid(a32)
        sil = a32 * sig
        da_ref[...] = (ds * b32 * (sig * (1.0 + a32 * (1.0 - sig)))).astype(BF16)
        db_ref[...] = (ds * sil).astype(BF16)
        s_ref[...] = (sil * b32).astype(BF16)

    return pl.pallas_call(
        body, name="ffn_bwd_down", grid=(t // tm,),
        in_specs=[_rows(tm, d), _rows(tm, f), _rows(tm, f), _whole(w2)],
        out_specs=[_rows(tm, f)] * 3,
        out_shape=[jax.ShapeDtypeStruct((t, f), BF16)] * 3,
        compiler_params=_row_params(),
    )(dh, a, b, w2)


def _lin_bwd_norm(dys, ws, h, g, dh, name):
    t, d = h.shape
    tm = _tile(t, ROW_TILE)
    k = len(dys)

    def body(*refs):
        dy_refs, w_refs = refs[:k], refs[k:2 * k]
        h_ref, g_ref, dh_ref, o_ref, dg_ref = refs[2 * k:]
        dn = _dot_nt(dy_refs[0][...].astype(BF16), w_refs[0][...])
        for j in range(1, k):
            dn = dn + _dot_nt(dy_refs[j][...].astype(BF16), w_refs[j][...])
        xh, r = _rms(h_ref[...])
        o_ref[...] = dh_ref[...] + _rms_bwd(dn, xh, r, g_ref[...])

        @pl.when(pl.program_id(0) == 0)
        def _():
            dg_ref[...] = jnp.zeros_like(dg_ref)

        dg_ref[...] += jnp.sum(dn * xh, axis=0, keepdims=True)

    return pl.pallas_call(
        body, name=name, grid=(t // tm,),
        in_specs=[_rows(tm, dy.shape[1]) for dy in dys] + [_whole(w) for w in ws] + [_rows(tm, d), _whole(g), _rows(tm, d)],
        out_specs=[_rows(tm, d), pl.BlockSpec((1, d), lambda i: (0, 0))],
        out_shape=[jax.ShapeDtypeStruct((t, d), F32), jax.ShapeDtypeStruct((1, d), F32)],
        compiler_params=_row_params("arbitrary"),
    )(*dys, *ws, h, g, dh)


def _wgrad(x, dy, name, scale=1.0):
    t, kk = x.shape
    n = dy.shape[1]
    tm = _tile(t, WGRAD_ROW_TILE)
    tn = _tile(n, max(128, WGRAD_OUT_BYTES // (4 * kk)), 128)
    steps = t // tm

    def body(x_ref, dy_ref, o_ref):
        i = pl.program_id(1)

        @pl.when(i == 0)
        def _():
            o_ref[...] = jnp.zeros_like(o_ref)

        o_ref[...] += _dot_tn(x_ref[...].astype(BF16), dy_ref[...].astype(BF16))
        if scale != 1.0:
            @pl.when(i == steps - 1)
            def _():
                o_ref[...] = o_ref[...] * scale

    return pl.pallas_call(
        body, name=name, grid=(n // tn, steps),
        in_specs=[pl.BlockSpec((tm, kk), lambda j, i: (i, 0)), pl.BlockSpec((tm, tn), lambda j, i: (i, j))],
        out_specs=pl.BlockSpec((kk, tn), lambda j, i: (0, j)),
        out_shape=jax.ShapeDtypeStruct((kk, n), F32),
        compiler_params=pltpu.CompilerParams(dimension_semantics=("parallel", "arbitrary"), vmem_limit_bytes=V7X_VMEM_LIMIT),
    )(x, dy)


def _norm_lin(h, g, ws, out_dtypes, name):
    t, d = h.shape
    tm = _tile(t, ROW_TILE)
    k = len(ws)

    def body(*refs):
        h_ref, g_ref = refs[:2]
        w_refs = refs[2:2 + k]
        n_ref = refs[2 + k]
        o_refs = refs[3 + k:]
        xh, _ = _rms(h_ref[...])
        n = (xh * g_ref[...]).astype(BF16)
        n_ref[...] = n
        for w_ref, o_ref in zip(w_refs, o_refs):
            o_ref[...] = _dot(n, w_ref[...]).astype(o_ref.dtype)

    return pl.pallas_call(
        body, name=name, grid=(t // tm,),
        in_specs=[_rows(tm, d), _whole(g)] + [_whole(w) for w in ws],
        out_specs=[_rows(tm, d)] + [_rows(tm, w.shape[1]) for w in ws],
        out_shape=[jax.ShapeDtypeStruct((t, d), BF16)] + [jax.ShapeDtypeStruct((t, w.shape[1]), dt) for w, dt in zip(ws, out_dtypes)],
        compiler_params=_row_params(),
    )(h, g, *ws)


def _lin_res(h, x, w, name):
    t, d = h.shape
    tm = _tile(t, ROW_TILE)

    def body(h_ref, x_ref, w_ref, o_ref):
        o_ref[...] = h_ref[...] + _dot(x_ref[...], w_ref[...])

    return pl.pallas_call(
        body, name=name, grid=(t // tm,),
        in_specs=[_rows(tm, d), _rows(tm, x.shape[1]), _whole(w)],
        out_specs=_rows(tm, d), out_shape=jax.ShapeDtypeStruct((t, d), F32),
        compiler_params=_row_params(),
    )(h, x, w)


def _lin_nt(dy, w, name):
    t = dy.shape[0]
    kk = w.shape[0]
    tm = _tile(t, ROW_TILE)

    def body(dy_ref, w_ref, o_ref):
        o_ref[...] = _dot_nt(dy_ref[...].astype(BF16), w_ref[...]).astype(BF16)

    return pl.pallas_call(
        body, name=name, grid=(t // tm,),
        in_specs=[_rows(tm, dy.shape[1]), _whole(w)],
        out_specs=_rows(tm, kk), out_shape=jax.ShapeDtypeStruct((t, kk), BF16),
        compiler_params=_row_params(),
    )(dy, w)


def _ple_fwd(h, g, wg, p, wp):
    t, d = h.shape
    tm = _tile(t, ROW_TILE)

    def body(h_ref, g_ref, wg_ref, p_ref, wp_ref, o_ref, n_ref):
        x = h_ref[...]
        xh, _ = _rms(x)
        n = (xh * g_ref[...]).astype(BF16)
        n_ref[...] = n
        gate = jax.nn.sigmoid(_dot(n, wg_ref[...]))
        o_ref[...] = x + _dot(p_ref[...].astype(BF16), wp_ref[...]) * gate

    return pl.pallas_call(
        body, name="ple_fwd", grid=(t // tm,),
        in_specs=[_rows(tm, d), _whole(g), _whole(wg), _rows(tm, p.shape[1]), _whole(wp)],
        out_specs=[_rows(tm, d), _rows(tm, d)],
        out_shape=[jax.ShapeDtypeStruct((t, d), F32), jax.ShapeDtypeStruct((t, d), BF16)],
        compiler_params=_row_params(),
    )(h, g, wg, p, wp)


def _ple_bwd(dh, h, g, n, p, wg, wp):
    t, d = h.shape
    tm = _tile(t, ROW_TILE)

    def body(dh_ref, h_ref, g_ref, n_ref, p_ref, wg_ref, wp_ref, o_ref, dgl_ref, dpp_ref, dg_ref):
        dh_v = dh_ref[...]
        gate = jax.nn.sigmoid(_dot(n_ref[...], wg_ref[...]))
        pp = _dot(p_ref[...].astype(BF16), wp_ref[...])
        dgl = (dh_v * pp * gate * (1.0 - gate)).astype(BF16)
        dgl_ref[...] = dgl
        dpp_ref[...] = (dh_v * gate).astype(BF16)
        dn = _dot_nt(dgl, wg_ref[...])
        xh, r = _rms(h_ref[...])
        o_ref[...] = dh_v + _rms_bwd(dn, xh, r, g_ref[...])

        @pl.when(pl.program_id(0) == 0)
        def _():
            dg_ref[...] = jnp.zeros_like(dg_ref)

        dg_ref[...] += jnp.sum(dn * xh, axis=0, keepdims=True)

    return pl.pallas_call(
        body, name="ple_bwd", grid=(t // tm,),
        in_specs=[_rows(tm, d), _rows(tm, d), _whole(g), _rows(tm, d), _rows(tm, p.shape[1]), _whole(wg), _whole(wp)],
        out_specs=[_rows(tm, d), _rows(tm, d), _rows(tm, d), pl.BlockSpec((1, d), lambda i: (0, 0))],
        out_shape=[jax.ShapeDtypeStruct((t, d), F32), jax.ShapeDtypeStruct((t, d), BF16),
                   jax.ShapeDtypeStruct((t, d), BF16), jax.ShapeDtypeStruct((1, d), F32)],
        compiler_params=_row_params("arbitrary"),
    )(dh, h, g, n, p, wg, wp)


def _loss_head(h, g, tgt):
    t, d = h.shape
    tm = _tile(t, ROW_TILE)

    def body(h_ref, g_ref, t_ref, l_ref, dh_ref, dg_ref):
        xh, r = _rms(h_ref[...])
        gg = g_ref[...]
        e = xh * gg - t_ref[...]
        dy = e * (1.0 / d)

        @pl.when(pl.program_id(0) == 0)
        def _():
            l_ref[...] = jnp.zeros_like(l_ref)
            dg_ref[...] = jnp.zeros_like(dg_ref)

        l_ref[...] += 0.5 * jnp.sum(jnp.mean(e * e, axis=-1, keepdims=True), axis=0, keepdims=True)
        dg_ref[...] += jnp.sum(dy * xh, axis=0, keepdims=True)
        dh_ref[...] = _rms_bwd(dy, xh, r, gg)

    return pl.pallas_call(
        body, name="loss_head", grid=(t // tm,),
        in_specs=[_rows(tm, d), _whole(g), _rows(tm, d)],
        out_specs=[pl.BlockSpec((1, 128), lambda i: (0, 0)), _rows(tm, d), pl.BlockSpec((1, d), lambda i: (0, 0))],
        out_shape=[jax.ShapeDtypeStruct((1, 128), F32), jax.ShapeDtypeStruct((t, d), F32), jax.ShapeDtypeStruct((1, d), F32)],
        compiler_params=_row_params("arbitrary"),
    )(h, g, tgt)


def _s5_disc_math(a_re, a_im, log_dt):
    lam_re = jnp.minimum(a_re, -1e-4)
    lam_im = a_im
    dt = jnp.exp(log_dt)
    mag = jnp.exp(lam_re * dt)
    abar_re = mag * jnp.cos(lam_im * dt)
    abar_im = mag * jnp.sin(lam_im * dt)
    den = lam_re * lam_re + lam_im * lam_im
    nr = abar_re - 1.0
    ni = abar_im
    return abar_re, abar_im, (nr * lam_re + ni * lam_im) / den, (ni * lam_re - nr * lam_im) / den


def _s5_disc(a_re, a_im, log_dt):
    gp = jax.ShapeDtypeStruct(a_re.shape, F32)

    def body(ar_ref, ai_ref, ld_ref, o0, o1, o2, o3):
        outs = _s5_disc_math(ar_ref[...], ai_ref[...], ld_ref[...])
        for o_ref, val in zip((o0, o1, o2, o3), outs):
            o_ref[...] = val

    return pl.pallas_call(body, name="s5_disc", out_shape=[gp] * 4)(a_re, a_im, log_dt)


def _s5_disc_bwd(a_re, a_im, log_dt, cts):
    def body(ar_ref, ai_ref, ld_ref, c0, c1, c2, c3, dar_ref, dai_ref, dld_ref):
        _, vjp = jax.vjp(_s5_disc_math, ar_ref[...], ai_ref[...], ld_ref[...])
        dar, dai, dld = vjp((c0[...], c1[...], c2[...], c3[...]))
        dar_ref[...] = dar
        dai_ref[...] = dai
        dld_ref[...] = dld

    return pl.pallas_call(
        body, name="s5_disc_bwd",
        out_shape=[jax.ShapeDtypeStruct(a_re.shape, F32), jax.ShapeDtypeStruct(a_im.shape, F32),
                   jax.ShapeDtypeStruct(log_dt.shape, F32)],
    )(a_re, a_im, log_dt, *cts)


def _blockdiag(w, gl):
    g, r, c = w.shape
    w = w.reshape(g // gl, gl, r, c)
    eye = jnp.eye(gl, dtype=w.dtype)
    return (w[:, :, :, None, :] * eye[None, :, None, :, None]).reshape(g // gl, gl * r, gl * c)


def _blockdiag_take(m, gl, r, c):
    nb = m.shape[0]
    return jnp.einsum("bgrgc->bgrc", m.reshape(nb, gl, r, gl, c)).reshape(nb * gl, r, c)


def _scan_rows(lc, step, carry, reverse=False):
    def blk(i, cs):
        i = (lc // SUBLANES - 1 - i) if reverse else i
        return step(pl.multiple_of(i * SUBLANES, SUBLANES), cs)

    return lax.fori_loop(0, lc // SUBLANES, blk, carry)


def _s5_scan_fwd(u, bre, bim, cre, cim, par, dskip, bl, seq, lc):
    t, d = u.shape
    nb, gw, ns = bre.shape
    nc = seq // lc

    def body(u_ref, bre_ref, bim_ref, cre_ref, cim_ref, par_ref, d_ref, y_ref, z_ref, st_ref, carry, sre, sim):
        @pl.when(pl.program_id(2) == 0)
        def _():
            carry[...] = jnp.zeros_like(carry)

        st_ref[...] = carry[...]
        uu = u_ref[...]
        ug = uu.astype(BF16)
        wre = _dot(ug, bre_ref[...])
        wim = _dot(ug, bim_ref[...])
        ar, ai = par_ref[0:1, :], par_ref[1:2, :]
        fr, fi = par_ref[2:3, :], par_ref[3:4, :]
        sre[...] = fr * wre - fi * wim
        sim[...] = fr * wim + fi * wre

        def step(base, cs):
            cr, ci = cs
            tr = sre[pl.ds(base, SUBLANES), :]
            ti = sim[pl.ds(base, SUBLANES), :]
            rows = lax.broadcasted_iota(jnp.int32, tr.shape, 0)
            outr, outi = tr, ti
            for k in range(SUBLANES):
                nr = ar * cr - ai * ci + tr[k:k + 1, :]
                ni = ar * ci + ai * cr + ti[k:k + 1, :]
                outr = jnp.where(rows == k, nr, outr)
                outi = jnp.where(rows == k, ni, outi)
                cr, ci = nr, ni
            sre[pl.ds(base, SUBLANES), :] = outr
            sim[pl.ds(base, SUBLANES), :] = outi
            return cr, ci

        cr, ci = _scan_rows(lc, step, (carry[0:1, :], carry[1:2, :]))
        carry[0:1, :] = cr
        carry[1:2, :] = ci
        y = _dot(sre[...].astype(BF16), cre_ref[...]) - _dot(sim[...].astype(BF16), cim_ref[...]) + d_ref[...] * uu
        y_ref[...] = y
        z_ref[...] = _gelu(y).astype(BF16)

    tok = pl.BlockSpec((lc, gw), lambda g, b, c: (b * nc + c, g))
    mat_b = pl.BlockSpec((None, gw, ns), lambda g, b, c: (g, 0, 0))
    mat_c = pl.BlockSpec((None, ns, gw), lambda g, b, c: (g, 0, 0))
    return pl.pallas_call(
        body, name="s5_scan_fwd", grid=(nb, bl, nc),
        in_specs=[tok, mat_b, mat_b, mat_c, mat_c, pl.BlockSpec((None, 8, ns), lambda g, b, c: (g, 0, 0)),
                  pl.BlockSpec((1, gw), lambda g, b, c: (0, g))],
        out_specs=[tok, tok, pl.BlockSpec((None, None, 2, ns), lambda g, b, c: (g, b * nc + c, 0, 0))],
        out_shape=[jax.ShapeDtypeStruct((t, d), F32), jax.ShapeDtypeStruct((t, d), BF16),
                   jax.ShapeDtypeStruct((nb, bl * nc, 2, ns), F32)],
        scratch_shapes=[pltpu.VMEM((2, ns), F32), pltpu.VMEM((lc, ns), F32), pltpu.VMEM((lc, ns), F32)],
        compiler_params=pltpu.CompilerParams(dimension_semantics=("parallel", "arbitrary", "arbitrary"),
                                             vmem_limit_bytes=V7X_VMEM_LIMIT),
    )(u, bre, bim, cre, cim, par, dskip)


def _s5_scan_bwd(u, dy, st, bre, bim, cre, cim, par, dskip, bl, seq, lc):
    t, d = u.shape
    nb, gw, ns = bre.shape
    nc = seq // lc

    def body(u_ref, dy_ref, st_ref, bre_ref, bim_ref, cre_ref, cim_ref, par_ref, d_ref,
             du_ref, dbre_ref, dbim_ref, dcre_ref, dcim_ref, dpar_ref, dd_ref,
             lcarry, sre, sim, pre, pim, wre_s, wim_s, lre, lim):
        b, c = pl.program_id(1), pl.program_id(2)

        @pl.when((b == 0) & (c == 0))
        def _():
            for ref in (dbre_ref, dbim_ref, dcre_ref, dcim_ref, dpar_ref, dd_ref):
                ref[...] = jnp.zeros_like(ref)

        @pl.when(c == 0)
        def _():
            lcarry[...] = jnp.zeros_like(lcarry)

        uu = u_ref[...]
        ug = uu.astype(BF16)
        dyv = dy_ref[...]
        dyb = dyv.astype(BF16)
        ar, ai = par_ref[0:1, :], par_ref[1:2, :]
        fr, fi = par_ref[2:3, :], par_ref[3:4, :]
        wre = _dot(ug, bre_ref[...])
        wim = _dot(ug, bim_ref[...])
        wre_s[...] = wre
        wim_s[...] = wim
        sre[...] = fr * wre - fi * wim
        sim[...] = fr * wim + fi * wre

        def fstep(base, cs):
            cr, ci = cs
            tr = sre[pl.ds(base, SUBLANES), :]
            ti = sim[pl.ds(base, SUBLANES), :]
            rows = lax.broadcasted_iota(jnp.int32, tr.shape, 0)
            outr, outi, prr, pri = tr, ti, tr, ti
            for k in range(SUBLANES):
                prr = jnp.where(rows == k, cr, prr)
                pri = jnp.where(rows == k, ci, pri)
                nr = ar * cr - ai * ci + tr[k:k + 1, :]
                ni = ar * ci + ai * cr + ti[k:k + 1, :]
                outr = jnp.where(rows == k, nr, outr)
                outi = jnp.where(rows == k, ni, outi)
                cr, ci = nr, ni
            sre[pl.ds(base, SUBLANES), :] = outr
            sim[pl.ds(base, SUBLANES), :] = outi
            pre[pl.ds(base, SUBLANES), :] = prr
            pim[pl.ds(base, SUBLANES), :] = pri
            return cr, ci

        _scan_rows(lc, fstep, (st_ref[0:1, :], st_ref[1:2, :]))

        lre[...] = _dot_nt(dyb, cre_ref[...])
        lim[...] = -_dot_nt(dyb, cim_ref[...])

        def bstep(base, cs):
            cr, ci = cs
            tr = lre[pl.ds(base, SUBLANES), :]
            ti = lim[pl.ds(base, SUBLANES), :]
            rows = lax.broadcasted_iota(jnp.int32, tr.shape, 0)
            outr, outi = tr, ti
            for k in range(SUBLANES - 1, -1, -1):
                nr = tr[k:k + 1, :] + ar * cr + ai * ci
                ni = ti[k:k + 1, :] + ar * ci - ai * cr
                outr = jnp.where(rows == k, nr, outr)
                outi = jnp.where(rows == k, ni, outi)
                cr, ci = nr, ni
            lre[pl.ds(base, SUBLANES), :] = outr
            lim[pl.ds(base, SUBLANES), :] = outi
            return cr, ci

        cr, ci = _scan_rows(lc, bstep, (lcarry[0:1, :], lcarry[1:2, :]), reverse=True)
        lcarry[0:1, :] = cr
        lcarry[1:2, :] = ci

        lr, li = lre[...], lim[...]
        spr, spi = pre[...], pim[...]
        wr, wi = wre_s[...], wim_s[...]
        dpar_ref[0:1, :] += jnp.sum(lr * spr + li * spi, axis=0, keepdims=True)
        dpar_ref[1:2, :] += jnp.sum(li * spr - lr * spi, axis=0, keepdims=True)
        dpar_ref[2:3, :] += jnp.sum(lr * wr + li * wi, axis=0, keepdims=True)
        dpar_ref[3:4, :] += jnp.sum(li * wr - lr * wi, axis=0, keepdims=True)
        dwr = (fr * lr + fi * li).astype(BF16)
        dwi = (fr * li - fi * lr).astype(BF16)
        dsk = d_ref[...]
        du_ref[...] = _dot_nt(dwr, bre_ref[...]) + _dot_nt(dwi, bim_ref[...]) + dsk * dyv
        dd_ref[...] += jnp.sum(dyv * uu, axis=0, keepdims=True)
        dbre_ref[...] += _dot_tn(ug, dwr)
        dbim_ref[...] += _dot_tn(ug, dwi)
        dcre_ref[...] += _dot_tn(sre[...].astype(BF16), dyb)
        dcim_ref[...] -= _dot_tn(sim[...].astype(BF16), dyb)

    tok = pl.BlockSpec((lc, gw), lambda g, b, c: (b * nc + nc - 1 - c, g))
    mat_b = pl.BlockSpec((None, gw, ns), lambda g, b, c: (g, 0, 0))
    mat_c = pl.BlockSpec((None, ns, gw), lambda g, b, c: (g, 0, 0))
    rows8 = pl.BlockSpec((None, 8, ns), lambda g, b, c: (g, 0, 0))
    dvec = pl.BlockSpec((1, gw), lambda g, b, c: (0, g))
    tile = pltpu.VMEM((lc, ns), F32)
    return pl.pallas_call(
        body, name="s5_scan_bwd", grid=(nb, bl, nc),
        in_specs=[tok, tok, pl.BlockSpec((None, None, 2, ns), lambda g, b, c: (g, b * nc + nc - 1 - c, 0, 0)),
                  mat_b, mat_b, mat_c, mat_c, rows8, dvec],
        out_specs=[tok, mat_b, mat_b, mat_c, mat_c, rows8, dvec],
        out_shape=[jax.ShapeDtypeStruct((t, d), F32),
                   jax.ShapeDtypeStruct((nb, gw, ns), F32), jax.ShapeDtypeStruct((nb, gw, ns), F32),
                   jax.ShapeDtypeStruct((nb, ns, gw), F32), jax.ShapeDtypeStruct((nb, ns, gw), F32),
                   jax.ShapeDtypeStruct((nb, 8, ns), F32), jax.ShapeDtypeStruct((1, d), F32)],
        scratch_shapes=[pltpu.VMEM((2, ns), F32)] + [tile] * 8,
        compiler_params=pltpu.CompilerParams(dimension_semantics=("arbitrary", "arbitrary", "arbitrary"),
                                             vmem_limit_bytes=V7X_VMEM_LIMIT),
    )(u, dy, st, bre, bim, cre, cim, par, dskip)


def _s5_out(h, z, wglu):
    t, d = h.shape
    tm = _tile(t, ROW_TILE)

    def body(h_ref, z_ref, w_ref, o_ref):
        zz = _dot(z_ref[...], w_ref[...])
        o_ref[...] = h_ref[...] + zz[:, :d] * jax.nn.sigmoid(zz[:, d:])

    return pl.pallas_call(
        body, name="s5_out", grid=(t // tm,),
        in_specs=[_rows(tm, d), _rows(tm, d), _whole(wglu)],
        out_specs=_rows(tm, d), out_shape=jax.ShapeDtypeStruct((t, d), F32),
        compiler_params=_row_params(),
    )(h, z, wglu)


def _s5_out_bwd(dh, z, y, wglu):
    t, d = dh.shape
    tm = _tile(t, ROW_TILE)

    def body(dh_ref, z_ref, y_ref, w_ref, dy_ref, dzz_ref):
        zz = _dot(z_ref[...], w_ref[...])
        out, sg = zz[:, :d], jax.nn.sigmoid(zz[:, d:])
        dh_v = dh_ref[...]
        dzz = jnp.concatenate([dh_v * sg, dh_v * out * sg * (1.0 - sg)], axis=1).astype(BF16)
        dzz_ref[...] = dzz
        dy_ref[...] = _dot_nt(dzz, w_ref[...]) * _gelu_grad(y_ref[...])

    return pl.pallas_call(
        body, name="s5_out_bwd", grid=(t // tm,),
        in_specs=[_rows(tm, d), _rows(tm, d), _rows(tm, d), _whole(wglu)],
        out_specs=[_rows(tm, d), _rows(tm, 2 * d)],
        out_shape=[jax.ShapeDtypeStruct((t, d), F32), jax.ShapeDtypeStruct((t, 2 * d), BF16)],
        compiler_params=_row_params(),
    )(dh, z, y, wglu)


def _head_spec(seq, w):
    return pl.BlockSpec((None, None, seq, w), lambda b, h: (b, h, 0, 0))


def _sb_fwd(q, k, v):
    bsz, nh, seq, dh = q.shape
    nq = seq // Q_BLOCK
    scale = dh ** -0.5

    def body(q_ref, k_ref, v_ref, o_ref, tot_ref):
        row = lax.broadcasted_iota(jnp.int32, (Q_BLOCK, Q_BLOCK), 0)
        col = lax.broadcasted_iota(jnp.int32, (Q_BLOCK, Q_BLOCK), 1)
        later_keys = (row > col).astype(BF16)

        def q_loop(qi, _):
            r0 = pl.multiple_of(qi * Q_BLOCK, Q_BLOCK)
            qb = q_ref[pl.ds(r0, Q_BLOCK), :]

            def k_loop(jj, carry):
                acc, run = carry
                c0 = pl.multiple_of((qi - jj) * Q_BLOCK, Q_BLOCK)
                kb = k_ref[pl.ds(c0, Q_BLOCK), :]
                vb = v_ref[pl.ds(c0, Q_BLOCK), :]
                z = _dot_nt(qb, kb) * scale
                lkr = _neg_softplus(z)
                mask = jnp.logical_or(col < row, jj > 0)
                lk = jnp.where(mask, lkr, 0.0)
                later = _dot3(lk, later_keys) + run
                att = jnp.where(mask, jnp.exp(z + lkr + later), 0.0)
                return acc + _dot(att.astype(BF16), vb), run + jnp.sum(lk, axis=1, keepdims=True)

            acc, run = lax.fori_loop(0, qi + 1, k_loop,
                                     (jnp.zeros((Q_BLOCK, dh), F32), jnp.zeros((Q_BLOCK, 1), F32)))
            o_ref[pl.ds(r0, Q_BLOCK), :] = acc.astype(BF16)
            tot_ref[pl.ds(r0, Q_BLOCK), :] = run
            return 0

        lax.fori_loop(0, nq, q_loop, 0)

    return pl.pallas_call(
        body, name="sb_fwd", grid=(bsz, nh),
        in_specs=[_head_spec(seq, dh)] * 3,
        out_specs=[_head_spec(seq, dh), _head_spec(seq, 1)],
        out_shape=[jax.ShapeDtypeStruct((bsz, nh, seq, dh), BF16), jax.ShapeDtypeStruct((bsz, nh, seq, 1), F32)],
        compiler_params=pltpu.CompilerParams(dimension_semantics=("parallel", "parallel"), vmem_limit_bytes=V7X_VMEM_LIMIT),
    )(q, k, v)


def _sb_bwd(q, k, v, do, tot):
    bsz, nh, seq, dh = q.shape
    nq = seq // Q_BLOCK
    scale = dh ** -0.5

    def body(q_ref, k_ref, v_ref, do_ref, tot_ref, dq_ref, dk_ref, dv_ref, dka, dva):
        dka[...] = jnp.zeros_like(dka)
        dva[...] = jnp.zeros_like(dva)
        row = lax.broadcasted_iota(jnp.int32, (Q_BLOCK, Q_BLOCK), 0)
        col = lax.broadcasted_iota(jnp.int32, (Q_BLOCK, Q_BLOCK), 1)
        upto = (row <= col).astype(BF16)
        before = (row < col).astype(BF16)

        def q_loop(qi, _):
            r0 = pl.multiple_of(qi * Q_BLOCK, Q_BLOCK)
            qb = q_ref[pl.ds(r0, Q_BLOCK), :]
            dob = do_ref[pl.ds(r0, Q_BLOCK), :]
            total = tot_ref[pl.ds(r0, Q_BLOCK), :]

            def k_loop(kj, carry):
                dqa, pre, gpre = carry
                c0 = pl.multiple_of(kj * Q_BLOCK, Q_BLOCK)
                kb = k_ref[pl.ds(c0, Q_BLOCK), :]
                vb = v_ref[pl.ds(c0, Q_BLOCK), :]
                z = _dot_nt(qb, kb) * scale
                lkr = _neg_softplus(z)
                mask = jnp.logical_or(col < row, kj < qi)
                lk = jnp.where(mask, lkr, 0.0)
                later = total - (_dot3(lk, upto) + pre)
                att = jnp.where(mask, jnp.exp(z + lkr + later), 0.0)
                gl = _dot_nt(dob, vb) * att
                gbefore = _dot3(gl, before) + gpre
                sig = jax.nn.sigmoid(z)
                dz = (jnp.where(mask, gl * (1.0 - sig) - gbefore * sig, 0.0) * scale).astype(BF16)
                attb = att.astype(BF16)
                dka[pl.ds(c0, Q_BLOCK), :] += _dot_tn(dz, qb)
                dva[pl.ds(c0, Q_BLOCK), :] += _dot_tn(attb, dob)
                return (dqa + _dot(dz, kb), pre + jnp.sum(lk, axis=1, keepdims=True),
                        gpre + jnp.sum(gl, axis=1, keepdims=True))

            dqa, _, _ = lax.fori_loop(0, qi + 1, k_loop,
                                      (jnp.zeros((Q_BLOCK, dh), F32), jnp.zeros((Q_BLOCK, 1), F32),
                                       jnp.zeros((Q_BLOCK, 1), F32)))
            dq_ref[pl.ds(r0, Q_BLOCK), :] = dqa.astype(BF16)
            return 0

        lax.fori_loop(0, nq, q_loop, 0)
        dk_ref[...] = dka[...].astype(BF16)
        dv_ref[...] = dva[...].astype(BF16)

    hs = _head_spec(seq, dh)
    return pl.pallas_call(
        body, name="sb_bwd", grid=(bsz, nh),
        in_specs=[hs, hs, hs, hs, _head_spec(seq, 1)],
        out_specs=[hs, hs, hs],
        out_shape=[jax.ShapeDtypeStruct((bsz, nh, seq, dh), BF16)] * 3,
        scratch_shapes=[pltpu.VMEM((seq, dh), F32), pltpu.VMEM((seq, dh), F32)],
        compiler_params=pltpu.CompilerParams(dimension_semantics=("parallel", "parallel"), vmem_limit_bytes=V7X_VMEM_LIMIT),
    )(q, k, v, do, tot)


def _to_heads(a, bsz, seq):
    return a.reshape(bsz, seq, -1, HEAD_DIM).transpose(0, 2, 1, 3)


def _from_heads(a):
    bsz, nh, seq, dh = a.shape
    return a.transpose(0, 2, 1, 3).reshape(bsz * seq, nh * dh)


def _coords():
    return lax.axis_index("x"), lax.axis_index("y"), lax.axis_index("c")


def _all_gather(x, name):
    r, c = x.shape

    def body(x_ref, out_ref, send_sems, recv_sems, local_sem):
        mx, my, mc = _coords()
        me, sibling = (mx, my, mc), (mx, my, 1 - mc)
        chips = [(1 - mx, my), (mx, 1 - my), (1 - mx, 1 - my)]

        def blk(px, py, pc):
            return out_ref.at[4 * px + 2 * py + pc]

        def copy(k, block, to, src=None):
            return pltpu.make_async_remote_copy(
                src_ref=blk(*block) if src is None else src, dst_ref=blk(*block),
                send_sem=send_sems.at[k], recv_sem=recv_sems.at[k], device_id=to, device_id_type=MESH)

        mine = pltpu.make_async_copy(x_ref, blk(*me), local_sem)
        mine.start()
        first = [copy(0, me, sibling, src=x_ref)]
        first += [copy(1 + j, me, (*chip, mc), src=x_ref) for j, chip in enumerate(chips)]
        for cp in first:
            cp.start()
        passed = [copy(4 + j, (*chip, mc), sibling) for j, chip in enumerate(chips)]
        for j, chip in enumerate(chips):
            copy(1 + j, (*chip, mc), me).wait_recv()
            passed[j].start()
        copy(0, sibling, me).wait_recv()
        for j, chip in enumerate(chips):
            copy(4 + j, (*chip, 1 - mc), me).wait_recv()
        for cp in first + passed:
            cp.wait_send()
        mine.wait()

    return pl.pallas_call(
        body, name=name,
        out_shape=jax.ShapeDtypeStruct((N_DEV, r, c), x.dtype),
        in_specs=[pl.BlockSpec(memory_space=pl.ANY)],
        out_specs=pl.BlockSpec(memory_space=pl.ANY),
        scratch_shapes=[pltpu.SemaphoreType.DMA((7,)), pltpu.SemaphoreType.DMA((7,)), pltpu.SemaphoreType.DMA],
    )(x)


def _swap_sibling(g):
    def body(g_ref, out_ref, send_sem, recv_sem):
        mx, my, mc = _coords()
        cp = pltpu.make_async_remote_copy(src_ref=g_ref, dst_ref=out_ref, send_sem=send_sem, recv_sem=recv_sem,
                                          device_id=(mx, my, 1 - mc), device_id_type=MESH)
        cp.start()
        cp.wait()

    return pl.pallas_call(
        body, name="rs_sibling", out_shape=jax.ShapeDtypeStruct(g.shape, g.dtype),
        in_specs=[pl.BlockSpec(memory_space=pl.ANY)], out_specs=pl.BlockSpec(memory_space=pl.ANY),
        scratch_shapes=[pltpu.SemaphoreType.DMA, pltpu.SemaphoreType.DMA],
    )(g)


def _add(a, b):
    n, r, c = a.shape
    tr = _tile(r, 512)

    def body(a_ref, b_ref, o_ref):
        o_ref[...] = a_ref[...] + b_ref[...]

    spec = pl.BlockSpec((None, tr, c), lambda i, j: (i, j, 0))
    return pl.pallas_call(
        body, name="rs_add", grid=(n, r // tr), in_specs=[spec, spec], out_specs=spec,
        out_shape=jax.ShapeDtypeStruct(a.shape, a.dtype),
        compiler_params=pltpu.CompilerParams(dimension_semantics=("parallel", "parallel")),
    )(a, b)


def _exchange_chips(p):
    def body(p_ref, out_ref, send_sems, recv_sems, local_sem):
        mx, my, mc = _coords()
        here = 2 * mx + my
        chips = [(1 - mx, my), (mx, 1 - my), (1 - mx, 1 - my)]
        mine = pltpu.make_async_copy(p_ref.at[here], out_ref.at[here], local_sem)
        mine.start()
        copies = [pltpu.make_async_remote_copy(
            src_ref=p_ref.at[2 * cx + cy], dst_ref=out_ref.at[here], send_sem=send_sems.at[j], recv_sem=recv_sems.at[j],
            device_id=(cx, cy, mc), device_id_type=MESH) for j, (cx, cy) in enumerate(chips)]
        for cp in copies:
            cp.start()
        for cp in copies:
            cp.wait()
        mine.wait()

    return pl.pallas_call(
        body, name="rs_chips", out_shape=jax.ShapeDtypeStruct(p.shape, p.dtype),
        in_specs=[pl.BlockSpec(memory_space=pl.ANY)], out_specs=pl.BlockSpec(memory_space=pl.ANY),
        scratch_shapes=[pltpu.SemaphoreType.DMA((3,)), pltpu.SemaphoreType.DMA((3,)), pltpu.SemaphoreType.DMA],
    )(p)


def _adamw(parts, w, m, v, name):
    n, r, c = parts.shape
    tr = _tile(r, 320)
    bc1 = 1.0 - ADAM_B1 ** ADAM_STEP
    bc2 = 1.0 - ADAM_B2 ** ADAM_STEP

    def body(p_ref, w_ref, m_ref, v_ref, g_ref, d_ref, mo_ref, vo_ref):
        g = p_ref[0]
        for j in range(1, n):
            g = g + p_ref[j]
        mn = ADAM_B1 * m_ref[...] + (1.0 - ADAM_B1) * g
        vn = ADAM_B2 * v_ref[...] + (1.0 - ADAM_B2) * (g * g)
        g_ref[...] = g
        mo_ref[...] = mn
        vo_ref[...] = vn
        d_ref[...] = -ADAM_LR * ((mn / bc1) / (jnp.sqrt(vn / bc2) + ADAM_EPS) + ADAM_WD * w_ref[...])

    flat = pl.BlockSpec((tr, c), lambda i: (i, 0))
    return pl.pallas_call(
        body, name=name, grid=(r // tr,),
        in_specs=[pl.BlockSpec((n, tr, c), lambda i: (0, i, 0)), flat, flat, flat],
        out_specs=[flat] * 4, out_shape=[jax.ShapeDtypeStruct((r, c), F32)] * 4,
        compiler_params=pltpu.CompilerParams(dimension_semantics=("parallel",)),
    )(parts, w, m, v)


def _pack(arrs, dtype=F32):
    cols = []
    for a in arrs:
        f = a.reshape(-1).astype(dtype)
        cols.append(jnp.pad(f, (0, -f.shape[0] % FLAT_W)))
    flat = jnp.concatenate(cols)
    flat = jnp.pad(flat, (0, -flat.shape[0] % (FLAT_W * SUBLANES)))
    return flat.reshape(-1, FLAT_W)


def _unpack(flat, shapes, lead=()):
    flat = flat.reshape(lead + (-1,))
    out, off = [], 0
    for s in shapes:
        n = math.prod(s)
        out.append(flat[..., off:off + n].reshape(lead + tuple(s)))
        off += n + (-n % FLAT_W)
    return out


def _split_cols(a):
    n = a.shape[-1] // N_DEV
    return jnp.moveaxis(a.reshape(a.shape[:-1] + (N_DEV, n)), -2, 0)


def _split_rows(a):
    k = a.shape[-2] // N_DEV
    return jnp.moveaxis(a.reshape(a.shape[:-2] + (N_DEV, k, a.shape[-1])), -3, 0)


def _join_cols(a):
    a = jnp.moveaxis(a, 0, -2)
    return a.reshape(a.shape[:-2] + (a.shape[-2] * a.shape[-1],))


def _join_rows(a):
    a = jnp.moveaxis(a, 0, -3)
    return a.reshape(a.shape[:-3] + (a.shape[-3] * a.shape[-2], a.shape[-1]))


def kernel(x, p, ffn1_norm, ffn1_w1, ffn1_w3, ffn1_w2, mix_norm, ffn2_norm, ffn2_w1, ffn2_w3, ffn2_w2, ple_norm, ple_proj, ple_gate, s5_w_in, s5_a_re, s5_a_im, s5_log_dt, s5_b_re, s5_b_im, s5_c_re, s5_c_im, s5_d, s5_w_glu, sb_w_qkv, sb_w_o, final_norm, loss_target, m_ffn1_norm, m_ffn1_w1, m_ffn1_w3, m_ffn1_w2, m_mix_norm, m_ffn2_norm, m_ffn2_w1, m_ffn2_w3, m_ffn2_w2, m_ple_norm, m_ple_proj, m_ple_gate, m_s5_w_in, m_s5_a_re, m_s5_a_im, m_s5_log_dt, m_s5_b_re, m_s5_b_im, m_s5_c_re, m_s5_c_im, m_s5_d, m_s5_w_glu, m_sb_w_qkv, m_sb_w_o, m_final_norm, v_ffn1_norm, v_ffn1_w1, v_ffn1_w3, v_ffn1_w2, v_mix_norm, v_ffn2_norm, v_ffn2_w1, v_ffn2_w3, v_ffn2_w2, v_ple_norm, v_ple_proj, v_ple_gate, v_s5_w_in, v_s5_a_re, v_s5_a_im, v_s5_log_dt, v_s5_b_re, v_s5_b_im, v_s5_c_re, v_s5_c_im, v_s5_d, v_s5_w_glu, v_sb_w_qkv, v_sb_w_o, v_final_norm):
    given = dict(locals())
    wts = {n: given[n] for n in WEIGHTS}
    mom = {n: given["m_" + n] for n in WEIGHTS}
    var = {n: given["v_" + n] for n in WEIGHTS}
    bl, seq, d = x.shape
    t = bl * seq
    depth = p.shape[0]
    my_c = lax.axis_index("c")

    shard_shapes = [wts[n].shape for n in SHARDED]
    gathered = _all_gather(_pack([wts[n] for n in SHARDED], BF16), "gather_weights")
    full = {}
    for n, blocks in zip(SHARDED, _unpack(gathered, shard_shapes, lead=(N_DEV,))):
        full[n] = _join_cols(blocks) if n in COL_SHARDED else _join_rows(blocks)

    def row(a):
        return a.reshape(1, -1)

    n_groups = d // S5_GROUP
    a_re, a_im = s5_a_re[0], s5_a_im[0]
    log_dt = s5_log_dt[0].reshape(n_groups, 1)
    disc = _s5_disc(a_re, a_im, log_dt)
    nb = n_groups // S5_BLOCK_GROUPS
    ns = S5_BLOCK_GROUPS * S5_STATE
    par = jnp.concatenate([jnp.stack([q.reshape(nb, ns) for q in disc], axis=1), jnp.zeros((nb, 4, ns), F32)], axis=1)
    bre = _blockdiag(s5_b_re[0].transpose(0, 2, 1), S5_BLOCK_GROUPS).astype(BF16)
    bim = _blockdiag(s5_b_im[0].transpose(0, 2, 1), S5_BLOCK_GROUPS).astype(BF16)
    cre = _blockdiag(s5_c_re[0].transpose(0, 2, 1), S5_BLOCK_GROUPS).astype(BF16)
    cim = _blockdiag(s5_c_im[0].transpose(0, 2, 1), S5_BLOCK_GROUPS).astype(BF16)
    lc = min(SCAN_CHUNK, seq)

    h = x.reshape(t, d)
    saved = []
    for i in range(depth):
        s = {"h0": h}
        h, s["n1"], s["a1"], s["b1"] = _ffn_fwd(h, row(ffn1_norm[i]), full["ffn1_w1"][i], full["ffn1_w3"][i], full["ffn1_w2"][i])
        s["h1"] = h
        j = i // 2
        if i % 2 == 0:
            s["hn"], s["u"] = _norm_lin(h, row(mix_norm[i]), [full["s5_w_in"][j]], [F32], "s5_in")
            s["y"], s["z"], s["st"] = _s5_scan_fwd(s["u"], bre, bim, cre, cim, par, row(s5_d[j]), bl, seq, lc)
            h = _s5_out(h, s["z"], full["s5_w_glu"][j])
        else:
            wq, wk, wv = jnp.split(full["sb_w_qkv"][j], 3, axis=1)
            s["hn"], q, k, v = _norm_lin(h, row(mix_norm[i]), [wq, wk, wv], [BF16] * 3, "sb_in")
            s["q"], s["k"], s["v"] = (_to_heads(a, bl, seq) for a in (q, k, v))
            o, s["tot"] = _sb_fwd(s["q"], s["k"], s["v"])
            s["o"] = _from_heads(o)
            h = _lin_res(h, s["o"], full["sb_w_o"][j], "sb_out")
        s["h2"] = h
        h, s["n2"], s["a2"], s["b2"] = _ffn_fwd(h, row(ffn2_norm[i]), full["ffn2_w1"][i], full["ffn2_w3"][i], full["ffn2_w2"][i])
        s["h3"] = h
        s["p"] = p[i].reshape(t, -1)
        h, s["npl"] = _ple_fwd(h, row(ple_norm[i]), full["ple_gate"][i], s["p"], full["ple_proj"][i])
        saved.append(s)

    loss_part, dh, g_final = _loss_head(h, row(final_norm), loss_target.reshape(t, d))
    loss = lax.psum(loss_part[0, 0], ("x", "y", "c"))

    grads = {n: [None] * wts[n].shape[0] for n in WEIGHTS if n != "final_norm"}
    grads["final_norm"] = g_final.reshape(-1)

    def ffn_bwd(dh, h_in, n, a, b, which, i):
        w1, w3, w2 = (full[f"{which}_{m}"][i] for m in ("w1", "w3", "w2"))
        da, db, sact = _ffn_bwd_down(dh, a, b, w2)
        grads[f"{which}_w2"][i] = _wgrad(sact, dh, f"{which}_w2_grad", 0.5)
        dh_in, dg = _lin_bwd_norm([da, db], [w1, w3], h_in, row(wts[f"{which}_norm"][i]), dh, f"{which}_bwd_up")
        grads[f"{which}_w1"][i] = _wgrad(n, da, f"{which}_w1_grad")
        grads[f"{which}_w3"][i] = _wgrad(n, db, f"{which}_w3_grad")
        grads[f"{which}_norm"][i] = dg.reshape(-1)
        return dh_in

    for i in reversed(range(depth)):
        s = saved[i]
        j = i // 2
        dh, dgl, dpp, dg = _ple_bwd(dh, s["h3"], row(ple_norm[i]), s["npl"], s["p"], full["ple_gate"][i], full["ple_proj"][i])
        grads["ple_norm"][i] = dg.reshape(-1)
        grads["ple_gate"][i] = _wgrad(s["npl"], dgl, "ple_gate_grad")
        grads["ple_proj"][i] = _wgrad(s["p"], dpp, "ple_proj_grad")
        dh = ffn_bwd(dh, s["h2"], s["n2"], s["a2"], s["b2"], "ffn2", i)
        if i % 2 == 0:
            wglu = full["s5_w_glu"][j]
            dy, dzz = _s5_out_bwd(dh, s["z"], s["y"], wglu)
            grads["s5_w_glu"][j] = _wgrad(s["z"], dzz, "s5_w_glu_grad")
            du, dbre, dbim, dcre, dcim, dpar, dd = _s5_scan_bwd(s["u"], dy, s["st"], bre, bim, cre, cim, par, row(s5_d[j]), bl, seq, lc)
            cts = [dpar[:, r, :].reshape(n_groups, S5_STATE) for r in range(4)]
            g_are, g_aim, g_ldt = _s5_disc_bwd(a_re, a_im, log_dt, cts)
            grads["s5_a_re"][j], grads["s5_a_im"][j], grads["s5_log_dt"][j] = g_are, g_aim, g_ldt.reshape(-1)
            take = functools.partial(_blockdiag_take, gl=S5_BLOCK_GROUPS)
            grads["s5_b_re"][j] = take(dbre, r=S5_GROUP, c=S5_STATE).transpose(0, 2, 1)
            grads["s5_b_im"][j] = take(dbim, r=S5_GROUP, c=S5_STATE).transpose(0, 2, 1)
            grads["s5_c_re"][j] = take(dcre, r=S5_STATE, c=S5_GROUP).transpose(0, 2, 1)
            grads["s5_c_im"][j] = take(dcim, r=S5_STATE, c=S5_GROUP).transpose(0, 2, 1)
            grads["s5_d"][j] = dd.reshape(-1)
            w_in = full["s5_w_in"][j]
            dh, dg = _lin_bwd_norm([du], [w_in], s["h1"], row(mix_norm[i]), dh, "s5_in_bwd")
            grads["s5_w_in"][j] = _wgrad(s["hn"], du, "s5_w_in_grad")
        else:
            w_o = full["sb_w_o"][j]
            do = _lin_nt(dh, w_o, "sb_out_bwd")
            grads["sb_w_o"][j] = _wgrad(s["o"], dh, "sb_w_o_grad")
            dq, dk, dv = _sb_bwd(s["q"], s["k"], s["v"], _to_heads(do, bl, seq), s["tot"])
            dq, dk, dv = (_from_heads(a) for a in (dq, dk, dv))
            wq, wk, wv = jnp.split(full["sb_w_qkv"][j], 3, axis=1)
            dh, dg = _lin_bwd_norm([dq, dk, dv], [wq, wk, wv], s["h1"], row(mix_norm[i]), dh, "sb_in_bwd")
            grads["sb_w_qkv"][j] = jnp.concatenate(
                [_wgrad(s["hn"], a, f"sb_w_{m}_grad") for a, m in ((dq, "q"), (dk, "k"), (dv, "v"))], axis=1)
        grads["mix_norm"][i] = dg.reshape(-1)
        dh = ffn_bwd(dh, s["h0"], s["n1"], s["a1"], s["b1"], "ffn1", i)
    grad_x = dh.reshape(x.shape)
    for n in WEIGHTS:
        if n != "final_norm":
            grads[n] = jnp.stack(grads[n])

    by_dev = {n: (_split_cols if n in COL_SHARDED else _split_rows)(grads[n]) for n in SHARDED}

    def for_core(core):
        slabs = []
        for chip in range(N_CHIP):
            slabs.append(_pack([lax.dynamic_index_in_dim(by_dev[n].reshape((N_CHIP, 2) + by_dev[n].shape[1:])[chip], core, 0, False)
                                for n in SHARDED]))
        return jnp.stack(slabs)

    from_sibling = _swap_sibling(for_core(1 - my_c))
    parts = _exchange_chips(_add(for_core(my_c), from_sibling))
    outs = _adamw(parts, _pack([wts[n] for n in SHARDED]), _pack([mom[n] for n in SHARDED]),
                  _pack([var[n] for n in SHARDED]), "adamw_sharded")
    res = {n: vals for n, vals in zip(SHARDED, zip(*[_unpack(o, shard_shapes) for o in outs]))}

    rep_shapes = [wts[n].shape for n in REPLICATED]
    rep_parts = _all_gather(_pack([grads[n].reshape(wts[n].shape) for n in REPLICATED]), "gather_small_grads")
    outs = _adamw(rep_parts, _pack([wts[n] for n in REPLICATED]), _pack([mom[n] for n in REPLICATED]),
                  _pack([var[n] for n in REPLICATED]), "adamw_replicated")
    res.update({n: vals for n, vals in zip(REPLICATED, zip(*[_unpack(o, rep_shapes) for o in outs]))})

    return (loss, grad_x, *[res[n][0] for n in WEIGHTS], *[res[n][1] for n in WEIGHTS],
            *[res[n][2] for n in WEIGHTS], *[res[n][3] for n in WEIGHTS])
```

```python
import functools
import math

import jax
import jax.numpy as jnp
from jax import lax
from jax.experimental import pallas as pl
from jax.experimental.pallas import tpu as pltpu

F32 = jnp.float32
BF16 = jnp.bfloat16
MESH = pl.DeviceIdType.MESH

N_DEV = 8
N_CHIP = 4
EPS = 1e-6
S5_GROUP = 16
S5_STATE = 64
S5_BLOCK_GROUPS = 16
HEAD_DIM = 64
Q_BLOCK = 128
SB_Q_TILE = 256
SB_HEADS_PER_STEP = 4
SCAN_CHUNK = 128
SUBLANES = 8
FLAT_W = 1024
ADAM_LR, ADAM_B1, ADAM_B2, ADAM_EPS, ADAM_WD, ADAM_STEP = 1e-3, 0.9, 0.999, 1e-8, 0.01, 10
V7X_VMEM_LIMIT = 56 * 1024 * 1024
ROW_TILE = 256
WGRAD_ROW_TILE = 512
WGRAD_OUT_BYTES = 6 * 1024 * 1024

SHARDED = ("ffn1_w1", "ffn1_w3", "ffn1_w2", "ffn2_w1", "ffn2_w3", "ffn2_w2", "ple_proj", "ple_gate",
           "s5_w_in", "s5_w_glu", "sb_w_qkv", "sb_w_o")
COL_SHARDED = ("ffn1_w1", "ffn1_w3", "ffn2_w1", "ffn2_w3", "ple_proj", "s5_w_glu", "sb_w_qkv")
REPLICATED = ("ffn1_norm", "mix_norm", "ffn2_norm", "ple_norm", "s5_a_re", "s5_a_im", "s5_log_dt",
              "s5_b_re", "s5_b_im", "s5_c_re", "s5_c_im", "s5_d", "final_norm")
WEIGHTS = ("ffn1_norm", "ffn1_w1", "ffn1_w3", "ffn1_w2", "mix_norm", "ffn2_norm", "ffn2_w1", "ffn2_w3", "ffn2_w2",
           "ple_norm", "ple_proj", "ple_gate", "s5_w_in", "s5_a_re", "s5_a_im", "s5_log_dt", "s5_b_re", "s5_b_im",
           "s5_c_re", "s5_c_im", "s5_d", "s5_w_glu", "sb_w_qkv", "sb_w_o", "final_norm")


def _dot(a, b):
    return jnp.dot(a, b, preferred_element_type=F32)


def _dot_nt(a, b):
    return lax.dot_general(a, b, (((1,), (1,)), ((), ())), preferred_element_type=F32)


def _dot_tn(a, b):
    return lax.dot_general(a, b, (((0,), (0,)), ((), ())), preferred_element_type=F32)


def _dot2(x, m):
    hi = x.astype(BF16)
    lo = (x - hi.astype(F32)).astype(BF16)
    return _dot(hi, m) + _dot(lo, m)


def _rms(x):
    r = lax.rsqrt(jnp.mean(x * x, axis=-1, keepdims=True) + EPS)
    return x * r, r


def _rms_bwd(dn, xh, r, g):
    gy = dn * g
    return r * (gy - xh * jnp.mean(gy * xh, axis=-1, keepdims=True))


_GELU_C = math.sqrt(2.0 / math.pi)


def _gelu(x):
    return 0.5 * x * (1.0 + jnp.tanh(_GELU_C * (x + 0.044715 * x * x * x)))


def _gelu_grad(x):
    th = jnp.tanh(_GELU_C * (x + 0.044715 * x * x * x))
    return 0.5 * (1.0 + th) + 0.5 * x * (1.0 - th * th) * _GELU_C * (1.0 + 3.0 * 0.044715 * x * x)


def _neg_softplus(z):
    return -(jnp.maximum(z, 0.0) + jnp.log(1.0 + jnp.exp(-jnp.abs(z))))


def _tile(n, want, mult=SUBLANES):
    for t in range(min(want, n), 0, -1):
        if n % t == 0 and t % mult == 0:
            return t
    return n


def _rows(tm, c):
    return pl.BlockSpec((tm, c), lambda i: (i, 0))


def _whole(a):
    nd = a.ndim
    return pl.BlockSpec(a.shape, lambda i: (0,) * nd)


def _row_params(sem="parallel"):
    return pltpu.CompilerParams(dimension_semantics=(sem,), vmem_limit_bytes=V7X_VMEM_LIMIT)


def _ffn_fwd(h, g, w1, w3, w2):
    t, d = h.shape
    f = w1.shape[1]
    tm = _tile(t, ROW_TILE)

    def body(h_ref, g_ref, w1_ref, w3_ref, w2_ref, ho_ref, n_ref, a_ref, b_ref):
        x = h_ref[...]
        xh, _ = _rms(x)
        n = (xh * g_ref[...]).astype(BF16)
        a = _dot(n, w1_ref[...])
        b = _dot(n, w3_ref[...])
        s = (a * jax.nn.sigmoid(a) * b).astype(BF16)
        ho_ref[...] = x + 0.5 * _dot(s, w2_ref[...])
        n_ref[...] = n
        a_ref[...] = a.astype(BF16)
        b_ref[...] = b.astype(BF16)

    return pl.pallas_call(
        body, name="ffn_fwd", grid=(t // tm,),
        in_specs=[_rows(tm, d), _whole(g), _whole(w1), _whole(w3), _whole(w2)],
        out_specs=[_rows(tm, d), _rows(tm, d), _rows(tm, f), _rows(tm, f)],
        out_shape=[jax.ShapeDtypeStruct((t, d), F32), jax.ShapeDtypeStruct((t, d), BF16),
                   jax.ShapeDtypeStruct((t, f), BF16), jax.ShapeDtypeStruct((t, f), BF16)],
        compiler_params=_row_params(),
    )(h, g, w1, w3, w2)


def _ffn_bwd_down(dh, a, b, w2):
    t, d = dh.shape
    f = a.shape[1]
    tm = _tile(t, ROW_TILE)

    def body(dh_ref, a_ref, b_ref, w2_ref, da_ref, db_ref, s_ref):
        ds = _dot_nt((0.5 * dh_ref[...]).astype(BF16), w2_ref[...])
        a32 = a_ref[...].astype(F32)
        b32 = b_ref[...].astype(F32)
        sig = jax.nn.sigmoid(a32)
        sil = a32 * sig
        da_ref[...] = (ds * b32 * (sig * (1.0 + a32 * (1.0 - sig)))).astype(BF16)
        db_ref[...] = (ds * sil).astype(BF16)
        s_ref[...] = (sil * b32).astype(BF16)

    return pl.pallas_call(
        body, name="ffn_bwd_down", grid=(t // tm,),
        in_specs=[_rows(tm, d), _rows(tm, f), _rows(tm, f), _whole(w2)],
        out_specs=[_rows(tm, f)] * 3,
        out_shape=[jax.ShapeDtypeStruct((t, f), BF16)] * 3,
        compiler_params=_row_params(),
    )(dh, a, b, w2)


def _lin_bwd_norm(dys, ws, h, g, dh, name):
    t, d = h.shape
    tm = _tile(t, ROW_TILE)
    k = len(dys)

    def body(*refs):
        dy_refs, w_refs = refs[:k], refs[k:2 * k]
        h_ref, g_ref, dh_ref, o_ref, dg_ref = refs[2 * k:]
        dn = _dot_nt(dy_refs[0][...].astype(BF16), w_refs[0][...])
        for j in range(1, k):
            dn = dn + _dot_nt(dy_refs[j][...].astype(BF16), w_refs[j][...])
        xh, r = _rms(h_ref[...])
        o_ref[...] = dh_ref[...] + _rms_bwd(dn, xh, r, g_ref[...])

        @pl.when(pl.program_id(0) == 0)
        def _():
            dg_ref[...] = jnp.zeros_like(dg_ref)

        dg_ref[...] += jnp.sum(dn * xh, axis=0, keepdims=True)

    return pl.pallas_call(
        body, name=name, grid=(t // tm,),
        in_specs=[_rows(tm, dy.shape[1]) for dy in dys] + [_whole(w) for w in ws] + [_rows(tm, d), _whole(g), _rows(tm, d)],
        out_specs=[_rows(tm, d), pl.BlockSpec((1, d), lambda i: (0, 0))],
        out_shape=[jax.ShapeDtypeStruct((t, d), F32), jax.ShapeDtypeStruct((1, d), F32)],
        compiler_params=_row_params("arbitrary"),
    )(*dys, *ws, h, g, dh)


def _wgrad(x, dy, name, scale=1.0):
    t, kk = x.shape
    n = dy.shape[1]
    tm = _tile(t, WGRAD_ROW_TILE)
    tn = _tile(n, max(128, WGRAD_OUT_BYTES // (4 * kk)), 128)
    steps = t // tm

    def body(x_ref, dy_ref, o_ref):
        i = pl.program_id(1)

        @pl.when(i == 0)
        def _():
            o_ref[...] = jnp.zeros_like(o_ref)

        o_ref[...] += _dot_tn(x_ref[...].astype(BF16), dy_ref[...].astype(BF16))
        if scale != 1.0:
            @pl.when(i == steps - 1)
            def _():
                o_ref[...] = o_ref[...] * scale

    return pl.pallas_call(
        body, name=name, grid=(n // tn, steps),
        in_specs=[pl.BlockSpec((tm, kk), lambda j, i: (i, 0)), pl.BlockSpec((tm, tn), lambda j, i: (i, j))],
        out_specs=pl.BlockSpec((kk, tn), lambda j, i: (0, j)),
        out_shape=jax.ShapeDtypeStruct((kk, n), F32),
        compiler_params=pltpu.CompilerParams(dimension_semantics=("parallel", "arbitrary"), vmem_limit_bytes=V7X_VMEM_LIMIT),
    )(x, dy)


def _norm_lin(h, g, ws, out_dtypes, name):
    t, d = h.shape
    tm = _tile(t, ROW_TILE)
    k = len(ws)

    def body(*refs):
        h_ref, g_ref = refs[:2]
        w_refs = refs[2:2 + k]
        n_ref = refs[2 + k]
        o_refs = refs[3 + k:]
        xh, _ = _rms(h_ref[...])
        n = (xh * g_ref[...]).astype(BF16)
        n_ref[...] = n
        for w_ref, o_ref in zip(w_refs, o_refs):
            o_ref[...] = _dot(n, w_ref[...]).astype(o_ref.dtype)

    return pl.pallas_call(
        body, name=name, grid=(t // tm,),
        in_specs=[_rows(tm, d), _whole(g)] + [_whole(w) for w in ws],
        out_specs=[_rows(tm, d)] + [_rows(tm, w.shape[1]) for w in ws],
        out_shape=[jax.ShapeDtypeStruct((t, d), BF16)] + [jax.ShapeDtypeStruct((t, w.shape[1]), dt) for w, dt in zip(ws, out_dtypes)],
        compiler_params=_row_params(),
    )(h, g, *ws)


def _lin_res(h, x, w, name):
    t, d = h.shape
    tm = _tile(t, ROW_TILE)

    def body(h_ref, x_ref, w_ref, o_ref):
        o_ref[...] = h_ref[...] + _dot(x_ref[...], w_ref[...])

    return pl.pallas_call(
        body, name=name, grid=(t // tm,),
        in_specs=[_rows(tm, d), _rows(tm, x.shape[1]), _whole(w)],
        out_specs=_rows(tm, d), out_shape=jax.ShapeDtypeStruct((t, d), F32),
        compiler_params=_row_params(),
    )(h, x, w)


def _lin_nt(dy, w, name):
    t = dy.shape[0]
    kk = w.shape[0]
    tm = _tile(t, ROW_TILE)

    def body(dy_ref, w_ref, o_ref):
        o_ref[...] = _dot_nt(dy_ref[...].astype(BF16), w_ref[...]).astype(BF16)

    return pl.pallas_call(
        body, name=name, grid=(t // tm,),
        in_specs=[_rows(tm, dy.shape[1]), _whole(w)],
        out_specs=_rows(tm, kk), out_shape=jax.ShapeDtypeStruct((t, kk), BF16),
        compiler_params=_row_params(),
    )(dy, w)


def _ple_fwd(h, g, wg, p, wp):
    t, d = h.shape
    tm = _tile(t, ROW_TILE)

    def body(h_ref, g_ref, wg_ref, p_ref, wp_ref, o_ref, n_ref):
        x = h_ref[...]
        xh, _ = _rms(x)
        n = (xh * g_ref[...]).astype(BF16)
        n_ref[...] = n
        gate = jax.nn.sigmoid(_dot(n, wg_ref[...]))
        o_ref[...] = x + _dot(p_ref[...].astype(BF16), wp_ref[...]) * gate

    return pl.pallas_call(
        body, name="ple_fwd", grid=(t // tm,),
        in_specs=[_rows(tm, d), _whole(g), _whole(wg), _rows(tm, p.shape[1]), _whole(wp)],
        out_specs=[_rows(tm, d), _rows(tm, d)],
        out_shape=[jax.ShapeDtypeStruct((t, d), F32), jax.ShapeDtypeStruct((t, d), BF16)],
        compiler_params=_row_params(),
    )(h, g, wg, p, wp)


def _ple_bwd(dh, h, g, n, p, wg, wp):
    t, d = h.shape
    tm = _tile(t, ROW_TILE)

    def body(dh_ref, h_ref, g_ref, n_ref, p_ref, wg_ref, wp_ref, o_ref, dgl_ref, dpp_ref, dg_ref):
        dh_v = dh_ref[...]
        gate = jax.nn.sigmoid(_dot(n_ref[...], wg_ref[...]))
        pp = _dot(p_ref[...].astype(BF16), wp_ref[...])
        dgl = (dh_v * pp * gate * (1.0 - gate)).astype(BF16)
        dgl_ref[...] = dgl
        dpp_ref[...] = (dh_v * gate).astype(BF16)
        dn = _dot_nt(dgl, wg_ref[...])
        xh, r = _rms(h_ref[...])
        o_ref[...] = dh_v + _rms_bwd(dn, xh, r, g_ref[...])

        @pl.when(pl.program_id(0) == 0)
        def _():
            dg_ref[...] = jnp.zeros_like(dg_ref)

        dg_ref[...] += jnp.sum(dn * xh, axis=0, keepdims=True)

    return pl.pallas_call(
        body, name="ple_bwd", grid=(t // tm,),
        in_specs=[_rows(tm, d), _rows(tm, d), _whole(g), _rows(tm, d), _rows(tm, p.shape[1]), _whole(wg), _whole(wp)],
        out_specs=[_rows(tm, d), _rows(tm, d), _rows(tm, d), pl.BlockSpec((1, d), lambda i: (0, 0))],
        out_shape=[jax.ShapeDtypeStruct((t, d), F32), jax.ShapeDtypeStruct((t, d), BF16),
                   jax.ShapeDtypeStruct((t, d), BF16), jax.ShapeDtypeStruct((1, d), F32)],
        compiler_params=_row_params("arbitrary"),
    )(dh, h, g, n, p, wg, wp)


def _loss_head(h, g, tgt):
    t, d = h.shape
    tm = _tile(t, ROW_TILE)

    def body(h_ref, g_ref, t_ref, l_ref, dh_ref, dg_ref):
        xh, r = _rms(h_ref[...])
        gg = g_ref[...]
        e = xh * gg - t_ref[...]
        dy = e * (1.0 / d)

        @pl.when(pl.program_id(0) == 0)
        def _():
            l_ref[...] = jnp.zeros_like(l_ref)
            dg_ref[...] = jnp.zeros_like(dg_ref)

        l_ref[...] += 0.5 * jnp.sum(jnp.mean(e * e, axis=-1, keepdims=True), axis=0, keepdims=True)
        dg_ref[...] += jnp.sum(dy * xh, axis=0, keepdims=True)
        dh_ref[...] = _rms_bwd(dy, xh, r, gg)

    return pl.pallas_call(
        body, name="loss_head", grid=(t // tm,),
        in_specs=[_rows(tm, d), _whole(g), _rows(tm, d)],
        out_specs=[pl.BlockSpec((1, 128), lambda i: (0, 0)), _rows(tm, d), pl.BlockSpec((1, d), lambda i: (0, 0))],
        out_shape=[jax.ShapeDtypeStruct((1, 128), F32), jax.ShapeDtypeStruct((t, d), F32), jax.ShapeDtypeStruct((1, d), F32)],
        compiler_params=_row_params("arbitrary"),
    )(h, g, tgt)


def _s5_disc_math(a_re, a_im, log_dt):
    lam_re = jnp.minimum(a_re, -1e-4)
    lam_im = a_im
    dt = jnp.exp(log_dt)
    mag = jnp.exp(lam_re * dt)
    abar_re = mag * jnp.cos(lam_im * dt)
    abar_im = mag * jnp.sin(lam_im * dt)
    den = lam_re * lam_re + lam_im * lam_im
    nr = abar_re - 1.0
    ni = abar_im
    return abar_re, abar_im, (nr * lam_re + ni * lam_im) / den, (ni * lam_re - nr * lam_im) / den


def _s5_disc(a_re, a_im, log_dt):
    gp = jax.ShapeDtypeStruct(a_re.shape, F32)

    def body(ar_ref, ai_ref, ld_ref, o0, o1, o2, o3):
        outs = _s5_disc_math(ar_ref[...], ai_ref[...], ld_ref[...])
        for o_ref, val in zip((o0, o1, o2, o3), outs):
            o_ref[...] = val

    return pl.pallas_call(body, name="s5_disc", out_shape=[gp] * 4)(a_re, a_im, log_dt)


def _s5_disc_bwd(a_re, a_im, log_dt, cts):
    def body(ar_ref, ai_ref, ld_ref, c0, c1, c2, c3, dar_ref, dai_ref, dld_ref):
        _, vjp = jax.vjp(_s5_disc_math, ar_ref[...], ai_ref[...], ld_ref[...])
        dar, dai, dld = vjp((c0[...], c1[...], c2[...], c3[...]))
        dar_ref[...] = dar
        dai_ref[...] = dai
        dld_ref[...] = dld

    return pl.pallas_call(
        body, name="s5_disc_bwd",
        out_shape=[jax.ShapeDtypeStruct(a_re.shape, F32), jax.ShapeDtypeStruct(a_im.shape, F32),
                   jax.ShapeDtypeStruct(log_dt.shape, F32)],
    )(a_re, a_im, log_dt, *cts)


def _blockdiag(w, gl):
    g, r, c = w.shape
    w = w.reshape(g // gl, gl, r, c)
    eye = jnp.eye(gl, dtype=w.dtype)
    return (w[:, :, :, None, :] * eye[None, :, None, :, None]).reshape(g // gl, gl * r, gl * c)


def _blockdiag_take(m, gl, r, c):
    nb = m.shape[0]
    m = m.reshape(nb, gl, r, gl, c)
    return jnp.stack([m[:, g, :, g, :] for g in range(gl)], axis=1).reshape(nb * gl, r, c)


def _scan_rows(lc, step, carry, reverse=False):
    def blk(i, cs):
        i = (lc // SUBLANES - 1 - i) if reverse else i
        return step(pl.multiple_of(i * SUBLANES, SUBLANES), cs)

    return lax.fori_loop(0, lc // SUBLANES, blk, carry)


def _s5_scan_fwd(u, bre, bim, cre, cim, par, dskip, bl, seq, lc):
    t, d = u.shape
    nb, gw, ns = bre.shape
    nc = seq // lc

    def body(u_ref, bre_ref, bim_ref, cre_ref, cim_ref, par_ref, d_ref, y_ref, z_ref, st_ref, carry, sre, sim):
        @pl.when(pl.program_id(2) == 0)
        def _():
            carry[...] = jnp.zeros_like(carry)

        st_ref[...] = carry[...]
        uu = u_ref[...]
        ug = uu.astype(BF16)
        wre = _dot(ug, bre_ref[...])
        wim = _dot(ug, bim_ref[...])
        ar, ai = par_ref[0:1, :], par_ref[1:2, :]
        fr, fi = par_ref[2:3, :], par_ref[3:4, :]
        sre[...] = fr * wre - fi * wim
        sim[...] = fr * wim + fi * wre

        def step(base, cs):
            cr, ci = cs
            tr = sre[pl.ds(base, SUBLANES), :]
            ti = sim[pl.ds(base, SUBLANES), :]
            rows = lax.broadcasted_iota(jnp.int32, tr.shape, 0)
            outr, outi = tr, ti
            for k in range(SUBLANES):
                nr = ar * cr - ai * ci + tr[k:k + 1, :]
                ni = ar * ci + ai * cr + ti[k:k + 1, :]
                outr = jnp.where(rows == k, nr, outr)
                outi = jnp.where(rows == k, ni, outi)
                cr, ci = nr, ni
            sre[pl.ds(base, SUBLANES), :] = outr
            sim[pl.ds(base, SUBLANES), :] = outi
            return cr, ci

        cr, ci = _scan_rows(lc, step, (carry[0:1, :], carry[1:2, :]))
        carry[0:1, :] = cr
        carry[1:2, :] = ci
        y = _dot(sre[...].astype(BF16), cre_ref[...]) - _dot(sim[...].astype(BF16), cim_ref[...]) + d_ref[...] * uu
        y_ref[...] = y
        z_ref[...] = _gelu(y).astype(BF16)

    tok = pl.BlockSpec((lc, gw), lambda g, b, c: (b * nc + c, g))
    mat_b = pl.BlockSpec((None, gw, ns), lambda g, b, c: (g, 0, 0))
    mat_c = pl.BlockSpec((None, ns, gw), lambda g, b, c: (g, 0, 0))
    return pl.pallas_call(
        body, name="s5_scan_fwd", grid=(nb, bl, nc),
        in_specs=[tok, mat_b, mat_b, mat_c, mat_c, pl.BlockSpec((None, 8, ns), lambda g, b, c: (g, 0, 0)),
                  pl.BlockSpec((1, gw), lambda g, b, c: (0, g))],
        out_specs=[tok, tok, pl.BlockSpec((None, None, 2, ns), lambda g, b, c: (g, b * nc + c, 0, 0))],
        out_shape=[jax.ShapeDtypeStruct((t, d), F32), jax.ShapeDtypeStruct((t, d), BF16),
                   jax.ShapeDtypeStruct((nb, bl * nc, 2, ns), F32)],
        scratch_shapes=[pltpu.VMEM((2, ns), F32), pltpu.VMEM((lc, ns), F32), pltpu.VMEM((lc, ns), F32)],
        compiler_params=pltpu.CompilerParams(dimension_semantics=("parallel", "arbitrary", "arbitrary"),
                                             vmem_limit_bytes=V7X_VMEM_LIMIT),
    )(u, bre, bim, cre, cim, par, dskip)


def _s5_scan_bwd(u, dy, st, bre, bim, cre, cim, par, dskip, bl, seq, lc):
    t, d = u.shape
    nb, gw, ns = bre.shape
    nc = seq // lc

    def body(u_ref, dy_ref, st_ref, bre_ref, bim_ref, cre_ref, cim_ref, par_ref, d_ref,
             du_ref, dbre_ref, dbim_ref, dcre_ref, dcim_ref, dpar_ref, dd_ref,
             lcarry, sre, sim, pre, pim, wre_s, wim_s, lre, lim):
        b, c = pl.program_id(1), pl.program_id(2)

        @pl.when((b == 0) & (c == 0))
        def _():
            for ref in (dbre_ref, dbim_ref, dcre_ref, dcim_ref, dpar_ref, dd_ref):
                ref[...] = jnp.zeros_like(ref)

        @pl.when(c == 0)
        def _():
            lcarry[...] = jnp.zeros_like(lcarry)

        uu = u_ref[...]
        ug = uu.astype(BF16)
        dyv = dy_ref[...]
        dyb = dyv.astype(BF16)
        ar, ai = par_ref[0:1, :], par_ref[1:2, :]
        fr, fi = par_ref[2:3, :], par_ref[3:4, :]
        wre = _dot(ug, bre_ref[...])
        wim = _dot(ug, bim_ref[...])
        wre_s[...] = wre
        wim_s[...] = wim
        sre[...] = fr * wre - fi * wim
        sim[...] = fr * wim + fi * wre

        def fstep(base, cs):
            cr, ci = cs
            tr = sre[pl.ds(base, SUBLANES), :]
            ti = sim[pl.ds(base, SUBLANES), :]
            rows = lax.broadcasted_iota(jnp.int32, tr.shape, 0)
            outr, outi, prr, pri = tr, ti, tr, ti
            for k in range(SUBLANES):
                prr = jnp.where(rows == k, cr, prr)
                pri = jnp.where(rows == k, ci, pri)
                nr = ar * cr - ai * ci + tr[k:k + 1, :]
                ni = ar * ci + ai * cr + ti[k:k + 1, :]
                outr = jnp.where(rows == k, nr, outr)
                outi = jnp.where(rows == k, ni, outi)
                cr, ci = nr, ni
            sre[pl.ds(base, SUBLANES), :] = outr
            sim[pl.ds(base, SUBLANES), :] = outi
            pre[pl.ds(base, SUBLANES), :] = prr
            pim[pl.ds(base, SUBLANES), :] = pri
            return cr, ci

        _scan_rows(lc, fstep, (st_ref[0:1, :], st_ref[1:2, :]))

        lre[...] = _dot_nt(dyb, cre_ref[...])
        lim[...] = -_dot_nt(dyb, cim_ref[...])

        def bstep(base, cs):
            cr, ci = cs
            tr = lre[pl.ds(base, SUBLANES), :]
            ti = lim[pl.ds(base, SUBLANES), :]
            rows = lax.broadcasted_iota(jnp.int32, tr.shape, 0)
            outr, outi = tr, ti
            for k in range(SUBLANES - 1, -1, -1):
                nr = tr[k:k + 1, :] + ar * cr + ai * ci
                ni = ti[k:k + 1, :] + ar * ci - ai * cr
                outr = jnp.where(rows == k, nr, outr)
                outi = jnp.where(rows == k, ni, outi)
                cr, ci = nr, ni
            lre[pl.ds(base, SUBLANES), :] = outr
            lim[pl.ds(base, SUBLANES), :] = outi
            return cr, ci

        cr, ci = _scan_rows(lc, bstep, (lcarry[0:1, :], lcarry[1:2, :]), reverse=True)
        lcarry[0:1, :] = cr
        lcarry[1:2, :] = ci

        lr, li = lre[...], lim[...]
        spr, spi = pre[...], pim[...]
        wr, wi = wre_s[...], wim_s[...]
        dpar_ref[0:1, :] += jnp.sum(lr * spr + li * spi, axis=0, keepdims=True)
        dpar_ref[1:2, :] += jnp.sum(li * spr - lr * spi, axis=0, keepdims=True)
        dpar_ref[2:3, :] += jnp.sum(lr * wr + li * wi, axis=0, keepdims=True)
        dpar_ref[3:4, :] += jnp.sum(li * wr - lr * wi, axis=0, keepdims=True)
        dwr = (fr * lr + fi * li).astype(BF16)
        dwi = (fr * li - fi * lr).astype(BF16)
        dsk = d_ref[...]
        du_ref[...] = _dot_nt(dwr, bre_ref[...]) + _dot_nt(dwi, bim_ref[...]) + dsk * dyv
        dd_ref[...] += jnp.sum(dyv * uu, axis=0, keepdims=True)
        dbre_ref[...] += _dot_tn(ug, dwr)
        dbim_ref[...] += _dot_tn(ug, dwi)
        dcre_ref[...] += _dot_tn(sre[...].astype(BF16), dyb)
        dcim_ref[...] -= _dot_tn(sim[...].astype(BF16), dyb)

    tok = pl.BlockSpec((lc, gw), lambda g, b, c: (b * nc + nc - 1 - c, g))
    mat_b = pl.BlockSpec((None, gw, ns), lambda g, b, c: (g, 0, 0))
    mat_c = pl.BlockSpec((None, ns, gw), lambda g, b, c: (g, 0, 0))
    rows8 = pl.BlockSpec((None, 8, ns), lambda g, b, c: (g, 0, 0))
    dvec = pl.BlockSpec((1, gw), lambda g, b, c: (0, g))
    tile = pltpu.VMEM((lc, ns), F32)
    return pl.pallas_call(
        body, name="s5_scan_bwd", grid=(nb, bl, nc),
        in_specs=[tok, tok, pl.BlockSpec((None, None, 2, ns), lambda g, b, c: (g, b * nc + nc - 1 - c, 0, 0)),
                  mat_b, mat_b, mat_c, mat_c, rows8, dvec],
        out_specs=[tok, mat_b, mat_b, mat_c, mat_c, rows8, dvec],
        out_shape=[jax.ShapeDtypeStruct((t, d), F32),
                   jax.ShapeDtypeStruct((nb, gw, ns), F32), jax.ShapeDtypeStruct((nb, gw, ns), F32),
                   jax.ShapeDtypeStruct((nb, ns, gw), F32), jax.ShapeDtypeStruct((nb, ns, gw), F32),
                   jax.ShapeDtypeStruct((nb, 8, ns), F32), jax.ShapeDtypeStruct((1, d), F32)],
        scratch_shapes=[pltpu.VMEM((2, ns), F32)] + [tile] * 8,
        compiler_params=pltpu.CompilerParams(dimension_semantics=("arbitrary", "arbitrary", "arbitrary"),
                                             vmem_limit_bytes=V7X_VMEM_LIMIT),
    )(u, dy, st, bre, bim, cre, cim, par, dskip)


def _s5_out(h, z, wglu):
    t, d = h.shape
    tm = _tile(t, ROW_TILE)

    def body(h_ref, z_ref, w_ref, o_ref):
        zz = _dot(z_ref[...], w_ref[...])
        o_ref[...] = h_ref[...] + zz[:, :d] * jax.nn.sigmoid(zz[:, d:])

    return pl.pallas_call(
        body, name="s5_out", grid=(t // tm,),
        in_specs=[_rows(tm, d), _rows(tm, d), _whole(wglu)],
        out_specs=_rows(tm, d), out_shape=jax.ShapeDtypeStruct((t, d), F32),
        compiler_params=_row_params(),
    )(h, z, wglu)


def _s5_out_bwd(dh, z, y, wglu):
    t, d = dh.shape
    tm = _tile(t, ROW_TILE)

    def body(dh_ref, z_ref, y_ref, w_ref, dy_ref, dzz_ref):
        zz = _dot(z_ref[...], w_ref[...])
        out, sg = zz[:, :d], jax.nn.sigmoid(zz[:, d:])
        dh_v = dh_ref[...]
        dzz = jnp.concatenate([dh_v * sg, dh_v * out * sg * (1.0 - sg)], axis=1).astype(BF16)
        dzz_ref[...] = dzz
        dy_ref[...] = _dot_nt(dzz, w_ref[...]) * _gelu_grad(y_ref[...])

    return pl.pallas_call(
        body, name="s5_out_bwd", grid=(t // tm,),
        in_specs=[_rows(tm, d), _rows(tm, d), _rows(tm, d), _whole(wglu)],
        out_specs=[_rows(tm, d), _rows(tm, 2 * d)],
        out_shape=[jax.ShapeDtypeStruct((t, d), F32), jax.ShapeDtypeStruct((t, 2 * d), BF16)],
        compiler_params=_row_params(),
    )(dh, z, y, wglu)


def _head_spec(seq, w):
    return pl.BlockSpec((None, SB_HEADS_PER_STEP, seq, w), lambda b, h: (b, h, 0, 0))


def _tri_and_ones(kind):
    row = lax.broadcasted_iota(jnp.int32, (Q_BLOCK, Q_BLOCK), 0)
    col = lax.broadcasted_iota(jnp.int32, (Q_BLOCK, Q_BLOCK), 1)
    tri = {"after": row > col, "upto": row <= col, "before": row < col}[kind]
    return jnp.concatenate([tri.astype(BF16), jnp.ones((Q_BLOCK, Q_BLOCK), BF16)], axis=1)


def _sb_fwd(q, k, v):
    bsz, nh, seq, dh = q.shape
    tq = min(SB_Q_TILE, seq // 2)
    nq, nsub = seq // tq, tq // Q_BLOCK
    scale = dh ** -0.5
    hp = SB_HEADS_PER_STEP
    qb = Q_BLOCK

    def body(q_ref, k_ref, v_ref, o_ref, tot_ref):
        strict = lax.broadcasted_iota(jnp.int32, (qb, qb), 1) < lax.broadcasted_iota(jnp.int32, (qb, qb), 0)
        sums = _tri_and_ones("after")

        def sweep(qs, c0, units, carry):
            heads = sorted({u[0] for u in units})
            kbs = {hh: k_ref[hh, pl.ds(c0, qb), :] for hh in heads}
            vbs = {hh: v_ref[hh, pl.ds(c0, qb), :] for hh in heads}
            zs = [_dot_nt(qs[hh][s], kbs[hh]) * scale for hh, s, _ in units]
            lkrs = [_neg_softplus(z) for z in zs]
            lks = [jnp.where(strict, lkr, 0.0) if dg else lkr for lkr, (_, _, dg) in zip(lkrs, units)]
            css = [_dot2(lk, sums) for lk in lks]
            carry = dict(carry)
            for (hh, s, dg), z, lkr, cs in zip(units, zs, lkrs, css):
                acc, run = carry[hh, s]
                att = jnp.exp(z + lkr + cs[:, :qb] + run)
                if dg:
                    att = jnp.where(strict, att, 0.0)
                carry[hh, s] = (acc + _dot(att.astype(BF16), vbs[hh]), run + cs[:, qb:])
            return carry

        def q_loop(qi, _):
            r0 = qi * tq
            qs = {hh: [q_ref[hh, pl.ds(pl.multiple_of(r0 + s * qb, qb), qb), :] for s in range(nsub)] for hh in range(hp)}
            carry = {(hh, s): (jnp.zeros((qb, dh), F32), jnp.zeros((qb, qb), F32)) for hh in range(hp) for s in range(nsub)}
            for jj in reversed(range(nsub)):
                units = [(hh, s, s == jj) for hh in range(hp) for s in range(jj, nsub)]
                carry = sweep(qs, pl.multiple_of(r0 + jj * qb, qb), units, carry)
            units = [(hh, s, False) for hh in range(hp) for s in range(nsub)]
            carry = lax.fori_loop(
                0, nsub * qi, lambda t, c: sweep(qs, pl.multiple_of((nsub * qi - 1 - t) * qb, qb), units, c), carry)
            for (hh, s), (acc, run) in carry.items():
                rows = pl.ds(pl.multiple_of(r0 + s * qb, qb), qb)
                o_ref[hh, rows, :] = acc.astype(BF16)
                tot_ref[hh, rows, :] = run[:, 0:1]
            return 0

        lax.fori_loop(0, nq, q_loop, 0)

    return pl.pallas_call(
        body, name="sb_fwd", grid=(bsz, nh // hp),
        in_specs=[_head_spec(seq, dh)] * 3,
        out_specs=[_head_spec(seq, dh), _head_spec(seq, 1)],
        out_shape=[jax.ShapeDtypeStruct((bsz, nh, seq, dh), BF16), jax.ShapeDtypeStruct((bsz, nh, seq, 1), F32)],
        compiler_params=pltpu.CompilerParams(dimension_semantics=("parallel", "parallel"), vmem_limit_bytes=V7X_VMEM_LIMIT),
    )(q, k, v)


def _sb_bwd(q, k, v, do, tot):
    bsz, nh, seq, dh = q.shape
    tq = min(SB_Q_TILE, seq // 2)
    nq, nsub = seq // tq, tq // Q_BLOCK
    scale = dh ** -0.5
    hp = SB_HEADS_PER_STEP
    qb = Q_BLOCK

    def body(q_ref, k_ref, v_ref, do_ref, tot_ref, dq_ref, dk_ref, dv_ref, dka, dva):
        dka[...] = jnp.zeros_like(dka)
        dva[...] = jnp.zeros_like(dva)
        strict = lax.broadcasted_iota(jnp.int32, (qb, qb), 1) < lax.broadcasted_iota(jnp.int32, (qb, qb), 0)
        upto = _tri_and_ones("upto")
        before = _tri_and_ones("before")

        def sweep(qf, dof, tots, c0, first, units, carry):
            heads = sorted({u[0] for u in units})
            kbs = {hh: k_ref[hh, pl.ds(c0, qb), :] for hh in heads}
            vbs = {hh: v_ref[hh, pl.ds(c0, qb), :] for hh in heads}
            sub = lambda a, s: a[s * qb:(s + 1) * qb, :]
            zs = [_dot_nt(sub(qf[hh], s), kbs[hh]) * scale for hh, s, _ in units]
            das = [_dot_nt(sub(dof[hh], s), vbs[hh]) for hh, s, _ in units]
            lkrs = [_neg_softplus(z) for z in zs]
            lks = [jnp.where(strict, lkr, 0.0) if dg else lkr for lkr, (_, _, dg) in zip(lkrs, units)]
            css = [_dot2(lk, upto) for lk in lks]
            lsigs, atts, gls = [], [], []
            for (hh, s, dg), z, lkr, cs, da in zip(units, zs, lkrs, css, das):
                lsig = z + lkr
                att = jnp.exp(lsig + (tots[hh, s] - (cs[:, :qb] + carry[hh, s][1])))
                if dg:
                    att = jnp.where(strict, att, 0.0)
                lsigs.append(lsig)
                atts.append(att)
                gls.append(da * att)
            gss = [_dot2(gl, before) for gl in gls]
            carry = dict(carry)
            dzs = {}
            for (hh, s, dg), lsig, gl, cs, gs in zip(units, lsigs, gls, css, gss):
                dqa, pre, gpre = carry[hh, s]
                sig = jnp.exp(lsig)
                dz = gl * (1.0 - sig) - (gs[:, :qb] + gpre) * sig
                if dg:
                    dz = jnp.where(strict, dz, 0.0)
                dz = (dz * scale).astype(BF16)
                dzs[hh, s] = dz
                carry[hh, s] = (dqa + _dot(dz, kbs[hh]), pre + cs[:, qb:], gpre + gs[:, qb:])
            att_of = {(hh, s): a for (hh, s, _), a in zip(units, atts)}
            for hh in heads:
                dzc = jnp.concatenate([dzs[hh, s] for s in range(first, nsub)], axis=0)
                attc = jnp.concatenate([att_of[hh, s].astype(BF16) for s in range(first, nsub)], axis=0)
                dka[hh, pl.ds(c0, qb), :] += _dot_tn(dzc, qf[hh][first * qb:, :])
                dva[hh, pl.ds(c0, qb), :] += _dot_tn(attc, dof[hh][first * qb:, :])
            return carry

        def q_loop(qi, _):
            r0 = pl.multiple_of(qi * tq, tq)
            qf = [q_ref[hh, pl.ds(r0, tq), :] for hh in range(hp)]
            dof = [do_ref[hh, pl.ds(r0, tq), :] for hh in range(hp)]
            tots = {(hh, s): jnp.broadcast_to(tot_ref[hh, pl.ds(pl.multiple_of(r0 + s * qb, qb), qb), :], (qb, qb))
                    for hh in range(hp) for s in range(nsub)}
            carry = {(hh, s): (jnp.zeros((qb, dh), F32), jnp.zeros((qb, qb), F32), jnp.zeros((qb, qb), F32))
                     for hh in range(hp) for s in range(nsub)}
            units = [(hh, s, False) for hh in range(hp) for s in range(nsub)]
            carry = lax.fori_loop(
                0, nsub * qi, lambda kj, c: sweep(qf, dof, tots, pl.multiple_of(kj * qb, qb), 0, units, c), carry)
            for jj in range(nsub):
                units = [(hh, s, s == jj) for hh in range(hp) for s in range(jj, nsub)]
                carry = sweep(qf, dof, tots, pl.multiple_of(r0 + jj * qb, qb), jj, units, carry)
            for (hh, s), (dqa, _, _) in carry.items():
                dq_ref[hh, pl.ds(pl.multiple_of(r0 + s * qb, qb), qb), :] = dqa.astype(BF16)
            return 0

        lax.fori_loop(0, nq, q_loop, 0)
        dk_ref[...] = dka[...].astype(BF16)
        dv_ref[...] = dva[...].astype(BF16)

    hs = _head_spec(seq, dh)
    return pl.pallas_call(
        body, name="sb_bwd", grid=(bsz, nh // hp),
        in_specs=[hs, hs, hs, hs, _head_spec(seq, 1)],
        out_specs=[hs, hs, hs],
        out_shape=[jax.ShapeDtypeStruct((bsz, nh, seq, dh), BF16)] * 3,
        scratch_shapes=[pltpu.VMEM((hp, seq, dh), F32), pltpu.VMEM((hp, seq, dh), F32)],
        compiler_params=pltpu.CompilerParams(dimension_semantics=("parallel", "parallel"), vmem_limit_bytes=V7X_VMEM_LIMIT),
    )(q, k, v, do, tot)


def _to_heads(a, bsz, seq):
    return a.reshape(bsz, seq, -1, HEAD_DIM).transpose(0, 2, 1, 3)


def _from_heads(a):
    bsz, nh, seq, dh = a.shape
    return a.transpose(0, 2, 1, 3).reshape(bsz * seq, nh * dh)


def _coords():
    return lax.axis_index("x"), lax.axis_index("y"), lax.axis_index("c")


def _all_gather(x, name):
    r, c = x.shape

    def body(x_ref, out_ref, send_sems, recv_sems, local_sem):
        mx, my, mc = _coords()
        me, sibling = (mx, my, mc), (mx, my, 1 - mc)
        chips = [(1 - mx, my), (mx, 1 - my), (1 - mx, 1 - my)]

        def blk(px, py, pc):
            return out_ref.at[4 * px + 2 * py + pc]

        def copy(k, block, to, src=None):
            return pltpu.make_async_remote_copy(
                src_ref=blk(*block) if src is None else src, dst_ref=blk(*block),
                send_sem=send_sems.at[k], recv_sem=recv_sems.at[k], device_id=to, device_id_type=MESH)

        mine = pltpu.make_async_copy(x_ref, blk(*me), local_sem)
        mine.start()
        first = [copy(0, me, sibling, src=x_ref)]
        first += [copy(1 + j, me, (*chip, mc), src=x_ref) for j, chip in enumerate(chips)]
        for cp in first:
            cp.start()
        passed = [copy(4 + j, (*chip, mc), sibling) for j, chip in enumerate(chips)]
        for j, chip in enumerate(chips):
            copy(1 + j, (*chip, mc), me).wait_recv()
            passed[j].start()
        copy(0, sibling, me).wait_recv()
        for j, chip in enumerate(chips):
            copy(4 + j, (*chip, 1 - mc), me).wait_recv()
        for cp in first + passed:
            cp.wait_send()
        mine.wait()

    return pl.pallas_call(
        body, name=name,
        out_shape=jax.ShapeDtypeStruct((N_DEV, r, c), x.dtype),
        in_specs=[pl.BlockSpec(memory_space=pl.ANY)],
        out_specs=pl.BlockSpec(memory_space=pl.ANY),
        scratch_shapes=[pltpu.SemaphoreType.DMA((7,)), pltpu.SemaphoreType.DMA((7,)), pltpu.SemaphoreType.DMA],
    )(x)


def _swap_sibling(g):
    def body(g_ref, out_ref, send_sem, recv_sem):
        mx, my, mc = _coords()
        cp = pltpu.make_async_remote_copy(src_ref=g_ref, dst_ref=out_ref, send_sem=send_sem, recv_sem=recv_sem,
                                          device_id=(mx, my, 1 - mc), device_id_type=MESH)
        cp.start()
        cp.wait()

    return pl.pallas_call(
        body, name="rs_sibling", out_shape=jax.ShapeDtypeStruct(g.shape, g.dtype),
        in_specs=[pl.BlockSpec(memory_space=pl.ANY)], out_specs=pl.BlockSpec(memory_space=pl.ANY),
        scratch_shapes=[pltpu.SemaphoreType.DMA, pltpu.SemaphoreType.DMA],
    )(g)


def _add(a, b):
    n, r, c = a.shape
    tr = _tile(r, 512)

    def body(a_ref, b_ref, o_ref):
        o_ref[...] = a_ref[...] + b_ref[...]

    spec = pl.BlockSpec((None, tr, c), lambda i, j: (i, j, 0))
    return pl.pallas_call(
        body, name="rs_add", grid=(n, r // tr), in_specs=[spec, spec], out_specs=spec,
        out_shape=jax.ShapeDtypeStruct(a.shape, a.dtype),
        compiler_params=pltpu.CompilerParams(dimension_semantics=("parallel", "parallel")),
    )(a, b)


def _exchange_chips(p):
    def body(p_ref, out_ref, send_sems, recv_sems, local_sem):
        mx, my, mc = _coords()
        here = 2 * mx + my
        chips = [(1 - mx, my), (mx, 1 - my), (1 - mx, 1 - my)]
        mine = pltpu.make_async_copy(p_ref.at[here], out_ref.at[here], local_sem)
        mine.start()
        copies = [pltpu.make_async_remote_copy(
            src_ref=p_ref.at[2 * cx + cy], dst_ref=out_ref.at[here], send_sem=send_sems.at[j], recv_sem=recv_sems.at[j],
            device_id=(cx, cy, mc), device_id_type=MESH) for j, (cx, cy) in enumerate(chips)]
        for cp in copies:
            cp.start()
        for cp in copies:
            cp.wait()
        mine.wait()

    return pl.pallas_call(
        body, name="rs_chips", out_shape=jax.ShapeDtypeStruct(p.shape, p.dtype),
        in_specs=[pl.BlockSpec(memory_space=pl.ANY)], out_specs=pl.BlockSpec(memory_space=pl.ANY),
        scratch_shapes=[pltpu.SemaphoreType.DMA((3,)), pltpu.SemaphoreType.DMA((3,)), pltpu.SemaphoreType.DMA],
    )(p)


def _adamw(parts, w, m, v, name):
    n, r, c = parts.shape
    tr = _tile(r, 320)
    bc1 = 1.0 - ADAM_B1 ** ADAM_STEP
    bc2 = 1.0 - ADAM_B2 ** ADAM_STEP

    def body(p_ref, w_ref, m_ref, v_ref, g_ref, d_ref, mo_ref, vo_ref):
        g = p_ref[0]
        for j in range(1, n):
            g = g + p_ref[j]
        mn = ADAM_B1 * m_ref[...] + (1.0 - ADAM_B1) * g
        vn = ADAM_B2 * v_ref[...] + (1.0 - ADAM_B2) * (g * g)
        g_ref[...] = g
        mo_ref[...] = mn
        vo_ref[...] = vn
        d_ref[...] = -ADAM_LR * ((mn / bc1) / (jnp.sqrt(vn / bc2) + ADAM_EPS) + ADAM_WD * w_ref[...])

    flat = pl.BlockSpec((tr, c), lambda i: (i, 0))
    return pl.pallas_call(
        body, name=name, grid=(r // tr,),
        in_specs=[pl.BlockSpec((n, tr, c), lambda i: (0, i, 0)), flat, flat, flat],
        out_specs=[flat] * 4, out_shape=[jax.ShapeDtypeStruct((r, c), F32)] * 4,
        compiler_params=pltpu.CompilerParams(dimension_semantics=("parallel",)),
    )(parts, w, m, v)


def _pack(arrs, dtype=F32):
    cols = []
    for a in arrs:
        f = a.reshape(-1).astype(dtype)
        cols.append(jnp.pad(f, (0, -f.shape[0] % FLAT_W)))
    flat = jnp.concatenate(cols)
    flat = jnp.pad(flat, (0, -flat.shape[0] % (FLAT_W * SUBLANES)))
    return flat.reshape(-1, FLAT_W)


def _unpack(flat, shapes, lead=()):
    flat = flat.reshape(lead + (-1,))
    out, off = [], 0
    for s in shapes:
        n = math.prod(s)
        out.append(flat[..., off:off + n].reshape(lead + tuple(s)))
        off += n + (-n % FLAT_W)
    return out


def _split_cols(a):
    n = a.shape[-1] // N_DEV
    return jnp.moveaxis(a.reshape(a.shape[:-1] + (N_DEV, n)), -2, 0)


def _split_rows(a):
    k = a.shape[-2] // N_DEV
    return jnp.moveaxis(a.reshape(a.shape[:-2] + (N_DEV, k, a.shape[-1])), -3, 0)


def _join_cols(a):
    a = jnp.moveaxis(a, 0, -2)
    return a.reshape(a.shape[:-2] + (a.shape[-2] * a.shape[-1],))


def _join_rows(a):
    a = jnp.moveaxis(a, 0, -3)
    return a.reshape(a.shape[:-3] + (a.shape[-3] * a.shape[-2], a.shape[-1]))


def kernel(x, p, ffn1_norm, ffn1_w1, ffn1_w3, ffn1_w2, mix_norm, ffn2_norm, ffn2_w1, ffn2_w3, ffn2_w2, ple_norm, ple_proj, ple_gate, s5_w_in, s5_a_re, s5_a_im, s5_log_dt, s5_b_re, s5_b_im, s5_c_re, s5_c_im, s5_d, s5_w_glu, sb_w_qkv, sb_w_o, final_norm, loss_target, m_ffn1_norm, m_ffn1_w1, m_ffn1_w3, m_ffn1_w2, m_mix_norm, m_ffn2_norm, m_ffn2_w1, m_ffn2_w3, m_ffn2_w2, m_ple_norm, m_ple_proj, m_ple_gate, m_s5_w_in, m_s5_a_re, m_s5_a_im, m_s5_log_dt, m_s5_b_re, m_s5_b_im, m_s5_c_re, m_s5_c_im, m_s5_d, m_s5_w_glu, m_sb_w_qkv, m_sb_w_o, m_final_norm, v_ffn1_norm, v_ffn1_w1, v_ffn1_w3, v_ffn1_w2, v_mix_norm, v_ffn2_norm, v_ffn2_w1, v_ffn2_w3, v_ffn2_w2, v_ple_norm, v_ple_proj, v_ple_gate, v_s5_w_in, v_s5_a_re, v_s5_a_im, v_s5_log_dt, v_s5_b_re, v_s5_b_im, v_s5_c_re, v_s5_c_im, v_s5_d, v_s5_w_glu, v_sb_w_qkv, v_sb_w_o, v_final_norm):
    given = dict(locals())
    wts = {n: given[n] for n in WEIGHTS}
    mom = {n: given["m_" + n] for n in WEIGHTS}
    var = {n: given["v_" + n] for n in WEIGHTS}
    bl, seq, d = x.shape
    t = bl * seq
    depth = p.shape[0]
    my_c = lax.axis_index("c")

    shard_shapes = [wts[n].shape for n in SHARDED]
    gathered = _all_gather(_pack([wts[n] for n in SHARDED], BF16), "gather_weights")
    full = {}
    for n, blocks in zip(SHARDED, _unpack(gathered, shard_shapes, lead=(N_DEV,))):
        full[n] = _join_cols(blocks) if n in COL_SHARDED else _join_rows(blocks)

    def row(a):
        return a.reshape(1, -1)

    n_groups = d // S5_GROUP
    a_re, a_im = s5_a_re[0], s5_a_im[0]
    log_dt = s5_log_dt[0].reshape(n_groups, 1)
    disc = _s5_disc(a_re, a_im, log_dt)
    nb = n_groups // S5_BLOCK_GROUPS
    ns = S5_BLOCK_GROUPS * S5_STATE
    par = jnp.concatenate([jnp.stack([q.reshape(nb, ns) for q in disc], axis=1), jnp.zeros((nb, 4, ns), F32)], axis=1)
    bre = _blockdiag(s5_b_re[0].transpose(0, 2, 1), S5_BLOCK_GROUPS).astype(BF16)
    bim = _blockdiag(s5_b_im[0].transpose(0, 2, 1), S5_BLOCK_GROUPS).astype(BF16)
    cre = _blockdiag(s5_c_re[0].transpose(0, 2, 1), S5_BLOCK_GROUPS).astype(BF16)
    cim = _blockdiag(s5_c_im[0].transpose(0, 2, 1), S5_BLOCK_GROUPS).astype(BF16)
    lc = min(SCAN_CHUNK, seq)

    h = x.reshape(t, d)
    saved = []
    for i in range(depth):
        s = {"h0": h}
        h, s["n1"], s["a1"], s["b1"] = _ffn_fwd(h, row(ffn1_norm[i]), full["ffn1_w1"][i], full["ffn1_w3"][i], full["ffn1_w2"][i])
        s["h1"] = h
        j = i // 2
        if i % 2 == 0:
            s["hn"], s["u"] = _norm_lin(h, row(mix_norm[i]), [full["s5_w_in"][j]], [F32], "s5_in")
            s["y"], s["z"], s["st"] = _s5_scan_fwd(s["u"], bre, bim, cre, cim, par, row(s5_d[j]), bl, seq, lc)
            h = _s5_out(h, s["z"], full["s5_w_glu"][j])
        else:
            wq, wk, wv = jnp.split(full["sb_w_qkv"][j], 3, axis=1)
            s["hn"], q, k, v = _norm_lin(h, row(mix_norm[i]), [wq, wk, wv], [BF16] * 3, "sb_in")
            s["q"], s["k"], s["v"] = (_to_heads(a, bl, seq) for a in (q, k, v))
            o, s["tot"] = _sb_fwd(s["q"], s["k"], s["v"])
            s["o"] = _from_heads(o)
            h = _lin_res(h, s["o"], full["sb_w_o"][j], "sb_out")
        s["h2"] = h
        h, s["n2"], s["a2"], s["b2"] = _ffn_fwd(h, row(ffn2_norm[i]), full["ffn2_w1"][i], full["ffn2_w3"][i], full["ffn2_w2"][i])
        s["h3"] = h
        s["p"] = p[i].reshape(t, -1)
        h, s["npl"] = _ple_fwd(h, row(ple_norm[i]), full["ple_gate"][i], s["p"], full["ple_proj"][i])
        saved.append(s)

    loss_part, dh, g_final = _loss_head(h, row(final_norm), loss_target.reshape(t, d))
    loss = lax.psum(loss_part[0, 0], ("x", "y", "c"))

    grads = {n: [None] * wts[n].shape[0] for n in WEIGHTS if n != "final_norm"}
    grads["final_norm"] = g_final.reshape(-1)

    def ffn_bwd(dh, h_in, n, a, b, which, i):
        w1, w3, w2 = (full[f"{which}_{m}"][i] for m in ("w1", "w3", "w2"))
        da, db, sact = _ffn_bwd_down(dh, a, b, w2)
        grads[f"{which}_w2"][i] = _wgrad(sact, dh, f"{which}_w2_grad", 0.5)
        dh_in, dg = _lin_bwd_norm([da, db], [w1, w3], h_in, row(wts[f"{which}_norm"][i]), dh, f"{which}_bwd_up")
        grads[f"{which}_w1"][i] = _wgrad(n, da, f"{which}_w1_grad")
        grads[f"{which}_w3"][i] = _wgrad(n, db, f"{which}_w3_grad")
        grads[f"{which}_norm"][i] = dg.reshape(-1)
        return dh_in

    for i in reversed(range(depth)):
        s = saved[i]
        j = i // 2
        dh, dgl, dpp, dg = _ple_bwd(dh, s["h3"], row(ple_norm[i]), s["npl"], s["p"], full["ple_gate"][i], full["ple_proj"][i])
        grads["ple_norm"][i] = dg.reshape(-1)
        grads["ple_gate"][i] = _wgrad(s["npl"], dgl, "ple_gate_grad")
        grads["ple_proj"][i] = _wgrad(s["p"], dpp, "ple_proj_grad")
        dh = ffn_bwd(dh, s["h2"], s["n2"], s["a2"], s["b2"], "ffn2", i)
        if i % 2 == 0:
            wglu = full["s5_w_glu"][j]
            dy, dzz = _s5_out_bwd(dh, s["z"], s["y"], wglu)
            grads["s5_w_glu"][j] = _wgrad(s["z"], dzz, "s5_w_glu_grad")
            du, dbre, dbim, dcre, dcim, dpar, dd = _s5_scan_bwd(s["u"], dy, s["st"], bre, bim, cre, cim, par, row(s5_d[j]), bl, seq, lc)
            cts = [dpar[:, r, :].reshape(n_groups, S5_STATE) for r in range(4)]
            g_are, g_aim, g_ldt = _s5_disc_bwd(a_re, a_im, log_dt, cts)
            grads["s5_a_re"][j], grads["s5_a_im"][j], grads["s5_log_dt"][j] = g_are, g_aim, g_ldt.reshape(-1)
            take = functools.partial(_blockdiag_take, gl=S5_BLOCK_GROUPS)
            grads["s5_b_re"][j] = take(dbre, r=S5_GROUP, c=S5_STATE).transpose(0, 2, 1)
            grads["s5_b_im"][j] = take(dbim, r=S5_GROUP, c=S5_STATE).transpose(0, 2, 1)
            grads["s5_c_re"][j] = take(dcre, r=S5_STATE, c=S5_GROUP).transpose(0, 2, 1)
            grads["s5_c_im"][j] = take(dcim, r=S5_STATE, c=S5_GROUP).transpose(0, 2, 1)
            grads["s5_d"][j] = dd.reshape(-1)
            w_in = full["s5_w_in"][j]
            dh, dg = _lin_bwd_norm([du], [w_in], s["h1"], row(mix_norm[i]), dh, "s5_in_bwd")
            grads["s5_w_in"][j] = _wgrad(s["hn"], du, "s5_w_in_grad")
        else:
            w_o = full["sb_w_o"][j]
            do = _lin_nt(dh, w_o, "sb_out_bwd")
            grads["sb_w_o"][j] = _wgrad(s["o"], dh, "sb_w_o_grad")
            dq, dk, dv = _sb_bwd(s["q"], s["k"], s["v"], _to_heads(do, bl, seq), s["tot"])
            dq, dk, dv = (_from_heads(a) for a in (dq, dk, dv))
            wq, wk, wv = jnp.split(full["sb_w_qkv"][j], 3, axis=1)
            dh, dg = _lin_bwd_norm([dq, dk, dv], [wq, wk, wv], s["h1"], row(mix_norm[i]), dh, "sb_in_bwd")
            grads["sb_w_qkv"][j] = jnp.concatenate(
                [_wgrad(s["hn"], a, f"sb_w_{m}_grad") for a, m in ((dq, "q"), (dk, "k"), (dv, "v"))], axis=1)
        grads["mix_norm"][i] = dg.reshape(-1)
        dh = ffn_bwd(dh, s["h0"], s["n1"], s["a1"], s["b1"], "ffn1", i)
    grad_x = dh.reshape(x.shape)
    for n in WEIGHTS:
        if n != "final_norm":
            grads[n] = jnp.stack(grads[n])

    by_dev = {n: (_split_cols if n in COL_SHARDED else _split_rows)(grads[n]) for n in SHARDED}

    def for_core(core):
        slabs = []
        for chip in range(N_CHIP):
            slabs.append(_pack([lax.dynamic_index_in_dim(by_dev[n].reshape((N_CHIP, 2) + by_dev[n].shape[1:])[chip], core, 0, False)
                                for n in SHARDED]))
        return jnp.stack(slabs)

    from_sibling = _swap_sibling(for_core(1 - my_c))
    parts = _exchange_chips(_add(for_core(my_c), from_sibling))
    outs = _adamw(parts, _pack([wts[n] for n in SHARDED]), _pack([mom[n] for n in SHARDED]),
                  _pack([var[n] for n in SHARDED]), "adamw_sharded")
    res = {n: vals for n, vals in zip(SHARDED, zip(*[_unpack(o, shard_shapes) for o in outs]))}

    rep_shapes = [wts[n].shape for n in REPLICATED]
    rep_parts = _all_gather(_pack([grads[n].reshape(wts[n].shape) for n in REPLICATED]), "gather_small_grads")
    outs = _adamw(rep_parts, _pack([wts[n] for n in REPLICATED]), _pack([mom[n] for n in REPLICATED]),
                  _pack([var[n] for n in REPLICATED]), "adamw_replicated")
    res.update({n: vals for n, vals in zip(REPLICATED, zip(*[_unpack(o, rep_shapes) for o in outs]))})

    return (loss, grad_x, *[res[n][0] for n in WEIGHTS], *[res[n][1] for n in WEIGHTS],
            *[res[n][2] for n in WEIGHTS], *[res[n][3] for n in WEIGHTS])
```

```python
import functools
import math

import jax
import jax.numpy as jnp
from jax import lax
from jax.experimental import pallas as pl
from jax.experimental.pallas import tpu as pltpu

F32 = jnp.float32
BF16 = jnp.bfloat16
MESH = pl.DeviceIdType.MESH

N_DEV = 8
N_CHIP = 4
EPS = 1e-6
S5_GROUP = 16
S5_STATE = 64
S5_BLOCK_GROUPS = 16
HEAD_DIM = 64
Q_BLOCK = 128
SB_Q_TILE = 256
SB_HEADS_PER_STEP = 4
SCAN_CHUNK = 128
SUBLANES = 8
LANES = 128
FLAT_W = 1024
ADAM_LR, ADAM_B1, ADAM_B2, ADAM_EPS, ADAM_WD, ADAM_STEP = 1e-3, 0.9, 0.999, 1e-8, 0.01, 10
V7X_VMEM_LIMIT = 56 * 1024 * 1024
ROW_TILE = 256
WGRAD_ROW_TILE = 512
WGRAD_OUT_BYTES = 6 * 1024 * 1024

SHARDED = ("ffn1_w1", "ffn1_w3", "ffn1_w2", "ffn2_w1", "ffn2_w3", "ffn2_w2", "ple_proj", "ple_gate",
           "s5_w_in", "s5_w_glu", "sb_w_qkv", "sb_w_o")
COL_SHARDED = ("ffn1_w1", "ffn1_w3", "ffn2_w1", "ffn2_w3", "ple_proj", "s5_w_glu", "sb_w_qkv")
FFN_COL = ("ffn1_w1", "ffn1_w3", "ffn2_w1", "ffn2_w3")
FFN_ROW = ("ffn1_w2", "ffn2_w2")
REPLICATED = ("ffn1_norm", "mix_norm", "ffn2_norm", "ple_norm", "s5_a_re", "s5_a_im", "s5_log_dt",
              "s5_b_re", "s5_b_im", "s5_c_re", "s5_c_im", "s5_d", "final_norm")
WEIGHTS = ("ffn1_norm", "ffn1_w1", "ffn1_w3", "ffn1_w2", "mix_norm", "ffn2_norm", "ffn2_w1", "ffn2_w3", "ffn2_w2",
           "ple_norm", "ple_proj", "ple_gate", "s5_w_in", "s5_a_re", "s5_a_im", "s5_log_dt", "s5_b_re", "s5_b_im",
           "s5_c_re", "s5_c_im", "s5_d", "s5_w_glu", "sb_w_qkv", "sb_w_o", "final_norm")


def _dot(a, b):
    return jnp.dot(a, b, preferred_element_type=F32)


def _dot_nt(a, b):
    return lax.dot_general(a, b, (((1,), (1,)), ((), ())), preferred_element_type=F32)


def _dot_tn(a, b):
    return lax.dot_general(a, b, (((0,), (0,)), ((), ())), preferred_element_type=F32)


def _dot2(x, m):
    hi = x.astype(BF16)
    lo = (x - hi.astype(F32)).astype(BF16)
    return _dot(hi, m) + _dot(lo, m)


def _rms(x):
    r = lax.rsqrt(jnp.mean(x * x, axis=-1, keepdims=True) + EPS)
    return x * r, r


def _rms_bwd(dn, xh, r, g):
    gy = dn * g
    return r * (gy - xh * jnp.mean(gy * xh, axis=-1, keepdims=True))


_GELU_C = math.sqrt(2.0 / math.pi)


def _gelu(x):
    return 0.5 * x * (1.0 + jnp.tanh(_GELU_C * (x + 0.044715 * x * x * x)))


def _gelu_grad(x):
    th = jnp.tanh(_GELU_C * (x + 0.044715 * x * x * x))
    return 0.5 * (1.0 + th) + 0.5 * x * (1.0 - th * th) * _GELU_C * (1.0 + 3.0 * 0.044715 * x * x)


def _neg_softplus(z):
    return -(jnp.maximum(z, 0.0) + jnp.log(1.0 + jnp.exp(-jnp.abs(z))))


def _tile(n, want, mult=SUBLANES):
    for t in range(min(want, n), 0, -1):
        if n % t == 0 and t % mult == 0:
            return t
    return n


def _rows(tm, c):
    return pl.BlockSpec((tm, c), lambda i: (i, 0))


def _whole(a):
    nd = a.ndim
    return pl.BlockSpec(a.shape, lambda i: (0,) * nd)


def _row_params(sem="parallel"):
    return pltpu.CompilerParams(dimension_semantics=(sem,), vmem_limit_bytes=V7X_VMEM_LIMIT)


def _ffn_fwd(h, g, w1, w3, w2):
    t, d = h.shape
    f = w1.shape[1]
    tm = _tile(t, ROW_TILE)

    def body(h_ref, g_ref, w1_ref, w3_ref, w2_ref, ho_ref, n_ref, a_ref, b_ref):
        x = h_ref[...]
        xh, _ = _rms(x)
        n = (xh * g_ref[...]).astype(BF16)
        a = _dot(n, w1_ref[...])
        b = _dot(n, w3_ref[...])
        s = (a * jax.nn.sigmoid(a) * b).astype(BF16)
        ho_ref[...] = x + 0.5 * _dot(s, w2_ref[...])
        n_ref[...] = n
        a_ref[...] = a.astype(BF16)
        b_ref[...] = b.astype(BF16)

    return pl.pallas_call(
        body, name="ffn_fwd", grid=(t // tm,),
        in_specs=[_rows(tm, d), _whole(g), _whole(w1), _whole(w3), _whole(w2)],
        out_specs=[_rows(tm, d), _rows(tm, d), _rows(tm, f), _rows(tm, f)],
        out_shape=[jax.ShapeDtypeStruct((t, d), F32), jax.ShapeDtypeStruct((t, d), BF16),
                   jax.ShapeDtypeStruct((t, f), BF16), jax.ShapeDtypeStruct((t, f), BF16)],
        compiler_params=_row_params(),
    )(h, g, w1, w3, w2)


def _ffn_bwd_down(dh, a, b, w2):
    t, d = dh.shape
    f = a.shape[1]
    tm = _tile(t, ROW_TILE)

    def body(dh_ref, a_ref, b_ref, w2_ref, da_ref, db_ref, s_ref):
        ds = _dot_nt((0.5 * dh_ref[...]).astype(BF16), w2_ref[...])
        a32 = a_ref[...].astype(F32)
        b32 = b_ref[...].astype(F32)
        sig = jax.nn.sigmoid(a32)
        sil = a32 * sig
        da_ref[...] = (ds * b32 * (sig * (1.0 + a32 * (1.0 - sig)))).astype(BF16)
        db_ref[...] = (ds * sil).astype(BF16)
        s_ref[...] = (sil * b32).astype(BF16)

    return pl.pallas_call(
        body, name="ffn_bwd_down", grid=(t // tm,),
        in_specs=[_rows(tm, d), _rows(tm, f), _rows(tm, f), _whole(w2)],
        out_specs=[_rows(tm, f)] * 3,
        out_shape=[jax.ShapeDtypeStruct((t, f), BF16)] * 3,
        compiler_params=_row_params(),
    )(dh, a, b, w2)


def _lin_bwd_norm(dys, ws, h, g, dh, name):
    t, d = h.shape
    tm = _tile(t, ROW_TILE)
    k = len(dys)

    def body(*refs):
        dy_refs, w_refs = refs[:k], refs[k:2 * k]
        h_ref, g_ref, dh_ref, o_ref, dg_ref = refs[2 * k:]
        dn = _dot_nt(dy_refs[0][...].astype(BF16), w_refs[0][...])
        for j in range(1, k):
            dn = dn + _dot_nt(dy_refs[j][...].astype(BF16), w_refs[j][...])
        xh, r = _rms(h_ref[...])
        o_ref[...] = dh_ref[...] + _rms_bwd(dn, xh, r, g_ref[...])

        @pl.when(pl.program_id(0) == 0)
        def _():
            dg_ref[...] = jnp.zeros_like(dg_ref)

        dg_ref[...] += jnp.sum(dn * xh, axis=0, keepdims=True)

    return pl.pallas_call(
        body, name=name, grid=(t // tm,),
        in_specs=[_rows(tm, dy.shape[1]) for dy in dys] + [_whole(w) for w in ws] + [_rows(tm, d), _whole(g), _rows(tm, d)],
        out_specs=[_rows(tm, d), pl.BlockSpec((1, d), lambda i: (0, 0))],
        out_shape=[jax.ShapeDtypeStruct((t, d), F32), jax.ShapeDtypeStruct((1, d), F32)],
        compiler_params=_row_params("arbitrary"),
    )(*dys, *ws, h, g, dh)


def _wgrad(x, dy, name, scale=1.0):
    t, kk = x.shape
    n = dy.shape[1]
    tm = _tile(t, WGRAD_ROW_TILE)
    tn = _tile(n, max(128, WGRAD_OUT_BYTES // (4 * kk)), 128)
    steps = t // tm

    def body(x_ref, dy_ref, o_ref):
        i = pl.program_id(1)

        @pl.when(i == 0)
        def _():
            o_ref[...] = jnp.zeros_like(o_ref)

        o_ref[...] += _dot_tn(x_ref[...].astype(BF16), dy_ref[...].astype(BF16))
        if scale != 1.0:
            @pl.when(i == steps - 1)
            def _():
                o_ref[...] = o_ref[...] * scale

    return pl.pallas_call(
        body, name=name, grid=(n // tn, steps),
        in_specs=[pl.BlockSpec((tm, kk), lambda j, i: (i, 0)), pl.BlockSpec((tm, tn), lambda j, i: (i, j))],
        out_specs=pl.BlockSpec((kk, tn), lambda j, i: (0, j)),
        out_shape=jax.ShapeDtypeStruct((kk, n), F32),
        compiler_params=pltpu.CompilerParams(dimension_semantics=("parallel", "arbitrary"), vmem_limit_bytes=V7X_VMEM_LIMIT),
    )(x, dy)


def _norm_lin(h, g, ws, out_dtypes, name):
    t, d = h.shape
    tm = _tile(t, ROW_TILE)
    k = len(ws)

    def body(*refs):
        h_ref, g_ref = refs[:2]
        w_refs = refs[2:2 + k]
        n_ref = refs[2 + k]
        o_refs = refs[3 + k:]
        xh, _ = _rms(h_ref[...])
        n = (xh * g_ref[...]).astype(BF16)
        n_ref[...] = n
        for w_ref, o_ref in zip(w_refs, o_refs):
            o_ref[...] = _dot(n, w_ref[...]).astype(o_ref.dtype)

    return pl.pallas_call(
        body, name=name, grid=(t // tm,),
        in_specs=[_rows(tm, d), _whole(g)] + [_whole(w) for w in ws],
        out_specs=[_rows(tm, d)] + [_rows(tm, w.shape[1]) for w in ws],
        out_shape=[jax.ShapeDtypeStruct((t, d), BF16)] + [jax.ShapeDtypeStruct((t, w.shape[1]), dt) for w, dt in zip(ws, out_dtypes)],
        compiler_params=_row_params(),
    )(h, g, *ws)


def _lin_res(h, x, w, name):
    t, d = h.shape
    tm = _tile(t, ROW_TILE)

    def body(h_ref, x_ref, w_ref, o_ref):
        o_ref[...] = h_ref[...] + _dot(x_ref[...], w_ref[...])

    return pl.pallas_call(
        body, name=name, grid=(t // tm,),
        in_specs=[_rows(tm, d), _rows(tm, x.shape[1]), _whole(w)],
        out_specs=_rows(tm, d), out_shape=jax.ShapeDtypeStruct((t, d), F32),
        compiler_params=_row_params(),
    )(h, x, w)


def _lin_nt(dy, w, name):
    t = dy.shape[0]
    kk = w.shape[0]
    tm = _tile(t, ROW_TILE)

    def body(dy_ref, w_ref, o_ref):
        o_ref[...] = _dot_nt(dy_ref[...].astype(BF16), w_ref[...]).astype(BF16)

    return pl.pallas_call(
        body, name=name, grid=(t // tm,),
        in_specs=[_rows(tm, dy.shape[1]), _whole(w)],
        out_specs=_rows(tm, kk), out_shape=jax.ShapeDtypeStruct((t, kk), BF16),
        compiler_params=_row_params(),
    )(dy, w)


def _ple_fwd(h, g, wg, p, wp):
    t, d = h.shape
    tm = _tile(t, ROW_TILE)

    def body(h_ref, g_ref, wg_ref, p_ref, wp_ref, o_ref, n_ref):
        x = h_ref[...]
        xh, _ = _rms(x)
        n = (xh * g_ref[...]).astype(BF16)
        n_ref[...] = n
        gate = jax.nn.sigmoid(_dot(n, wg_ref[...]))
        o_ref[...] = x + _dot(p_ref[...].astype(BF16), wp_ref[...]) * gate

    return pl.pallas_call(
        body, name="ple_fwd", grid=(t // tm,),
        in_specs=[_rows(tm, d), _whole(g), _whole(wg), _rows(tm, p.shape[1]), _whole(wp)],
        out_specs=[_rows(tm, d), _rows(tm, d)],
        out_shape=[jax.ShapeDtypeStruct((t, d), F32), jax.ShapeDtypeStruct((t, d), BF16)],
        compiler_params=_row_params(),
    )(h, g, wg, p, wp)


def _ple_bwd(dh, h, g, n, p, wg, wp):
    t, d = h.shape
    tm = _tile(t, ROW_TILE)

    def body(dh_ref, h_ref, g_ref, n_ref, p_ref, wg_ref, wp_ref, o_ref, dgl_ref, dpp_ref, dg_ref):
        dh_v = dh_ref[...]
        gate = jax.nn.sigmoid(_dot(n_ref[...], wg_ref[...]))
        pp = _dot(p_ref[...].astype(BF16), wp_ref[...])
        dgl = (dh_v * pp * gate * (1.0 - gate)).astype(BF16)
        dgl_ref[...] = dgl
        dpp_ref[...] = (dh_v * gate).astype(BF16)
        dn = _dot_nt(dgl, wg_ref[...])
        xh, r = _rms(h_ref[...])
        o_ref[...] = dh_v + _rms_bwd(dn, xh, r, g_ref[...])

        @pl.when(pl.program_id(0) == 0)
        def _():
            dg_ref[...] = jnp.zeros_like(dg_ref)

        dg_ref[...] += jnp.sum(dn * xh, axis=0, keepdims=True)

    return pl.pallas_call(
        body, name="ple_bwd", grid=(t // tm,),
        in_specs=[_rows(tm, d), _rows(tm, d), _whole(g), _rows(tm, d), _rows(tm, p.shape[1]), _whole(wg), _whole(wp)],
        out_specs=[_rows(tm, d), _rows(tm, d), _rows(tm, d), pl.BlockSpec((1, d), lambda i: (0, 0))],
        out_shape=[jax.ShapeDtypeStruct((t, d), F32), jax.ShapeDtypeStruct((t, d), BF16),
                   jax.ShapeDtypeStruct((t, d), BF16), jax.ShapeDtypeStruct((1, d), F32)],
        compiler_params=_row_params("arbitrary"),
    )(dh, h, g, n, p, wg, wp)


def _loss_head(h, g, tgt):
    t, d = h.shape
    tm = _tile(t, ROW_TILE)

    def body(h_ref, g_ref, t_ref, l_ref, dh_ref, dg_ref):
        xh, r = _rms(h_ref[...])
        gg = g_ref[...]
        e = xh * gg - t_ref[...]
        dy = e * (1.0 / d)

        @pl.when(pl.program_id(0) == 0)
        def _():
            l_ref[...] = jnp.zeros_like(l_ref)
            dg_ref[...] = jnp.zeros_like(dg_ref)

        l_ref[...] += 0.5 * jnp.sum(jnp.mean(e * e, axis=-1, keepdims=True), axis=0, keepdims=True)
        dg_ref[...] += jnp.sum(dy * xh, axis=0, keepdims=True)
        dh_ref[...] = _rms_bwd(dy, xh, r, gg)

    return pl.pallas_call(
        body, name="loss_head", grid=(t // tm,),
        in_specs=[_rows(tm, d), _whole(g), _rows(tm, d)],
        out_specs=[pl.BlockSpec((1, 128), lambda i: (0, 0)), _rows(tm, d), pl.BlockSpec((1, d), lambda i: (0, 0))],
        out_shape=[jax.ShapeDtypeStruct((1, 128), F32), jax.ShapeDtypeStruct((t, d), F32), jax.ShapeDtypeStruct((1, d), F32)],
        compiler_params=_row_params("arbitrary"),
    )(h, g, tgt)


def _s5_disc_math(a_re, a_im, log_dt):
    lam_re = jnp.minimum(a_re, -1e-4)
    lam_im = a_im
    dt = jnp.exp(log_dt)
    mag = jnp.exp(lam_re * dt)
    abar_re = mag * jnp.cos(lam_im * dt)
    abar_im = mag * jnp.sin(lam_im * dt)
    den = lam_re * lam_re + lam_im * lam_im
    nr = abar_re - 1.0
    ni = abar_im
    return abar_re, abar_im, (nr * lam_re + ni * lam_im) / den, (ni * lam_re - nr * lam_im) / den


def _s5_disc(a_re, a_im, log_dt):
    gp = jax.ShapeDtypeStruct(a_re.shape, F32)

    def body(ar_ref, ai_ref, ld_ref, o0, o1, o2, o3):
        outs = _s5_disc_math(ar_ref[...], ai_ref[...], ld_ref[...])
        for o_ref, val in zip((o0, o1, o2, o3), outs):
            o_ref[...] = val

    return pl.pallas_call(body, name="s5_disc", out_shape=[gp] * 4)(a_re, a_im, log_dt)


def _s5_disc_bwd(a_re, a_im, log_dt, cts):
    def body(ar_ref, ai_ref, ld_ref, c0, c1, c2, c3, dar_ref, dai_ref, dld_ref):
        _, vjp = jax.vjp(_s5_disc_math, ar_ref[...], ai_ref[...], ld_ref[...])
        dar, dai, dld = vjp((c0[...], c1[...], c2[...], c3[...]))
        dar_ref[...] = dar
        dai_ref[...] = dai
        dld_ref[...] = dld

    return pl.pallas_call(
        body, name="s5_disc_bwd",
        out_shape=[jax.ShapeDtypeStruct(a_re.shape, F32), jax.ShapeDtypeStruct(a_im.shape, F32),
                   jax.ShapeDtypeStruct(log_dt.shape, F32)],
    )(a_re, a_im, log_dt, *cts)


def _blockdiag(w, gl):
    g, r, c = w.shape
    w = w.reshape(g // gl, gl, r, c)
    eye = jnp.eye(gl, dtype=w.dtype)
    return (w[:, :, :, None, :] * eye[None, :, None, :, None]).reshape(g // gl, gl * r, gl * c)


def _blockdiag_take(m, gl, r, c):
    nb = m.shape[0]
    m = m.reshape(nb, gl, r, gl, c)
    return jnp.stack([m[:, g, :, g, :] for g in range(gl)], axis=1).reshape(nb * gl, r, c)


def _scan_rows(lc, step, carry, reverse=False):
    def blk(i, cs):
        i = (lc // SUBLANES - 1 - i) if reverse else i
        return step(pl.multiple_of(i * SUBLANES, SUBLANES), cs)

    return lax.fori_loop(0, lc // SUBLANES, blk, carry)


def _s5_scan_fwd(u, bre, bim, cre, cim, par, dskip, bl, seq, lc):
    t, d = u.shape
    nb, gw, ns = bre.shape
    nc = seq // lc

    def body(u_ref, bre_ref, bim_ref, cre_ref, cim_ref, par_ref, d_ref, y_ref, z_ref, st_ref, carry, sre, sim):
        @pl.when(pl.program_id(2) == 0)
        def _():
            carry[...] = jnp.zeros_like(carry)

        st_ref[...] = carry[...]
        uu = u_ref[...]
        ug = uu.astype(BF16)
        wre = _dot(ug, bre_ref[...])
        wim = _dot(ug, bim_ref[...])
        ar, ai = par_ref[0:1, :], par_ref[1:2, :]
        fr, fi = par_ref[2:3, :], par_ref[3:4, :]
        sre[...] = fr * wre - fi * wim
        sim[...] = fr * wim + fi * wre

        def step(base, cs):
            cr, ci = cs
            tr = sre[pl.ds(base, SUBLANES), :]
            ti = sim[pl.ds(base, SUBLANES), :]
            rows = lax.broadcasted_iota(jnp.int32, tr.shape, 0)
            outr, outi = tr, ti
            for k in range(SUBLANES):
                nr = ar * cr - ai * ci + tr[k:k + 1, :]
                ni = ar * ci + ai * cr + ti[k:k + 1, :]
                outr = jnp.where(rows == k, nr, outr)
                outi = jnp.where(rows == k, ni, outi)
                cr, ci = nr, ni
            sre[pl.ds(base, SUBLANES), :] = outr
            sim[pl.ds(base, SUBLANES), :] = outi
            return cr, ci

        cr, ci = _scan_rows(lc, step, (carry[0:1, :], carry[1:2, :]))
        carry[0:1, :] = cr
        carry[1:2, :] = ci
        y = _dot(sre[...].astype(BF16), cre_ref[...]) - _dot(sim[...].astype(BF16), cim_ref[...]) + d_ref[...] * uu
        y_ref[...] = y
        z_ref[...] = _gelu(y).astype(BF16)

    tok = pl.BlockSpec((lc, gw), lambda g, b, c: (b * nc + c, g))
    mat_b = pl.BlockSpec((None, gw, ns), lambda g, b, c: (g, 0, 0))
    mat_c = pl.BlockSpec((None, ns, gw), lambda g, b, c: (g, 0, 0))
    return pl.pallas_call(
        body, name="s5_scan_fwd", grid=(nb, bl, nc),
        in_specs=[tok, mat_b, mat_b, mat_c, mat_c, pl.BlockSpec((None, 8, ns), lambda g, b, c: (g, 0, 0)),
                  pl.BlockSpec((1, gw), lambda g, b, c: (0, g))],
        out_specs=[tok, tok, pl.BlockSpec((None, None, 2, ns), lambda g, b, c: (g, b * nc + c, 0, 0))],
        out_shape=[jax.ShapeDtypeStruct((t, d), F32), jax.ShapeDtypeStruct((t, d), BF16),
                   jax.ShapeDtypeStruct((nb, bl * nc, 2, ns), F32)],
        scratch_shapes=[pltpu.VMEM((2, ns), F32), pltpu.VMEM((lc, ns), F32), pltpu.VMEM((lc, ns), F32)],
        compiler_params=pltpu.CompilerParams(dimension_semantics=("parallel", "arbitrary", "arbitrary"),
                                             vmem_limit_bytes=V7X_VMEM_LIMIT),
    )(u, bre, bim, cre, cim, par, dskip)


def _s5_scan_bwd(u, dy, st, bre, bim, cre, cim, par, dskip, bl, seq, lc):
    t, d = u.shape
    nb, gw, ns = bre.shape
    nc = seq // lc

    def body(u_ref, dy_ref, st_ref, bre_ref, bim_ref, cre_ref, cim_ref, par_ref, d_ref,
             du_ref, dbre_ref, dbim_ref, dcre_ref, dcim_ref, dpar_ref, dd_ref,
             lcarry, sre, sim, pre, pim, wre_s, wim_s, lre, lim):
        b, c = pl.program_id(1), pl.program_id(2)

        @pl.when((b == 0) & (c == 0))
        def _():
            for ref in (dbre_ref, dbim_ref, dcre_ref, dcim_ref, dpar_ref, dd_ref):
                ref[...] = jnp.zeros_like(ref)

        @pl.when(c == 0)
        def _():
            lcarry[...] = jnp.zeros_like(lcarry)

        uu = u_ref[...]
        ug = uu.astype(BF16)
        dyv = dy_ref[...]
        dyb = dyv.astype(BF16)
        ar, ai = par_ref[0:1, :], par_ref[1:2, :]
        fr, fi = par_ref[2:3, :], par_ref[3:4, :]
        wre = _dot(ug, bre_ref[...])
        wim = _dot(ug, bim_ref[...])
        wre_s[...] = wre
        wim_s[...] = wim
        sre[...] = fr * wre - fi * wim
        sim[...] = fr * wim + fi * wre

        def fstep(base, cs):
            cr, ci = cs
            tr = sre[pl.ds(base, SUBLANES), :]
            ti = sim[pl.ds(base, SUBLANES), :]
            rows = lax.broadcasted_iota(jnp.int32, tr.shape, 0)
            outr, outi, prr, pri = tr, ti, tr, ti
            for k in range(SUBLANES):
                prr = jnp.where(rows == k, cr, prr)
                pri = jnp.where(rows == k, ci, pri)
                nr = ar * cr - ai * ci + tr[k:k + 1, :]
                ni = ar * ci + ai * cr + ti[k:k + 1, :]
                outr = jnp.where(rows == k, nr, outr)
                outi = jnp.where(rows == k, ni, outi)
                cr, ci = nr, ni
            sre[pl.ds(base, SUBLANES), :] = outr
            sim[pl.ds(base, SUBLANES), :] = outi
            pre[pl.ds(base, SUBLANES), :] = prr
            pim[pl.ds(base, SUBLANES), :] = pri
            return cr, ci

        _scan_rows(lc, fstep, (st_ref[0:1, :], st_ref[1:2, :]))

        lre[...] = _dot_nt(dyb, cre_ref[...])
        lim[...] = -_dot_nt(dyb, cim_ref[...])

        def bstep(base, cs):
            cr, ci = cs
            tr = lre[pl.ds(base, SUBLANES), :]
            ti = lim[pl.ds(base, SUBLANES), :]
            rows = lax.broadcasted_iota(jnp.int32, tr.shape, 0)
            outr, outi = tr, ti
            for k in range(SUBLANES - 1, -1, -1):
                nr = tr[k:k + 1, :] + ar * cr + ai * ci
                ni = ti[k:k + 1, :] + ar * ci - ai * cr
                outr = jnp.where(rows == k, nr, outr)
                outi = jnp.where(rows == k, ni, outi)
                cr, ci = nr, ni
            lre[pl.ds(base, SUBLANES), :] = outr
            lim[pl.ds(base, SUBLANES), :] = outi
            return cr, ci

        cr, ci = _scan_rows(lc, bstep, (lcarry[0:1, :], lcarry[1:2, :]), reverse=True)
        lcarry[0:1, :] = cr
        lcarry[1:2, :] = ci

        lr, li = lre[...], lim[...]
        spr, spi = pre[...], pim[...]
        wr, wi = wre_s[...], wim_s[...]
        dpar_ref[0:1, :] += jnp.sum(lr * spr + li * spi, axis=0, keepdims=True)
        dpar_ref[1:2, :] += jnp.sum(li * spr - lr * spi, axis=0, keepdims=True)
        dpar_ref[2:3, :] += jnp.sum(lr * wr + li * wi, axis=0, keepdims=True)
        dpar_ref[3:4, :] += jnp.sum(li * wr - lr * wi, axis=0, keepdims=True)
        dwr = (fr * lr + fi * li).astype(BF16)
        dwi = (fr * li - fi * lr).astype(BF16)
        dsk = d_ref[...]
        du_ref[...] = _dot_nt(dwr, bre_ref[...]) + _dot_nt(dwi, bim_ref[...]) + dsk * dyv
        dd_ref[...] += jnp.sum(dyv * uu, axis=0, keepdims=True)
        dbre_ref[...] += _dot_tn(ug, dwr)
        dbim_ref[...] += _dot_tn(ug, dwi)
        dcre_ref[...] += _dot_tn(sre[...].astype(BF16), dyb)
        dcim_ref[...] -= _dot_tn(sim[...].astype(BF16), dyb)

    tok = pl.BlockSpec((lc, gw), lambda g, b, c: (b * nc + nc - 1 - c, g))
    mat_b = pl.BlockSpec((None, gw, ns), lambda g, b, c: (g, 0, 0))
    mat_c = pl.BlockSpec((None, ns, gw), lambda g, b, c: (g, 0, 0))
    rows8 = pl.BlockSpec((None, 8, ns), lambda g, b, c: (g, 0, 0))
    dvec = pl.BlockSpec((1, gw), lambda g, b, c: (0, g))
    tile = pltpu.VMEM((lc, ns), F32)
    return pl.pallas_call(
        body, name="s5_scan_bwd", grid=(nb, bl, nc),
        in_specs=[tok, tok, pl.BlockSpec((None, None, 2, ns), lambda g, b, c: (g, b * nc + nc - 1 - c, 0, 0)),
                  mat_b, mat_b, mat_c, mat_c, rows8, dvec],
        out_specs=[tok, mat_b, mat_b, mat_c, mat_c, rows8, dvec],
        out_shape=[jax.ShapeDtypeStruct((t, d), F32),
                   jax.ShapeDtypeStruct((nb, gw, ns), F32), jax.ShapeDtypeStruct((nb, gw, ns), F32),
                   jax.ShapeDtypeStruct((nb, ns, gw), F32), jax.ShapeDtypeStruct((nb, ns, gw), F32),
                   jax.ShapeDtypeStruct((nb, 8, ns), F32), jax.ShapeDtypeStruct((1, d), F32)],
        scratch_shapes=[pltpu.VMEM((2, ns), F32)] + [tile] * 8,
        compiler_params=pltpu.CompilerParams(dimension_semantics=("arbitrary", "arbitrary", "arbitrary"),
                                             vmem_limit_bytes=V7X_VMEM_LIMIT),
    )(u, dy, st, bre, bim, cre, cim, par, dskip)


def _s5_out(h, z, wglu):
    t, d = h.shape
    tm = _tile(t, ROW_TILE)

    def body(h_ref, z_ref, w_ref, o_ref):
        zz = _dot(z_ref[...], w_ref[...])
        o_ref[...] = h_ref[...] + zz[:, :d] * jax.nn.sigmoid(zz[:, d:])

    return pl.pallas_call(
        body, name="s5_out", grid=(t // tm,),
        in_specs=[_rows(tm, d), _rows(tm, d), _whole(wglu)],
        out_specs=_rows(tm, d), out_shape=jax.ShapeDtypeStruct((t, d), F32),
        compiler_params=_row_params(),
    )(h, z, wglu)


def _s5_out_bwd(dh, z, y, wglu):
    t, d = dh.shape
    tm = _tile(t, ROW_TILE)

    def body(dh_ref, z_ref, y_ref, w_ref, dy_ref, dzz_ref):
        zz = _dot(z_ref[...], w_ref[...])
        out, sg = zz[:, :d], jax.nn.sigmoid(zz[:, d:])
        dh_v = dh_ref[...]
        dzz = jnp.concatenate([dh_v * sg, dh_v * out * sg * (1.0 - sg)], axis=1).astype(BF16)
        dzz_ref[...] = dzz
        dy_ref[...] = _dot_nt(dzz, w_ref[...]) * _gelu_grad(y_ref[...])

    return pl.pallas_call(
        body, name="s5_out_bwd", grid=(t // tm,),
        in_specs=[_rows(tm, d), _rows(tm, d), _rows(tm, d), _whole(wglu)],
        out_specs=[_rows(tm, d), _rows(tm, 2 * d)],
        out_shape=[jax.ShapeDtypeStruct((t, d), F32), jax.ShapeDtypeStruct((t, 2 * d), BF16)],
        compiler_params=_row_params(),
    )(dh, z, y, wglu)


def _head_spec(seq, w):
    return pl.BlockSpec((None, SB_HEADS_PER_STEP, seq, w), lambda b, h: (b, h, 0, 0))


def _tri_and_ones(kind):
    row = lax.broadcasted_iota(jnp.int32, (Q_BLOCK, Q_BLOCK), 0)
    col = lax.broadcasted_iota(jnp.int32, (Q_BLOCK, Q_BLOCK), 1)
    tri = {"after": row > col, "upto": row <= col, "before": row < col}[kind]
    return jnp.concatenate([tri.astype(BF16), jnp.ones((Q_BLOCK, Q_BLOCK), BF16)], axis=1)


def _sb_fwd(q, k, v):
    bsz, nh, seq, dh = q.shape
    tq = min(SB_Q_TILE, seq // 2)
    nq, nsub = seq // tq, tq // Q_BLOCK
    scale = dh ** -0.5
    hp = SB_HEADS_PER_STEP
    qb = Q_BLOCK

    def body(q_ref, k_ref, v_ref, o_ref, tot_ref):
        strict = lax.broadcasted_iota(jnp.int32, (qb, qb), 1) < lax.broadcasted_iota(jnp.int32, (qb, qb), 0)
        sums = _tri_and_ones("after")

        def sweep(qs, c0, units, carry):
            heads = sorted({u[0] for u in units})
            kbs = {hh: k_ref[hh, pl.ds(c0, qb), :] for hh in heads}
            vbs = {hh: v_ref[hh, pl.ds(c0, qb), :] for hh in heads}
            zs = [_dot_nt(qs[hh][s], kbs[hh]) * scale for hh, s, _ in units]
            lkrs = [_neg_softplus(z) for z in zs]
            lks = [jnp.where(strict, lkr, 0.0) if dg else lkr for lkr, (_, _, dg) in zip(lkrs, units)]
            css = [_dot2(lk, sums) for lk in lks]
            carry = dict(carry)
            for (hh, s, dg), z, lkr, cs in zip(units, zs, lkrs, css):
                acc, run = carry[hh, s]
                att = jnp.exp(z + lkr + cs[:, :qb] + run)
                if dg:
                    att = jnp.where(strict, att, 0.0)
                carry[hh, s] = (acc + _dot(att.astype(BF16), vbs[hh]), run + cs[:, qb:])
            return carry

        def q_loop(qi, _):
            r0 = qi * tq
            qs = {hh: [q_ref[hh, pl.ds(pl.multiple_of(r0 + s * qb, qb), qb), :] for s in range(nsub)] for hh in range(hp)}
            carry = {(hh, s): (jnp.zeros((qb, dh), F32), jnp.zeros((qb, qb), F32)) for hh in range(hp) for s in range(nsub)}
            for jj in reversed(range(nsub)):
                units = [(hh, s, s == jj) for hh in range(hp) for s in range(jj, nsub)]
                carry = sweep(qs, pl.multiple_of(r0 + jj * qb, qb), units, carry)
            units = [(hh, s, False) for hh in range(hp) for s in range(nsub)]
            carry = lax.fori_loop(
                0, nsub * qi, lambda t, c: sweep(qs, pl.multiple_of((nsub * qi - 1 - t) * qb, qb), units, c), carry)
            for (hh, s), (acc, run) in carry.items():
                rows = pl.ds(pl.multiple_of(r0 + s * qb, qb), qb)
                o_ref[hh, rows, :] = acc.astype(BF16)
                tot_ref[hh, rows, :] = run[:, 0:1]
            return 0

        lax.fori_loop(0, nq, q_loop, 0)

    return pl.pallas_call(
        body, name="sb_fwd", grid=(bsz, nh // hp),
        in_specs=[_head_spec(seq, dh)] * 3,
        out_specs=[_head_spec(seq, dh), _head_spec(seq, 1)],
        out_shape=[jax.ShapeDtypeStruct((bsz, nh, seq, dh), BF16), jax.ShapeDtypeStruct((bsz, nh, seq, 1), F32)],
        compiler_params=pltpu.CompilerParams(dimension_semantics=("parallel", "parallel"), vmem_limit_bytes=V7X_VMEM_LIMIT),
    )(q, k, v)


def _sb_bwd(q, k, v, do, tot):
    bsz, nh, seq, dh = q.shape
    tq = min(SB_Q_TILE, seq // 2)
    nq, nsub = seq // tq, tq // Q_BLOCK
    scale = dh ** -0.5
    hp = SB_HEADS_PER_STEP
    qb = Q_BLOCK

    def body(q_ref, k_ref, v_ref, do_ref, tot_ref, dq_ref, dk_ref, dv_ref, dka, dva):
        dka[...] = jnp.zeros_like(dka)
        dva[...] = jnp.zeros_like(dva)
        strict = lax.broadcasted_iota(jnp.int32, (qb, qb), 1) < lax.broadcasted_iota(jnp.int32, (qb, qb), 0)
        upto = _tri_and_ones("upto")
        before = _tri_and_ones("before")

        def sweep(qf, dof, tots, c0, first, units, carry):
            heads = sorted({u[0] for u in units})
            kbs = {hh: k_ref[hh, pl.ds(c0, qb), :] for hh in heads}
            vbs = {hh: v_ref[hh, pl.ds(c0, qb), :] for hh in heads}
            sub = lambda a, s: a[s * qb:(s + 1) * qb, :]
            zs = [_dot_nt(sub(qf[hh], s), kbs[hh]) * scale for hh, s, _ in units]
            das = [_dot_nt(sub(dof[hh], s), vbs[hh]) for hh, s, _ in units]
            lkrs = [_neg_softplus(z) for z in zs]
            lks = [jnp.where(strict, lkr, 0.0) if dg else lkr for lkr, (_, _, dg) in zip(lkrs, units)]
            css = [_dot2(lk, upto) for lk in lks]
            lsigs, atts, gls = [], [], []
            for (hh, s, dg), z, lkr, cs, da in zip(units, zs, lkrs, css, das):
                lsig = z + lkr
                att = jnp.exp(lsig + (tots[hh, s] - (cs[:, :qb] + carry[hh, s][1])))
                if dg:
                    att = jnp.where(strict, att, 0.0)
                lsigs.append(lsig)
                atts.append(att)
                gls.append(da * att)
            gss = [_dot2(gl, before) for gl in gls]
            carry = dict(carry)
            dzs = {}
            for (hh, s, dg), lsig, gl, cs, gs in zip(units, lsigs, gls, css, gss):
                dqa, pre, gpre = carry[hh, s]
                sig = jnp.exp(lsig)
                dz = gl * (1.0 - sig) - (gs[:, :qb] + gpre) * sig
                if dg:
                    dz = jnp.where(strict, dz, 0.0)
                dz = (dz * scale).astype(BF16)
                dzs[hh, s] = dz
                carry[hh, s] = (dqa + _dot(dz, kbs[hh]), pre + cs[:, qb:], gpre + gs[:, qb:])
            att_of = {(hh, s): a for (hh, s, _), a in zip(units, atts)}
            for hh in heads:
                dzc = jnp.concatenate([dzs[hh, s] for s in range(first, nsub)], axis=0)
                attc = jnp.concatenate([att_of[hh, s].astype(BF16) for s in range(first, nsub)], axis=0)
                dka[hh, pl.ds(c0, qb), :] += _dot_tn(dzc, qf[hh][first * qb:, :])
                dva[hh, pl.ds(c0, qb), :] += _dot_tn(attc, dof[hh][first * qb:, :])
            return carry

        def q_loop(qi, _):
            r0 = pl.multiple_of(qi * tq, tq)
            qf = [q_ref[hh, pl.ds(r0, tq), :] for hh in range(hp)]
            dof = [do_ref[hh, pl.ds(r0, tq), :] for hh in range(hp)]
            tots = {(hh, s): jnp.broadcast_to(tot_ref[hh, pl.ds(pl.multiple_of(r0 + s * qb, qb), qb), :], (qb, qb))
                    for hh in range(hp) for s in range(nsub)}
            carry = {(hh, s): (jnp.zeros((qb, dh), F32), jnp.zeros((qb, qb), F32), jnp.zeros((qb, qb), F32))
                     for hh in range(hp) for s in range(nsub)}
            units = [(hh, s, False) for hh in range(hp) for s in range(nsub)]
            carry = lax.fori_loop(
                0, nsub * qi, lambda kj, c: sweep(qf, dof, tots, pl.multiple_of(kj * qb, qb), 0, units, c), carry)
            for jj in range(nsub):
                units = [(hh, s, s == jj) for hh in range(hp) for s in range(jj, nsub)]
                carry = sweep(qf, dof, tots, pl.multiple_of(r0 + jj * qb, qb), jj, units, carry)
            for (hh, s), (dqa, _, _) in carry.items():
                dq_ref[hh, pl.ds(pl.multiple_of(r0 + s * qb, qb), qb), :] = dqa.astype(BF16)
            return 0

        lax.fori_loop(0, nq, q_loop, 0)
        dk_ref[...] = dka[...].astype(BF16)
        dv_ref[...] = dva[...].astype(BF16)

    hs = _head_spec(seq, dh)
    return pl.pallas_call(
        body, name="sb_bwd", grid=(bsz, nh // hp),
        in_specs=[hs, hs, hs, hs, _head_spec(seq, 1)],
        out_specs=[hs, hs, hs],
        out_shape=[jax.ShapeDtypeStruct((bsz, nh, seq, dh), BF16)] * 3,
        scratch_shapes=[pltpu.VMEM((hp, seq, dh), F32), pltpu.VMEM((hp, seq, dh), F32)],
        compiler_params=pltpu.CompilerParams(dimension_semantics=("parallel", "parallel"), vmem_limit_bytes=V7X_VMEM_LIMIT),
    )(q, k, v, do, tot)


def _to_heads(a, bsz, seq):
    return a.reshape(bsz, seq, -1, HEAD_DIM).transpose(0, 2, 1, 3)


def _from_heads(a):
    bsz, nh, seq, dh = a.shape
    return a.transpose(0, 2, 1, 3).reshape(bsz * seq, nh * dh)


def _coords():
    return lax.axis_index("x"), lax.axis_index("y"), lax.axis_index("c")


def _all_gather(x, name):
    r, c = x.shape

    def body(x_ref, out_ref, send_sems, recv_sems, local_sem):
        mx, my, mc = _coords()
        me, sibling = (mx, my, mc), (mx, my, 1 - mc)
        chips = [(1 - mx, my), (mx, 1 - my), (1 - mx, 1 - my)]

        def blk(px, py, pc):
            return out_ref.at[4 * px + 2 * py + pc]

        def copy(k, block, to, src=None):
            return pltpu.make_async_remote_copy(
                src_ref=blk(*block) if src is None else src, dst_ref=blk(*block),
                send_sem=send_sems.at[k], recv_sem=recv_sems.at[k], device_id=to, device_id_type=MESH)

        mine = pltpu.make_async_copy(x_ref, blk(*me), local_sem)
        mine.start()
        first = [copy(0, me, sibling, src=x_ref)]
        first += [copy(1 + j, me, (*chip, mc), src=x_ref) for j, chip in enumerate(chips)]
        for cp in first:
            cp.start()
        passed = [copy(4 + j, (*chip, mc), sibling) for j, chip in enumerate(chips)]
        for j, chip in enumerate(chips):
            copy(1 + j, (*chip, mc), me).wait_recv()
            passed[j].start()
        copy(0, sibling, me).wait_recv()
        for j, chip in enumerate(chips):
            copy(4 + j, (*chip, 1 - mc), me).wait_recv()
        for cp in first + passed:
            cp.wait_send()
        mine.wait()

    return pl.pallas_call(
        body, name=name,
        out_shape=jax.ShapeDtypeStruct((N_DEV, r, c), x.dtype),
        in_specs=[pl.BlockSpec(memory_space=pl.ANY)],
        out_specs=pl.BlockSpec(memory_space=pl.ANY),
        scratch_shapes=[pltpu.SemaphoreType.DMA((7,)), pltpu.SemaphoreType.DMA((7,)), pltpu.SemaphoreType.DMA],
    )(x)


def _swap_sibling(g):
    def body(g_ref, out_ref, send_sem, recv_sem):
        mx, my, mc = _coords()
        cp = pltpu.make_async_remote_copy(src_ref=g_ref, dst_ref=out_ref, send_sem=send_sem, recv_sem=recv_sem,
                                          device_id=(mx, my, 1 - mc), device_id_type=MESH)
        cp.start()
        cp.wait()

    return pl.pallas_call(
        body, name="rs_sibling", out_shape=jax.ShapeDtypeStruct(g.shape, g.dtype),
        in_specs=[pl.BlockSpec(memory_space=pl.ANY)], out_specs=pl.BlockSpec(memory_space=pl.ANY),
        scratch_shapes=[pltpu.SemaphoreType.DMA, pltpu.SemaphoreType.DMA],
    )(g)


def _add(a, b):
    n, r, c = a.shape
    tr = _tile(r, 512, 2 * SUBLANES)

    def body(a_ref, b_ref, o_ref):
        o_ref[...] = (a_ref[...] + b_ref[...].astype(F32)).astype(o_ref.dtype)

    spec = pl.BlockSpec((None, tr, c), lambda i, j: (i, j, 0))
    return pl.pallas_call(
        body, name="rs_add", grid=(n, r // tr), in_specs=[spec, spec], out_specs=spec,
        out_shape=jax.ShapeDtypeStruct(a.shape, b.dtype),
        compiler_params=pltpu.CompilerParams(dimension_semantics=("parallel", "parallel")),
    )(a, b)


def _exchange_chips(p):
    def body(p_ref, out_ref, send_sems, recv_sems, local_sem):
        mx, my, mc = _coords()
        here = 2 * mx + my
        chips = [(1 - mx, my), (mx, 1 - my), (1 - mx, 1 - my)]
        mine = pltpu.make_async_copy(p_ref.at[here], out_ref.at[here], local_sem)
        mine.start()
        copies = [pltpu.make_async_remote_copy(
            src_ref=p_ref.at[2 * cx + cy], dst_ref=out_ref.at[here], send_sem=send_sems.at[j], recv_sem=recv_sems.at[j],
            device_id=(cx, cy, mc), device_id_type=MESH) for j, (cx, cy) in enumerate(chips)]
        for cp in copies:
            cp.start()
        for cp in copies:
            cp.wait()
        mine.wait()

    return pl.pallas_call(
        body, name="rs_chips", out_shape=jax.ShapeDtypeStruct(p.shape, p.dtype),
        in_specs=[pl.BlockSpec(memory_space=pl.ANY)], out_specs=pl.BlockSpec(memory_space=pl.ANY),
        scratch_shapes=[pltpu.SemaphoreType.DMA((3,)), pltpu.SemaphoreType.DMA((3,)), pltpu.SemaphoreType.DMA],
    )(p)


def _adamw(parts, w, m, v, name):
    n, r, c = parts.shape
    tr = _tile(r, 320, 2 * SUBLANES)
    bc1 = 1.0 - ADAM_B1 ** ADAM_STEP
    bc2 = 1.0 - ADAM_B2 ** ADAM_STEP

    def body(p_ref, w_ref, m_ref, v_ref, g_ref, d_ref, mo_ref, vo_ref):
        g = p_ref[0].astype(F32)
        for j in range(1, n):
            g = g + p_ref[j].astype(F32)
        mn = ADAM_B1 * m_ref[...] + (1.0 - ADAM_B1) * g
        vn = ADAM_B2 * v_ref[...] + (1.0 - ADAM_B2) * (g * g)
        g_ref[...] = g
        mo_ref[...] = mn
        vo_ref[...] = vn
        d_ref[...] = -ADAM_LR * ((mn / bc1) / (jnp.sqrt(vn / bc2) + ADAM_EPS) + ADAM_WD * w_ref[...])

    flat = pl.BlockSpec((tr, c), lambda i: (i, 0))
    return pl.pallas_call(
        body, name=name, grid=(r // tr,),
        in_specs=[pl.BlockSpec((n, tr, c), lambda i: (0, i, 0)), flat, flat, flat],
        out_specs=[flat] * 4, out_shape=[jax.ShapeDtypeStruct((r, c), F32)] * 4,
        compiler_params=pltpu.CompilerParams(dimension_semantics=("parallel",)),
    )(parts, w, m, v)


def _pack(arrs, dtype=F32):
    cols = []
    for a in arrs:
        f = a.reshape(-1).astype(dtype)
        cols.append(jnp.pad(f, (0, -f.shape[0] % FLAT_W)))
    flat = jnp.concatenate(cols)
    flat = jnp.pad(flat, (0, -flat.shape[0] % (FLAT_W * SUBLANES)))
    return flat.reshape(-1, FLAT_W)


def _unpack(flat, shapes, lead=()):
    flat = flat.reshape(lead + (-1,))
    out, off = [], 0
    for s in shapes:
        n = math.prod(s)
        out.append(flat[..., off:off + n].reshape(lead + tuple(s)))
        off += n + (-n % FLAT_W)
    return out


def _split_cols(a):
    n = a.shape[-1] // N_DEV
    return jnp.moveaxis(a.reshape(a.shape[:-1] + (N_DEV, n)), -2, 0)


def _split_rows(a):
    k = a.shape[-2] // N_DEV
    return jnp.moveaxis(a.reshape(a.shape[:-2] + (N_DEV, k, a.shape[-1])), -3, 0)


def _join_cols(a):
    a = jnp.moveaxis(a, 0, -2)
    return a.reshape(a.shape[:-2] + (a.shape[-2] * a.shape[-1],))


def _join_rows(a):
    a = jnp.moveaxis(a, 0, -3)
    return a.reshape(a.shape[:-3] + (a.shape[-3] * a.shape[-2], a.shape[-1]))


def kernel(x, p, ffn1_norm, ffn1_w1, ffn1_w3, ffn1_w2, mix_norm, ffn2_norm, ffn2_w1, ffn2_w3, ffn2_w2, ple_norm, ple_proj, ple_gate, s5_w_in, s5_a_re, s5_a_im, s5_log_dt, s5_b_re, s5_b_im, s5_c_re, s5_c_im, s5_d, s5_w_glu, sb_w_qkv, sb_w_o, final_norm, loss_target, m_ffn1_norm, m_ffn1_w1, m_ffn1_w3, m_ffn1_w2, m_mix_norm, m_ffn2_norm, m_ffn2_w1, m_ffn2_w3, m_ffn2_w2, m_ple_norm, m_ple_proj, m_ple_gate, m_s5_w_in, m_s5_a_re, m_s5_a_im, m_s5_log_dt, m_s5_b_re, m_s5_b_im, m_s5_c_re, m_s5_c_im, m_s5_d, m_s5_w_glu, m_sb_w_qkv, m_sb_w_o, m_final_norm, v_ffn1_norm, v_ffn1_w1, v_ffn1_w3, v_ffn1_w2, v_mix_norm, v_ffn2_norm, v_ffn2_w1, v_ffn2_w3, v_ffn2_w2, v_ple_norm, v_ple_proj, v_ple_gate, v_s5_w_in, v_s5_a_re, v_s5_a_im, v_s5_log_dt, v_s5_b_re, v_s5_b_im, v_s5_c_re, v_s5_c_im, v_s5_d, v_s5_w_glu, v_sb_w_qkv, v_sb_w_o, v_final_norm):
    given = dict(locals())
    wts = {n: given[n] for n in WEIGHTS}
    mom = {n: given["m_" + n] for n in WEIGHTS}
    var = {n: given["v_" + n] for n in WEIGHTS}
    bl, seq, d = x.shape
    t = bl * seq
    depth = p.shape[0]
    my_c = lax.axis_index("c")

    shard_shapes = [wts[n].shape for n in SHARDED]
    hid = wts["ffn1_w1"].shape[-1]
    hid_pad = -hid % LANES

    def padded(n):
        a = wts[n].astype(BF16)
        if n in FFN_COL:
            return jnp.pad(a, ((0, 0), (0, 0), (0, hid_pad)))
        if n in FFN_ROW:
            return jnp.pad(a, ((0, 0), (0, hid_pad), (0, 0)))
        return a

    sent = [padded(n) for n in SHARDED]
    gathered = _all_gather(_pack(sent, BF16), "gather_weights")
    full = {}
    for n, blocks in zip(SHARDED, _unpack(gathered, [a.shape for a in sent], lead=(N_DEV,))):
        full[n] = _join_cols(blocks) if n in COL_SHARDED else _join_rows(blocks)

    def row(a):
        return a.reshape(1, -1)

    n_groups = d // S5_GROUP
    a_re, a_im = s5_a_re[0], s5_a_im[0]
    log_dt = s5_log_dt[0].reshape(n_groups, 1)
    disc = _s5_disc(a_re, a_im, log_dt)
    nb = n_groups // S5_BLOCK_GROUPS
    ns = S5_BLOCK_GROUPS * S5_STATE
    par = jnp.concatenate([jnp.stack([q.reshape(nb, ns) for q in disc], axis=1), jnp.zeros((nb, 4, ns), F32)], axis=1)
    bre = _blockdiag(s5_b_re[0].transpose(0, 2, 1), S5_BLOCK_GROUPS).astype(BF16)
    bim = _blockdiag(s5_b_im[0].transpose(0, 2, 1), S5_BLOCK_GROUPS).astype(BF16)
    cre = _blockdiag(s5_c_re[0].transpose(0, 2, 1), S5_BLOCK_GROUPS).astype(BF16)
    cim = _blockdiag(s5_c_im[0].transpose(0, 2, 1), S5_BLOCK_GROUPS).astype(BF16)
    lc = min(SCAN_CHUNK, seq)

    h = x.reshape(t, d)
    saved = []
    for i in range(depth):
        s = {"h0": h}
        h, s["n1"], s["a1"], s["b1"] = _ffn_fwd(h, row(ffn1_norm[i]), full["ffn1_w1"][i], full["ffn1_w3"][i], full["ffn1_w2"][i])
        s["h1"] = h
        j = i // 2
        if i % 2 == 0:
            s["hn"], s["u"] = _norm_lin(h, row(mix_norm[i]), [full["s5_w_in"][j]], [F32], "s5_in")
            s["y"], s["z"], s["st"] = _s5_scan_fwd(s["u"], bre, bim, cre, cim, par, row(s5_d[j]), bl, seq, lc)
            h = _s5_out(h, s["z"], full["s5_w_glu"][j])
        else:
            wq, wk, wv = jnp.split(full["sb_w_qkv"][j], 3, axis=1)
            s["hn"], q, k, v = _norm_lin(h, row(mix_norm[i]), [wq, wk, wv], [BF16] * 3, "sb_in")
            s["q"], s["k"], s["v"] = (_to_heads(a, bl, seq) for a in (q, k, v))
            o, s["tot"] = _sb_fwd(s["q"], s["k"], s["v"])
            s["o"] = _from_heads(o)
            h = _lin_res(h, s["o"], full["sb_w_o"][j], "sb_out")
        s["h2"] = h
        h, s["n2"], s["a2"], s["b2"] = _ffn_fwd(h, row(ffn2_norm[i]), full["ffn2_w1"][i], full["ffn2_w3"][i], full["ffn2_w2"][i])
        s["h3"] = h
        s["p"] = p[i].reshape(t, -1)
        h, s["npl"] = _ple_fwd(h, row(ple_norm[i]), full["ple_gate"][i], s["p"], full["ple_proj"][i])
        saved.append(s)

    loss_part, dh, g_final = _loss_head(h, row(final_norm), loss_target.reshape(t, d))
    loss = lax.psum(loss_part[0, 0], ("x", "y", "c"))

    grads = {n: [None] * wts[n].shape[0] for n in WEIGHTS if n != "final_norm"}
    grads["final_norm"] = g_final.reshape(-1)

    def ffn_bwd(dh, h_in, n, a, b, which, i):
        w1, w3, w2 = (full[f"{which}_{m}"][i] for m in ("w1", "w3", "w2"))
        da, db, sact = _ffn_bwd_down(dh, a, b, w2)
        grads[f"{which}_w2"][i] = _wgrad(sact, dh, f"{which}_w2_grad", 0.5)
        dh_in, dg = _lin_bwd_norm([da, db], [w1, w3], h_in, row(wts[f"{which}_norm"][i]), dh, f"{which}_bwd_up")
        grads[f"{which}_w1"][i] = _wgrad(n, da, f"{which}_w1_grad")
        grads[f"{which}_w3"][i] = _wgrad(n, db, f"{which}_w3_grad")
        grads[f"{which}_norm"][i] = dg.reshape(-1)
        return dh_in

    for i in reversed(range(depth)):
        s = saved[i]
        j = i // 2
        dh, dgl, dpp, dg = _ple_bwd(dh, s["h3"], row(ple_norm[i]), s["npl"], s["p"], full["ple_gate"][i], full["ple_proj"][i])
        grads["ple_norm"][i] = dg.reshape(-1)
        grads["ple_gate"][i] = _wgrad(s["npl"], dgl, "ple_gate_grad")
        grads["ple_proj"][i] = _wgrad(s["p"], dpp, "ple_proj_grad")
        dh = ffn_bwd(dh, s["h2"], s["n2"], s["a2"], s["b2"], "ffn2", i)
        if i % 2 == 0:
            wglu = full["s5_w_glu"][j]
            dy, dzz = _s5_out_bwd(dh, s["z"], s["y"], wglu)
            grads["s5_w_glu"][j] = _wgrad(s["z"], dzz, "s5_w_glu_grad")
            du, dbre, dbim, dcre, dcim, dpar, dd = _s5_scan_bwd(s["u"], dy, s["st"], bre, bim, cre, cim, par, row(s5_d[j]), bl, seq, lc)
            cts = [dpar[:, r, :].reshape(n_groups, S5_STATE) for r in range(4)]
            g_are, g_aim, g_ldt = _s5_disc_bwd(a_re, a_im, log_dt, cts)
            grads["s5_a_re"][j], grads["s5_a_im"][j], grads["s5_log_dt"][j] = g_are, g_aim, g_ldt.reshape(-1)
            take = functools.partial(_blockdiag_take, gl=S5_BLOCK_GROUPS)
            grads["s5_b_re"][j] = take(dbre, r=S5_GROUP, c=S5_STATE).transpose(0, 2, 1)
            grads["s5_b_im"][j] = take(dbim, r=S5_GROUP, c=S5_STATE).transpose(0, 2, 1)
            grads["s5_c_re"][j] = take(dcre, r=S5_STATE, c=S5_GROUP).transpose(0, 2, 1)
            grads["s5_c_im"][j] = take(dcim, r=S5_STATE, c=S5_GROUP).transpose(0, 2, 1)
            grads["s5_d"][j] = dd.reshape(-1)
            w_in = full["s5_w_in"][j]
            dh, dg = _lin_bwd_norm([du], [w_in], s["h1"], row(mix_norm[i]), dh, "s5_in_bwd")
            grads["s5_w_in"][j] = _wgrad(s["hn"], du, "s5_w_in_grad")
        else:
            w_o = full["sb_w_o"][j]
            do = _lin_nt(dh, w_o, "sb_out_bwd")
            grads["sb_w_o"][j] = _wgrad(s["o"], dh, "sb_w_o_grad")
            dq, dk, dv = _sb_bwd(s["q"], s["k"], s["v"], _to_heads(do, bl, seq), s["tot"])
            dq, dk, dv = (_from_heads(a) for a in (dq, dk, dv))
            wq, wk, wv = jnp.split(full["sb_w_qkv"][j], 3, axis=1)
            dh, dg = _lin_bwd_norm([dq, dk, dv], [wq, wk, wv], s["h1"], row(mix_norm[i]), dh, "sb_in_bwd")
            grads["sb_w_qkv"][j] = jnp.concatenate(
                [_wgrad(s["hn"], a, f"sb_w_{m}_grad") for a, m in ((dq, "q"), (dk, "k"), (dv, "v"))], axis=1)
        grads["mix_norm"][i] = dg.reshape(-1)
        dh = ffn_bwd(dh, s["h0"], s["n1"], s["a1"], s["b1"], "ffn1", i)
    grad_x = dh.reshape(x.shape)
    for n in WEIGHTS:
        if n != "final_norm":
            grads[n] = jnp.stack(grads[n])

    by_dev = {n: (_split_cols if n in COL_SHARDED else _split_rows)(grads[n]) for n in SHARDED}
    for n in FFN_COL:
        by_dev[n] = by_dev[n][..., :hid]
    for n in FFN_ROW:
        by_dev[n] = by_dev[n][..., :hid, :]

    def for_core(core, dtype):
        slabs = []
        for chip in range(N_CHIP):
            slabs.append(_pack([lax.dynamic_index_in_dim(by_dev[n].reshape((N_CHIP, 2) + by_dev[n].shape[1:])[chip], core, 0, False)
                                for n in SHARDED], dtype))
        return jnp.stack(slabs)

    from_sibling = _swap_sibling(for_core(1 - my_c, BF16))
    parts = _exchange_chips(_add(for_core(my_c, F32), from_sibling))
    outs = _adamw(parts, _pack([wts[n] for n in SHARDED]), _pack([mom[n] for n in SHARDED]),
                  _pack([var[n] for n in SHARDED]), "adamw_sharded")
    res = {n: vals for n, vals in zip(SHARDED, zip(*[_unpack(o, shard_shapes) for o in outs]))}

    rep_shapes = [wts[n].shape for n in REPLICATED]
    rep_parts = _all_gather(_pack([grads[n].reshape(wts[n].shape) for n in REPLICATED]), "gather_small_grads")
    outs = _adamw(rep_parts, _pack([wts[n] for n in REPLICATED]), _pack([mom[n] for n in REPLICATED]),
                  _pack([var[n] for n in REPLICATED]), "adamw_replicated")
    res.update({n: vals for n, vals in zip(REPLICATED, zip(*[_unpack(o, rep_shapes) for o in outs]))})

    return (loss, grad_x, *[res[n][0] for n in WEIGHTS], *[res[n][1] for n in WEIGHTS],
            *[res[n][2] for n in WEIGHTS], *[res[n][3] for n in WEIGHTS])
```

```python
import math
from typing import NamedTuple, Optional

import jax
import jax.numpy as jnp
from jax import lax
from jax.experimental import pallas as pl
from jax.experimental.pallas import tpu as pltpu

F32 = jnp.float32
BF16 = jnp.bfloat16
MESH = pl.DeviceIdType.MESH

N_DEV = 8
N_CHIP = 4
EPS = 1e-6
S5_GROUP = 16
S5_STATE = 64
S5_BLOCK_GROUPS = 16
HEAD_DIM = 64
Q_BLOCK = 128
SB_Q_TILE = 256
SB_HEADS_PER_STEP = 4
SCAN_CHUNK = 128
SUBLANES = 8
LANES = 128
FLAT_W = 1024
ADAM_LR, ADAM_B1, ADAM_B2, ADAM_EPS, ADAM_WD, ADAM_STEP = 1e-3, 0.9, 0.999, 1e-8, 0.01, 10
V7X_VMEM_LIMIT = 56 * 1024 * 1024
ROW_TILE = 256
WGRAD_ROW_TILE = 512
WGRAD_OUT_BYTES = 6 * 1024 * 1024

SHARDED = ("ffn1_w1", "ffn1_w3", "ffn1_w2", "ffn2_w1", "ffn2_w3", "ffn2_w2", "ple_proj", "ple_gate",
           "s5_w_in", "s5_w_glu", "sb_w_qkv", "sb_w_o")
COL_SHARDED = ("ffn1_w1", "ffn1_w3", "ffn2_w1", "ffn2_w3", "ple_proj", "s5_w_glu", "sb_w_qkv")
FFN_COL = ("ffn1_w1", "ffn1_w3", "ffn2_w1", "ffn2_w3")
FFN_ROW = ("ffn1_w2", "ffn2_w2")
REPLICATED = ("ffn1_norm", "mix_norm", "ffn2_norm", "ple_norm", "s5_a_re", "s5_a_im", "s5_log_dt",
              "s5_b_re", "s5_b_im", "s5_c_re", "s5_c_im", "s5_d", "final_norm")
WEIGHTS = ("ffn1_norm", "ffn1_w1", "ffn1_w3", "ffn1_w2", "mix_norm", "ffn2_norm", "ffn2_w1", "ffn2_w3", "ffn2_w2",
           "ple_norm", "ple_proj", "ple_gate", "s5_w_in", "s5_a_re", "s5_a_im", "s5_log_dt", "s5_b_re", "s5_b_im",
           "s5_c_re", "s5_c_im", "s5_d", "s5_w_glu", "sb_w_qkv", "sb_w_o", "final_norm")


def _dot(a, b):
    return jnp.dot(a, b, preferred_element_type=F32)


def _dot_nt(a, b):
    return lax.dot_general(a, b, (((1,), (1,)), ((), ())), preferred_element_type=F32)


def _dot_tn(a, b):
    return lax.dot_general(a, b, (((0,), (0,)), ((), ())), preferred_element_type=F32)


def _dot2(x, m):
    hi = x.astype(BF16)
    lo = (x - hi.astype(F32)).astype(BF16)
    return _dot(hi, m) + _dot(lo, m)


def _rms(x):
    r = lax.rsqrt(jnp.mean(x * x, axis=-1, keepdims=True) + EPS)
    return x * r, r


def _rms_bwd(dn, xh, r, g):
    gy = dn * g
    return r * (gy - xh * jnp.mean(gy * xh, axis=-1, keepdims=True))


_GELU_C = math.sqrt(2.0 / math.pi)


def _gelu(x):
    return 0.5 * x * (1.0 + jnp.tanh(_GELU_C * (x + 0.044715 * x * x * x)))


def _gelu_grad(x):
    th = jnp.tanh(_GELU_C * (x + 0.044715 * x * x * x))
    return 0.5 * (1.0 + th) + 0.5 * x * (1.0 - th * th) * _GELU_C * (1.0 + 3.0 * 0.044715 * x * x)


def _neg_softplus(z):
    return -(jnp.maximum(z, 0.0) + jnp.log(1.0 + jnp.exp(-jnp.abs(z))))


def _tile(n, want, mult=SUBLANES):
    for t in range(min(want, n), 0, -1):
        if n % t == 0 and t % mult == 0:
            return t
    return n


def _rows(tm, c):
    return pl.BlockSpec((tm, c), lambda i: (i, 0))


def _whole(a):
    nd = a.ndim
    return pl.BlockSpec(a.shape, lambda i: (0,) * nd)


class _W(NamedTuple):
    arr: jax.Array
    layer: int = 0
    col: int = 0
    width: Optional[int] = None

    @property
    def shape(self):
        return self.arr.shape[1], self.width or self.arr.shape[2]


def _wspec(w):
    return pl.BlockSpec((None,) + w.shape, lambda *_: (w.layer, 0, w.col))


def _row_params(sem="parallel"):
    return pltpu.CompilerParams(dimension_semantics=(sem,), vmem_limit_bytes=V7X_VMEM_LIMIT)


def _ffn_fwd(h, g, w1, w3, w2):
    t, d = h.shape
    f = w1.shape[1]
    tm = _tile(t, ROW_TILE)

    def body(h_ref, g_ref, w1_ref, w3_ref, w2_ref, ho_ref, n_ref, a_ref, b_ref):
        x = h_ref[...]
        xh, _ = _rms(x)
        n = (xh * g_ref[...]).astype(BF16)
        a = _dot(n, w1_ref[...])
        b = _dot(n, w3_ref[...])
        s = (a * jax.nn.sigmoid(a) * b).astype(BF16)
        ho_ref[...] = x + 0.5 * _dot(s, w2_ref[...])
        n_ref[...] = n
        a_ref[...] = a.astype(BF16)
        b_ref[...] = b.astype(BF16)

    return pl.pallas_call(
        body, name="ffn_fwd", grid=(t // tm,),
        in_specs=[_rows(tm, d), _whole(g), _wspec(w1), _wspec(w3), _wspec(w2)],
        out_specs=[_rows(tm, d), _rows(tm, d), _rows(tm, f), _rows(tm, f)],
        out_shape=[jax.ShapeDtypeStruct((t, d), F32), jax.ShapeDtypeStruct((t, d), BF16),
                   jax.ShapeDtypeStruct((t, f), BF16), jax.ShapeDtypeStruct((t, f), BF16)],
        compiler_params=_row_params(),
    )(h, g, w1.arr, w3.arr, w2.arr)


def _ffn_bwd_down(dh, a, b, w2):
    t, d = dh.shape
    f = a.shape[1]
    tm = _tile(t, ROW_TILE)

    def body(dh_ref, a_ref, b_ref, w2_ref, da_ref, db_ref, s_ref):
        ds = _dot_nt((0.5 * dh_ref[...]).astype(BF16), w2_ref[...])
        a32 = a_ref[...].astype(F32)
        b32 = b_ref[...].astype(F32)
        sig = jax.nn.sigmoid(a32)
        sil = a32 * sig
        da_ref[...] = (ds * b32 * (sig * (1.0 + a32 * (1.0 - sig)))).astype(BF16)
        db_ref[...] = (ds * sil).astype(BF16)
        s_ref[...] = (sil * b32).astype(BF16)

    return pl.pallas_call(
        body, name="ffn_bwd_down", grid=(t // tm,),
        in_specs=[_rows(tm, d), _rows(tm, f), _rows(tm, f), _wspec(w2)],
        out_specs=[_rows(tm, f)] * 3,
        out_shape=[jax.ShapeDtypeStruct((t, f), BF16)] * 3,
        compiler_params=_row_params(),
    )(dh, a, b, w2.arr)


def _lin_bwd_norm(dys, ws, h, g, dh, name):
    t, d = h.shape
    tm = _tile(t, ROW_TILE)
    k = len(dys)

    def body(*refs):
        dy_refs, w_refs = refs[:k], refs[k:2 * k]
        h_ref, g_ref, dh_ref, o_ref, dg_ref = refs[2 * k:]
        dn = _dot_nt(dy_refs[0][...].astype(BF16), w_refs[0][...])
        for j in range(1, k):
            dn = dn + _dot_nt(dy_refs[j][...].astype(BF16), w_refs[j][...])
        xh, r = _rms(h_ref[...])
        o_ref[...] = dh_ref[...] + _rms_bwd(dn, xh, r, g_ref[...])

        @pl.when(pl.program_id(0) == 0)
        def _():
            dg_ref[...] = jnp.zeros_like(dg_ref)

        dg_ref[...] += jnp.sum(dn * xh, axis=0, keepdims=True)

    return pl.pallas_call(
        body, name=name, grid=(t // tm,),
        in_specs=[_rows(tm, dy.shape[1]) for dy in dys] + [_wspec(w) for w in ws] + [_rows(tm, d), _whole(g), _rows(tm, d)],
        out_specs=[_rows(tm, d), pl.BlockSpec((1, d), lambda i: (0, 0))],
        out_shape=[jax.ShapeDtypeStruct((t, d), F32), jax.ShapeDtypeStruct((1, d), F32)],
        compiler_params=_row_params("arbitrary"),
    )(*dys, *[w.arr for w in ws], h, g, dh)


def _wgrad(x, dy, name, like, into=None, layer=0, col=0, scale=1.0):
    t, kk = x.shape
    n = dy.shape[1]
    tm = _tile(t, WGRAD_ROW_TILE)
    tn = _tile(n, max(128, WGRAD_OUT_BYTES // (4 * kk)), 128)
    steps = t // tm
    col0 = col * (n // tn)

    def body(x_ref, dy_ref, *rest):
        o_ref = rest[-1]
        i = pl.program_id(1)

        @pl.when(i == 0)
        def _():
            o_ref[...] = jnp.zeros_like(o_ref)

        o_ref[...] += _dot_tn(x_ref[...].astype(BF16), dy_ref[...].astype(BF16))
        if scale != 1.0:
            @pl.when(i == steps - 1)
            def _():
                o_ref[...] = o_ref[...] * scale

    held = [] if into is None else [into]
    return pl.pallas_call(
        body, name=name, grid=(n // tn, steps),
        in_specs=[pl.BlockSpec((tm, kk), lambda j, i: (i, 0)), pl.BlockSpec((tm, tn), lambda j, i: (i, j))]
        + [pl.BlockSpec(memory_space=pl.ANY)] * len(held),
        out_specs=pl.BlockSpec((None, kk, tn), lambda j, i: (layer, 0, col0 + j)),
        out_shape=jax.ShapeDtypeStruct(like, F32),
        input_output_aliases={2: 0} if held else {},
        compiler_params=pltpu.CompilerParams(dimension_semantics=("parallel", "arbitrary"), vmem_limit_bytes=V7X_VMEM_LIMIT),
    )(x, dy, *held)


def _norm_lin(h, g, ws, out_dtypes, name):
    t, d = h.shape
    tm = _tile(t, ROW_TILE)
    k = len(ws)

    def body(*refs):
        h_ref, g_ref = refs[:2]
        w_refs = refs[2:2 + k]
        n_ref = refs[2 + k]
        o_refs = refs[3 + k:]
        xh, _ = _rms(h_ref[...])
        n = (xh * g_ref[...]).astype(BF16)
        n_ref[...] = n
        for w_ref, o_ref in zip(w_refs, o_refs):
            o_ref[...] = _dot(n, w_ref[...]).astype(o_ref.dtype)

    return pl.pallas_call(
        body, name=name, grid=(t // tm,),
        in_specs=[_rows(tm, d), _whole(g)] + [_wspec(w) for w in ws],
        out_specs=[_rows(tm, d)] + [_rows(tm, w.shape[1]) for w in ws],
        out_shape=[jax.ShapeDtypeStruct((t, d), BF16)] + [jax.ShapeDtypeStruct((t, w.shape[1]), dt) for w, dt in zip(ws, out_dtypes)],
        compiler_params=_row_params(),
    )(h, g, *[w.arr for w in ws])


def _lin_res(h, x, w, name):
    t, d = h.shape
    tm = _tile(t, ROW_TILE)

    def body(h_ref, x_ref, w_ref, o_ref):
        o_ref[...] = h_ref[...] + _dot(x_ref[...], w_ref[...])

    return pl.pallas_call(
        body, name=name, grid=(t // tm,),
        in_specs=[_rows(tm, d), _rows(tm, x.shape[1]), _wspec(w)],
        out_specs=_rows(tm, d), out_shape=jax.ShapeDtypeStruct((t, d), F32),
        compiler_params=_row_params(),
    )(h, x, w.arr)


def _lin_nt(dy, w, name):
    t = dy.shape[0]
    kk = w.shape[0]
    tm = _tile(t, ROW_TILE)

    def body(dy_ref, w_ref, o_ref):
        o_ref[...] = _dot_nt(dy_ref[...].astype(BF16), w_ref[...]).astype(BF16)

    return pl.pallas_call(
        body, name=name, grid=(t // tm,),
        in_specs=[_rows(tm, dy.shape[1]), _wspec(w)],
        out_specs=_rows(tm, kk), out_shape=jax.ShapeDtypeStruct((t, kk), BF16),
        compiler_params=_row_params(),
    )(dy, w.arr)


def _ple_fwd(h, g, wg, p, wp):
    t, d = h.shape
    tm = _tile(t, ROW_TILE)

    def body(h_ref, g_ref, wg_ref, p_ref, wp_ref, o_ref, n_ref):
        x = h_ref[...]
        xh, _ = _rms(x)
        n = (xh * g_ref[...]).astype(BF16)
        n_ref[...] = n
        gate = jax.nn.sigmoid(_dot(n, wg_ref[...]))
        o_ref[...] = x + _dot(p_ref[...].astype(BF16), wp_ref[...]) * gate

    return pl.pallas_call(
        body, name="ple_fwd", grid=(t // tm,),
        in_specs=[_rows(tm, d), _whole(g), _wspec(wg), _rows(tm, p.shape[1]), _wspec(wp)],
        out_specs=[_rows(tm, d), _rows(tm, d)],
        out_shape=[jax.ShapeDtypeStruct((t, d), F32), jax.ShapeDtypeStruct((t, d), BF16)],
        compiler_params=_row_params(),
    )(h, g, wg.arr, p, wp.arr)


def _ple_bwd(dh, h, g, n, p, wg, wp):
    t, d = h.shape
    tm = _tile(t, ROW_TILE)

    def body(dh_ref, h_ref, g_ref, n_ref, p_ref, wg_ref, wp_ref, o_ref, dgl_ref, dpp_ref, dg_ref):
        dh_v = dh_ref[...]
        gate = jax.nn.sigmoid(_dot(n_ref[...], wg_ref[...]))
        pp = _dot(p_ref[...].astype(BF16), wp_ref[...])
        dgl = (dh_v * pp * gate * (1.0 - gate)).astype(BF16)
        dgl_ref[...] = dgl
        dpp_ref[...] = (dh_v * gate).astype(BF16)
        dn = _dot_nt(dgl, wg_ref[...])
        xh, r = _rms(h_ref[...])
        o_ref[...] = dh_v + _rms_bwd(dn, xh, r, g_ref[...])

        @pl.when(pl.program_id(0) == 0)
        def _():
            dg_ref[...] = jnp.zeros_like(dg_ref)

        dg_ref[...] += jnp.sum(dn * xh, axis=0, keepdims=True)

    return pl.pallas_call(
        body, name="ple_bwd", grid=(t // tm,),
        in_specs=[_rows(tm, d), _rows(tm, d), _whole(g), _rows(tm, d), _rows(tm, p.shape[1]), _wspec(wg), _wspec(wp)],
        out_specs=[_rows(tm, d), _rows(tm, d), _rows(tm, d), pl.BlockSpec((1, d), lambda i: (0, 0))],
        out_shape=[jax.ShapeDtypeStruct((t, d), F32), jax.ShapeDtypeStruct((t, d), BF16),
                   jax.ShapeDtypeStruct((t, d), BF16), jax.ShapeDtypeStruct((1, d), F32)],
        compiler_params=_row_params("arbitrary"),
    )(dh, h, g, n, p, wg.arr, wp.arr)


def _loss_head(h, g, tgt):
    t, d = h.shape
    tm = _tile(t, ROW_TILE)

    def body(h_ref, g_ref, t_ref, l_ref, dh_ref, dg_ref):
        xh, r = _rms(h_ref[...])
        gg = g_ref[...]
        e = xh * gg - t_ref[...]
        dy = e * (1.0 / d)

        @pl.when(pl.program_id(0) == 0)
        def _():
            l_ref[...] = jnp.zeros_like(l_ref)
            dg_ref[...] = jnp.zeros_like(dg_ref)

        l_ref[...] += 0.5 * jnp.sum(jnp.mean(e * e, axis=-1, keepdims=True), axis=0, keepdims=True)
        dg_ref[...] += jnp.sum(dy * xh, axis=0, keepdims=True)
        dh_ref[...] = _rms_bwd(dy, xh, r, gg)

    return pl.pallas_call(
        body, name="loss_head", grid=(t // tm,),
        in_specs=[_rows(tm, d), _whole(g), _rows(tm, d)],
        out_specs=[pl.BlockSpec((1, 128), lambda i: (0, 0)), _rows(tm, d), pl.BlockSpec((1, d), lambda i: (0, 0))],
        out_shape=[jax.ShapeDtypeStruct((1, 128), F32), jax.ShapeDtypeStruct((t, d), F32), jax.ShapeDtypeStruct((1, d), F32)],
        compiler_params=_row_params("arbitrary"),
    )(h, g, tgt)


def _s5_disc_math(a_re, a_im, log_dt):
    lam_re = jnp.minimum(a_re, -1e-4)
    lam_im = a_im
    dt = jnp.exp(log_dt)
    mag = jnp.exp(lam_re * dt)
    abar_re = mag * jnp.cos(lam_im * dt)
    abar_im = mag * jnp.sin(lam_im * dt)
    den = lam_re * lam_re + lam_im * lam_im
    nr = abar_re - 1.0
    ni = abar_im
    return abar_re, abar_im, (nr * lam_re + ni * lam_im) / den, (ni * lam_re - nr * lam_im) / den


def _s5_disc(a_re, a_im, log_dt):
    gp = jax.ShapeDtypeStruct(a_re.shape, F32)

    def body(ar_ref, ai_ref, ld_ref, o0, o1, o2, o3):
        outs = _s5_disc_math(ar_ref[...], ai_ref[...], ld_ref[...])
        for o_ref, val in zip((o0, o1, o2, o3), outs):
            o_ref[...] = val

    return pl.pallas_call(body, name="s5_disc", out_shape=[gp] * 4)(a_re, a_im, log_dt)


def _s5_disc_bwd(a_re, a_im, log_dt, cts):
    def body(ar_ref, ai_ref, ld_ref, c0, c1, c2, c3, dar_ref, dai_ref, dld_ref):
        _, vjp = jax.vjp(_s5_disc_math, ar_ref[...], ai_ref[...], ld_ref[...])
        dar, dai, dld = vjp((c0[...], c1[...], c2[...], c3[...]))
        dar_ref[...] = dar
        dai_ref[...] = dai
        dld_ref[...] = dld

    return pl.pallas_call(
        body, name="s5_disc_bwd",
        out_shape=[jax.ShapeDtypeStruct(a_re.shape, F32), jax.ShapeDtypeStruct(a_im.shape, F32),
                   jax.ShapeDtypeStruct(log_dt.shape, F32)],
    )(a_re, a_im, log_dt, *cts)


def _blockdiag(w, gl):
    g, r, c = w.shape
    w = w.reshape(g // gl, gl, r, c)
    eye = jnp.eye(gl, dtype=w.dtype)
    return (w[:, :, :, None, :] * eye[None, :, None, :, None]).reshape(g // gl, gl * r, gl * c)


def _blockdiag_take(m, gl, r, c):
    nb = m.shape[0]
    m = m.reshape(nb, gl, r, gl, c)
    return jnp.stack([m[:, g, :, g, :] for g in range(gl)], axis=1).reshape(nb * gl, r, c)


def _scan_rows(lc, step, carry, reverse=False):
    def blk(i, cs):
        i = (lc // SUBLANES - 1 - i) if reverse else i
        return step(pl.multiple_of(i * SUBLANES, SUBLANES), cs)

    return lax.fori_loop(0, lc // SUBLANES, blk, carry)


def _s5_scan_fwd(u, bre, bim, cre, cim, par, dskip, bl, seq, lc):
    t, d = u.shape
    nb, gw, ns = bre.shape
    nc = seq // lc

    def body(u_ref, bre_ref, bim_ref, cre_ref, cim_ref, par_ref, d_ref, y_ref, z_ref, st_ref, carry, sre, sim):
        @pl.when(pl.program_id(2) == 0)
        def _():
            carry[...] = jnp.zeros_like(carry)

        st_ref[...] = carry[...]
        uu = u_ref[...]
        ug = uu.astype(BF16)
        wre = _dot(ug, bre_ref[...])
        wim = _dot(ug, bim_ref[...])
        ar, ai = par_ref[0:1, :], par_ref[1:2, :]
        fr, fi = par_ref[2:3, :], par_ref[3:4, :]
        sre[...] = fr * wre - fi * wim
        sim[...] = fr * wim + fi * wre

        def step(base, cs):
            cr, ci = cs
            tr = sre[pl.ds(base, SUBLANES), :]
            ti = sim[pl.ds(base, SUBLANES), :]
            rows = lax.broadcasted_iota(jnp.int32, tr.shape, 0)
            outr, outi = tr, ti
            for k in range(SUBLANES):
                nr = ar * cr - ai * ci + tr[k:k + 1, :]
                ni = ar * ci + ai * cr + ti[k:k + 1, :]
                outr = jnp.where(rows == k, nr, outr)
                outi = jnp.where(rows == k, ni, outi)
                cr, ci = nr, ni
            sre[pl.ds(base, SUBLANES), :] = outr
            sim[pl.ds(base, SUBLANES), :] = outi
            return cr, ci

        cr, ci = _scan_rows(lc, step, (carry[0:1, :], carry[1:2, :]))
        carry[0:1, :] = cr
        carry[1:2, :] = ci
        y = _dot(sre[...].astype(BF16), cre_ref[...]) - _dot(sim[...].astype(BF16), cim_ref[...]) + d_ref[...] * uu
        y_ref[...] = y
        z_ref[...] = _gelu(y).astype(BF16)

    tok = pl.BlockSpec((lc, gw), lambda g, b, c: (b * nc + c, g))
    mat_b = pl.BlockSpec((None, gw, ns), lambda g, b, c: (g, 0, 0))
    mat_c = pl.BlockSpec((None, ns, gw), lambda g, b, c: (g, 0, 0))
    return pl.pallas_call(
        body, name="s5_scan_fwd", grid=(nb, bl, nc),
        in_specs=[tok, mat_b, mat_b, mat_c, mat_c, pl.BlockSpec((None, 8, ns), lambda g, b, c: (g, 0, 0)),
                  pl.BlockSpec((1, gw), lambda g, b, c: (0, g))],
        out_specs=[tok, tok, pl.BlockSpec((None, None, 2, ns), lambda g, b, c: (g, b * nc + c, 0, 0))],
        out_shape=[jax.ShapeDtypeStruct((t, d), F32), jax.ShapeDtypeStruct((t, d), BF16),
                   jax.ShapeDtypeStruct((nb, bl * nc, 2, ns), F32)],
        scratch_shapes=[pltpu.VMEM((2, ns), F32), pltpu.VMEM((lc, ns), F32), pltpu.VMEM((lc, ns), F32)],
        compiler_params=pltpu.CompilerParams(dimension_semantics=("parallel", "arbitrary", "arbitrary"),
                                             vmem_limit_bytes=V7X_VMEM_LIMIT),
    )(u, bre, bim, cre, cim, par, dskip)


def _s5_scan_bwd(u, dy, st, bre, bim, cre, cim, par, dskip, bl, seq, lc):
    t, d = u.shape
    nb, gw, ns = bre.shape
    nc = seq // lc

    def body(u_ref, dy_ref, st_ref, bre_ref, bim_ref, cre_ref, cim_ref, par_ref, d_ref,
             du_ref, dbre_ref, dbim_ref, dcre_ref, dcim_ref, dpar_ref, dd_ref,
             lcarry, sre, sim, pre, pim, wre_s, wim_s, lre, lim):
        b, c = pl.program_id(1), pl.program_id(2)

        @pl.when((b == 0) & (c == 0))
        def _():
            for ref in (dbre_ref, dbim_ref, dcre_ref, dcim_ref, dpar_ref, dd_ref):
                ref[...] = jnp.zeros_like(ref)

        @pl.when(c == 0)
        def _():
            lcarry[...] = jnp.zeros_like(lcarry)

        uu = u_ref[...]
        ug = uu.astype(BF16)
        dyv = dy_ref[...]
        dyb = dyv.astype(BF16)
        ar, ai = par_ref[0:1, :], par_ref[1:2, :]
        fr, fi = par_ref[2:3, :], par_ref[3:4, :]
        wre = _dot(ug, bre_ref[...])
        wim = _dot(ug, bim_ref[...])
        wre_s[...] = wre
        wim_s[...] = wim
        sre[...] = fr * wre - fi * wim
        sim[...] = fr * wim + fi * wre

        def fstep(base, cs):
            cr, ci = cs
            tr = sre[pl.ds(base, SUBLANES), :]
            ti = sim[pl.ds(base, SUBLANES), :]
            rows = lax.broadcasted_iota(jnp.int32, tr.shape, 0)
            outr, outi, prr, pri = tr, ti, tr, ti
            for k in range(SUBLANES):
                prr = jnp.where(rows == k, cr, prr)
                pri = jnp.where(rows == k, ci, pri)
                nr = ar * cr - ai * ci + tr[k:k + 1, :]
                ni = ar * ci + ai * cr + ti[k:k + 1, :]
                outr = jnp.where(rows == k, nr, outr)
                outi = jnp.where(rows == k, ni, outi)
                cr, ci = nr, ni
            sre[pl.ds(base, SUBLANES), :] = outr
            sim[pl.ds(base, SUBLANES), :] = outi
            pre[pl.ds(base, SUBLANES), :] = prr
            pim[pl.ds(base, SUBLANES), :] = pri
            return cr, ci

        _scan_rows(lc, fstep, (st_ref[0:1, :], st_ref[1:2, :]))

        lre[...] = _dot_nt(dyb, cre_ref[...])
        lim[...] = -_dot_nt(dyb, cim_ref[...])

        def bstep(base, cs):
            cr, ci = cs
            tr = lre[pl.ds(base, SUBLANES), :]
            ti = lim[pl.ds(base, SUBLANES), :]
            rows = lax.broadcasted_iota(jnp.int32, tr.shape, 0)
            outr, outi = tr, ti
            for k in range(SUBLANES - 1, -1, -1):
                nr = tr[k:k + 1, :] + ar * cr + ai * ci
                ni = ti[k:k + 1, :] + ar * ci - ai * cr
                outr = jnp.where(rows == k, nr, outr)
                outi = jnp.where(rows == k, ni, outi)
                cr, ci = nr, ni
            lre[pl.ds(base, SUBLANES), :] = outr
            lim[pl.ds(base, SUBLANES), :] = outi
            return cr, ci

        cr, ci = _scan_rows(lc, bstep, (lcarry[0:1, :], lcarry[1:2, :]), reverse=True)
        lcarry[0:1, :] = cr
        lcarry[1:2, :] = ci

        lr, li = lre[...], lim[...]
        spr, spi = pre[...], pim[...]
        wr, wi = wre_s[...], wim_s[...]
        dpar_ref[0:1, :] += jnp.sum(lr * spr + li * spi, axis=0, keepdims=True)
        dpar_ref[1:2, :] += jnp.sum(li * spr - lr * spi, axis=0, keepdims=True)
        dpar_ref[2:3, :] += jnp.sum(lr * wr + li * wi, axis=0, keepdims=True)
        dpar_ref[3:4, :] += jnp.sum(li * wr - lr * wi, axis=0, keepdims=True)
        dwr = (fr * lr + fi * li).astype(BF16)
        dwi = (fr * li - fi * lr).astype(BF16)
        dsk = d_ref[...]
        du_ref[...] = _dot_nt(dwr, bre_ref[...]) + _dot_nt(dwi, bim_ref[...]) + dsk * dyv
        dd_ref[...] += jnp.sum(dyv * uu, axis=0, keepdims=True)
        dbre_ref[...] += _dot_tn(ug, dwr)
        dbim_ref[...] += _dot_tn(ug, dwi)
        dcre_ref[...] += _dot_tn(sre[...].astype(BF16), dyb)
        dcim_ref[...] -= _dot_tn(sim[...].astype(BF16), dyb)

    tok = pl.BlockSpec((lc, gw), lambda g, b, c: (b * nc + nc - 1 - c, g))
    mat_b = pl.BlockSpec((None, gw, ns), lambda g, b, c: (g, 0, 0))
    mat_c = pl.BlockSpec((None, ns, gw), lambda g, b, c: (g, 0, 0))
    rows8 = pl.BlockSpec((None, 8, ns), lambda g, b, c: (g, 0, 0))
    dvec = pl.BlockSpec((1, gw), lambda g, b, c: (0, g))
    tile = pltpu.VMEM((lc, ns), F32)
    return pl.pallas_call(
        body, name="s5_scan_bwd", grid=(nb, bl, nc),
        in_specs=[tok, tok, pl.BlockSpec((None, None, 2, ns), lambda g, b, c: (g, b * nc + nc - 1 - c, 0, 0)),
                  mat_b, mat_b, mat_c, mat_c, rows8, dvec],
        out_specs=[tok, mat_b, mat_b, mat_c, mat_c, rows8, dvec],
        out_shape=[jax.ShapeDtypeStruct((t, d), F32),
                   jax.ShapeDtypeStruct((nb, gw, ns), F32), jax.ShapeDtypeStruct((nb, gw, ns), F32),
                   jax.ShapeDtypeStruct((nb, ns, gw), F32), jax.ShapeDtypeStruct((nb, ns, gw), F32),
                   jax.ShapeDtypeStruct((nb, 8, ns), F32), jax.ShapeDtypeStruct((1, d), F32)],
        scratch_shapes=[pltpu.VMEM((2, ns), F32)] + [tile] * 8,
        compiler_params=pltpu.CompilerParams(dimension_semantics=("arbitrary", "arbitrary", "arbitrary"),
                                             vmem_limit_bytes=V7X_VMEM_LIMIT),
    )(u, dy, st, bre, bim, cre, cim, par, dskip)


def _s5_out(h, z, wglu):
    t, d = h.shape
    tm = _tile(t, ROW_TILE)

    def body(h_ref, z_ref, w_ref, o_ref):
        zz = _dot(z_ref[...], w_ref[...])
        o_ref[...] = h_ref[...] + zz[:, :d] * jax.nn.sigmoid(zz[:, d:])

    return pl.pallas_call(
        body, name="s5_out", grid=(t // tm,),
        in_specs=[_rows(tm, d), _rows(tm, d), _wspec(wglu)],
        out_specs=_rows(tm, d), out_shape=jax.ShapeDtypeStruct((t, d), F32),
        compiler_params=_row_params(),
    )(h, z, wglu.arr)


def _s5_out_bwd(dh, z, y, wglu):
    t, d = dh.shape
    tm = _tile(t, ROW_TILE)

    def body(dh_ref, z_ref, y_ref, w_ref, dy_ref, dzz_ref):
        zz = _dot(z_ref[...], w_ref[...])
        out, sg = zz[:, :d], jax.nn.sigmoid(zz[:, d:])
        dh_v = dh_ref[...]
        dzz = jnp.concatenate([dh_v * sg, dh_v * out * sg * (1.0 - sg)], axis=1).astype(BF16)
        dzz_ref[...] = dzz
        dy_ref[...] = _dot_nt(dzz, w_ref[...]) * _gelu_grad(y_ref[...])

    return pl.pallas_call(
        body, name="s5_out_bwd", grid=(t // tm,),
        in_specs=[_rows(tm, d), _rows(tm, d), _rows(tm, d), _wspec(wglu)],
        out_specs=[_rows(tm, d), _rows(tm, 2 * d)],
        out_shape=[jax.ShapeDtypeStruct((t, d), F32), jax.ShapeDtypeStruct((t, 2 * d), BF16)],
        compiler_params=_row_params(),
    )(dh, z, y, wglu.arr)


def _head_spec(seq, w):
    return pl.BlockSpec((None, SB_HEADS_PER_STEP, seq, w), lambda b, h: (b, h, 0, 0))


def _tri_and_ones(kind):
    row = lax.broadcasted_iota(jnp.int32, (Q_BLOCK, Q_BLOCK), 0)
    col = lax.broadcasted_iota(jnp.int32, (Q_BLOCK, Q_BLOCK), 1)
    tri = {"after": row > col, "upto": row <= col, "before": row < col}[kind]
    return jnp.concatenate([tri.astype(BF16), jnp.ones((Q_BLOCK, Q_BLOCK), BF16)], axis=1)


def _sb_fwd(q, k, v):
    bsz, nh, seq, dh = q.shape
    tq = min(SB_Q_TILE, seq // 2)
    nq, nsub = seq // tq, tq // Q_BLOCK
    scale = dh ** -0.5
    hp = SB_HEADS_PER_STEP
    qb = Q_BLOCK

    def body(q_ref, k_ref, v_ref, o_ref, tot_ref):
        strict = lax.broadcasted_iota(jnp.int32, (qb, qb), 1) < lax.broadcasted_iota(jnp.int32, (qb, qb), 0)
        sums = _tri_and_ones("after")

        def sweep(qs, c0, units, carry):
            heads = sorted({u[0] for u in units})
            kbs = {hh: k_ref[hh, pl.ds(c0, qb), :] for hh in heads}
            vbs = {hh: v_ref[hh, pl.ds(c0, qb), :] for hh in heads}
            zs = [_dot_nt(qs[hh][s], kbs[hh]) * scale for hh, s, _ in units]
            lkrs = [_neg_softplus(z) for z in zs]
            lks = [jnp.where(strict, lkr, 0.0) if dg else lkr for lkr, (_, _, dg) in zip(lkrs, units)]
            css = [_dot2(lk, sums) for lk in lks]
            carry = dict(carry)
            for (hh, s, dg), z, lkr, cs in zip(units, zs, lkrs, css):
                acc, run = carry[hh, s]
                att = jnp.exp(z + lkr + cs[:, :qb] + run)
                if dg:
                    att = jnp.where(strict, att, 0.0)
                carry[hh, s] = (acc + _dot(att.astype(BF16), vbs[hh]), run + cs[:, qb:])
            return carry

        def q_loop(qi, _):
            r0 = qi * tq
            qs = {hh: [q_ref[hh, pl.ds(pl.multiple_of(r0 + s * qb, qb), qb), :] for s in range(nsub)] for hh in range(hp)}
            carry = {(hh, s): (jnp.zeros((qb, dh), F32), jnp.zeros((qb, qb), F32)) for hh in range(hp) for s in range(nsub)}
            for jj in reversed(range(nsub)):
                units = [(hh, s, s == jj) for hh in range(hp) for s in range(jj, nsub)]
                carry = sweep(qs, pl.multiple_of(r0 + jj * qb, qb), units, carry)
            units = [(hh, s, False) for hh in range(hp) for s in range(nsub)]
            carry = lax.fori_loop(
                0, nsub * qi, lambda t, c: sweep(qs, pl.multiple_of((nsub * qi - 1 - t) * qb, qb), units, c), carry)
            for (hh, s), (acc, run) in carry.items():
                rows = pl.ds(pl.multiple_of(r0 + s * qb, qb), qb)
                o_ref[hh, rows, :] = acc.astype(BF16)
                tot_ref[hh, rows, :] = run[:, 0:1]
            return 0

        lax.fori_loop(0, nq, q_loop, 0)

    return pl.pallas_call(
        body, name="sb_fwd", grid=(bsz, nh // hp),
        in_specs=[_head_spec(seq, dh)] * 3,
        out_specs=[_head_spec(seq, dh), _head_spec(seq, 1)],
        out_shape=[jax.ShapeDtypeStruct((bsz, nh, seq, dh), BF16), jax.ShapeDtypeStruct((bsz, nh, seq, 1), F32)],
        compiler_params=pltpu.CompilerParams(dimension_semantics=("parallel", "parallel"), vmem_limit_bytes=V7X_VMEM_LIMIT),
    )(q, k, v)


def _sb_bwd(q, k, v, do, tot):
    bsz, nh, seq, dh = q.shape
    tq = min(SB_Q_TILE, seq // 2)
    nq, nsub = seq // tq, tq // Q_BLOCK
    scale = dh ** -0.5
    hp = SB_HEADS_PER_STEP
    qb = Q_BLOCK

    def body(q_ref, k_ref, v_ref, do_ref, tot_ref, dq_ref, dk_ref, dv_ref, dka, dva):
        dka[...] = jnp.zeros_like(dka)
        dva[...] = jnp.zeros_like(dva)
        strict = lax.broadcasted_iota(jnp.int32, (qb, qb), 1) < lax.broadcasted_iota(jnp.int32, (qb, qb), 0)
        upto = _tri_and_ones("upto")
        before = _tri_and_ones("before")

        def sweep(qf, dof, tots, c0, first, units, carry):
            heads = sorted({u[0] for u in units})
            kbs = {hh: k_ref[hh, pl.ds(c0, qb), :] for hh in heads}
            vbs = {hh: v_ref[hh, pl.ds(c0, qb), :] for hh in heads}
            sub = lambda a, s: a[s * qb:(s + 1) * qb, :]
            zs = [_dot_nt(sub(qf[hh], s), kbs[hh]) * scale for hh, s, _ in units]
            das = [_dot_nt(sub(dof[hh], s), vbs[hh]) for hh, s, _ in units]
            lkrs = [_neg_softplus(z) for z in zs]
            lks = [jnp.where(strict, lkr, 0.0) if dg else lkr for lkr, (_, _, dg) in zip(lkrs, units)]
            css = [_dot2(lk, upto) for lk in lks]
            lsigs, atts, gls = [], [], []
            for (hh, s, dg), z, lkr, cs, da in zip(units, zs, lkrs, css, das):
                lsig = z + lkr
                att = jnp.exp(lsig + (tots[hh, s] - (cs[:, :qb] + carry[hh, s][1])))
                if dg:
                    att = jnp.where(strict, att, 0.0)
                lsigs.append(lsig)
                atts.append(att)
                gls.append(da * att)
            gss = [_dot2(gl, before) for gl in gls]
            carry = dict(carry)
            dzs = {}
            for (hh, s, dg), lsig, gl, cs, gs in zip(units, lsigs, gls, css, gss):
                dqa, pre, gpre = carry[hh, s]
                sig = jnp.exp(lsig)
                dz = gl * (1.0 - sig) - (gs[:, :qb] + gpre) * sig
                if dg:
                    dz = jnp.where(strict, dz, 0.0)
                dz = (dz * scale).astype(BF16)
                dzs[hh, s] = dz
                carry[hh, s] = (dqa + _dot(dz, kbs[hh]), pre + cs[:, qb:], gpre + gs[:, qb:])
            att_of = {(hh, s): a for (hh, s, _), a in zip(units, atts)}
            for hh in heads:
                dzc = jnp.concatenate([dzs[hh, s] for s in range(first, nsub)], axis=0)
                attc = jnp.concatenate([att_of[hh, s].astype(BF16) for s in range(first, nsub)], axis=0)
                dka[hh, pl.ds(c0, qb), :] += _dot_tn(dzc, qf[hh][first * qb:, :])
                dva[hh, pl.ds(c0, qb), :] += _dot_tn(attc, dof[hh][first * qb:, :])
            return carry

        def q_loop(qi, _):
            r0 = pl.multiple_of(qi * tq, tq)
            qf = [q_ref[hh, pl.ds(r0, tq), :] for hh in range(hp)]
            dof = [do_ref[hh, pl.ds(r0, tq), :] for hh in range(hp)]
            tots = {(hh, s): jnp.broadcast_to(tot_ref[hh, pl.ds(pl.multiple_of(r0 + s * qb, qb), qb), :], (qb, qb))
                    for hh in range(hp) for s in range(nsub)}
            carry = {(hh, s): (jnp.zeros((qb, dh), F32), jnp.zeros((qb, qb), F32), jnp.zeros((qb, qb), F32))
                     for hh in range(hp) for s in range(nsub)}
            units = [(hh, s, False) for hh in range(hp) for s in range(nsub)]
            carry = lax.fori_loop(
                0, nsub * qi, lambda kj, c: sweep(qf, dof, tots, pl.multiple_of(kj * qb, qb), 0, units, c), carry)
            for jj in range(nsub):
                units = [(hh, s, s == jj) for hh in range(hp) for s in range(jj, nsub)]
                carry = sweep(qf, dof, tots, pl.multiple_of(r0 + jj * qb, qb), jj, units, carry)
            for (hh, s), (dqa, _, _) in carry.items():
                dq_ref[hh, pl.ds(pl.multiple_of(r0 + s * qb, qb), qb), :] = dqa.astype(BF16)
            return 0

        lax.fori_loop(0, nq, q_loop, 0)
        dk_ref[...] = dka[...].astype(BF16)
        dv_ref[...] = dva[...].astype(BF16)

    hs = _head_spec(seq, dh)
    return pl.pallas_call(
        body, name="sb_bwd", grid=(bsz, nh // hp),
        in_specs=[hs, hs, hs, hs, _head_spec(seq, 1)],
        out_specs=[hs, hs, hs],
        out_shape=[jax.ShapeDtypeStruct((bsz, nh, seq, dh), BF16)] * 3,
        scratch_shapes=[pltpu.VMEM((hp, seq, dh), F32), pltpu.VMEM((hp, seq, dh), F32)],
        compiler_params=pltpu.CompilerParams(dimension_semantics=("parallel", "parallel"), vmem_limit_bytes=V7X_VMEM_LIMIT),
    )(q, k, v, do, tot)


def _to_heads(a, bsz, seq):
    return a.reshape(bsz, seq, -1, HEAD_DIM).transpose(0, 2, 1, 3)


def _from_heads(a):
    bsz, nh, seq, dh = a.shape
    return a.transpose(0, 2, 1, 3).reshape(bsz * seq, nh * dh)


def _coords():
    return lax.axis_index("x"), lax.axis_index("y"), lax.axis_index("c")


def _all_gather(x, name):
    r, c = x.shape

    def body(x_ref, out_ref, send_sems, recv_sems, local_sem):
        mx, my, mc = _coords()
        me, sibling = (mx, my, mc), (mx, my, 1 - mc)
        chips = [(1 - mx, my), (mx, 1 - my), (1 - mx, 1 - my)]

        def blk(px, py, pc):
            return out_ref.at[4 * px + 2 * py + pc]

        def copy(k, block, to, src=None):
            return pltpu.make_async_remote_copy(
                src_ref=blk(*block) if src is None else src, dst_ref=blk(*block),
                send_sem=send_sems.at[k], recv_sem=recv_sems.at[k], device_id=to, device_id_type=MESH)

        mine = pltpu.make_async_copy(x_ref, blk(*me), local_sem)
        mine.start()
        first = [copy(0, me, sibling, src=x_ref)]
        first += [copy(1 + j, me, (*chip, mc), src=x_ref) for j, chip in enumerate(chips)]
        for cp in first:
            cp.start()
        passed = [copy(4 + j, (*chip, mc), sibling) for j, chip in enumerate(chips)]
        for j, chip in enumerate(chips):
            copy(1 + j, (*chip, mc), me).wait_recv()
            passed[j].start()
        copy(0, sibling, me).wait_recv()
        for j, chip in enumerate(chips):
            copy(4 + j, (*chip, 1 - mc), me).wait_recv()
        for cp in first + passed:
            cp.wait_send()
        mine.wait()

    return pl.pallas_call(
        body, name=name,
        out_shape=jax.ShapeDtypeStruct((N_DEV, r, c), x.dtype),
        in_specs=[pl.BlockSpec(memory_space=pl.ANY)],
        out_specs=pl.BlockSpec(memory_space=pl.ANY),
        scratch_shapes=[pltpu.SemaphoreType.DMA((7,)), pltpu.SemaphoreType.DMA((7,)), pltpu.SemaphoreType.DMA],
    )(x)


_ANY = pl.BlockSpec(memory_space=pl.ANY)


def _window(ref, shard_shape, by_cols, dev):
    if by_cols:
        n = shard_shape[2]
        return ref.at[:, :, pl.ds(pl.multiple_of(dev * n, n), n)]
    k = shard_shape[1]
    return ref.at[:, pl.ds(pl.multiple_of(dev * k, k), k), :]


def _gather_weights(shards, by_cols):
    na = len(shards)
    fulls = [jax.ShapeDtypeStruct(
        (s.shape[0], s.shape[1], s.shape[2] * N_DEV) if c else (s.shape[0], s.shape[1] * N_DEV, s.shape[2]), s.dtype)
        for s, c in zip(shards, by_cols)]

    def body(*refs):
        x_refs, out_refs = refs[:na], refs[na:2 * na]
        send_sems, recv_sems, local_sems = refs[2 * na:]
        mx, my, mc = _coords()
        me, sibling = (mx, my, mc), (mx, my, 1 - mc)
        chips = [(1 - mx, my), (mx, 1 - my), (1 - mx, 1 - my)]

        def blk(a, dev):
            px, py, pc = dev
            return _window(out_refs[a], x_refs[a].shape, by_cols[a], 4 * px + 2 * py + pc)

        def copy(a, k, block, to, src=None):
            return pltpu.make_async_remote_copy(
                src_ref=blk(a, block) if src is None else src, dst_ref=blk(a, block),
                send_sem=send_sems.at[a, k], recv_sem=recv_sems.at[a, k], device_id=to, device_id_type=MESH)

        mines = [pltpu.make_async_copy(x_refs[a], blk(a, me), local_sems.at[a]) for a in range(na)]
        first = [[copy(a, 0, me, sibling, src=x_refs[a])]
                 + [copy(a, 1 + j, me, (*chip, mc), src=x_refs[a]) for j, chip in enumerate(chips)] for a in range(na)]
        passed = [[copy(a, 4 + j, (*chip, mc), sibling) for j, chip in enumerate(chips)] for a in range(na)]
        for a in range(na):
            mines[a].start()
            for cp in first[a]:
                cp.start()
        for a in range(na):
            for j, chip in enumerate(chips):
                copy(a, 1 + j, (*chip, mc), me).wait_recv()
                passed[a][j].start()
        for a in range(na):
            copy(a, 0, sibling, me).wait_recv()
            for j, chip in enumerate(chips):
                copy(a, 4 + j, (*chip, 1 - mc), me).wait_recv()
        for a in range(na):
            for cp in first[a] + passed[a]:
                cp.wait_send()
            mines[a].wait()

    return pl.pallas_call(
        body, name="gather_weights", out_shape=fulls, in_specs=[_ANY] * na, out_specs=[_ANY] * na,
        scratch_shapes=[pltpu.SemaphoreType.DMA((na, 7)), pltpu.SemaphoreType.DMA((na, 7)), pltpu.SemaphoreType.DMA((na,))],
    )(*shards)


def _rs_sibling(grads, shard_shapes, by_cols):
    na = len(grads)
    slabs = [jax.ShapeDtypeStruct((N_CHIP,) + tuple(s), F32) for s in shard_shapes]

    def body(*refs):
        g_refs, own_refs, got_refs = refs[:na], refs[na:2 * na], refs[2 * na:3 * na]
        send_sems, recv_sems, local_sems = refs[3 * na:]
        mx, my, mc = _coords()
        copies = []
        for a in range(na):
            for q in range(N_CHIP):
                mine = _window(g_refs[a], shard_shapes[a], by_cols[a], 2 * q + mc)
                theirs = _window(g_refs[a], shard_shapes[a], by_cols[a], 2 * q + 1 - mc)
                copies.append(pltpu.make_async_copy(mine, own_refs[a].at[q], local_sems.at[a, q]))
                copies.append(pltpu.make_async_remote_copy(
                    src_ref=theirs, dst_ref=got_refs[a].at[q], send_sem=send_sems.at[a, q], recv_sem=recv_sems.at[a, q],
                    device_id=(mx, my, 1 - mc), device_id_type=MESH))
        for cp in copies:
            cp.start()
        for cp in copies:
            cp.wait()

    sems = pltpu.SemaphoreType.DMA((na, N_CHIP))
    return pl.pallas_call(
        body, name="rs_sibling", out_shape=slabs + slabs, in_specs=[_ANY] * na, out_specs=[_ANY] * (2 * na),
        scratch_shapes=[sems, sems, sems],
    )(*grads)


def _rs_add(own, got, name):
    nq, nl, r, c = own.shape
    tr = _tile(r, 256, 2 * SUBLANES)

    def body(a_ref, b_ref, o_ref):
        o_ref[...] = (a_ref[...] + b_ref[...]).astype(BF16)

    spec = pl.BlockSpec((None, None, tr, c), lambda q, l, i: (q, l, i, 0))
    return pl.pallas_call(
        body, name=name, grid=(nq, nl, r // tr), in_specs=[spec, spec], out_specs=spec,
        out_shape=jax.ShapeDtypeStruct(own.shape, BF16),
        compiler_params=pltpu.CompilerParams(dimension_semantics=("parallel",) * 3),
    )(own, got)


def _rs_chips(parts):
    na = len(parts)

    def body(*refs):
        p_refs, out_refs = refs[:na], refs[na:2 * na]
        send_sems, recv_sems, local_sems = refs[2 * na:]
        mx, my, mc = _coords()
        here = 2 * mx + my
        chips = [(1 - mx, my), (mx, 1 - my), (1 - mx, 1 - my)]
        copies = []
        for a in range(na):
            copies.append(pltpu.make_async_copy(p_refs[a].at[here], out_refs[a].at[here], local_sems.at[a]))
            for j, (cx, cy) in enumerate(chips):
                copies.append(pltpu.make_async_remote_copy(
                    src_ref=p_refs[a].at[2 * cx + cy], dst_ref=out_refs[a].at[here],
                    send_sem=send_sems.at[a, j], recv_sem=recv_sems.at[a, j], device_id=(cx, cy, mc), device_id_type=MESH))
        for cp in copies:
            cp.start()
        for cp in copies:
            cp.wait()

    return pl.pallas_call(
        body, name="rs_chips", out_shape=[jax.ShapeDtypeStruct(p.shape, p.dtype) for p in parts],
        in_specs=[_ANY] * na, out_specs=[_ANY] * na,
        scratch_shapes=[pltpu.SemaphoreType.DMA((na, 3)), pltpu.SemaphoreType.DMA((na, 3)), pltpu.SemaphoreType.DMA((na,))],
    )(*parts)


def _adamw(parts, w, m, v, name):
    n, r, c = parts.shape
    tr = _tile(r, 320, 2 * SUBLANES)
    bc1 = 1.0 - ADAM_B1 ** ADAM_STEP
    bc2 = 1.0 - ADAM_B2 ** ADAM_STEP

    def body(p_ref, w_ref, m_ref, v_ref, g_ref, d_ref, mo_ref, vo_ref):
        g = p_ref[0].astype(F32)
        for j in range(1, n):
            g = g + p_ref[j].astype(F32)
        mn = ADAM_B1 * m_ref[...] + (1.0 - ADAM_B1) * g
        vn = ADAM_B2 * v_ref[...] + (1.0 - ADAM_B2) * (g * g)
        g_ref[...] = g
        mo_ref[...] = mn
        vo_ref[...] = vn
        d_ref[...] = -ADAM_LR * ((mn / bc1) / (jnp.sqrt(vn / bc2) + ADAM_EPS) + ADAM_WD * w_ref[...])

    flat = pl.BlockSpec((tr, c), lambda i: (i, 0))
    return pl.pallas_call(
        body, name=name, grid=(r // tr,),
        in_specs=[pl.BlockSpec((n, tr, c), lambda i: (0, i, 0)), flat, flat, flat],
        out_specs=[flat] * 4, out_shape=[jax.ShapeDtypeStruct((r, c), F32)] * 4,
        compiler_params=pltpu.CompilerParams(dimension_semantics=("parallel",)),
    )(parts, w, m, v)


def _adam_math(g, w, m, v):
    bc1 = 1.0 - ADAM_B1 ** ADAM_STEP
    bc2 = 1.0 - ADAM_B2 ** ADAM_STEP
    mn = ADAM_B1 * m + (1.0 - ADAM_B1) * g
    vn = ADAM_B2 * v + (1.0 - ADAM_B2) * (g * g)
    return -ADAM_LR * ((mn / bc1) / (jnp.sqrt(vn / bc2) + ADAM_EPS) + ADAM_WD * w), mn, vn


def _adamw_shard(parts, w, m, v, name):
    nl, r, c = w.shape
    tr = _tile(r, 256, 2 * SUBLANES)

    def body(p_ref, w_ref, m_ref, v_ref, g_ref, d_ref, mo_ref, vo_ref):
        g = p_ref[0].astype(F32)
        for q in range(1, N_CHIP):
            g = g + p_ref[q].astype(F32)
        g = g[:, :c]
        g_ref[...] = g
        d_ref[...], mo_ref[...], vo_ref[...] = _adam_math(g, w_ref[...], m_ref[...], v_ref[...])

    native = pl.BlockSpec((None, tr, c), lambda l, i: (l, i, 0))
    return pl.pallas_call(
        body, name=name, grid=(nl, r // tr),
        in_specs=[pl.BlockSpec((N_CHIP, None, tr, parts.shape[3]), lambda l, i: (0, l, i, 0)), native, native, native],
        out_specs=[native] * 4, out_shape=[jax.ShapeDtypeStruct(w.shape, F32)] * 4,
        compiler_params=pltpu.CompilerParams(dimension_semantics=("parallel", "parallel"), vmem_limit_bytes=V7X_VMEM_LIMIT),
    )(parts, w, m, v)


def _pack(arrs, dtype=F32):
    cols = []
    for a in arrs:
        f = a.reshape(-1).astype(dtype)
        cols.append(jnp.pad(f, (0, -f.shape[0] % FLAT_W)))
    flat = jnp.concatenate(cols)
    flat = jnp.pad(flat, (0, -flat.shape[0] % (FLAT_W * SUBLANES)))
    return flat.reshape(-1, FLAT_W)


def _unpack(flat, shapes, lead=()):
    flat = flat.reshape(lead + (-1,))
    out, off = [], 0
    for s in shapes:
        n = math.prod(s)
        out.append(flat[..., off:off + n].reshape(lead + tuple(s)))
        off += n + (-n % FLAT_W)
    return out


def kernel(x, p, ffn1_norm, ffn1_w1, ffn1_w3, ffn1_w2, mix_norm, ffn2_norm, ffn2_w1, ffn2_w3, ffn2_w2, ple_norm, ple_proj, ple_gate, s5_w_in, s5_a_re, s5_a_im, s5_log_dt, s5_b_re, s5_b_im, s5_c_re, s5_c_im, s5_d, s5_w_glu, sb_w_qkv, sb_w_o, final_norm, loss_target, m_ffn1_norm, m_ffn1_w1, m_ffn1_w3, m_ffn1_w2, m_mix_norm, m_ffn2_norm, m_ffn2_w1, m_ffn2_w3, m_ffn2_w2, m_ple_norm, m_ple_proj, m_ple_gate, m_s5_w_in, m_s5_a_re, m_s5_a_im, m_s5_log_dt, m_s5_b_re, m_s5_b_im, m_s5_c_re, m_s5_c_im, m_s5_d, m_s5_w_glu, m_sb_w_qkv, m_sb_w_o, m_final_norm, v_ffn1_norm, v_ffn1_w1, v_ffn1_w3, v_ffn1_w2, v_mix_norm, v_ffn2_norm, v_ffn2_w1, v_ffn2_w3, v_ffn2_w2, v_ple_norm, v_ple_proj, v_ple_gate, v_s5_w_in, v_s5_a_re, v_s5_a_im, v_s5_log_dt, v_s5_b_re, v_s5_b_im, v_s5_c_re, v_s5_c_im, v_s5_d, v_s5_w_glu, v_sb_w_qkv, v_sb_w_o, v_final_norm):
    given = dict(locals())
    wts = {n: given[n] for n in WEIGHTS}
    mom = {n: given["m_" + n] for n in WEIGHTS}
    var = {n: given["v_" + n] for n in WEIGHTS}
    bl, seq, d = x.shape
    t = bl * seq
    depth = p.shape[0]

    hid_pad = -wts["ffn1_w1"].shape[-1] % LANES

    def padded(n):
        a = wts[n].astype(BF16)
        if n in FFN_COL:
            return jnp.pad(a, ((0, 0), (0, 0), (0, hid_pad)))
        if n in FFN_ROW:
            return jnp.pad(a, ((0, 0), (0, hid_pad), (0, 0)))
        return a

    sent = [padded(n) for n in SHARDED]
    by_cols = [n in COL_SHARDED for n in SHARDED]
    full = dict(zip(SHARDED, _gather_weights(sent, by_cols)))

    def row(a):
        return a.reshape(1, -1)

    def wt(n, layer=0, col=0, width=None):
        return _W(full[n], layer, col, width)

    gbuf = {}

    def wgrad(n, x_act, dy, layer=0, col=0, scale=1.0, tag=""):
        gbuf[n] = _wgrad(x_act, dy, f"{n}{tag}_grad", full[n].shape, gbuf.get(n), layer, col, scale)

    n_groups = d // S5_GROUP
    a_re, a_im = s5_a_re[0], s5_a_im[0]
    log_dt = s5_log_dt[0].reshape(n_groups, 1)
    disc = _s5_disc(a_re, a_im, log_dt)
    nb = n_groups // S5_BLOCK_GROUPS
    ns = S5_BLOCK_GROUPS * S5_STATE
    par = jnp.concatenate([jnp.stack([q.reshape(nb, ns) for q in disc], axis=1), jnp.zeros((nb, 4, ns), F32)], axis=1)
    bre = _blockdiag(s5_b_re[0].transpose(0, 2, 1), S5_BLOCK_GROUPS).astype(BF16)
    bim = _blockdiag(s5_b_im[0].transpose(0, 2, 1), S5_BLOCK_GROUPS).astype(BF16)
    cre = _blockdiag(s5_c_re[0].transpose(0, 2, 1), S5_BLOCK_GROUPS).astype(BF16)
    cim = _blockdiag(s5_c_im[0].transpose(0, 2, 1), S5_BLOCK_GROUPS).astype(BF16)
    lc = min(SCAN_CHUNK, seq)

    h = x.reshape(t, d)
    saved = []
    for i in range(depth):
        s = {"h0": h}
        h, s["n1"], s["a1"], s["b1"] = _ffn_fwd(h, row(ffn1_norm[i]), wt("ffn1_w1", i), wt("ffn1_w3", i), wt("ffn1_w2", i))
        s["h1"] = h
        j = i // 2
        if i % 2 == 0:
            s["hn"], s["u"] = _norm_lin(h, row(mix_norm[i]), [wt("s5_w_in", j)], [F32], "s5_in")
            s["y"], s["z"], s["st"] = _s5_scan_fwd(s["u"], bre, bim, cre, cim, par, row(s5_d[j]), bl, seq, lc)
            h = _s5_out(h, s["z"], wt("s5_w_glu", j))
        else:
            wqkv = [wt("sb_w_qkv", j, col, d) for col in range(3)]
            s["hn"], q, k, v = _norm_lin(h, row(mix_norm[i]), wqkv, [BF16] * 3, "sb_in")
            s["q"], s["k"], s["v"] = (_to_heads(a, bl, seq) for a in (q, k, v))
            o, s["tot"] = _sb_fwd(s["q"], s["k"], s["v"])
            s["o"] = _from_heads(o)
            h = _lin_res(h, s["o"], wt("sb_w_o", j), "sb_out")
        s["h2"] = h
        h, s["n2"], s["a2"], s["b2"] = _ffn_fwd(h, row(ffn2_norm[i]), wt("ffn2_w1", i), wt("ffn2_w3", i), wt("ffn2_w2", i))
        s["h3"] = h
        s["p"] = p[i].reshape(t, -1)
        h, s["npl"] = _ple_fwd(h, row(ple_norm[i]), wt("ple_gate", i), s["p"], wt("ple_proj", i))
        saved.append(s)

    loss_part, dh, g_final = _loss_head(h, row(final_norm), loss_target.reshape(t, d))
    loss = lax.psum(loss_part[0, 0], ("x", "y", "c"))

    grads = {n: [None] * wts[n].shape[0] for n in REPLICATED if n != "final_norm"}
    grads["final_norm"] = g_final.reshape(-1)

    def ffn_bwd(dh, h_in, n, a, b, which, i):
        da, db, sact = _ffn_bwd_down(dh, a, b, wt(f"{which}_w2", i))
        wgrad(f"{which}_w2", sact, dh, i, scale=0.5)
        dh_in, dg = _lin_bwd_norm([da, db], [wt(f"{which}_w1", i), wt(f"{which}_w3", i)], h_in,
                                  row(wts[f"{which}_norm"][i]), dh, f"{which}_bwd_up")
        wgrad(f"{which}_w1", n, da, i)
        wgrad(f"{which}_w3", n, db, i)
        grads[f"{which}_norm"][i] = dg.reshape(-1)
        return dh_in

    for i in reversed(range(depth)):
        s = saved[i]
        j = i // 2
        dh, dgl, dpp, dg = _ple_bwd(dh, s["h3"], row(ple_norm[i]), s["npl"], s["p"], wt("ple_gate", i), wt("ple_proj", i))
        grads["ple_norm"][i] = dg.reshape(-1)
        wgrad("ple_gate", s["npl"], dgl, i)
        wgrad("ple_proj", s["p"], dpp, i)
        dh = ffn_bwd(dh, s["h2"], s["n2"], s["a2"], s["b2"], "ffn2", i)
        if i % 2 == 0:
            dy, dzz = _s5_out_bwd(dh, s["z"], s["y"], wt("s5_w_glu", j))
            wgrad("s5_w_glu", s["z"], dzz, j)
            du, dbre, dbim, dcre, dcim, dpar, dd = _s5_scan_bwd(s["u"], dy, s["st"], bre, bim, cre, cim, par, row(s5_d[j]), bl, seq, lc)
            cts = [dpar[:, r, :].reshape(n_groups, S5_STATE) for r in range(4)]
            g_are, g_aim, g_ldt = _s5_disc_bwd(a_re, a_im, log_dt, cts)
            grads["s5_a_re"][j], grads["s5_a_im"][j], grads["s5_log_dt"][j] = g_are, g_aim, g_ldt.reshape(-1)
            grads["s5_b_re"][j] = _blockdiag_take(dbre, S5_BLOCK_GROUPS, S5_GROUP, S5_STATE).transpose(0, 2, 1)
            grads["s5_b_im"][j] = _blockdiag_take(dbim, S5_BLOCK_GROUPS, S5_GROUP, S5_STATE).transpose(0, 2, 1)
            grads["s5_c_re"][j] = _blockdiag_take(dcre, S5_BLOCK_GROUPS, S5_STATE, S5_GROUP).transpose(0, 2, 1)
            grads["s5_c_im"][j] = _blockdiag_take(dcim, S5_BLOCK_GROUPS, S5_STATE, S5_GROUP).transpose(0, 2, 1)
            grads["s5_d"][j] = dd.reshape(-1)
            dh, dg = _lin_bwd_norm([du], [wt("s5_w_in", j)], s["h1"], row(mix_norm[i]), dh, "s5_in_bwd")
            wgrad("s5_w_in", s["hn"], du, j)
        else:
            do = _lin_nt(dh, wt("sb_w_o", j), "sb_out_bwd")
            wgrad("sb_w_o", s["o"], dh, j)
            dqkv = _sb_bwd(s["q"], s["k"], s["v"], _to_heads(do, bl, seq), s["tot"])
            dqkv = [_from_heads(a) for a in dqkv]
            dh, dg = _lin_bwd_norm(dqkv, [wt("sb_w_qkv", j, col, d) for col in range(3)], s["h1"], row(mix_norm[i]), dh, "sb_in_bwd")
            for col, (a, tag) in enumerate(zip(dqkv, "qkv")):
                wgrad("sb_w_qkv", s["hn"], a, j, col, tag="_" + tag)
        grads["mix_norm"][i] = dg.reshape(-1)
        dh = ffn_bwd(dh, s["h0"], s["n1"], s["a1"], s["b1"], "ffn1", i)
    grad_x = dh.reshape(x.shape)
    for n in REPLICATED:
        if n != "final_norm":
            grads[n] = jnp.stack(grads[n])

    pair = _rs_sibling([gbuf[n] for n in SHARDED], [a.shape for a in sent], by_cols)
    own, got = pair[:len(SHARDED)], pair[len(SHARDED):]
    parts = _rs_chips([_rs_add(o, g, f"rs_add_{n}") for n, o, g in zip(SHARDED, own, got)])
    res = {n: _adamw_shard(part, wts[n], mom[n], var[n], f"adamw_{n}") for n, part in zip(SHARDED, parts)}

    rep_shapes = [wts[n].shape for n in REPLICATED]
    rep_parts = _all_gather(_pack([grads[n].reshape(wts[n].shape) for n in REPLICATED]), "gather_small_grads")
    outs = _adamw(rep_parts, _pack([wts[n] for n in REPLICATED]), _pack([mom[n] for n in REPLICATED]),
                  _pack([var[n] for n in REPLICATED]), "adamw_replicated")
    res.update({n: vals for n, vals in zip(REPLICATED, zip(*[_unpack(o, rep_shapes) for o in outs]))})

    return (loss, grad_x, *[res[n][0] for n in WEIGHTS], *[res[n][1] for n in WEIGHTS],
            *[res[n][2] for n in WEIGHTS], *[res[n][3] for n in WEIGHTS])
```

```python
import math
from typing import NamedTuple, Optional

import jax
import jax.numpy as jnp
from jax import lax
from jax.experimental import pallas as pl
from jax.experimental.pallas import tpu as pltpu

F32 = jnp.float32
BF16 = jnp.bfloat16
MESH = pl.DeviceIdType.MESH

N_DEV = 8
N_CHIP = 4
EPS = 1e-6
S5_GROUP = 16
S5_STATE = 64
S5_BLOCK_GROUPS = 16
HEAD_DIM = 64
Q_BLOCK = 128
SB_Q_TILE = 256
SB_HEADS_PER_STEP = 4
SCAN_CHUNK = 128
SUBLANES = 8
LANES = 128
FLAT_W = 1024
ADAM_LR, ADAM_B1, ADAM_B2, ADAM_EPS, ADAM_WD, ADAM_STEP = 1e-3, 0.9, 0.999, 1e-8, 0.01, 10
V7X_VMEM_LIMIT = 56 * 1024 * 1024
ROW_TILE = 256
WGRAD_ROW_TILE = 512
WGRAD_OUT_BYTES = 6 * 1024 * 1024

SHARDED = ("ffn1_w1", "ffn1_w3", "ffn1_w2", "ffn2_w1", "ffn2_w3", "ffn2_w2", "ple_proj", "ple_gate",
           "s5_w_in", "s5_w_glu", "sb_w_qkv", "sb_w_o")
COL_SHARDED = ("ffn1_w1", "ffn1_w3", "ffn2_w1", "ffn2_w3", "ple_proj", "s5_w_glu", "sb_w_qkv")
FFN_COL = ("ffn1_w1", "ffn1_w3", "ffn2_w1", "ffn2_w3")
FFN_ROW = ("ffn1_w2", "ffn2_w2")
REPLICATED = ("ffn1_norm", "mix_norm", "ffn2_norm", "ple_norm", "s5_a_re", "s5_a_im", "s5_log_dt",
              "s5_b_re", "s5_b_im", "s5_c_re", "s5_c_im", "s5_d", "final_norm")
WEIGHTS = ("ffn1_norm", "ffn1_w1", "ffn1_w3", "ffn1_w2", "mix_norm", "ffn2_norm", "ffn2_w1", "ffn2_w3", "ffn2_w2",
           "ple_norm", "ple_proj", "ple_gate", "s5_w_in", "s5_a_re", "s5_a_im", "s5_log_dt", "s5_b_re", "s5_b_im",
           "s5_c_re", "s5_c_im", "s5_d", "s5_w_glu", "sb_w_qkv", "sb_w_o", "final_norm")


def _dot(a, b):
    return jnp.dot(a, b, preferred_element_type=F32)


def _dot_nt(a, b):
    return lax.dot_general(a, b, (((1,), (1,)), ((), ())), preferred_element_type=F32)


def _dot_tn(a, b):
    return lax.dot_general(a, b, (((0,), (0,)), ((), ())), preferred_element_type=F32)


def _dot2(x, m):
    hi = x.astype(BF16)
    lo = (x - hi.astype(F32)).astype(BF16)
    return _dot(hi, m) + _dot(lo, m)


def _rms(x):
    r = lax.rsqrt(jnp.mean(x * x, axis=-1, keepdims=True) + EPS)
    return x * r, r


def _rms_bwd(dn, xh, r, g):
    gy = dn * g
    return r * (gy - xh * jnp.mean(gy * xh, axis=-1, keepdims=True))


_GELU_C = math.sqrt(2.0 / math.pi)


def _gelu(x):
    return 0.5 * x * (1.0 + jnp.tanh(_GELU_C * (x + 0.044715 * x * x * x)))


def _gelu_grad(x):
    th = jnp.tanh(_GELU_C * (x + 0.044715 * x * x * x))
    return 0.5 * (1.0 + th) + 0.5 * x * (1.0 - th * th) * _GELU_C * (1.0 + 3.0 * 0.044715 * x * x)


def _neg_softplus(z):
    return -(jnp.maximum(z, 0.0) + jnp.log(1.0 + jnp.exp(-jnp.abs(z))))


def _tile(n, want, mult=SUBLANES):
    for t in range(min(want, n), 0, -1):
        if n % t == 0 and t % mult == 0:
            return t
    return n


def _rows(tm, c):
    return pl.BlockSpec((tm, c), lambda i: (i, 0))


def _whole(a):
    nd = a.ndim
    return pl.BlockSpec(a.shape, lambda i: (0,) * nd)


class _W(NamedTuple):
    arr: jax.Array
    layer: int = 0
    col: int = 0
    width: Optional[int] = None

    @property
    def shape(self):
        return self.arr.shape[1], self.width or self.arr.shape[2]


def _wspec(w):
    return pl.BlockSpec((None,) + w.shape, lambda *_: (w.layer, 0, w.col))


def _row_params(sem="parallel"):
    return pltpu.CompilerParams(dimension_semantics=(sem,), vmem_limit_bytes=V7X_VMEM_LIMIT)


def _ffn_fwd(h, g, w1, w3, w2):
    t, d = h.shape
    f = w1.shape[1]
    tm = _tile(t, ROW_TILE)

    def body(h_ref, g_ref, w1_ref, w3_ref, w2_ref, ho_ref, n_ref, a_ref, b_ref):
        x = h_ref[...]
        xh, _ = _rms(x)
        n = (xh * g_ref[...]).astype(BF16)
        a = _dot(n, w1_ref[...])
        b = _dot(n, w3_ref[...])
        s = (a * jax.nn.sigmoid(a) * b).astype(BF16)
        ho_ref[...] = x + 0.5 * _dot(s, w2_ref[...])
        n_ref[...] = n
        a_ref[...] = a.astype(BF16)
        b_ref[...] = b.astype(BF16)

    return pl.pallas_call(
        body, name="ffn_fwd", grid=(t // tm,),
        in_specs=[_rows(tm, d), _whole(g), _wspec(w1), _wspec(w3), _wspec(w2)],
        out_specs=[_rows(tm, d), _rows(tm, d), _rows(tm, f), _rows(tm, f)],
        out_shape=[jax.ShapeDtypeStruct((t, d), F32), jax.ShapeDtypeStruct((t, d), BF16),
                   jax.ShapeDtypeStruct((t, f), BF16), jax.ShapeDtypeStruct((t, f), BF16)],
        compiler_params=_row_params(),
    )(h, g, w1.arr, w3.arr, w2.arr)


def _ffn_bwd_down(dh, a, b, w2):
    t, d = dh.shape
    f = a.shape[1]
    tm = _tile(t, ROW_TILE)

    def body(dh_ref, a_ref, b_ref, w2_ref, da_ref, db_ref, s_ref):
        ds = _dot_nt((0.5 * dh_ref[...]).astype(BF16), w2_ref[...])
        a32 = a_ref[...].astype(F32)
        b32 = b_ref[...].astype(F32)
        sig = jax.nn.sigmoid(a32)
        sil = a32 * sig
        da_ref[...] = (ds * b32 * (sig * (1.0 + a32 * (1.0 - sig)))).astype(BF16)
        db_ref[...] = (ds * sil).astype(BF16)
        s_ref[...] = (sil * b32).astype(BF16)

    return pl.pallas_call(
        body, name="ffn_bwd_down", grid=(t // tm,),
        in_specs=[_rows(tm, d), _rows(tm, f), _rows(tm, f), _wspec(w2)],
        out_specs=[_rows(tm, f)] * 3,
        out_shape=[jax.ShapeDtypeStruct((t, f), BF16)] * 3,
        compiler_params=_row_params(),
    )(dh, a, b, w2.arr)


def _lin_bwd_norm(dys, ws, h, g, dh, name):
    t, d = h.shape
    tm = _tile(t, ROW_TILE)
    k = len(dys)

    def body(*refs):
        dy_refs, w_refs = refs[:k], refs[k:2 * k]
        h_ref, g_ref, dh_ref, o_ref, dg_ref = refs[2 * k:]
        dn = _dot_nt(dy_refs[0][...].astype(BF16), w_refs[0][...])
        for j in range(1, k):
            dn = dn + _dot_nt(dy_refs[j][...].astype(BF16), w_refs[j][...])
        xh, r = _rms(h_ref[...])
        o_ref[...] = dh_ref[...] + _rms_bwd(dn, xh, r, g_ref[...])

        @pl.when(pl.program_id(0) == 0)
        def _():
            dg_ref[...] = jnp.zeros_like(dg_ref)

        dg_ref[...] += jnp.sum(dn * xh, axis=0, keepdims=True)

    return pl.pallas_call(
        body, name=name, grid=(t // tm,),
        in_specs=[_rows(tm, dy.shape[1]) for dy in dys] + [_wspec(w) for w in ws] + [_rows(tm, d), _whole(g), _rows(tm, d)],
        out_specs=[_rows(tm, d), pl.BlockSpec((1, d), lambda i: (0, 0))],
        out_shape=[jax.ShapeDtypeStruct((t, d), F32), jax.ShapeDtypeStruct((1, d), F32)],
        compiler_params=_row_params("arbitrary"),
    )(*dys, *[w.arr for w in ws], h, g, dh)


def _wgrad(x, dy, name, like, into=None, layer=0, scale=1.0):
    t, kk = x.shape
    n = dy.shape[1]
    tm = _tile(t, WGRAD_ROW_TILE)
    steps = t // tm
    blocked = len(like) == 4
    if blocked:
        sc = like[3]
        nd = max(g for g in (1, 2, 4, 8) if g * sc * kk * 4 <= max(WGRAD_OUT_BYTES, sc * kk * 4))
        tn = nd * sc
        out_spec = pl.BlockSpec((nd, None, kk, sc), lambda j, i: (j, layer, 0, 0))
    else:
        tn = _tile(n, max(128, WGRAD_OUT_BYTES // (4 * kk)), 128)
        out_spec = pl.BlockSpec((None, kk, tn), lambda j, i: (layer, 0, j))

    def body(x_ref, dy_ref, *rest):
        o_ref = rest[-1]
        i = pl.program_id(1)

        @pl.when(i == 0)
        def _():
            o_ref[...] = jnp.zeros_like(o_ref)

        acc = _dot_tn(x_ref[...].astype(BF16), dy_ref[...].astype(BF16))
        if blocked:
            for dd in range(nd):
                o_ref[dd] += acc[:, dd * sc:(dd + 1) * sc]
        else:
            o_ref[...] += acc
        if scale != 1.0:
            @pl.when(i == steps - 1)
            def _():
                o_ref[...] = o_ref[...] * scale

    held = [] if into is None else [into]
    return pl.pallas_call(
        body, name=name, grid=(n // tn, steps),
        in_specs=[pl.BlockSpec((tm, kk), lambda j, i: (i, 0)), pl.BlockSpec((tm, tn), lambda j, i: (i, j))]
        + [pl.BlockSpec(memory_space=pl.ANY)] * len(held),
        out_specs=out_spec,
        out_shape=jax.ShapeDtypeStruct(like, F32),
        input_output_aliases={2: 0} if held else {},
        compiler_params=pltpu.CompilerParams(dimension_semantics=("parallel", "arbitrary"), vmem_limit_bytes=V7X_VMEM_LIMIT),
    )(x, dy, *held)


def _norm_lin(h, g, ws, out_dtypes, name):
    t, d = h.shape
    tm = _tile(t, ROW_TILE)
    k = len(ws)

    def body(*refs):
        h_ref, g_ref = refs[:2]
        w_refs = refs[2:2 + k]
        n_ref = refs[2 + k]
        o_refs = refs[3 + k:]
        xh, _ = _rms(h_ref[...])
        n = (xh * g_ref[...]).astype(BF16)
        n_ref[...] = n
        for w_ref, o_ref in zip(w_refs, o_refs):
            o_ref[...] = _dot(n, w_ref[...]).astype(o_ref.dtype)

    return pl.pallas_call(
        body, name=name, grid=(t // tm,),
        in_specs=[_rows(tm, d), _whole(g)] + [_wspec(w) for w in ws],
        out_specs=[_rows(tm, d)] + [_rows(tm, w.shape[1]) for w in ws],
        out_shape=[jax.ShapeDtypeStruct((t, d), BF16)] + [jax.ShapeDtypeStruct((t, w.shape[1]), dt) for w, dt in zip(ws, out_dtypes)],
        compiler_params=_row_params(),
    )(h, g, *[w.arr for w in ws])


def _lin_res(h, x, w, name):
    t, d = h.shape
    tm = _tile(t, ROW_TILE)

    def body(h_ref, x_ref, w_ref, o_ref):
        o_ref[...] = h_ref[...] + _dot(x_ref[...], w_ref[...])

    return pl.pallas_call(
        body, name=name, grid=(t // tm,),
        in_specs=[_rows(tm, d), _rows(tm, x.shape[1]), _wspec(w)],
        out_specs=_rows(tm, d), out_shape=jax.ShapeDtypeStruct((t, d), F32),
        compiler_params=_row_params(),
    )(h, x, w.arr)


def _lin_nt(dy, w, name):
    t = dy.shape[0]
    kk = w.shape[0]
    tm = _tile(t, ROW_TILE)

    def body(dy_ref, w_ref, o_ref):
        o_ref[...] = _dot_nt(dy_ref[...].astype(BF16), w_ref[...]).astype(BF16)

    return pl.pallas_call(
        body, name=name, grid=(t // tm,),
        in_specs=[_rows(tm, dy.shape[1]), _wspec(w)],
        out_specs=_rows(tm, kk), out_shape=jax.ShapeDtypeStruct((t, kk), BF16),
        compiler_params=_row_params(),
    )(dy, w.arr)


def _ple_fwd(h, g, wg, p, wp):
    t, d = h.shape
    tm = _tile(t, ROW_TILE)

    def body(h_ref, g_ref, wg_ref, p_ref, wp_ref, o_ref, n_ref):
        x = h_ref[...]
        xh, _ = _rms(x)
        n = (xh * g_ref[...]).astype(BF16)
        n_ref[...] = n
        gate = jax.nn.sigmoid(_dot(n, wg_ref[...]))
        o_ref[...] = x + _dot(p_ref[...].astype(BF16), wp_ref[...]) * gate

    return pl.pallas_call(
        body, name="ple_fwd", grid=(t // tm,),
        in_specs=[_rows(tm, d), _whole(g), _wspec(wg), _rows(tm, p.shape[1]), _wspec(wp)],
        out_specs=[_rows(tm, d), _rows(tm, d)],
        out_shape=[jax.ShapeDtypeStruct((t, d), F32), jax.ShapeDtypeStruct((t, d), BF16)],
        compiler_params=_row_params(),
    )(h, g, wg.arr, p, wp.arr)


def _ple_bwd(dh, h, g, n, p, wg, wp):
    t, d = h.shape
    tm = _tile(t, ROW_TILE)

    def body(dh_ref, h_ref, g_ref, n_ref, p_ref, wg_ref, wp_ref, o_ref, dgl_ref, dpp_ref, dg_ref):
        dh_v = dh_ref[...]
        gate = jax.nn.sigmoid(_dot(n_ref[...], wg_ref[...]))
        pp = _dot(p_ref[...].astype(BF16), wp_ref[...])
        dgl = (dh_v * pp * gate * (1.0 - gate)).astype(BF16)
        dgl_ref[...] = dgl
        dpp_ref[...] = (dh_v * gate).astype(BF16)
        dn = _dot_nt(dgl, wg_ref[...])
        xh, r = _rms(h_ref[...])
        o_ref[...] = dh_v + _rms_bwd(dn, xh, r, g_ref[...])

        @pl.when(pl.program_id(0) == 0)
        def _():
            dg_ref[...] = jnp.zeros_like(dg_ref)

        dg_ref[...] += jnp.sum(dn * xh, axis=0, keepdims=True)

    return pl.pallas_call(
        body, name="ple_bwd", grid=(t // tm,),
        in_specs=[_rows(tm, d), _rows(tm, d), _whole(g), _rows(tm, d), _rows(tm, p.shape[1]), _wspec(wg), _wspec(wp)],
        out_specs=[_rows(tm, d), _rows(tm, d), _rows(tm, d), pl.BlockSpec((1, d), lambda i: (0, 0))],
        out_shape=[jax.ShapeDtypeStruct((t, d), F32), jax.ShapeDtypeStruct((t, d), BF16),
                   jax.ShapeDtypeStruct((t, d), BF16), jax.ShapeDtypeStruct((1, d), F32)],
        compiler_params=_row_params("arbitrary"),
    )(dh, h, g, n, p, wg.arr, wp.arr)


def _loss_head(h, g, tgt):
    t, d = h.shape
    tm = _tile(t, ROW_TILE)

    def body(h_ref, g_ref, t_ref, l_ref, dh_ref, dg_ref):
        xh, r = _rms(h_ref[...])
        gg = g_ref[...]
        e = xh * gg - t_ref[...]
        dy = e * (1.0 / d)

        @pl.when(pl.program_id(0) == 0)
        def _():
            l_ref[...] = jnp.zeros_like(l_ref)
            dg_ref[...] = jnp.zeros_like(dg_ref)

        l_ref[...] += 0.5 * jnp.sum(jnp.mean(e * e, axis=-1, keepdims=True), axis=0, keepdims=True)
        dg_ref[...] += jnp.sum(dy * xh, axis=0, keepdims=True)
        dh_ref[...] = _rms_bwd(dy, xh, r, gg)

    return pl.pallas_call(
        body, name="loss_head", grid=(t // tm,),
        in_specs=[_rows(tm, d), _whole(g), _rows(tm, d)],
        out_specs=[pl.BlockSpec((1, 128), lambda i: (0, 0)), _rows(tm, d), pl.BlockSpec((1, d), lambda i: (0, 0))],
        out_shape=[jax.ShapeDtypeStruct((1, 128), F32), jax.ShapeDtypeStruct((t, d), F32), jax.ShapeDtypeStruct((1, d), F32)],
        compiler_params=_row_params("arbitrary"),
    )(h, g, tgt)


def _s5_disc_math(a_re, a_im, log_dt):
    lam_re = jnp.minimum(a_re, -1e-4)
    lam_im = a_im
    dt = jnp.exp(log_dt)
    mag = jnp.exp(lam_re * dt)
    abar_re = mag * jnp.cos(lam_im * dt)
    abar_im = mag * jnp.sin(lam_im * dt)
    den = lam_re * lam_re + lam_im * lam_im
    nr = abar_re - 1.0
    ni = abar_im
    return abar_re, abar_im, (nr * lam_re + ni * lam_im) / den, (ni * lam_re - nr * lam_im) / den


def _s5_disc(a_re, a_im, log_dt):
    gp = jax.ShapeDtypeStruct(a_re.shape, F32)

    def body(ar_ref, ai_ref, ld_ref, o0, o1, o2, o3):
        outs = _s5_disc_math(ar_ref[...], ai_ref[...], ld_ref[...])
        for o_ref, val in zip((o0, o1, o2, o3), outs):
            o_ref[...] = val

    return pl.pallas_call(body, name="s5_disc", out_shape=[gp] * 4)(a_re, a_im, log_dt)


def _s5_disc_bwd(a_re, a_im, log_dt, cts):
    def body(ar_ref, ai_ref, ld_ref, c0, c1, c2, c3, dar_ref, dai_ref, dld_ref):
        _, vjp = jax.vjp(_s5_disc_math, ar_ref[...], ai_ref[...], ld_ref[...])
        dar, dai, dld = vjp((c0[...], c1[...], c2[...], c3[...]))
        dar_ref[...] = dar
        dai_ref[...] = dai
        dld_ref[...] = dld

    return pl.pallas_call(
        body, name="s5_disc_bwd",
        out_shape=[jax.ShapeDtypeStruct(a_re.shape, F32), jax.ShapeDtypeStruct(a_im.shape, F32),
                   jax.ShapeDtypeStruct(log_dt.shape, F32)],
    )(a_re, a_im, log_dt, *cts)


def _blockdiag(w, gl):
    g, r, c = w.shape
    w = w.reshape(g // gl, gl, r, c)
    eye = jnp.eye(gl, dtype=w.dtype)
    return (w[:, :, :, None, :] * eye[None, :, None, :, None]).reshape(g // gl, gl * r, gl * c)


def _blockdiag_take(m, gl, r, c):
    nb = m.shape[0]
    m = m.reshape(nb, gl, r, gl, c)
    return jnp.stack([m[:, g, :, g, :] for g in range(gl)], axis=1).reshape(nb * gl, r, c)


def _scan_rows(lc, step, carry, reverse=False):
    def blk(i, cs):
        i = (lc // SUBLANES - 1 - i) if reverse else i
        return step(pl.multiple_of(i * SUBLANES, SUBLANES), cs)

    return lax.fori_loop(0, lc // SUBLANES, blk, carry)


def _s5_scan_fwd(u, bre, bim, cre, cim, par, dskip, bl, seq, lc):
    t, d = u.shape
    nb, gw, ns = bre.shape
    nc = seq // lc

    def body(u_ref, bre_ref, bim_ref, cre_ref, cim_ref, par_ref, d_ref, y_ref, z_ref, st_ref, carry, sre, sim):
        @pl.when(pl.program_id(2) == 0)
        def _():
            carry[...] = jnp.zeros_like(carry)

        st_ref[...] = carry[...]
        uu = u_ref[...]
        ug = uu.astype(BF16)
        wre = _dot(ug, bre_ref[...])
        wim = _dot(ug, bim_ref[...])
        ar, ai = par_ref[0:1, :], par_ref[1:2, :]
        fr, fi = par_ref[2:3, :], par_ref[3:4, :]
        sre[...] = fr * wre - fi * wim
        sim[...] = fr * wim + fi * wre

        def step(base, cs):
            cr, ci = cs
            tr = sre[pl.ds(base, SUBLANES), :]
            ti = sim[pl.ds(base, SUBLANES), :]
            rows = lax.broadcasted_iota(jnp.int32, tr.shape, 0)
            outr, outi = tr, ti
            for k in range(SUBLANES):
                nr = ar * cr - ai * ci + tr[k:k + 1, :]
                ni = ar * ci + ai * cr + ti[k:k + 1, :]
                outr = jnp.where(rows == k, nr, outr)
                outi = jnp.where(rows == k, ni, outi)
                cr, ci = nr, ni
            sre[pl.ds(base, SUBLANES), :] = outr
            sim[pl.ds(base, SUBLANES), :] = outi
            return cr, ci

        cr, ci = _scan_rows(lc, step, (carry[0:1, :], carry[1:2, :]))
        carry[0:1, :] = cr
        carry[1:2, :] = ci
        y = _dot(sre[...].astype(BF16), cre_ref[...]) - _dot(sim[...].astype(BF16), cim_ref[...]) + d_ref[...] * uu
        y_ref[...] = y
        z_ref[...] = _gelu(y).astype(BF16)

    tok = pl.BlockSpec((lc, gw), lambda g, b, c: (b * nc + c, g))
    mat_b = pl.BlockSpec((None, gw, ns), lambda g, b, c: (g, 0, 0))
    mat_c = pl.BlockSpec((None, ns, gw), lambda g, b, c: (g, 0, 0))
    return pl.pallas_call(
        body, name="s5_scan_fwd", grid=(nb, bl, nc),
        in_specs=[tok, mat_b, mat_b, mat_c, mat_c, pl.BlockSpec((None, 8, ns), lambda g, b, c: (g, 0, 0)),
                  pl.BlockSpec((1, gw), lambda g, b, c: (0, g))],
        out_specs=[tok, tok, pl.BlockSpec((None, None, 2, ns), lambda g, b, c: (g, b * nc + c, 0, 0))],
        out_shape=[jax.ShapeDtypeStruct((t, d), F32), jax.ShapeDtypeStruct((t, d), BF16),
                   jax.ShapeDtypeStruct((nb, bl * nc, 2, ns), F32)],
        scratch_shapes=[pltpu.VMEM((2, ns), F32), pltpu.VMEM((lc, ns), F32), pltpu.VMEM((lc, ns), F32)],
        compiler_params=pltpu.CompilerParams(dimension_semantics=("parallel", "arbitrary", "arbitrary"),
                                             vmem_limit_bytes=V7X_VMEM_LIMIT),
    )(u, bre, bim, cre, cim, par, dskip)


def _s5_scan_bwd(u, dy, st, bre, bim, cre, cim, par, dskip, bl, seq, lc):
    t, d = u.shape
    nb, gw, ns = bre.shape
    nc = seq // lc

    def body(u_ref, dy_ref, st_ref, bre_ref, bim_ref, cre_ref, cim_ref, par_ref, d_ref,
             du_ref, dbre_ref, dbim_ref, dcre_ref, dcim_ref, dpar_ref, dd_ref,
             lcarry, sre, sim, pre, pim, wre_s, wim_s, lre, lim):
        b, c = pl.program_id(1), pl.program_id(2)

        @pl.when((b == 0) & (c == 0))
        def _():
            for ref in (dbre_ref, dbim_ref, dcre_ref, dcim_ref, dpar_ref, dd_ref):
                ref[...] = jnp.zeros_like(ref)

        @pl.when(c == 0)
        def _():
            lcarry[...] = jnp.zeros_like(lcarry)

        uu = u_ref[...]
        ug = uu.astype(BF16)
        dyv = dy_ref[...]
        dyb = dyv.astype(BF16)
        ar, ai = par_ref[0:1, :], par_ref[1:2, :]
        fr, fi = par_ref[2:3, :], par_ref[3:4, :]
        wre = _dot(ug, bre_ref[...])
        wim = _dot(ug, bim_ref[...])
        wre_s[...] = wre
        wim_s[...] = wim
        sre[...] = fr * wre - fi * wim
        sim[...] = fr * wim + fi * wre

        def fstep(base, cs):
            cr, ci = cs
            tr = sre[pl.ds(base, SUBLANES), :]
            ti = sim[pl.ds(base, SUBLANES), :]
            rows = lax.broadcasted_iota(jnp.int32, tr.shape, 0)
            outr, outi, prr, pri = tr, ti, tr, ti
            for k in range(SUBLANES):
                prr = jnp.where(rows == k, cr, prr)
                pri = jnp.where(rows == k, ci, pri)
                nr = ar * cr - ai * ci + tr[k:k + 1, :]
                ni = ar * ci + ai * cr + ti[k:k + 1, :]
                outr = jnp.where(rows == k, nr, outr)
                outi = jnp.where(rows == k, ni, outi)
                cr, ci = nr, ni
            sre[pl.ds(base, SUBLANES), :] = outr
            sim[pl.ds(base, SUBLANES), :] = outi
            pre[pl.ds(base, SUBLANES), :] = prr
            pim[pl.ds(base, SUBLANES), :] = pri
            return cr, ci

        _scan_rows(lc, fstep, (st_ref[0:1, :], st_ref[1:2, :]))

        lre[...] = _dot_nt(dyb, cre_ref[...])
        lim[...] = -_dot_nt(dyb, cim_ref[...])

        def bstep(base, cs):
            cr, ci = cs
            tr = lre[pl.ds(base, SUBLANES), :]
            ti = lim[pl.ds(base, SUBLANES), :]
            rows = lax.broadcasted_iota(jnp.int32, tr.shape, 0)
            outr, outi = tr, ti
            for k in range(SUBLANES - 1, -1, -1):
                nr = tr[k:k + 1, :] + ar * cr + ai * ci
                ni = ti[k:k + 1, :] + ar * ci - ai * cr
                outr = jnp.where(rows == k, nr, outr)
                outi = jnp.where(rows == k, ni, outi)
                cr, ci = nr, ni
            lre[pl.ds(base, SUBLANES), :] = outr
            lim[pl.ds(base, SUBLANES), :] = outi
            return cr, ci

        cr, ci = _scan_rows(lc, bstep, (lcarry[0:1, :], lcarry[1:2, :]), reverse=True)
        lcarry[0:1, :] = cr
        lcarry[1:2, :] = ci

        lr, li = lre[...], lim[...]
        spr, spi = pre[...], pim[...]
        wr, wi = wre_s[...], wim_s[...]
        dpar_ref[0:1, :] += jnp.sum(lr * spr + li * spi, axis=0, keepdims=True)
        dpar_ref[1:2, :] += jnp.sum(li * spr - lr * spi, axis=0, keepdims=True)
        dpar_ref[2:3, :] += jnp.sum(lr * wr + li * wi, axis=0, keepdims=True)
        dpar_ref[3:4, :] += jnp.sum(li * wr - lr * wi, axis=0, keepdims=True)
        dwr = (fr * lr + fi * li).astype(BF16)
        dwi = (fr * li - fi * lr).astype(BF16)
        dsk = d_ref[...]
        du_ref[...] = _dot_nt(dwr, bre_ref[...]) + _dot_nt(dwi, bim_ref[...]) + dsk * dyv
        dd_ref[...] += jnp.sum(dyv * uu, axis=0, keepdims=True)
        dbre_ref[...] += _dot_tn(ug, dwr)
        dbim_ref[...] += _dot_tn(ug, dwi)
        dcre_ref[...] += _dot_tn(sre[...].astype(BF16), dyb)
        dcim_ref[...] -= _dot_tn(sim[...].astype(BF16), dyb)

    tok = pl.BlockSpec((lc, gw), lambda g, b, c: (b * nc + nc - 1 - c, g))
    mat_b = pl.BlockSpec((None, gw, ns), lambda g, b, c: (g, 0, 0))
    mat_c = pl.BlockSpec((None, ns, gw), lambda g, b, c: (g, 0, 0))
    rows8 = pl.BlockSpec((None, 8, ns), lambda g, b, c: (g, 0, 0))
    dvec = pl.BlockSpec((1, gw), lambda g, b, c: (0, g))
    tile = pltpu.VMEM((lc, ns), F32)
    return pl.pallas_call(
        body, name="s5_scan_bwd", grid=(nb, bl, nc),
        in_specs=[tok, tok, pl.BlockSpec((None, None, 2, ns), lambda g, b, c: (g, b * nc + nc - 1 - c, 0, 0)),
                  mat_b, mat_b, mat_c, mat_c, rows8, dvec],
        out_specs=[tok, mat_b, mat_b, mat_c, mat_c, rows8, dvec],
        out_shape=[jax.ShapeDtypeStruct((t, d), F32),
                   jax.ShapeDtypeStruct((nb, gw, ns), F32), jax.ShapeDtypeStruct((nb, gw, ns), F32),
                   jax.ShapeDtypeStruct((nb, ns, gw), F32), jax.ShapeDtypeStruct((nb, ns, gw), F32),
                   jax.ShapeDtypeStruct((nb, 8, ns), F32), jax.ShapeDtypeStruct((1, d), F32)],
        scratch_shapes=[pltpu.VMEM((2, ns), F32)] + [tile] * 8,
        compiler_params=pltpu.CompilerParams(dimension_semantics=("arbitrary", "arbitrary", "arbitrary"),
                                             vmem_limit_bytes=V7X_VMEM_LIMIT),
    )(u, dy, st, bre, bim, cre, cim, par, dskip)


def _s5_out(h, z, wglu):
    t, d = h.shape
    tm = _tile(t, ROW_TILE)

    def body(h_ref, z_ref, w_ref, o_ref):
        zz = _dot(z_ref[...], w_ref[...])
        o_ref[...] = h_ref[...] + zz[:, :d] * jax.nn.sigmoid(zz[:, d:])

    return pl.pallas_call(
        body, name="s5_out", grid=(t // tm,),
        in_specs=[_rows(tm, d), _rows(tm, d), _wspec(wglu)],
        out_specs=_rows(tm, d), out_shape=jax.ShapeDtypeStruct((t, d), F32),
        compiler_params=_row_params(),
    )(h, z, wglu.arr)


def _s5_out_bwd(dh, z, y, wglu):
    t, d = dh.shape
    tm = _tile(t, ROW_TILE)

    def body(dh_ref, z_ref, y_ref, w_ref, dy_ref, dzz_ref):
        zz = _dot(z_ref[...], w_ref[...])
        out, sg = zz[:, :d], jax.nn.sigmoid(zz[:, d:])
        dh_v = dh_ref[...]
        dzz = jnp.concatenate([dh_v * sg, dh_v * out * sg * (1.0 - sg)], axis=1).astype(BF16)
        dzz_ref[...] = dzz
        dy_ref[...] = _dot_nt(dzz, w_ref[...]) * _gelu_grad(y_ref[...])

    return pl.pallas_call(
        body, name="s5_out_bwd", grid=(t // tm,),
        in_specs=[_rows(tm, d), _rows(tm, d), _rows(tm, d), _wspec(wglu)],
        out_specs=[_rows(tm, d), _rows(tm, 2 * d)],
        out_shape=[jax.ShapeDtypeStruct((t, d), F32), jax.ShapeDtypeStruct((t, 2 * d), BF16)],
        compiler_params=_row_params(),
    )(dh, z, y, wglu.arr)


def _head_spec(seq, w):
    return pl.BlockSpec((None, SB_HEADS_PER_STEP, seq, w), lambda b, h: (b, h, 0, 0))


def _tri_and_ones(kind):
    row = lax.broadcasted_iota(jnp.int32, (Q_BLOCK, Q_BLOCK), 0)
    col = lax.broadcasted_iota(jnp.int32, (Q_BLOCK, Q_BLOCK), 1)
    tri = {"after": row > col, "upto": row <= col, "before": row < col}[kind]
    return jnp.concatenate([tri.astype(BF16), jnp.ones((Q_BLOCK, Q_BLOCK), BF16)], axis=1)


def _sb_fwd(q, k, v):
    bsz, nh, seq, dh = q.shape
    tq = min(SB_Q_TILE, seq // 2)
    nq, nsub = seq // tq, tq // Q_BLOCK
    scale = dh ** -0.5
    hp = SB_HEADS_PER_STEP
    qb = Q_BLOCK

    def body(q_ref, k_ref, v_ref, o_ref, tot_ref):
        strict = lax.broadcasted_iota(jnp.int32, (qb, qb), 1) < lax.broadcasted_iota(jnp.int32, (qb, qb), 0)
        sums = _tri_and_ones("after")

        def sweep(qs, c0, units, carry):
            heads = sorted({u[0] for u in units})
            kbs = {hh: k_ref[hh, pl.ds(c0, qb), :] for hh in heads}
            vbs = {hh: v_ref[hh, pl.ds(c0, qb), :] for hh in heads}
            zs = [_dot_nt(qs[hh][s], kbs[hh]) * scale for hh, s, _ in units]
            lkrs = [_neg_softplus(z) for z in zs]
            lks = [jnp.where(strict, lkr, 0.0) if dg else lkr for lkr, (_, _, dg) in zip(lkrs, units)]
            css = [_dot2(lk, sums) for lk in lks]
            carry = dict(carry)
            for (hh, s, dg), z, lkr, cs in zip(units, zs, lkrs, css):
                acc, run = carry[hh, s]
                att = jnp.exp(z + lkr + cs[:, :qb] + run)
                if dg:
                    att = jnp.where(strict, att, 0.0)
                carry[hh, s] = (acc + _dot(att.astype(BF16), vbs[hh]), run + cs[:, qb:])
            return carry

        def q_loop(qi, _):
            r0 = qi * tq
            qs = {hh: [q_ref[hh, pl.ds(pl.multiple_of(r0 + s * qb, qb), qb), :] for s in range(nsub)] for hh in range(hp)}
            carry = {(hh, s): (jnp.zeros((qb, dh), F32), jnp.zeros((qb, qb), F32)) for hh in range(hp) for s in range(nsub)}
            for jj in reversed(range(nsub)):
                units = [(hh, s, s == jj) for hh in range(hp) for s in range(jj, nsub)]
                carry = sweep(qs, pl.multiple_of(r0 + jj * qb, qb), units, carry)
            units = [(hh, s, False) for hh in range(hp) for s in range(nsub)]
            carry = lax.fori_loop(
                0, nsub * qi, lambda t, c: sweep(qs, pl.multiple_of((nsub * qi - 1 - t) * qb, qb), units, c), carry)
            for (hh, s), (acc, run) in carry.items():
                rows = pl.ds(pl.multiple_of(r0 + s * qb, qb), qb)
                o_ref[hh, rows, :] = acc.astype(BF16)
                tot_ref[hh, rows, :] = run[:, 0:1]
            return 0

        lax.fori_loop(0, nq, q_loop, 0)

    return pl.pallas_call(
        body, name="sb_fwd", grid=(bsz, nh // hp),
        in_specs=[_head_spec(seq, dh)] * 3,
        out_specs=[_head_spec(seq, dh), _head_spec(seq, 1)],
        out_shape=[jax.ShapeDtypeStruct((bsz, nh, seq, dh), BF16), jax.ShapeDtypeStruct((bsz, nh, seq, 1), F32)],
        compiler_params=pltpu.CompilerParams(dimension_semantics=("parallel", "parallel"), vmem_limit_bytes=V7X_VMEM_LIMIT),
    )(q, k, v)


def _sb_bwd(q, k, v, do, tot):
    bsz, nh, seq, dh = q.shape
    tq = min(SB_Q_TILE, seq // 2)
    nq, nsub = seq // tq, tq // Q_BLOCK
    scale = dh ** -0.5
    hp = SB_HEADS_PER_STEP
    qb = Q_BLOCK

    def body(q_ref, k_ref, v_ref, do_ref, tot_ref, dq_ref, dk_ref, dv_ref, dka, dva):
        dka[...] = jnp.zeros_like(dka)
        dva[...] = jnp.zeros_like(dva)
        strict = lax.broadcasted_iota(jnp.int32, (qb, qb), 1) < lax.broadcasted_iota(jnp.int32, (qb, qb), 0)
        upto = _tri_and_ones("upto")
        before = _tri_and_ones("before")

        def sweep(qf, dof, tots, c0, first, units, carry):
            heads = sorted({u[0] for u in units})
            kbs = {hh: k_ref[hh, pl.ds(c0, qb), :] for hh in heads}
            vbs = {hh: v_ref[hh, pl.ds(c0, qb), :] for hh in heads}
            sub = lambda a, s: a[s * qb:(s + 1) * qb, :]
            zs = [_dot_nt(sub(qf[hh], s), kbs[hh]) * scale for hh, s, _ in units]
            das = [_dot_nt(sub(dof[hh], s), vbs[hh]) for hh, s, _ in units]
            lkrs = [_neg_softplus(z) for z in zs]
            lks = [jnp.where(strict, lkr, 0.0) if dg else lkr for lkr, (_, _, dg) in zip(lkrs, units)]
            css = [_dot2(lk, upto) for lk in lks]
            lsigs, atts, gls = [], [], []
            for (hh, s, dg), z, lkr, cs, da in zip(units, zs, lkrs, css, das):
                lsig = z + lkr
                att = jnp.exp(lsig + (tots[hh, s] - (cs[:, :qb] + carry[hh, s][1])))
                if dg:
                    att = jnp.where(strict, att, 0.0)
                lsigs.append(lsig)
                atts.append(att)
                gls.append(da * att)
            gss = [_dot2(gl, before) for gl in gls]
            carry = dict(carry)
            dzs = {}
            for (hh, s, dg), lsig, gl, cs, gs in zip(units, lsigs, gls, css, gss):
                dqa, pre, gpre = carry[hh, s]
                sig = jnp.exp(lsig)
                dz = gl * (1.0 - sig) - (gs[:, :qb] + gpre) * sig
                if dg:
                    dz = jnp.where(strict, dz, 0.0)
                dz = (dz * scale).astype(BF16)
                dzs[hh, s] = dz
                carry[hh, s] = (dqa + _dot(dz, kbs[hh]), pre + cs[:, qb:], gpre + gs[:, qb:])
            att_of = {(hh, s): a for (hh, s, _), a in zip(units, atts)}
            for hh in heads:
                dzc = jnp.concatenate([dzs[hh, s] for s in range(first, nsub)], axis=0)
                attc = jnp.concatenate([att_of[hh, s].astype(BF16) for s in range(first, nsub)], axis=0)
                dka[hh, pl.ds(c0, qb), :] += _dot_tn(dzc, qf[hh][first * qb:, :])
                dva[hh, pl.ds(c0, qb), :] += _dot_tn(attc, dof[hh][first * qb:, :])
            return carry

        def q_loop(qi, _):
            r0 = pl.multiple_of(qi * tq, tq)
            qf = [q_ref[hh, pl.ds(r0, tq), :] for hh in range(hp)]
            dof = [do_ref[hh, pl.ds(r0, tq), :] for hh in range(hp)]
            tots = {(hh, s): jnp.broadcast_to(tot_ref[hh, pl.ds(pl.multiple_of(r0 + s * qb, qb), qb), :], (qb, qb))
                    for hh in range(hp) for s in range(nsub)}
            carry = {(hh, s): (jnp.zeros((qb, dh), F32), jnp.zeros((qb, qb), F32), jnp.zeros((qb, qb), F32))
                     for hh in range(hp) for s in range(nsub)}
            units = [(hh, s, False) for hh in range(hp) for s in range(nsub)]
            carry = lax.fori_loop(
                0, nsub * qi, lambda kj, c: sweep(qf, dof, tots, pl.multiple_of(kj * qb, qb), 0, units, c), carry)
            for jj in range(nsub):
                units = [(hh, s, s == jj) for hh in range(hp) for s in range(jj, nsub)]
                carry = sweep(qf, dof, tots, pl.multiple_of(r0 + jj * qb, qb), jj, units, carry)
            for (hh, s), (dqa, _, _) in carry.items():
                dq_ref[hh, pl.ds(pl.multiple_of(r0 + s * qb, qb), qb), :] = dqa.astype(BF16)
            return 0

        lax.fori_loop(0, nq, q_loop, 0)
        dk_ref[...] = dka[...].astype(BF16)
        dv_ref[...] = dva[...].astype(BF16)

    hs = _head_spec(seq, dh)
    return pl.pallas_call(
        body, name="sb_bwd", grid=(bsz, nh // hp),
        in_specs=[hs, hs, hs, hs, _head_spec(seq, 1)],
        out_specs=[hs, hs, hs],
        out_shape=[jax.ShapeDtypeStruct((bsz, nh, seq, dh), BF16)] * 3,
        scratch_shapes=[pltpu.VMEM((hp, seq, dh), F32), pltpu.VMEM((hp, seq, dh), F32)],
        compiler_params=pltpu.CompilerParams(dimension_semantics=("parallel", "parallel"), vmem_limit_bytes=V7X_VMEM_LIMIT),
    )(q, k, v, do, tot)


def _to_heads(a, bsz, seq):
    return a.reshape(bsz, seq, -1, HEAD_DIM).transpose(0, 2, 1, 3)


def _from_heads(a):
    bsz, nh, seq, dh = a.shape
    return a.transpose(0, 2, 1, 3).reshape(bsz * seq, nh * dh)


def _coords():
    return lax.axis_index("x"), lax.axis_index("y"), lax.axis_index("c")


def _all_gather(x, name):
    r, c = x.shape

    def body(x_ref, out_ref, send_sems, recv_sems, local_sem):
        mx, my, mc = _coords()
        me, sibling = (mx, my, mc), (mx, my, 1 - mc)
        chips = [(1 - mx, my), (mx, 1 - my), (1 - mx, 1 - my)]

        def blk(px, py, pc):
            return out_ref.at[4 * px + 2 * py + pc]

        def copy(k, block, to, src=None):
            return pltpu.make_async_remote_copy(
                src_ref=blk(*block) if src is None else src, dst_ref=blk(*block),
                send_sem=send_sems.at[k], recv_sem=recv_sems.at[k], device_id=to, device_id_type=MESH)

        mine = pltpu.make_async_copy(x_ref, blk(*me), local_sem)
        mine.start()
        first = [copy(0, me, sibling, src=x_ref)]
        first += [copy(1 + j, me, (*chip, mc), src=x_ref) for j, chip in enumerate(chips)]
        for cp in first:
            cp.start()
        passed = [copy(4 + j, (*chip, mc), sibling) for j, chip in enumerate(chips)]
        for j, chip in enumerate(chips):
            copy(1 + j, (*chip, mc), me).wait_recv()
            passed[j].start()
        copy(0, sibling, me).wait_recv()
        for j, chip in enumerate(chips):
            copy(4 + j, (*chip, 1 - mc), me).wait_recv()
        for cp in first + passed:
            cp.wait_send()
        mine.wait()

    return pl.pallas_call(
        body, name=name,
        out_shape=jax.ShapeDtypeStruct((N_DEV, r, c), x.dtype),
        in_specs=[pl.BlockSpec(memory_space=pl.ANY)],
        out_specs=pl.BlockSpec(memory_space=pl.ANY),
        scratch_shapes=[pltpu.SemaphoreType.DMA((7,)), pltpu.SemaphoreType.DMA((7,)), pltpu.SemaphoreType.DMA],
    )(x)


_ANY = pl.BlockSpec(memory_space=pl.ANY)


def _window(ref, shard_shape, by_cols, dev):
    if by_cols:
        n = shard_shape[2]
        return ref.at[:, :, pl.ds(pl.multiple_of(dev * n, n), n)]
    k = shard_shape[1]
    return ref.at[:, pl.ds(pl.multiple_of(dev * k, k), k), :]


def _gather_weights(shards, by_cols):
    na = len(shards)
    fulls = [jax.ShapeDtypeStruct(
        (s.shape[0], s.shape[1], s.shape[2] * N_DEV) if c else (s.shape[0], s.shape[1] * N_DEV, s.shape[2]), s.dtype)
        for s, c in zip(shards, by_cols)]

    def body(*refs):
        x_refs, out_refs = refs[:na], refs[na:2 * na]
        send_sems, recv_sems, local_sems = refs[2 * na:]
        mx, my, mc = _coords()
        me, sibling = (mx, my, mc), (mx, my, 1 - mc)
        chips = [(1 - mx, my), (mx, 1 - my), (1 - mx, 1 - my)]

        def blk(a, dev):
            px, py, pc = dev
            return _window(out_refs[a], x_refs[a].shape, by_cols[a], 4 * px + 2 * py + pc)

        def copy(a, k, block, to, src=None):
            return pltpu.make_async_remote_copy(
                src_ref=blk(a, block) if src is None else src, dst_ref=blk(a, block),
                send_sem=send_sems.at[a, k], recv_sem=recv_sems.at[a, k], device_id=to, device_id_type=MESH)

        mines = [pltpu.make_async_copy(x_refs[a], blk(a, me), local_sems.at[a]) for a in range(na)]
        first = [[copy(a, 0, me, sibling, src=x_refs[a])]
                 + [copy(a, 1 + j, me, (*chip, mc), src=x_refs[a]) for j, chip in enumerate(chips)] for a in range(na)]
        passed = [[copy(a, 4 + j, (*chip, mc), sibling) for j, chip in enumerate(chips)] for a in range(na)]
        for a in range(na):
            mines[a].start()
            for cp in first[a]:
                cp.start()
        for a in range(na):
            for j, chip in enumerate(chips):
                copy(a, 1 + j, (*chip, mc), me).wait_recv()
                passed[a][j].start()
        for a in range(na):
            copy(a, 0, sibling, me).wait_recv()
            for j, chip in enumerate(chips):
                copy(a, 4 + j, (*chip, 1 - mc), me).wait_recv()
        for a in range(na):
            for cp in first[a] + passed[a]:
                cp.wait_send()
            mines[a].wait()

    return pl.pallas_call(
        body, name="gather_weights", out_shape=fulls, in_specs=[_ANY] * na, out_specs=[_ANY] * na,
        scratch_shapes=[pltpu.SemaphoreType.DMA((na, 7)), pltpu.SemaphoreType.DMA((na, 7)), pltpu.SemaphoreType.DMA((na,))],
    )(*shards)


def _rs_sibling(grads, shard_shapes, by_cols):
    na = len(grads)
    slabs = [jax.ShapeDtypeStruct((N_CHIP,) + tuple(s), F32) for s in shard_shapes]

    def body(*refs):
        g_refs, got_refs = refs[:na], refs[na:2 * na]
        send_sems, recv_sems = refs[2 * na:]
        mx, my, mc = _coords()
        copies = []
        for a in range(na):
            for q in range(N_CHIP):
                dev = 2 * q + 1 - mc
                theirs = g_refs[a].at[dev] if by_cols[a] else _window(g_refs[a], shard_shapes[a], False, dev)
                copies.append(pltpu.make_async_remote_copy(
                    src_ref=theirs, dst_ref=got_refs[a].at[q], send_sem=send_sems.at[a, q], recv_sem=recv_sems.at[a, q],
                    device_id=(mx, my, 1 - mc), device_id_type=MESH))
        for cp in copies:
            cp.start()
        for cp in copies:
            cp.wait()

    sems = pltpu.SemaphoreType.DMA((na, N_CHIP))
    return pl.pallas_call(
        body, name="rs_sibling", out_shape=slabs, in_specs=[_ANY] * na, out_specs=[_ANY] * na,
        scratch_shapes=[sems, sems],
    )(*grads)


def _rs_add(g, got, core, by_cols, name):
    nq, nl, r, c = got.shape
    tr = _tile(r, 256, 2 * SUBLANES)

    def body(core_ref, a_ref, b_ref, o_ref):
        o_ref[...] = (a_ref[...] + b_ref[...]).astype(BF16)

    if by_cols:
        mine = pl.BlockSpec((None, None, tr, c), lambda q, l, i, core_ref: (2 * q + core_ref[0], l, i, 0))
    else:
        mine = pl.BlockSpec((None, tr, c), lambda q, l, i, core_ref: (l, (2 * q + core_ref[0]) * (r // tr) + i, 0))
    spec = pl.BlockSpec((None, None, tr, c), lambda q, l, i, core_ref: (q, l, i, 0))
    return pl.pallas_call(
        body, name=name, out_shape=jax.ShapeDtypeStruct(got.shape, BF16),
        grid_spec=pltpu.PrefetchScalarGridSpec(num_scalar_prefetch=1, grid=(nq, nl, r // tr),
                                               in_specs=[mine, spec], out_specs=spec),
        compiler_params=pltpu.CompilerParams(dimension_semantics=("parallel",) * 3),
    )(core, g, got)


def _rs_chips(parts):
    na = len(parts)

    def body(*refs):
        p_refs, out_refs = refs[:na], refs[na:2 * na]
        send_sems, recv_sems, local_sems = refs[2 * na:]
        mx, my, mc = _coords()
        here = 2 * mx + my
        chips = [(1 - mx, my), (mx, 1 - my), (1 - mx, 1 - my)]
        copies = []
        for a in range(na):
            copies.append(pltpu.make_async_copy(p_refs[a].at[here], out_refs[a].at[here], local_sems.at[a]))
            for j, (cx, cy) in enumerate(chips):
                copies.append(pltpu.make_async_remote_copy(
                    src_ref=p_refs[a].at[2 * cx + cy], dst_ref=out_refs[a].at[here],
                    send_sem=send_sems.at[a, j], recv_sem=recv_sems.at[a, j], device_id=(cx, cy, mc), device_id_type=MESH))
        for cp in copies:
            cp.start()
        for cp in copies:
            cp.wait()

    return pl.pallas_call(
        body, name="rs_chips", out_shape=[jax.ShapeDtypeStruct(p.shape, p.dtype) for p in parts],
        in_specs=[_ANY] * na, out_specs=[_ANY] * na,
        scratch_shapes=[pltpu.SemaphoreType.DMA((na, 3)), pltpu.SemaphoreType.DMA((na, 3)), pltpu.SemaphoreType.DMA((na,))],
    )(*parts)


def _adamw(parts, w, m, v, name):
    n, r, c = parts.shape
    tr = _tile(r, 320, 2 * SUBLANES)
    bc1 = 1.0 - ADAM_B1 ** ADAM_STEP
    bc2 = 1.0 - ADAM_B2 ** ADAM_STEP

    def body(p_ref, w_ref, m_ref, v_ref, g_ref, d_ref, mo_ref, vo_ref):
        g = p_ref[0].astype(F32)
        for j in range(1, n):
            g = g + p_ref[j].astype(F32)
        mn = ADAM_B1 * m_ref[...] + (1.0 - ADAM_B1) * g
        vn = ADAM_B2 * v_ref[...] + (1.0 - ADAM_B2) * (g * g)
        g_ref[...] = g
        mo_ref[...] = mn
        vo_ref[...] = vn
        d_ref[...] = -ADAM_LR * ((mn / bc1) / (jnp.sqrt(vn / bc2) + ADAM_EPS) + ADAM_WD * w_ref[...])

    flat = pl.BlockSpec((tr, c), lambda i: (i, 0))
    return pl.pallas_call(
        body, name=name, grid=(r // tr,),
        in_specs=[pl.BlockSpec((n, tr, c), lambda i: (0, i, 0)), flat, flat, flat],
        out_specs=[flat] * 4, out_shape=[jax.ShapeDtypeStruct((r, c), F32)] * 4,
        compiler_params=pltpu.CompilerParams(dimension_semantics=("parallel",)),
    )(parts, w, m, v)


def _adam_math(g, w, m, v):
    bc1 = 1.0 - ADAM_B1 ** ADAM_STEP
    bc2 = 1.0 - ADAM_B2 ** ADAM_STEP
    mn = ADAM_B1 * m + (1.0 - ADAM_B1) * g
    vn = ADAM_B2 * v + (1.0 - ADAM_B2) * (g * g)
    return -ADAM_LR * ((mn / bc1) / (jnp.sqrt(vn / bc2) + ADAM_EPS) + ADAM_WD * w), mn, vn


def _adamw_shard(parts, w, m, v, name):
    nl, r, c = w.shape
    tr = _tile(r, 256, 2 * SUBLANES)

    def body(p_ref, w_ref, m_ref, v_ref, g_ref, d_ref, mo_ref, vo_ref):
        g = p_ref[0].astype(F32)
        for q in range(1, N_CHIP):
            g = g + p_ref[q].astype(F32)
        g = g[:, :c]
        g_ref[...] = g
        d_ref[...], mo_ref[...], vo_ref[...] = _adam_math(g, w_ref[...], m_ref[...], v_ref[...])

    native = pl.BlockSpec((None, tr, c), lambda l, i: (l, i, 0))
    return pl.pallas_call(
        body, name=name, grid=(nl, r // tr),
        in_specs=[pl.BlockSpec((N_CHIP, None, tr, parts.shape[3]), lambda l, i: (0, l, i, 0)), native, native, native],
        out_specs=[native] * 4, out_shape=[jax.ShapeDtypeStruct(w.shape, F32)] * 4,
        compiler_params=pltpu.CompilerParams(dimension_semantics=("parallel", "parallel"), vmem_limit_bytes=V7X_VMEM_LIMIT),
    )(parts, w, m, v)


def _pack(arrs, dtype=F32):
    cols = []
    for a in arrs:
        f = a.reshape(-1).astype(dtype)
        cols.append(jnp.pad(f, (0, -f.shape[0] % FLAT_W)))
    flat = jnp.concatenate(cols)
    flat = jnp.pad(flat, (0, -flat.shape[0] % (FLAT_W * SUBLANES)))
    return flat.reshape(-1, FLAT_W)


def _unpack(flat, shapes, lead=()):
    flat = flat.reshape(lead + (-1,))
    out, off = [], 0
    for s in shapes:
        n = math.prod(s)
        out.append(flat[..., off:off + n].reshape(lead + tuple(s)))
        off += n + (-n % FLAT_W)
    return out


def kernel(x, p, ffn1_norm, ffn1_w1, ffn1_w3, ffn1_w2, mix_norm, ffn2_norm, ffn2_w1, ffn2_w3, ffn2_w2, ple_norm, ple_proj, ple_gate, s5_w_in, s5_a_re, s5_a_im, s5_log_dt, s5_b_re, s5_b_im, s5_c_re, s5_c_im, s5_d, s5_w_glu, sb_w_qkv, sb_w_o, final_norm, loss_target, m_ffn1_norm, m_ffn1_w1, m_ffn1_w3, m_ffn1_w2, m_mix_norm, m_ffn2_norm, m_ffn2_w1, m_ffn2_w3, m_ffn2_w2, m_ple_norm, m_ple_proj, m_ple_gate, m_s5_w_in, m_s5_a_re, m_s5_a_im, m_s5_log_dt, m_s5_b_re, m_s5_b_im, m_s5_c_re, m_s5_c_im, m_s5_d, m_s5_w_glu, m_sb_w_qkv, m_sb_w_o, m_final_norm, v_ffn1_norm, v_ffn1_w1, v_ffn1_w3, v_ffn1_w2, v_mix_norm, v_ffn2_norm, v_ffn2_w1, v_ffn2_w3, v_ffn2_w2, v_ple_norm, v_ple_proj, v_ple_gate, v_s5_w_in, v_s5_a_re, v_s5_a_im, v_s5_log_dt, v_s5_b_re, v_s5_b_im, v_s5_c_re, v_s5_c_im, v_s5_d, v_s5_w_glu, v_sb_w_qkv, v_sb_w_o, v_final_norm):
    given = dict(locals())
    wts = {n: given[n] for n in WEIGHTS}
    mom = {n: given["m_" + n] for n in WEIGHTS}
    var = {n: given["v_" + n] for n in WEIGHTS}
    bl, seq, d = x.shape
    t = bl * seq
    depth = p.shape[0]

    hid_pad = -wts["ffn1_w1"].shape[-1] % LANES

    def padded(n):
        a = wts[n].astype(BF16)
        if n in FFN_COL:
            return jnp.pad(a, ((0, 0), (0, 0), (0, hid_pad)))
        if n in FFN_ROW:
            return jnp.pad(a, ((0, 0), (0, hid_pad), (0, 0)))
        return a

    sent = [padded(n) for n in SHARDED]
    shard_shape = {n: a.shape for n, a in zip(SHARDED, sent)}
    by_cols = [n in COL_SHARDED for n in SHARDED]
    full = dict(zip(SHARDED, _gather_weights(sent, by_cols)))

    def row(a):
        return a.reshape(1, -1)

    def wt(n, layer=0, col=0, width=None):
        return _W(full[n], layer, col, width)

    gbuf = {}

    def wgrad(n, x_act, dy, layer=0, scale=1.0):
        like = (N_DEV,) + shard_shape[n] if n in COL_SHARDED else full[n].shape
        gbuf[n] = _wgrad(x_act, dy, f"{n}_grad", like, gbuf.get(n), layer, scale)

    n_groups = d // S5_GROUP
    a_re, a_im = s5_a_re[0], s5_a_im[0]
    log_dt = s5_log_dt[0].reshape(n_groups, 1)
    disc = _s5_disc(a_re, a_im, log_dt)
    nb = n_groups // S5_BLOCK_GROUPS
    ns = S5_BLOCK_GROUPS * S5_STATE
    par = jnp.concatenate([jnp.stack([q.reshape(nb, ns) for q in disc], axis=1), jnp.zeros((nb, 4, ns), F32)], axis=1)
    bre = _blockdiag(s5_b_re[0].transpose(0, 2, 1), S5_BLOCK_GROUPS).astype(BF16)
    bim = _blockdiag(s5_b_im[0].transpose(0, 2, 1), S5_BLOCK_GROUPS).astype(BF16)
    cre = _blockdiag(s5_c_re[0].transpose(0, 2, 1), S5_BLOCK_GROUPS).astype(BF16)
    cim = _blockdiag(s5_c_im[0].transpose(0, 2, 1), S5_BLOCK_GROUPS).astype(BF16)
    lc = min(SCAN_CHUNK, seq)

    h = x.reshape(t, d)
    saved = []
    for i in range(depth):
        s = {"h0": h}
        h, s["n1"], s["a1"], s["b1"] = _ffn_fwd(h, row(ffn1_norm[i]), wt("ffn1_w1", i), wt("ffn1_w3", i), wt("ffn1_w2", i))
        s["h1"] = h
        j = i // 2
        if i % 2 == 0:
            s["hn"], s["u"] = _norm_lin(h, row(mix_norm[i]), [wt("s5_w_in", j)], [F32], "s5_in")
            s["y"], s["z"], s["st"] = _s5_scan_fwd(s["u"], bre, bim, cre, cim, par, row(s5_d[j]), bl, seq, lc)
            h = _s5_out(h, s["z"], wt("s5_w_glu", j))
        else:
            wqkv = [wt("sb_w_qkv", j, col, d) for col in range(3)]
            s["hn"], q, k, v = _norm_lin(h, row(mix_norm[i]), wqkv, [BF16] * 3, "sb_in")
            s["q"], s["k"], s["v"] = (_to_heads(a, bl, seq) for a in (q, k, v))
            o, s["tot"] = _sb_fwd(s["q"], s["k"], s["v"])
            s["o"] = _from_heads(o)
            h = _lin_res(h, s["o"], wt("sb_w_o", j), "sb_out")
        s["h2"] = h
        h, s["n2"], s["a2"], s["b2"] = _ffn_fwd(h, row(ffn2_norm[i]), wt("ffn2_w1", i), wt("ffn2_w3", i), wt("ffn2_w2", i))
        s["h3"] = h
        s["p"] = p[i].reshape(t, -1)
        h, s["npl"] = _ple_fwd(h, row(ple_norm[i]), wt("ple_gate", i), s["p"], wt("ple_proj", i))
        saved.append(s)

    loss_part, dh, g_final = _loss_head(h, row(final_norm), loss_target.reshape(t, d))
    loss = lax.psum(loss_part[0, 0], ("x", "y", "c"))

    grads = {n: [None] * wts[n].shape[0] for n in REPLICATED if n != "final_norm"}
    grads["final_norm"] = g_final.reshape(-1)

    def ffn_bwd(dh, h_in, n, a, b, which, i):
        da, db, sact = _ffn_bwd_down(dh, a, b, wt(f"{which}_w2", i))
        wgrad(f"{which}_w2", sact, dh, i, scale=0.5)
        dh_in, dg = _lin_bwd_norm([da, db], [wt(f"{which}_w1", i), wt(f"{which}_w3", i)], h_in,
                                  row(wts[f"{which}_norm"][i]), dh, f"{which}_bwd_up")
        wgrad(f"{which}_w1", n, da, i)
        wgrad(f"{which}_w3", n, db, i)
        grads[f"{which}_norm"][i] = dg.reshape(-1)
        return dh_in

    for i in reversed(range(depth)):
        s = saved[i]
        j = i // 2
        dh, dgl, dpp, dg = _ple_bwd(dh, s["h3"], row(ple_norm[i]), s["npl"], s["p"], wt("ple_gate", i), wt("ple_proj", i))
        grads["ple_norm"][i] = dg.reshape(-1)
        wgrad("ple_gate", s["npl"], dgl, i)
        wgrad("ple_proj", s["p"], dpp, i)
        dh = ffn_bwd(dh, s["h2"], s["n2"], s["a2"], s["b2"], "ffn2", i)
        if i % 2 == 0:
            dy, dzz = _s5_out_bwd(dh, s["z"], s["y"], wt("s5_w_glu", j))
            wgrad("s5_w_glu", s["z"], dzz, j)
            du, dbre, dbim, dcre, dcim, dpar, dd = _s5_scan_bwd(s["u"], dy, s["st"], bre, bim, cre, cim, par, row(s5_d[j]), bl, seq, lc)
            cts = [dpar[:, r, :].reshape(n_groups, S5_STATE) for r in range(4)]
            g_are, g_aim, g_ldt = _s5_disc_bwd(a_re, a_im, log_dt, cts)
            grads["s5_a_re"][j], grads["s5_a_im"][j], grads["s5_log_dt"][j] = g_are, g_aim, g_ldt.reshape(-1)
            grads["s5_b_re"][j] = _blockdiag_take(dbre, S5_BLOCK_GROUPS, S5_GROUP, S5_STATE).transpose(0, 2, 1)
            grads["s5_b_im"][j] = _blockdiag_take(dbim, S5_BLOCK_GROUPS, S5_GROUP, S5_STATE).transpose(0, 2, 1)
            grads["s5_c_re"][j] = _blockdiag_take(dcre, S5_BLOCK_GROUPS, S5_STATE, S5_GROUP).transpose(0, 2, 1)
            grads["s5_c_im"][j] = _blockdiag_take(dcim, S5_BLOCK_GROUPS, S5_STATE, S5_GROUP).transpose(0, 2, 1)
            grads["s5_d"][j] = dd.reshape(-1)
            dh, dg = _lin_bwd_norm([du], [wt("s5_w_in", j)], s["h1"], row(mix_norm[i]), dh, "s5_in_bwd")
            wgrad("s5_w_in", s["hn"], du, j)
        else:
            do = _lin_nt(dh, wt("sb_w_o", j), "sb_out_bwd")
            wgrad("sb_w_o", s["o"], dh, j)
            dqkv = _sb_bwd(s["q"], s["k"], s["v"], _to_heads(do, bl, seq), s["tot"])
            dqkv = [_from_heads(a) for a in dqkv]
            dh, dg = _lin_bwd_norm(dqkv, [wt("sb_w_qkv", j, col, d) for col in range(3)], s["h1"], row(mix_norm[i]), dh, "sb_in_bwd")
            wgrad("sb_w_qkv", s["hn"], jnp.concatenate(dqkv, axis=1), j)
        grads["mix_norm"][i] = dg.reshape(-1)
        dh = ffn_bwd(dh, s["h0"], s["n1"], s["a1"], s["b1"], "ffn1", i)
    grad_x = dh.reshape(x.shape)
    for n in REPLICATED:
        if n != "final_norm":
            grads[n] = jnp.stack(grads[n])

    got = _rs_sibling([gbuf[n] for n in SHARDED], [shard_shape[n] for n in SHARDED], by_cols)
    core = lax.axis_index("c").astype(jnp.int32).reshape(1)
    parts = _rs_chips([_rs_add(gbuf[n], g, core, n in COL_SHARDED, f"rs_add_{n}") for n, g in zip(SHARDED, got)])
    res = {n: _adamw_shard(part, wts[n], mom[n], var[n], f"adamw_{n}") for n, part in zip(SHARDED, parts)}

    rep_shapes = [wts[n].shape for n in REPLICATED]
    rep_parts = _all_gather(_pack([grads[n].reshape(wts[n].shape) for n in REPLICATED]), "gather_small_grads")
    outs = _adamw(rep_parts, _pack([wts[n] for n in REPLICATED]), _pack([mom[n] for n in REPLICATED]),
                  _pack([var[n] for n in REPLICATED]), "adamw_replicated")
    res.update({n: vals for n, vals in zip(REPLICATED, zip(*[_unpack(o, rep_shapes) for o in outs]))})

    return (loss, grad_x, *[res[n][0] for n in WEIGHTS], *[res[n][1] for n in WEIGHTS],
            *[res[n][2] for n in WEIGHTS], *[res[n][3] for n in WEIGHTS])
```

```python
import math
from typing import NamedTuple, Optional

import jax
import jax.numpy as jnp
from jax import lax
from jax.experimental import pallas as pl
from jax.experimental.pallas import tpu as pltpu

F32 = jnp.float32
BF16 = jnp.bfloat16
MESH = pl.DeviceIdType.MESH

N_DEV = 8
N_CHIP = 4
EPS = 1e-6
S5_GROUP = 16
S5_STATE = 64
S5_BLOCK_GROUPS = 16
HEAD_DIM = 64
Q_BLOCK = 128
SB_Q_TILE = 256
SB_HEADS_PER_STEP = 4
SCAN_CHUNK = 128
SUBLANES = 8
LANES = 128
FLAT_W = 1024
ADAM_LR, ADAM_B1, ADAM_B2, ADAM_EPS, ADAM_WD, ADAM_STEP = 1e-3, 0.9, 0.999, 1e-8, 0.01, 10
V7X_VMEM_LIMIT = 56 * 1024 * 1024
ROW_TILE = 256
WGRAD_ROW_TILE = 512
WGRAD_OUT_BYTES = 6 * 1024 * 1024

SHARDED = ("ffn1_w1", "ffn1_w3", "ffn1_w2", "ffn2_w1", "ffn2_w3", "ffn2_w2", "ple_proj", "ple_gate",
           "s5_w_in", "s5_w_glu", "sb_w_qkv", "sb_w_o")
COL_SHARDED = ("ffn1_w1", "ffn1_w3", "ffn2_w1", "ffn2_w3", "ple_proj", "s5_w_glu", "sb_w_qkv")
FFN_COL = ("ffn1_w1", "ffn1_w3", "ffn2_w1", "ffn2_w3")
FFN_ROW = ("ffn1_w2", "ffn2_w2")
REPLICATED = ("ffn1_norm", "mix_norm", "ffn2_norm", "ple_norm", "s5_a_re", "s5_a_im", "s5_log_dt",
              "s5_b_re", "s5_b_im", "s5_c_re", "s5_c_im", "s5_d", "final_norm")
WEIGHTS = ("ffn1_norm", "ffn1_w1", "ffn1_w3", "ffn1_w2", "mix_norm", "ffn2_norm", "ffn2_w1", "ffn2_w3", "ffn2_w2",
           "ple_norm", "ple_proj", "ple_gate", "s5_w_in", "s5_a_re", "s5_a_im", "s5_log_dt", "s5_b_re", "s5_b_im",
           "s5_c_re", "s5_c_im", "s5_d", "s5_w_glu", "sb_w_qkv", "sb_w_o", "final_norm")


def _dot(a, b):
    return jnp.dot(a, b, preferred_element_type=F32)


def _dot_nt(a, b):
    return lax.dot_general(a, b, (((1,), (1,)), ((), ())), preferred_element_type=F32)


def _dot_tn(a, b):
    return lax.dot_general(a, b, (((0,), (0,)), ((), ())), preferred_element_type=F32)


def _dot2(x, m):
    hi = x.astype(BF16)
    lo = (x - hi.astype(F32)).astype(BF16)
    return _dot(hi, m) + _dot(lo, m)


def _rms(x):
    r = lax.rsqrt(jnp.mean(x * x, axis=-1, keepdims=True) + EPS)
    return x * r, r


def _rms_bwd(dn, xh, r, g):
    gy = dn * g
    return r * (gy - xh * jnp.mean(gy * xh, axis=-1, keepdims=True))


_GELU_C = math.sqrt(2.0 / math.pi)


def _gelu(x):
    return 0.5 * x * (1.0 + jnp.tanh(_GELU_C * (x + 0.044715 * x * x * x)))


def _gelu_grad(x):
    th = jnp.tanh(_GELU_C * (x + 0.044715 * x * x * x))
    return 0.5 * (1.0 + th) + 0.5 * x * (1.0 - th * th) * _GELU_C * (1.0 + 3.0 * 0.044715 * x * x)


def _neg_softplus(z):
    return -(jnp.maximum(z, 0.0) + jnp.log(1.0 + jnp.exp(-jnp.abs(z))))


def _tile(n, want, mult=SUBLANES):
    for t in range(min(want, n), 0, -1):
        if n % t == 0 and t % mult == 0:
            return t
    return n


def _rows(tm, c):
    return pl.BlockSpec((tm, c), lambda i: (i, 0))


def _whole(a):
    nd = a.ndim
    return pl.BlockSpec(a.shape, lambda i: (0,) * nd)


class _W(NamedTuple):
    arr: jax.Array
    layer: int = 0
    col: int = 0
    width: Optional[int] = None

    @property
    def shape(self):
        return self.arr.shape[1], self.width or self.arr.shape[2]


def _wspec(w):
    return pl.BlockSpec((None,) + w.shape, lambda *_: (w.layer, 0, w.col))


def _row_params(sem="parallel"):
    return pltpu.CompilerParams(dimension_semantics=(sem,), vmem_limit_bytes=V7X_VMEM_LIMIT)


def _ffn_fwd(h, g, w1, w3, w2):
    t, d = h.shape
    f = w1.shape[1]
    tm = _tile(t, ROW_TILE)

    def body(h_ref, g_ref, w1_ref, w3_ref, w2_ref, ho_ref, n_ref, a_ref, b_ref):
        x = h_ref[...]
        xh, _ = _rms(x)
        n = (xh * g_ref[...]).astype(BF16)
        a = _dot(n, w1_ref[...])
        b = _dot(n, w3_ref[...])
        s = (a * jax.nn.sigmoid(a) * b).astype(BF16)
        ho_ref[...] = x + 0.5 * _dot(s, w2_ref[...])
        n_ref[...] = n
        a_ref[...] = a.astype(BF16)
        b_ref[...] = b.astype(BF16)

    return pl.pallas_call(
        body, name="ffn_fwd", grid=(t // tm,),
        in_specs=[_rows(tm, d), _whole(g), _wspec(w1), _wspec(w3), _wspec(w2)],
        out_specs=[_rows(tm, d), _rows(tm, d), _rows(tm, f), _rows(tm, f)],
        out_shape=[jax.ShapeDtypeStruct((t, d), F32), jax.ShapeDtypeStruct((t, d), BF16),
                   jax.ShapeDtypeStruct((t, f), BF16), jax.ShapeDtypeStruct((t, f), BF16)],
        compiler_params=_row_params(),
    )(h, g, w1.arr, w3.arr, w2.arr)


def _ffn_bwd_down(dh, a, b, w2):
    t, d = dh.shape
    f = a.shape[1]
    tm = _tile(t, ROW_TILE)

    def body(dh_ref, a_ref, b_ref, w2_ref, da_ref, db_ref, s_ref):
        ds = _dot_nt((0.5 * dh_ref[...]).astype(BF16), w2_ref[...])
        a32 = a_ref[...].astype(F32)
        b32 = b_ref[...].astype(F32)
        sig = jax.nn.sigmoid(a32)
        sil = a32 * sig
        da_ref[...] = (ds * b32 * (sig * (1.0 + a32 * (1.0 - sig)))).astype(BF16)
        db_ref[...] = (ds * sil).astype(BF16)
        s_ref[...] = (sil * b32).astype(BF16)

    return pl.pallas_call(
        body, name="ffn_bwd_down", grid=(t // tm,),
        in_specs=[_rows(tm, d), _rows(tm, f), _rows(tm, f), _wspec(w2)],
        out_specs=[_rows(tm, f)] * 3,
        out_shape=[jax.ShapeDtypeStruct((t, f), BF16)] * 3,
        compiler_params=_row_params(),
    )(dh, a, b, w2.arr)


def _lin_bwd_norm(dys, ws, h, g, dh, name):
    t, d = h.shape
    tm = _tile(t, ROW_TILE)
    k = len(dys)

    def body(*refs):
        dy_refs, w_refs = refs[:k], refs[k:2 * k]
        h_ref, g_ref, dh_ref, o_ref, dg_ref = refs[2 * k:]
        dn = _dot_nt(dy_refs[0][...].astype(BF16), w_refs[0][...])
        for j in range(1, k):
            dn = dn + _dot_nt(dy_refs[j][...].astype(BF16), w_refs[j][...])
        xh, r = _rms(h_ref[...])
        o_ref[...] = dh_ref[...] + _rms_bwd(dn, xh, r, g_ref[...])

        @pl.when(pl.program_id(0) == 0)
        def _():
            dg_ref[...] = jnp.zeros_like(dg_ref)

        dg_ref[...] += jnp.sum(dn * xh, axis=0, keepdims=True)

    return pl.pallas_call(
        body, name=name, grid=(t // tm,),
        in_specs=[_rows(tm, dy.shape[1]) for dy in dys] + [_wspec(w) for w in ws] + [_rows(tm, d), _whole(g), _rows(tm, d)],
        out_specs=[_rows(tm, d), pl.BlockSpec((1, d), lambda i: (0, 0))],
        out_shape=[jax.ShapeDtypeStruct((t, d), F32), jax.ShapeDtypeStruct((1, d), F32)],
        compiler_params=_row_params("arbitrary"),
    )(*dys, *[w.arr for w in ws], h, g, dh)


def _wgrad(x, dy, name, like, into=None, layer=0, scale=1.0):
    t, kk = x.shape
    n = dy.shape[1]
    tm = _tile(t, WGRAD_ROW_TILE)
    steps = t // tm
    blocked = len(like) == 4
    if blocked:
        sc = like[3]
        nd = max(g for g in (1, 2, 4, 8) if g * sc * kk * 4 <= max(WGRAD_OUT_BYTES, sc * kk * 4))
        tn = nd * sc
        out_spec = pl.BlockSpec((nd, None, kk, sc), lambda j, i: (j, layer, 0, 0))
    else:
        tn = _tile(n, max(128, WGRAD_OUT_BYTES // (4 * kk)), 128)
        out_spec = pl.BlockSpec((None, kk, tn), lambda j, i: (layer, 0, j))

    def body(x_ref, dy_ref, *rest):
        o_ref = rest[-1]
        i = pl.program_id(1)

        @pl.when(i == 0)
        def _():
            o_ref[...] = jnp.zeros_like(o_ref)

        acc = _dot_tn(x_ref[...].astype(BF16), dy_ref[...].astype(BF16))
        if blocked:
            for dd in range(nd):
                o_ref[dd] += acc[:, dd * sc:(dd + 1) * sc]
        else:
            o_ref[...] += acc
        if scale != 1.0:
            @pl.when(i == steps - 1)
            def _():
                o_ref[...] = o_ref[...] * scale

    held = [] if into is None else [into]
    return pl.pallas_call(
        body, name=name, grid=(n // tn, steps),
        in_specs=[pl.BlockSpec((tm, kk), lambda j, i: (i, 0)), pl.BlockSpec((tm, tn), lambda j, i: (i, j))]
        + [pl.BlockSpec(memory_space=pl.ANY)] * len(held),
        out_specs=out_spec,
        out_shape=jax.ShapeDtypeStruct(like, F32),
        input_output_aliases={2: 0} if held else {},
        compiler_params=pltpu.CompilerParams(dimension_semantics=("parallel", "arbitrary"), vmem_limit_bytes=V7X_VMEM_LIMIT),
    )(x, dy, *held)


def _norm_lin(h, g, ws, out_dtypes, name):
    t, d = h.shape
    tm = _tile(t, ROW_TILE)
    k = len(ws)

    def body(*refs):
        h_ref, g_ref = refs[:2]
        w_refs = refs[2:2 + k]
        n_ref = refs[2 + k]
        o_refs = refs[3 + k:]
        xh, _ = _rms(h_ref[...])
        n = (xh * g_ref[...]).astype(BF16)
        n_ref[...] = n
        for w_ref, o_ref in zip(w_refs, o_refs):
            o_ref[...] = _dot(n, w_ref[...]).astype(o_ref.dtype)

    return pl.pallas_call(
        body, name=name, grid=(t // tm,),
        in_specs=[_rows(tm, d), _whole(g)] + [_wspec(w) for w in ws],
        out_specs=[_rows(tm, d)] + [_rows(tm, w.shape[1]) for w in ws],
        out_shape=[jax.ShapeDtypeStruct((t, d), BF16)] + [jax.ShapeDtypeStruct((t, w.shape[1]), dt) for w, dt in zip(ws, out_dtypes)],
        compiler_params=_row_params(),
    )(h, g, *[w.arr for w in ws])


def _lin_res(h, x, w, name):
    t, d = h.shape
    tm = _tile(t, ROW_TILE)

    def body(h_ref, x_ref, w_ref, o_ref):
        o_ref[...] = h_ref[...] + _dot(x_ref[...], w_ref[...])

    return pl.pallas_call(
        body, name=name, grid=(t // tm,),
        in_specs=[_rows(tm, d), _rows(tm, x.shape[1]), _wspec(w)],
        out_specs=_rows(tm, d), out_shape=jax.ShapeDtypeStruct((t, d), F32),
        compiler_params=_row_params(),
    )(h, x, w.arr)


def _lin_nt(dy, w, name):
    t = dy.shape[0]
    kk = w.shape[0]
    tm = _tile(t, ROW_TILE)

    def body(dy_ref, w_ref, o_ref):
        o_ref[...] = _dot_nt(dy_ref[...].astype(BF16), w_ref[...]).astype(BF16)

    return pl.pallas_call(
        body, name=name, grid=(t // tm,),
        in_specs=[_rows(tm, dy.shape[1]), _wspec(w)],
        out_specs=_rows(tm, kk), out_shape=jax.ShapeDtypeStruct((t, kk), BF16),
        compiler_params=_row_params(),
    )(dy, w.arr)


def _ple_fwd(h, g, wg, p, wp):
    t, d = h.shape
    tm = _tile(t, ROW_TILE)

    def body(h_ref, g_ref, wg_ref, p_ref, wp_ref, o_ref, n_ref):
        x = h_ref[...]
        xh, _ = _rms(x)
        n = (xh * g_ref[...]).astype(BF16)
        n_ref[...] = n
        gate = jax.nn.sigmoid(_dot(n, wg_ref[...]))
        o_ref[...] = x + _dot(p_ref[...].astype(BF16), wp_ref[...]) * gate

    return pl.pallas_call(
        body, name="ple_fwd", grid=(t // tm,),
        in_specs=[_rows(tm, d), _whole(g), _wspec(wg), _rows(tm, p.shape[1]), _wspec(wp)],
        out_specs=[_rows(tm, d), _rows(tm, d)],
        out_shape=[jax.ShapeDtypeStruct((t, d), F32), jax.ShapeDtypeStruct((t, d), BF16)],
        compiler_params=_row_params(),
    )(h, g, wg.arr, p, wp.arr)


def _ple_bwd(dh, h, g, n, p, wg, wp):
    t, d = h.shape
    tm = _tile(t, ROW_TILE)

    def body(dh_ref, h_ref, g_ref, n_ref, p_ref, wg_ref, wp_ref, o_ref, dgl_ref, dpp_ref, dg_ref):
        dh_v = dh_ref[...]
        gate = jax.nn.sigmoid(_dot(n_ref[...], wg_ref[...]))
        pp = _dot(p_ref[...].astype(BF16), wp_ref[...])
        dgl = (dh_v * pp * gate * (1.0 - gate)).astype(BF16)
        dgl_ref[...] = dgl
        dpp_ref[...] = (dh_v * gate).astype(BF16)
        dn = _dot_nt(dgl, wg_ref[...])
        xh, r = _rms(h_ref[...])
        o_ref[...] = dh_v + _rms_bwd(dn, xh, r, g_ref[...])

        @pl.when(pl.program_id(0) == 0)
        def _():
            dg_ref[...] = jnp.zeros_like(dg_ref)

        dg_ref[...] += jnp.sum(dn * xh, axis=0, keepdims=True)

    return pl.pallas_call(
        body, name="ple_bwd", grid=(t // tm,),
        in_specs=[_rows(tm, d), _rows(tm, d), _whole(g), _rows(tm, d), _rows(tm, p.shape[1]), _wspec(wg), _wspec(wp)],
        out_specs=[_rows(tm, d), _rows(tm, d), _rows(tm, d), pl.BlockSpec((1, d), lambda i: (0, 0))],
        out_shape=[jax.ShapeDtypeStruct((t, d), F32), jax.ShapeDtypeStruct((t, d), BF16),
                   jax.ShapeDtypeStruct((t, d), BF16), jax.ShapeDtypeStruct((1, d), F32)],
        compiler_params=_row_params("arbitrary"),
    )(dh, h, g, n, p, wg.arr, wp.arr)


def _loss_head(h, g, tgt):
    t, d = h.shape
    tm = _tile(t, ROW_TILE)

    def body(h_ref, g_ref, t_ref, l_ref, dh_ref, dg_ref):
        xh, r = _rms(h_ref[...])
        gg = g_ref[...]
        e = xh * gg - t_ref[...]
        dy = e * (1.0 / d)

        @pl.when(pl.program_id(0) == 0)
        def _():
            l_ref[...] = jnp.zeros_like(l_ref)
            dg_ref[...] = jnp.zeros_like(dg_ref)

        l_ref[...] += 0.5 * jnp.sum(jnp.mean(e * e, axis=-1, keepdims=True), axis=0, keepdims=True)
        dg_ref[...] += jnp.sum(dy * xh, axis=0, keepdims=True)
        dh_ref[...] = _rms_bwd(dy, xh, r, gg)

    return pl.pallas_call(
        body, name="loss_head", grid=(t // tm,),
        in_specs=[_rows(tm, d), _whole(g), _rows(tm, d)],
        out_specs=[pl.BlockSpec((1, 128), lambda i: (0, 0)), _rows(tm, d), pl.BlockSpec((1, d), lambda i: (0, 0))],
        out_shape=[jax.ShapeDtypeStruct((1, 128), F32), jax.ShapeDtypeStruct((t, d), F32), jax.ShapeDtypeStruct((1, d), F32)],
        compiler_params=_row_params("arbitrary"),
    )(h, g, tgt)


def _s5_disc_math(a_re, a_im, log_dt):
    lam_re = jnp.minimum(a_re, -1e-4)
    lam_im = a_im
    dt = jnp.exp(log_dt)
    mag = jnp.exp(lam_re * dt)
    abar_re = mag * jnp.cos(lam_im * dt)
    abar_im = mag * jnp.sin(lam_im * dt)
    den = lam_re * lam_re + lam_im * lam_im
    nr = abar_re - 1.0
    ni = abar_im
    return abar_re, abar_im, (nr * lam_re + ni * lam_im) / den, (ni * lam_re - nr * lam_im) / den


def _s5_disc(a_re, a_im, log_dt):
    gp = jax.ShapeDtypeStruct(a_re.shape, F32)

    def body(ar_ref, ai_ref, ld_ref, o0, o1, o2, o3):
        outs = _s5_disc_math(ar_ref[...], ai_ref[...], ld_ref[...])
        for o_ref, val in zip((o0, o1, o2, o3), outs):
            o_ref[...] = val

    return pl.pallas_call(body, name="s5_disc", out_shape=[gp] * 4)(a_re, a_im, log_dt)


def _s5_disc_bwd(a_re, a_im, log_dt, cts):
    def body(ar_ref, ai_ref, ld_ref, c0, c1, c2, c3, dar_ref, dai_ref, dld_ref):
        _, vjp = jax.vjp(_s5_disc_math, ar_ref[...], ai_ref[...], ld_ref[...])
        dar, dai, dld = vjp((c0[...], c1[...], c2[...], c3[...]))
        dar_ref[...] = dar
        dai_ref[...] = dai
        dld_ref[...] = dld

    return pl.pallas_call(
        body, name="s5_disc_bwd",
        out_shape=[jax.ShapeDtypeStruct(a_re.shape, F32), jax.ShapeDtypeStruct(a_im.shape, F32),
                   jax.ShapeDtypeStruct(log_dt.shape, F32)],
    )(a_re, a_im, log_dt, *cts)


def _blockdiag(w, gl):
    g, r, c = w.shape
    w = w.reshape(g // gl, gl, r, c)
    eye = jnp.eye(gl, dtype=w.dtype)
    return (w[:, :, :, None, :] * eye[None, :, None, :, None]).reshape(g // gl, gl * r, gl * c)


def _blockdiag_take(m, gl, r, c):
    nb = m.shape[0]
    m = m.reshape(nb, gl, r, gl, c)
    return jnp.stack([m[:, g, :, g, :] for g in range(gl)], axis=1).reshape(nb * gl, r, c)


def _cmul(ar, ai, br, bi):
    return ar * br - ai * bi, ar * bi + ai * br


def _powers(ar, ai, n):
    out = [(ar, ai)]
    for _ in range(n - 1):
        out.append(_cmul(ar, ai, *out[-1]))
    return out


def _interleave(a, lc):
    t, d = a.shape
    return a.reshape(t // lc, SUBLANES, lc // SUBLANES, d).transpose(0, 2, 1, 3).reshape(t, d)


def _deinterleave(a, lc):
    t, d = a.shape
    return a.reshape(t // lc, lc // SUBLANES, SUBLANES, d).transpose(0, 2, 1, 3).reshape(t, d)


def _seg_rows(k):
    return slice(k * SUBLANES, (k + 1) * SUBLANES)


def _seg_scan(re_ref, im_ref, ar, ai, seg, reverse=False):
    cr = ci = jnp.zeros((SUBLANES, re_ref.shape[1]), F32)
    for k in (reversed(range(seg)) if reverse else range(seg)):
        rows = _seg_rows(k)
        cr, ci = ar * cr - ai * ci + re_ref[rows, :], ar * ci + ai * cr + im_ref[rows, :]
        re_ref[rows, :] = cr
        im_ref[rows, :] = ci
    return cr, ci


def _seg_entering(er, ei, pr, pi, c0r, c0i, reverse=False):
    rows = lax.broadcasted_iota(jnp.int32, er.shape, 0)
    vr = vi = jnp.zeros_like(er)
    cr, ci = c0r, c0i
    for j in (reversed(range(SUBLANES)) if reverse else range(SUBLANES)):
        vr = jnp.where(rows == j, cr, vr)
        vi = jnp.where(rows == j, ci, vi)
        cr, ci = er[j:j + 1, :] + pr * cr - pi * ci, ei[j:j + 1, :] + pr * ci + pi * cr
    return vr, vi, cr, ci


def _s5_scan_fwd(u, bre, bim, cre, cim, par, dskip, bl, seq, lc):
    t, d = u.shape
    nb, gw, ns = bre.shape
    nc = seq // lc
    seg = lc // SUBLANES

    def body(u_ref, bre_ref, bim_ref, cre_ref, cim_ref, par_ref, d_ref, y_ref, z_ref, st_ref, carry, sre, sim):
        @pl.when(pl.program_id(2) == 0)
        def _():
            carry[...] = jnp.zeros_like(carry)

        st_ref[...] = carry[...]
        uu = u_ref[...]
        ug = uu.astype(BF16)
        wre = _dot(ug, bre_ref[...])
        wim = _dot(ug, bim_ref[...])
        ar, ai = par_ref[0:1, :], par_ref[1:2, :]
        fr, fi = par_ref[2:3, :], par_ref[3:4, :]
        sre[...] = fr * wre - fi * wim
        sim[...] = fr * wim + fi * wre
        pows = _powers(ar, ai, seg)
        er, ei = _seg_scan(sre, sim, ar, ai, seg)
        vr, vi, cr, ci = _seg_entering(er, ei, *pows[-1], carry[0:1, :], carry[1:2, :])
        carry[0:1, :] = cr
        carry[1:2, :] = ci
        for k in range(seg):
            rows = _seg_rows(k)
            fr_k, fi_k = _cmul(*pows[k], vr, vi)
            sre[rows, :] += fr_k
            sim[rows, :] += fi_k
        y = _dot(sre[...].astype(BF16), cre_ref[...]) - _dot(sim[...].astype(BF16), cim_ref[...]) + d_ref[...] * uu
        y_ref[...] = y
        z_ref[...] = _gelu(y).astype(BF16)

    tok = pl.BlockSpec((lc, gw), lambda g, b, c: (b * nc + c, g))
    mat_b = pl.BlockSpec((None, gw, ns), lambda g, b, c: (g, 0, 0))
    mat_c = pl.BlockSpec((None, ns, gw), lambda g, b, c: (g, 0, 0))
    return pl.pallas_call(
        body, name="s5_scan_fwd", grid=(nb, bl, nc),
        in_specs=[tok, mat_b, mat_b, mat_c, mat_c, pl.BlockSpec((None, 8, ns), lambda g, b, c: (g, 0, 0)),
                  pl.BlockSpec((1, gw), lambda g, b, c: (0, g))],
        out_specs=[tok, tok, pl.BlockSpec((None, None, 2, ns), lambda g, b, c: (g, b * nc + c, 0, 0))],
        out_shape=[jax.ShapeDtypeStruct((t, d), F32), jax.ShapeDtypeStruct((t, d), BF16),
                   jax.ShapeDtypeStruct((nb, bl * nc, 2, ns), F32)],
        scratch_shapes=[pltpu.VMEM((2, ns), F32), pltpu.VMEM((lc, ns), F32), pltpu.VMEM((lc, ns), F32)],
        compiler_params=pltpu.CompilerParams(dimension_semantics=("parallel", "arbitrary", "arbitrary"),
                                             vmem_limit_bytes=V7X_VMEM_LIMIT),
    )(u, bre, bim, cre, cim, par, dskip)


def _s5_scan_bwd(u, dy, st, bre, bim, cre, cim, par, dskip, bl, seq, lc):
    t, d = u.shape
    nb, gw, ns = bre.shape
    nc = seq // lc
    seg = lc // SUBLANES

    def body(u_ref, dy_ref, st_ref, bre_ref, bim_ref, cre_ref, cim_ref, par_ref, d_ref,
             du_ref, dbre_ref, dbim_ref, dcre_ref, dcim_ref, dpar_ref, dd_ref,
             lcarry, sre, sim, wre_s, wim_s, lre, lim):
        b, c = pl.program_id(1), pl.program_id(2)

        @pl.when((b == 0) & (c == 0))
        def _():
            for ref in (dbre_ref, dbim_ref, dcre_ref, dcim_ref, dpar_ref, dd_ref):
                ref[...] = jnp.zeros_like(ref)

        @pl.when(c == 0)
        def _():
            lcarry[...] = jnp.zeros_like(lcarry)

        uu = u_ref[...]
        ug = uu.astype(BF16)
        dyv = dy_ref[...]
        dyb = dyv.astype(BF16)
        ar, ai = par_ref[0:1, :], par_ref[1:2, :]
        fr, fi = par_ref[2:3, :], par_ref[3:4, :]
        wre = _dot(ug, bre_ref[...])
        wim = _dot(ug, bim_ref[...])
        wre_s[...] = wre
        wim_s[...] = wim
        sre[...] = fr * wre - fi * wim
        sim[...] = fr * wim + fi * wre

        pows = _powers(ar, ai, seg)
        er, ei = _seg_scan(sre, sim, ar, ai, seg)
        svr, svi, _, _ = _seg_entering(er, ei, *pows[-1], st_ref[0:1, :], st_ref[1:2, :])
        for k in range(seg):
            rows = _seg_rows(k)
            fr_k, fi_k = _cmul(*pows[k], svr, svi)
            sre[rows, :] += fr_k
            sim[rows, :] += fi_k

        lre[...] = _dot_nt(dyb, cre_ref[...])
        lim[...] = -_dot_nt(dyb, cim_ref[...])
        er, ei = _seg_scan(lre, lim, ar, -ai, seg, reverse=True)
        lvr, lvi, cr, ci = _seg_entering(er, ei, pows[-1][0], -pows[-1][1], lcarry[0:1, :], lcarry[1:2, :], reverse=True)
        lcarry[0:1, :] = cr
        lcarry[1:2, :] = ci
        dar = dai = jnp.zeros_like(er)
        for k in range(seg):
            rows = _seg_rows(k)
            pr, pi = pows[seg - 1 - k]
            fr_k, fi_k = _cmul(pr, -pi, lvr, lvi)
            lr = lre[rows, :] + fr_k
            li = lim[rows, :] + fi_k
            lre[rows, :] = lr
            lim[rows, :] = li
            spr, spi = (svr, svi) if k == 0 else (sre[_seg_rows(k - 1), :], sim[_seg_rows(k - 1), :])
            dar = dar + lr * spr + li * spi
            dai = dai + li * spr - lr * spi

        lr, li = lre[...], lim[...]
        wr, wi = wre_s[...], wim_s[...]
        dpar_ref[0:1, :] += jnp.sum(dar, axis=0, keepdims=True)
        dpar_ref[1:2, :] += jnp.sum(dai, axis=0, keepdims=True)
        dpar_ref[2:3, :] += jnp.sum(lr * wr + li * wi, axis=0, keepdims=True)
        dpar_ref[3:4, :] += jnp.sum(li * wr - lr * wi, axis=0, keepdims=True)
        dwr = (fr * lr + fi * li).astype(BF16)
        dwi = (fr * li - fi * lr).astype(BF16)
        dsk = d_ref[...]
        du_ref[...] = _dot_nt(dwr, bre_ref[...]) + _dot_nt(dwi, bim_ref[...]) + dsk * dyv
        dd_ref[...] += jnp.sum(dyv * uu, axis=0, keepdims=True)
        dbre_ref[...] += _dot_tn(ug, dwr)
        dbim_ref[...] += _dot_tn(ug, dwi)
        dcre_ref[...] += _dot_tn(sre[...].astype(BF16), dyb)
        dcim_ref[...] -= _dot_tn(sim[...].astype(BF16), dyb)

    tok = pl.BlockSpec((lc, gw), lambda g, b, c: (b * nc + nc - 1 - c, g))
    mat_b = pl.BlockSpec((None, gw, ns), lambda g, b, c: (g, 0, 0))
    mat_c = pl.BlockSpec((None, ns, gw), lambda g, b, c: (g, 0, 0))
    rows8 = pl.BlockSpec((None, 8, ns), lambda g, b, c: (g, 0, 0))
    dvec = pl.BlockSpec((1, gw), lambda g, b, c: (0, g))
    tile = pltpu.VMEM((lc, ns), F32)
    return pl.pallas_call(
        body, name="s5_scan_bwd", grid=(nb, bl, nc),
        in_specs=[tok, tok, pl.BlockSpec((None, None, 2, ns), lambda g, b, c: (g, b * nc + nc - 1 - c, 0, 0)),
                  mat_b, mat_b, mat_c, mat_c, rows8, dvec],
        out_specs=[tok, mat_b, mat_b, mat_c, mat_c, rows8, dvec],
        out_shape=[jax.ShapeDtypeStruct((t, d), F32),
                   jax.ShapeDtypeStruct((nb, gw, ns), F32), jax.ShapeDtypeStruct((nb, gw, ns), F32),
                   jax.ShapeDtypeStruct((nb, ns, gw), F32), jax.ShapeDtypeStruct((nb, ns, gw), F32),
                   jax.ShapeDtypeStruct((nb, 8, ns), F32), jax.ShapeDtypeStruct((1, d), F32)],
        scratch_shapes=[pltpu.VMEM((2, ns), F32)] + [tile] * 6,
        compiler_params=pltpu.CompilerParams(dimension_semantics=("arbitrary", "arbitrary", "arbitrary"),
                                             vmem_limit_bytes=V7X_VMEM_LIMIT),
    )(u, dy, st, bre, bim, cre, cim, par, dskip)


def _s5_out(h, z, wglu):
    t, d = h.shape
    tm = _tile(t, ROW_TILE)

    def body(h_ref, z_ref, w_ref, o_ref):
        zz = _dot(z_ref[...], w_ref[...])
        o_ref[...] = h_ref[...] + zz[:, :d] * jax.nn.sigmoid(zz[:, d:])

    return pl.pallas_call(
        body, name="s5_out", grid=(t // tm,),
        in_specs=[_rows(tm, d), _rows(tm, d), _wspec(wglu)],
        out_specs=_rows(tm, d), out_shape=jax.ShapeDtypeStruct((t, d), F32),
        compiler_params=_row_params(),
    )(h, z, wglu.arr)


def _s5_out_bwd(dh, z, y, wglu):
    t, d = dh.shape
    tm = _tile(t, ROW_TILE)

    def body(dh_ref, z_ref, y_ref, w_ref, dy_ref, dzz_ref):
        zz = _dot(z_ref[...], w_ref[...])
        out, sg = zz[:, :d], jax.nn.sigmoid(zz[:, d:])
        dh_v = dh_ref[...]
        dzz = jnp.concatenate([dh_v * sg, dh_v * out * sg * (1.0 - sg)], axis=1).astype(BF16)
        dzz_ref[...] = dzz
        dy_ref[...] = _dot_nt(dzz, w_ref[...]) * _gelu_grad(y_ref[...])

    return pl.pallas_call(
        body, name="s5_out_bwd", grid=(t // tm,),
        in_specs=[_rows(tm, d), _rows(tm, d), _rows(tm, d), _wspec(wglu)],
        out_specs=[_rows(tm, d), _rows(tm, 2 * d)],
        out_shape=[jax.ShapeDtypeStruct((t, d), F32), jax.ShapeDtypeStruct((t, 2 * d), BF16)],
        compiler_params=_row_params(),
    )(dh, z, y, wglu.arr)


def _head_spec(seq, w):
    return pl.BlockSpec((None, SB_HEADS_PER_STEP, seq, w), lambda b, h: (b, h, 0, 0))


def _tri_and_ones(kind):
    row = lax.broadcasted_iota(jnp.int32, (Q_BLOCK, Q_BLOCK), 0)
    col = lax.broadcasted_iota(jnp.int32, (Q_BLOCK, Q_BLOCK), 1)
    tri = {"after": row > col, "upto": row <= col, "before": row < col}[kind]
    return jnp.concatenate([tri.astype(BF16), jnp.ones((Q_BLOCK, Q_BLOCK), BF16)], axis=1)


def _sb_fwd(q, k, v):
    bsz, nh, seq, dh = q.shape
    tq = min(SB_Q_TILE, seq // 2)
    nq, nsub = seq // tq, tq // Q_BLOCK
    scale = dh ** -0.5
    hp = SB_HEADS_PER_STEP
    qb = Q_BLOCK

    def body(q_ref, k_ref, v_ref, o_ref, tot_ref):
        strict = lax.broadcasted_iota(jnp.int32, (qb, qb), 1) < lax.broadcasted_iota(jnp.int32, (qb, qb), 0)
        sums = _tri_and_ones("after")

        def sweep(qs, c0, units, carry):
            heads = sorted({u[0] for u in units})
            kbs = {hh: k_ref[hh, pl.ds(c0, qb), :] for hh in heads}
            vbs = {hh: v_ref[hh, pl.ds(c0, qb), :] for hh in heads}
            zs = [_dot_nt(qs[hh][s], kbs[hh]) * scale for hh, s, _ in units]
            lkrs = [_neg_softplus(z) for z in zs]
            lks = [jnp.where(strict, lkr, 0.0) if dg else lkr for lkr, (_, _, dg) in zip(lkrs, units)]
            css = [_dot2(lk, sums) for lk in lks]
            carry = dict(carry)
            for (hh, s, dg), z, lkr, cs in zip(units, zs, lkrs, css):
                acc, run = carry[hh, s]
                att = jnp.exp(z + lkr + cs[:, :qb] + run)
                if dg:
                    att = jnp.where(strict, att, 0.0)
                carry[hh, s] = (acc + _dot(att.astype(BF16), vbs[hh]), run + cs[:, qb:])
            return carry

        def q_loop(qi, _):
            r0 = qi * tq
            qs = {hh: [q_ref[hh, pl.ds(pl.multiple_of(r0 + s * qb, qb), qb), :] for s in range(nsub)] for hh in range(hp)}
            carry = {(hh, s): (jnp.zeros((qb, dh), F32), jnp.zeros((qb, qb), F32)) for hh in range(hp) for s in range(nsub)}
            for jj in reversed(range(nsub)):
                units = [(hh, s, s == jj) for hh in range(hp) for s in range(jj, nsub)]
                carry = sweep(qs, pl.multiple_of(r0 + jj * qb, qb), units, carry)
            units = [(hh, s, False) for hh in range(hp) for s in range(nsub)]
            carry = lax.fori_loop(
                0, nsub * qi, lambda t, c: sweep(qs, pl.multiple_of((nsub * qi - 1 - t) * qb, qb), units, c), carry)
            for (hh, s), (acc, run) in carry.items():
                rows = pl.ds(pl.multiple_of(r0 + s * qb, qb), qb)
                o_ref[hh, rows, :] = acc.astype(BF16)
                tot_ref[hh, rows, :] = run[:, 0:1]
            return 0

        lax.fori_loop(0, nq, q_loop, 0)

    return pl.pallas_call(
        body, name="sb_fwd", grid=(bsz, nh // hp),
        in_specs=[_head_spec(seq, dh)] * 3,
        out_specs=[_head_spec(seq, dh), _head_spec(seq, 1)],
        out_shape=[jax.ShapeDtypeStruct((bsz, nh, seq, dh), BF16), jax.ShapeDtypeStruct((bsz, nh, seq, 1), F32)],
        compiler_params=pltpu.CompilerParams(dimension_semantics=("parallel", "parallel"), vmem_limit_bytes=V7X_VMEM_LIMIT),
    )(q, k, v)


def _sb_bwd(q, k, v, do, tot):
    bsz, nh, seq, dh = q.shape
    tq = min(SB_Q_TILE, seq // 2)
    nq, nsub = seq // tq, tq // Q_BLOCK
    scale = dh ** -0.5
    hp = SB_HEADS_PER_STEP
    qb = Q_BLOCK

    def body(q_ref, k_ref, v_ref, do_ref, tot_ref, dq_ref, dk_ref, dv_ref, dka, dva):
        dka[...] = jnp.zeros_like(dka)
        dva[...] = jnp.zeros_like(dva)
        strict = lax.broadcasted_iota(jnp.int32, (qb, qb), 1) < lax.broadcasted_iota(jnp.int32, (qb, qb), 0)
        upto = _tri_and_ones("upto")
        before = _tri_and_ones("before")

        def sweep(qf, dof, tots, c0, first, units, carry):
            heads = sorted({u[0] for u in units})
            kbs = {hh: k_ref[hh, pl.ds(c0, qb), :] for hh in heads}
            vbs = {hh: v_ref[hh, pl.ds(c0, qb), :] for hh in heads}
            sub = lambda a, s: a[s * qb:(s + 1) * qb, :]
            zs = [_dot_nt(sub(qf[hh], s), kbs[hh]) * scale for hh, s, _ in units]
            das = [_dot_nt(sub(dof[hh], s), vbs[hh]) for hh, s, _ in units]
            lkrs = [_neg_softplus(z) for z in zs]
            lks = [jnp.where(strict, lkr, 0.0) if dg else lkr for lkr, (_, _, dg) in zip(lkrs, units)]
            css = [_dot2(lk, upto) for lk in lks]
            lsigs, atts, gls = [], [], []
            for (hh, s, dg), z, lkr, cs, da in zip(units, zs, lkrs, css, das):
                lsig = z + lkr
                att = jnp.exp(lsig + (tots[hh, s] - (cs[:, :qb] + carry[hh, s][1])))
                if dg:
                    att = jnp.where(strict, att, 0.0)
                lsigs.append(lsig)
                atts.append(att)
                gls.append(da * att)
            gss = [_dot2(gl, before) for gl in gls]
            carry = dict(carry)
            dzs = {}
            for (hh, s, dg), lsig, gl, cs, gs in zip(units, lsigs, gls, css, gss):
                dqa, pre, gpre = carry[hh, s]
                sig = jnp.exp(lsig)
                dz = gl * (1.0 - sig) - (gs[:, :qb] + gpre) * sig
                if dg:
                    dz = jnp.where(strict, dz, 0.0)
                dz = (dz * scale).astype(BF16)
                dzs[hh, s] = dz
                carry[hh, s] = (dqa + _dot(dz, kbs[hh]), pre + cs[:, qb:], gpre + gs[:, qb:])
            att_of = {(hh, s): a for (hh, s, _), a in zip(units, atts)}
            for hh in heads:
                dzc = jnp.concatenate([dzs[hh, s] for s in range(first, nsub)], axis=0)
                attc = jnp.concatenate([att_of[hh, s].astype(BF16) for s in range(first, nsub)], axis=0)
                dka[hh, pl.ds(c0, qb), :] += _dot_tn(dzc, qf[hh][first * qb:, :])
                dva[hh, pl.ds(c0, qb), :] += _dot_tn(attc, dof[hh][first * qb:, :])
            return carry

        def q_loop(qi, _):
            r0 = pl.multiple_of(qi * tq, tq)
            qf = [q_ref[hh, pl.ds(r0, tq), :] for hh in range(hp)]
            dof = [do_ref[hh, pl.ds(r0, tq), :] for hh in range(hp)]
            tots = {(hh, s): jnp.broadcast_to(tot_ref[hh, pl.ds(pl.multiple_of(r0 + s * qb, qb), qb), :], (qb, qb))
                    for hh in range(hp) for s in range(nsub)}
            carry = {(hh, s): (jnp.zeros((qb, dh), F32), jnp.zeros((qb, qb), F32), jnp.zeros((qb, qb), F32))
                     for hh in range(hp) for s in range(nsub)}
            units = [(hh, s, False) for hh in range(hp) for s in range(nsub)]
            carry = lax.fori_loop(
                0, nsub * qi, lambda kj, c: sweep(qf, dof, tots, pl.multiple_of(kj * qb, qb), 0, units, c), carry)
            for jj in range(nsub):
                units = [(hh, s, s == jj) for hh in range(hp) for s in range(jj, nsub)]
                carry = sweep(qf, dof, tots, pl.multiple_of(r0 + jj * qb, qb), jj, units, carry)
            for (hh, s), (dqa, _, _) in carry.items():
                dq_ref[hh, pl.ds(pl.multiple_of(r0 + s * qb, qb), qb), :] = dqa.astype(BF16)
            return 0

        lax.fori_loop(0, nq, q_loop, 0)
        dk_ref[...] = dka[...].astype(BF16)
        dv_ref[...] = dva[...].astype(BF16)

    hs = _head_spec(seq, dh)
    return pl.pallas_call(
        body, name="sb_bwd", grid=(bsz, nh // hp),
        in_specs=[hs, hs, hs, hs, _head_spec(seq, 1)],
        out_specs=[hs, hs, hs],
        out_shape=[jax.ShapeDtypeStruct((bsz, nh, seq, dh), BF16)] * 3,
        scratch_shapes=[pltpu.VMEM((hp, seq, dh), F32), pltpu.VMEM((hp, seq, dh), F32)],
        compiler_params=pltpu.CompilerParams(dimension_semantics=("parallel", "parallel"), vmem_limit_bytes=V7X_VMEM_LIMIT),
    )(q, k, v, do, tot)


def _to_heads(a, bsz, seq):
    return a.reshape(bsz, seq, -1, HEAD_DIM).transpose(0, 2, 1, 3)


def _from_heads(a):
    bsz, nh, seq, dh = a.shape
    return a.transpose(0, 2, 1, 3).reshape(bsz * seq, nh * dh)


def _coords():
    return lax.axis_index("x"), lax.axis_index("y"), lax.axis_index("c")


def _all_gather(x, name):
    r, c = x.shape

    def body(x_ref, out_ref, send_sems, recv_sems, local_sem):
        mx, my, mc = _coords()
        me, sibling = (mx, my, mc), (mx, my, 1 - mc)
        chips = [(1 - mx, my), (mx, 1 - my), (1 - mx, 1 - my)]

        def blk(px, py, pc):
            return out_ref.at[4 * px + 2 * py + pc]

        def copy(k, block, to, src=None):
            return pltpu.make_async_remote_copy(
                src_ref=blk(*block) if src is None else src, dst_ref=blk(*block),
                send_sem=send_sems.at[k], recv_sem=recv_sems.at[k], device_id=to, device_id_type=MESH)

        mine = pltpu.make_async_copy(x_ref, blk(*me), local_sem)
        mine.start()
        first = [copy(0, me, sibling, src=x_ref)]
        first += [copy(1 + j, me, (*chip, mc), src=x_ref) for j, chip in enumerate(chips)]
        for cp in first:
            cp.start()
        passed = [copy(4 + j, (*chip, mc), sibling) for j, chip in enumerate(chips)]
        for j, chip in enumerate(chips):
            copy(1 + j, (*chip, mc), me).wait_recv()
            passed[j].start()
        copy(0, sibling, me).wait_recv()
        for j, chip in enumerate(chips):
            copy(4 + j, (*chip, 1 - mc), me).wait_recv()
        for cp in first + passed:
            cp.wait_send()
        mine.wait()

    return pl.pallas_call(
        body, name=name,
        out_shape=jax.ShapeDtypeStruct((N_DEV, r, c), x.dtype),
        in_specs=[pl.BlockSpec(memory_space=pl.ANY)],
        out_specs=pl.BlockSpec(memory_space=pl.ANY),
        scratch_shapes=[pltpu.SemaphoreType.DMA((7,)), pltpu.SemaphoreType.DMA((7,)), pltpu.SemaphoreType.DMA],
    )(x)


_ANY = pl.BlockSpec(memory_space=pl.ANY)


def _window(ref, shard_shape, by_cols, dev):
    if by_cols:
        n = shard_shape[2]
        return ref.at[:, :, pl.ds(pl.multiple_of(dev * n, n), n)]
    k = shard_shape[1]
    return ref.at[:, pl.ds(pl.multiple_of(dev * k, k), k), :]


def _gather_weights(shards, by_cols):
    na = len(shards)
    fulls = [jax.ShapeDtypeStruct(
        (s.shape[0], s.shape[1], s.shape[2] * N_DEV) if c else (s.shape[0], s.shape[1] * N_DEV, s.shape[2]), s.dtype)
        for s, c in zip(shards, by_cols)]

    def body(*refs):
        x_refs, out_refs = refs[:na], refs[na:2 * na]
        send_sems, recv_sems, local_sems = refs[2 * na:]
        mx, my, mc = _coords()
        me, sibling = (mx, my, mc), (mx, my, 1 - mc)
        chips = [(1 - mx, my), (mx, 1 - my), (1 - mx, 1 - my)]

        def blk(a, dev):
            px, py, pc = dev
            return _window(out_refs[a], x_refs[a].shape, by_cols[a], 4 * px + 2 * py + pc)

        def copy(a, k, block, to, src=None):
            return pltpu.make_async_remote_copy(
                src_ref=blk(a, block) if src is None else src, dst_ref=blk(a, block),
                send_sem=send_sems.at[a, k], recv_sem=recv_sems.at[a, k], device_id=to, device_id_type=MESH)

        mines = [pltpu.make_async_copy(x_refs[a], blk(a, me), local_sems.at[a]) for a in range(na)]
        first = [[copy(a, 0, me, sibling, src=x_refs[a])]
                 + [copy(a, 1 + j, me, (*chip, mc), src=x_refs[a]) for j, chip in enumerate(chips)] for a in range(na)]
        passed = [[copy(a, 4 + j, (*chip, mc), sibling) for j, chip in enumerate(chips)] for a in range(na)]
        for a in range(na):
            mines[a].start()
            for cp in first[a]:
                cp.start()
        for a in range(na):
            for j, chip in enumerate(chips):
                copy(a, 1 + j, (*chip, mc), me).wait_recv()
                passed[a][j].start()
        for a in range(na):
            copy(a, 0, sibling, me).wait_recv()
            for j, chip in enumerate(chips):
                copy(a, 4 + j, (*chip, 1 - mc), me).wait_recv()
        for a in range(na):
            for cp in first[a] + passed[a]:
                cp.wait_send()
            mines[a].wait()

    return pl.pallas_call(
        body, name="gather_weights", out_shape=fulls, in_specs=[_ANY] * na, out_specs=[_ANY] * na,
        scratch_shapes=[pltpu.SemaphoreType.DMA((na, 7)), pltpu.SemaphoreType.DMA((na, 7)), pltpu.SemaphoreType.DMA((na,))],
    )(*shards)


def _rs_sibling(grads, shard_shapes, by_cols):
    na = len(grads)
    slabs = [jax.ShapeDtypeStruct((N_CHIP,) + tuple(s), F32) for s in shard_shapes]

    def body(*refs):
        g_refs, got_refs = refs[:na], refs[na:2 * na]
        send_sems, recv_sems = refs[2 * na:]
        mx, my, mc = _coords()
        copies = []
        for a in range(na):
            for q in range(N_CHIP):
                dev = 2 * q + 1 - mc
                theirs = g_refs[a].at[dev] if by_cols[a] else _window(g_refs[a], shard_shapes[a], False, dev)
                copies.append(pltpu.make_async_remote_copy(
                    src_ref=theirs, dst_ref=got_refs[a].at[q], send_sem=send_sems.at[a, q], recv_sem=recv_sems.at[a, q],
                    device_id=(mx, my, 1 - mc), device_id_type=MESH))
        for cp in copies:
            cp.start()
        for cp in copies:
            cp.wait()

    sems = pltpu.SemaphoreType.DMA((na, N_CHIP))
    return pl.pallas_call(
        body, name="rs_sibling", out_shape=slabs, in_specs=[_ANY] * na, out_specs=[_ANY] * na,
        scratch_shapes=[sems, sems],
    )(*grads)


def _rs_add(g, got, core, by_cols, name):
    nq, nl, r, c = got.shape
    tr = _tile(r, 256, 2 * SUBLANES)

    def body(core_ref, a_ref, b_ref, o_ref):
        o_ref[...] = (a_ref[...] + b_ref[...]).astype(BF16)

    if by_cols:
        mine = pl.BlockSpec((None, None, tr, c), lambda q, l, i, core_ref: (2 * q + core_ref[0], l, i, 0))
    else:
        mine = pl.BlockSpec((None, tr, c), lambda q, l, i, core_ref: (l, (2 * q + core_ref[0]) * (r // tr) + i, 0))
    spec = pl.BlockSpec((None, None, tr, c), lambda q, l, i, core_ref: (q, l, i, 0))
    return pl.pallas_call(
        body, name=name, out_shape=jax.ShapeDtypeStruct(got.shape, BF16),
        grid_spec=pltpu.PrefetchScalarGridSpec(num_scalar_prefetch=1, grid=(nq, nl, r // tr),
                                               in_specs=[mine, spec], out_specs=spec),
        compiler_params=pltpu.CompilerParams(dimension_semantics=("parallel",) * 3),
    )(core, g, got)


def _rs_chips(parts):
    na = len(parts)

    def body(*refs):
        p_refs, out_refs = refs[:na], refs[na:2 * na]
        send_sems, recv_sems, local_sems = refs[2 * na:]
        mx, my, mc = _coords()
        here = 2 * mx + my
        chips = [(1 - mx, my), (mx, 1 - my), (1 - mx, 1 - my)]
        copies = []
        for a in range(na):
            copies.append(pltpu.make_async_copy(p_refs[a].at[here], out_refs[a].at[here], local_sems.at[a]))
            for j, (cx, cy) in enumerate(chips):
                copies.append(pltpu.make_async_remote_copy(
                    src_ref=p_refs[a].at[2 * cx + cy], dst_ref=out_refs[a].at[here],
                    send_sem=send_sems.at[a, j], recv_sem=recv_sems.at[a, j], device_id=(cx, cy, mc), device_id_type=MESH))
        for cp in copies:
            cp.start()
        for cp in copies:
            cp.wait()

    return pl.pallas_call(
        body, name="rs_chips", out_shape=[jax.ShapeDtypeStruct(p.shape, p.dtype) for p in parts],
        in_specs=[_ANY] * na, out_specs=[_ANY] * na,
        scratch_shapes=[pltpu.SemaphoreType.DMA((na, 3)), pltpu.SemaphoreType.DMA((na, 3)), pltpu.SemaphoreType.DMA((na,))],
    )(*parts)


def _adamw(parts, w, m, v, name):
    n, r, c = parts.shape
    tr = _tile(r, 320, 2 * SUBLANES)
    bc1 = 1.0 - ADAM_B1 ** ADAM_STEP
    bc2 = 1.0 - ADAM_B2 ** ADAM_STEP

    def body(p_ref, w_ref, m_ref, v_ref, g_ref, d_ref, mo_ref, vo_ref):
        g = p_ref[0].astype(F32)
        for j in range(1, n):
            g = g + p_ref[j].astype(F32)
        mn = ADAM_B1 * m_ref[...] + (1.0 - ADAM_B1) * g
        vn = ADAM_B2 * v_ref[...] + (1.0 - ADAM_B2) * (g * g)
        g_ref[...] = g
        mo_ref[...] = mn
        vo_ref[...] = vn
        d_ref[...] = -ADAM_LR * ((mn / bc1) / (jnp.sqrt(vn / bc2) + ADAM_EPS) + ADAM_WD * w_ref[...])

    flat = pl.BlockSpec((tr, c), lambda i: (i, 0))
    return pl.pallas_call(
        body, name=name, grid=(r // tr,),
        in_specs=[pl.BlockSpec((n, tr, c), lambda i: (0, i, 0)), flat, flat, flat],
        out_specs=[flat] * 4, out_shape=[jax.ShapeDtypeStruct((r, c), F32)] * 4,
        compiler_params=pltpu.CompilerParams(dimension_semantics=("parallel",)),
    )(parts, w, m, v)


def _adam_math(g, w, m, v):
    bc1 = 1.0 - ADAM_B1 ** ADAM_STEP
    bc2 = 1.0 - ADAM_B2 ** ADAM_STEP
    mn = ADAM_B1 * m + (1.0 - ADAM_B1) * g
    vn = ADAM_B2 * v + (1.0 - ADAM_B2) * (g * g)
    return -ADAM_LR * ((mn / bc1) / (jnp.sqrt(vn / bc2) + ADAM_EPS) + ADAM_WD * w), mn, vn


def _adamw_shard(parts, w, m, v, name):
    nl, r, c = w.shape
    tr = _tile(r, 256, 2 * SUBLANES)

    def body(p_ref, w_ref, m_ref, v_ref, g_ref, d_ref, mo_ref, vo_ref):
        g = p_ref[0].astype(F32)
        for q in range(1, N_CHIP):
            g = g + p_ref[q].astype(F32)
        g = g[:, :c]
        g_ref[...] = g
        d_ref[...], mo_ref[...], vo_ref[...] = _adam_math(g, w_ref[...], m_ref[...], v_ref[...])

    native = pl.BlockSpec((None, tr, c), lambda l, i: (l, i, 0))
    return pl.pallas_call(
        body, name=name, grid=(nl, r // tr),
        in_specs=[pl.BlockSpec((N_CHIP, None, tr, parts.shape[3]), lambda l, i: (0, l, i, 0)), native, native, native],
        out_specs=[native] * 4, out_shape=[jax.ShapeDtypeStruct(w.shape, F32)] * 4,
        compiler_params=pltpu.CompilerParams(dimension_semantics=("parallel", "parallel"), vmem_limit_bytes=V7X_VMEM_LIMIT),
    )(parts, w, m, v)


def _pack(arrs, dtype=F32):
    cols = []
    for a in arrs:
        f = a.reshape(-1).astype(dtype)
        cols.append(jnp.pad(f, (0, -f.shape[0] % FLAT_W)))
    flat = jnp.concatenate(cols)
    flat = jnp.pad(flat, (0, -flat.shape[0] % (FLAT_W * SUBLANES)))
    return flat.reshape(-1, FLAT_W)


def _unpack(flat, shapes, lead=()):
    flat = flat.reshape(lead + (-1,))
    out, off = [], 0
    for s in shapes:
        n = math.prod(s)
        out.append(flat[..., off:off + n].reshape(lead + tuple(s)))
        off += n + (-n % FLAT_W)
    return out


def kernel(x, p, ffn1_norm, ffn1_w1, ffn1_w3, ffn1_w2, mix_norm, ffn2_norm, ffn2_w1, ffn2_w3, ffn2_w2, ple_norm, ple_proj, ple_gate, s5_w_in, s5_a_re, s5_a_im, s5_log_dt, s5_b_re, s5_b_im, s5_c_re, s5_c_im, s5_d, s5_w_glu, sb_w_qkv, sb_w_o, final_norm, loss_target, m_ffn1_norm, m_ffn1_w1, m_ffn1_w3, m_ffn1_w2, m_mix_norm, m_ffn2_norm, m_ffn2_w1, m_ffn2_w3, m_ffn2_w2, m_ple_norm, m_ple_proj, m_ple_gate, m_s5_w_in, m_s5_a_re, m_s5_a_im, m_s5_log_dt, m_s5_b_re, m_s5_b_im, m_s5_c_re, m_s5_c_im, m_s5_d, m_s5_w_glu, m_sb_w_qkv, m_sb_w_o, m_final_norm, v_ffn1_norm, v_ffn1_w1, v_ffn1_w3, v_ffn1_w2, v_mix_norm, v_ffn2_norm, v_ffn2_w1, v_ffn2_w3, v_ffn2_w2, v_ple_norm, v_ple_proj, v_ple_gate, v_s5_w_in, v_s5_a_re, v_s5_a_im, v_s5_log_dt, v_s5_b_re, v_s5_b_im, v_s5_c_re, v_s5_c_im, v_s5_d, v_s5_w_glu, v_sb_w_qkv, v_sb_w_o, v_final_norm):
    given = dict(locals())
    wts = {n: given[n] for n in WEIGHTS}
    mom = {n: given["m_" + n] for n in WEIGHTS}
    var = {n: given["v_" + n] for n in WEIGHTS}
    bl, seq, d = x.shape
    t = bl * seq
    depth = p.shape[0]

    hid_pad = -wts["ffn1_w1"].shape[-1] % LANES

    def padded(n):
        a = wts[n].astype(BF16)
        if n in FFN_COL:
            return jnp.pad(a, ((0, 0), (0, 0), (0, hid_pad)))
        if n in FFN_ROW:
            return jnp.pad(a, ((0, 0), (0, hid_pad), (0, 0)))
        return a

    sent = [padded(n) for n in SHARDED]
    shard_shape = {n: a.shape for n, a in zip(SHARDED, sent)}
    by_cols = [n in COL_SHARDED for n in SHARDED]
    full = dict(zip(SHARDED, _gather_weights(sent, by_cols)))

    def row(a):
        return a.reshape(1, -1)

    def wt(n, layer=0, col=0, width=None):
        return _W(full[n], layer, col, width)

    gbuf = {}

    def wgrad(n, x_act, dy, layer=0, scale=1.0):
        like = (N_DEV,) + shard_shape[n] if n in COL_SHARDED else full[n].shape
        gbuf[n] = _wgrad(x_act, dy, f"{n}_grad", like, gbuf.get(n), layer, scale)

    n_groups = d // S5_GROUP
    a_re, a_im = s5_a_re[0], s5_a_im[0]
    log_dt = s5_log_dt[0].reshape(n_groups, 1)
    disc = _s5_disc(a_re, a_im, log_dt)
    nb = n_groups // S5_BLOCK_GROUPS
    ns = S5_BLOCK_GROUPS * S5_STATE
    par = jnp.concatenate([jnp.stack([q.reshape(nb, ns) for q in disc], axis=1), jnp.zeros((nb, 4, ns), F32)], axis=1)
    bre = _blockdiag(s5_b_re[0].transpose(0, 2, 1), S5_BLOCK_GROUPS).astype(BF16)
    bim = _blockdiag(s5_b_im[0].transpose(0, 2, 1), S5_BLOCK_GROUPS).astype(BF16)
    cre = _blockdiag(s5_c_re[0].transpose(0, 2, 1), S5_BLOCK_GROUPS).astype(BF16)
    cim = _blockdiag(s5_c_im[0].transpose(0, 2, 1), S5_BLOCK_GROUPS).astype(BF16)
    lc = min(SCAN_CHUNK, seq)

    h = x.reshape(t, d)
    saved = []
    for i in range(depth):
        s = {"h0": h}
        h, s["n1"], s["a1"], s["b1"] = _ffn_fwd(h, row(ffn1_norm[i]), wt("ffn1_w1", i), wt("ffn1_w3", i), wt("ffn1_w2", i))
        s["h1"] = h
        j = i // 2
        if i % 2 == 0:
            s["hn"], u = _norm_lin(h, row(mix_norm[i]), [wt("s5_w_in", j)], [F32], "s5_in")
            s["u"] = _interleave(u, lc)
            y, z, s["st"] = _s5_scan_fwd(s["u"], bre, bim, cre, cim, par, row(s5_d[j]), bl, seq, lc)
            s["y"], s["z"] = _deinterleave(y, lc), _deinterleave(z, lc)
            h = _s5_out(h, s["z"], wt("s5_w_glu", j))
        else:
            wqkv = [wt("sb_w_qkv", j, col, d) for col in range(3)]
            s["hn"], q, k, v = _norm_lin(h, row(mix_norm[i]), wqkv, [BF16] * 3, "sb_in")
            s["q"], s["k"], s["v"] = (_to_heads(a, bl, seq) for a in (q, k, v))
            o, s["tot"] = _sb_fwd(s["q"], s["k"], s["v"])
            s["o"] = _from_heads(o)
            h = _lin_res(h, s["o"], wt("sb_w_o", j), "sb_out")
        s["h2"] = h
        h, s["n2"], s["a2"], s["b2"] = _ffn_fwd(h, row(ffn2_norm[i]), wt("ffn2_w1", i), wt("ffn2_w3", i), wt("ffn2_w2", i))
        s["h3"] = h
        s["p"] = p[i].reshape(t, -1)
        h, s["npl"] = _ple_fwd(h, row(ple_norm[i]), wt("ple_gate", i), s["p"], wt("ple_proj", i))
        saved.append(s)

    loss_part, dh, g_final = _loss_head(h, row(final_norm), loss_target.reshape(t, d))
    loss = lax.psum(loss_part[0, 0], ("x", "y", "c"))

    grads = {n: [None] * wts[n].shape[0] for n in REPLICATED if n != "final_norm"}
    grads["final_norm"] = g_final.reshape(-1)

    def ffn_bwd(dh, h_in, n, a, b, which, i):
        da, db, sact = _ffn_bwd_down(dh, a, b, wt(f"{which}_w2", i))
        wgrad(f"{which}_w2", sact, dh, i, scale=0.5)
        dh_in, dg = _lin_bwd_norm([da, db], [wt(f"{which}_w1", i), wt(f"{which}_w3", i)], h_in,
                                  row(wts[f"{which}_norm"][i]), dh, f"{which}_bwd_up")
        wgrad(f"{which}_w1", n, da, i)
        wgrad(f"{which}_w3", n, db, i)
        grads[f"{which}_norm"][i] = dg.reshape(-1)
        return dh_in

    for i in reversed(range(depth)):
        s = saved[i]
        j = i // 2
        dh, dgl, dpp, dg = _ple_bwd(dh, s["h3"], row(ple_norm[i]), s["npl"], s["p"], wt("ple_gate", i), wt("ple_proj", i))
        grads["ple_norm"][i] = dg.reshape(-1)
        wgrad("ple_gate", s["npl"], dgl, i)
        wgrad("ple_proj", s["p"], dpp, i)
        dh = ffn_bwd(dh, s["h2"], s["n2"], s["a2"], s["b2"], "ffn2", i)
        if i % 2 == 0:
            dy, dzz = _s5_out_bwd(dh, s["z"], s["y"], wt("s5_w_glu", j))
            wgrad("s5_w_glu", s["z"], dzz, j)
            du, dbre, dbim, dcre, dcim, dpar, dd = _s5_scan_bwd(
                s["u"], _interleave(dy, lc), s["st"], bre, bim, cre, cim, par, row(s5_d[j]), bl, seq, lc)
            du = _deinterleave(du, lc)
            cts = [dpar[:, r, :].reshape(n_groups, S5_STATE) for r in range(4)]
            g_are, g_aim, g_ldt = _s5_disc_bwd(a_re, a_im, log_dt, cts)
            grads["s5_a_re"][j], grads["s5_a_im"][j], grads["s5_log_dt"][j] = g_are, g_aim, g_ldt.reshape(-1)
            grads["s5_b_re"][j] = _blockdiag_take(dbre, S5_BLOCK_GROUPS, S5_GROUP, S5_STATE).transpose(0, 2, 1)
            grads["s5_b_im"][j] = _blockdiag_take(dbim, S5_BLOCK_GROUPS, S5_GROUP, S5_STATE).transpose(0, 2, 1)
            grads["s5_c_re"][j] = _blockdiag_take(dcre, S5_BLOCK_GROUPS, S5_STATE, S5_GROUP).transpose(0, 2, 1)
            grads["s5_c_im"][j] = _blockdiag_take(dcim, S5_BLOCK_GROUPS, S5_STATE, S5_GROUP).transpose(0, 2, 1)
            grads["s5_d"][j] = dd.reshape(-1)
            dh, dg = _lin_bwd_norm([du], [wt("s5_w_in", j)], s["h1"], row(mix_norm[i]), dh, "s5_in_bwd")
            wgrad("s5_w_in", s["hn"], du, j)
        else:
            do = _lin_nt(dh, wt("sb_w_o", j), "sb_out_bwd")
            wgrad("sb_w_o", s["o"], dh, j)
            dqkv = _sb_bwd(s["q"], s["k"], s["v"], _to_heads(do, bl, seq), s["tot"])
            dqkv = [_from_heads(a) for a in dqkv]
            dh, dg = _lin_bwd_norm(dqkv, [wt("sb_w_qkv", j, col, d) for col in range(3)], s["h1"], row(mix_norm[i]), dh, "sb_in_bwd")
            wgrad("sb_w_qkv", s["hn"], jnp.concatenate(dqkv, axis=1), j)
        grads["mix_norm"][i] = dg.reshape(-1)
        dh = ffn_bwd(dh, s["h0"], s["n1"], s["a1"], s["b1"], "ffn1", i)
    grad_x = dh.reshape(x.shape)
    for n in REPLICATED:
        if n != "final_norm":
            grads[n] = jnp.stack(grads[n])

    got = _rs_sibling([gbuf[n] for n in SHARDED], [shard_shape[n] for n in SHARDED], by_cols)
    core = lax.axis_index("c").astype(jnp.int32).reshape(1)
    parts = _rs_chips([_rs_add(gbuf[n], g, core, n in COL_SHARDED, f"rs_add_{n}") for n, g in zip(SHARDED, got)])
    res = {n: _adamw_shard(part, wts[n], mom[n], var[n], f"adamw_{n}") for n, part in zip(SHARDED, parts)}

    rep_shapes = [wts[n].shape for n in REPLICATED]
    rep_parts = _all_gather(_pack([grads[n].reshape(wts[n].shape) for n in REPLICATED]), "gather_small_grads")
    outs = _adamw(rep_parts, _pack([wts[n] for n in REPLICATED]), _pack([mom[n] for n in REPLICATED]),
                  _pack([var[n] for n in REPLICATED]), "adamw_replicated")
    res.update({n: vals for n, vals in zip(REPLICATED, zip(*[_unpack(o, rep_shapes) for o in outs]))})

    return (loss, grad_x, *[res[n][0] for n in WEIGHTS], *[res[n][1] for n in WEIGHTS],
            *[res[n][2] for n in WEIGHTS], *[res[n][3] for n in WEIGHTS])
```

```python
import math
from typing import NamedTuple, Optional

import jax
import jax.numpy as jnp
from jax import lax
from jax.experimental import pallas as pl
from jax.experimental.pallas import tpu as pltpu

F32 = jnp.float32
BF16 = jnp.bfloat16
MESH = pl.DeviceIdType.MESH

N_DEV = 8
N_CHIP = 4
EPS = 1e-6
S5_GROUP = 16
S5_STATE = 64
S5_BLOCK_GROUPS = 16
HEAD_DIM = 64
Q_BLOCK = 128
SB_Q_TILE = 256
SB_HEADS_PER_STEP = 4
SCAN_CHUNK = 256
SUBLANES = 8
LANES = 128
FLAT_W = 1024
ADAM_LR, ADAM_B1, ADAM_B2, ADAM_EPS, ADAM_WD, ADAM_STEP = 1e-3, 0.9, 0.999, 1e-8, 0.01, 10
V7X_VMEM_LIMIT = 56 * 1024 * 1024
ROW_TILE = 256
WGRAD_ROW_TILE = 1024
WGRAD_OUT_BYTES = 6 * 1024 * 1024
ELEMENTWISE_ROW_TILE = 512

SHARDED = ("ffn1_w1", "ffn1_w3", "ffn1_w2", "ffn2_w1", "ffn2_w3", "ffn2_w2", "ple_proj", "ple_gate",
           "s5_w_in", "s5_w_glu", "sb_w_qkv", "sb_w_o")
COL_SHARDED = ("ffn1_w1", "ffn1_w3", "ffn2_w1", "ffn2_w3", "ple_proj", "s5_w_glu", "sb_w_qkv")
FFN_COL = ("ffn1_w1", "ffn1_w3", "ffn2_w1", "ffn2_w3")
FFN_ROW = ("ffn1_w2", "ffn2_w2")
REPLICATED = ("ffn1_norm", "mix_norm", "ffn2_norm", "ple_norm", "s5_a_re", "s5_a_im", "s5_log_dt",
              "s5_b_re", "s5_b_im", "s5_c_re", "s5_c_im", "s5_d", "final_norm")
WEIGHTS = ("ffn1_norm", "ffn1_w1", "ffn1_w3", "ffn1_w2", "mix_norm", "ffn2_norm", "ffn2_w1", "ffn2_w3", "ffn2_w2",
           "ple_norm", "ple_proj", "ple_gate", "s5_w_in", "s5_a_re", "s5_a_im", "s5_log_dt", "s5_b_re", "s5_b_im",
           "s5_c_re", "s5_c_im", "s5_d", "s5_w_glu", "sb_w_qkv", "sb_w_o", "final_norm")


def _dot(a, b):
    return jnp.dot(a, b, preferred_element_type=F32)


def _dot_nt(a, b):
    return lax.dot_general(a, b, (((1,), (1,)), ((), ())), preferred_element_type=F32)


def _dot_tn(a, b):
    return lax.dot_general(a, b, (((0,), (0,)), ((), ())), preferred_element_type=F32)


def _dot2(x, m):
    hi = x.astype(BF16)
    lo = (x - hi.astype(F32)).astype(BF16)
    return _dot(hi, m) + _dot(lo, m)


def _rms(x):
    r = lax.rsqrt(jnp.mean(x * x, axis=-1, keepdims=True) + EPS)
    return x * r, r


def _rms_bwd(dn, xh, r, g):
    gy = dn * g
    return r * (gy - xh * jnp.mean(gy * xh, axis=-1, keepdims=True))


_GELU_C = math.sqrt(2.0 / math.pi)


def _gelu(x):
    return 0.5 * x * (1.0 + jnp.tanh(_GELU_C * (x + 0.044715 * x * x * x)))


def _gelu_grad(x):
    th = jnp.tanh(_GELU_C * (x + 0.044715 * x * x * x))
    return 0.5 * (1.0 + th) + 0.5 * x * (1.0 - th * th) * _GELU_C * (1.0 + 3.0 * 0.044715 * x * x)


def _neg_softplus(z):
    return -(jnp.maximum(z, 0.0) + jnp.log(1.0 + jnp.exp(-jnp.abs(z))))


def _tile(n, want, mult=SUBLANES):
    for t in range(min(want, n), 0, -1):
        if n % t == 0 and t % mult == 0:
            return t
    return n


def _rows(tm, c):
    return pl.BlockSpec((tm, c), lambda i: (i, 0))


def _whole(a):
    nd = a.ndim
    return pl.BlockSpec(a.shape, lambda i: (0,) * nd)


class _W(NamedTuple):
    arr: jax.Array
    layer: int = 0
    col: int = 0
    width: Optional[int] = None

    @property
    def shape(self):
        return self.arr.shape[1], self.width or self.arr.shape[2]


def _wspec(w):
    return pl.BlockSpec((None,) + w.shape, lambda *_: (w.layer, 0, w.col))


def _row_params(sem="parallel"):
    return pltpu.CompilerParams(dimension_semantics=(sem,), vmem_limit_bytes=V7X_VMEM_LIMIT)


def _ffn_fwd(h, g, w1, w3, w2):
    t, d = h.shape
    f = w1.shape[1]
    tm = _tile(t, ROW_TILE)

    def body(h_ref, g_ref, w1_ref, w3_ref, w2_ref, ho_ref, n_ref, a_ref, b_ref):
        x = h_ref[...]
        xh, _ = _rms(x)
        n = (xh * g_ref[...]).astype(BF16)
        a = _dot(n, w1_ref[...])
        b = _dot(n, w3_ref[...])
        s = (a * jax.nn.sigmoid(a) * b).astype(BF16)
        ho_ref[...] = x + 0.5 * _dot(s, w2_ref[...])
        n_ref[...] = n
        a_ref[...] = a.astype(BF16)
        b_ref[...] = b.astype(BF16)

    return pl.pallas_call(
        body, name="ffn_fwd", grid=(t // tm,),
        in_specs=[_rows(tm, d), _whole(g), _wspec(w1), _wspec(w3), _wspec(w2)],
        out_specs=[_rows(tm, d), _rows(tm, d), _rows(tm, f), _rows(tm, f)],
        out_shape=[jax.ShapeDtypeStruct((t, d), F32), jax.ShapeDtypeStruct((t, d), BF16),
                   jax.ShapeDtypeStruct((t, f), BF16), jax.ShapeDtypeStruct((t, f), BF16)],
        compiler_params=_row_params(),
    )(h, g, w1.arr, w3.arr, w2.arr)


def _ffn_bwd_down(dh, a, b, w2):
    t, d = dh.shape
    f = a.shape[1]
    tm = _tile(t, ROW_TILE)

    def body(dh_ref, a_ref, b_ref, w2_ref, da_ref, db_ref, s_ref):
        ds = _dot_nt((0.5 * dh_ref[...]).astype(BF16), w2_ref[...])
        a32 = a_ref[...].astype(F32)
        b32 = b_ref[...].astype(F32)
        sig = jax.nn.sigmoid(a32)
        sil = a32 * sig
        da_ref[...] = (ds * b32 * (sig * (1.0 + a32 * (1.0 - sig)))).astype(BF16)
        db_ref[...] = (ds * sil).astype(BF16)
        s_ref[...] = (sil * b32).astype(BF16)

    return pl.pallas_call(
        body, name="ffn_bwd_down", grid=(t // tm,),
        in_specs=[_rows(tm, d), _rows(tm, f), _rows(tm, f), _wspec(w2)],
        out_specs=[_rows(tm, f)] * 3,
        out_shape=[jax.ShapeDtypeStruct((t, f), BF16)] * 3,
        compiler_params=_row_params(),
    )(dh, a, b, w2.arr)


def _lin_bwd_norm(dys, ws, h, g, dh, name):
    t, d = h.shape
    tm = _tile(t, ROW_TILE)
    k = len(dys)

    def body(*refs):
        dy_refs, w_refs = refs[:k], refs[k:2 * k]
        h_ref, g_ref, dh_ref, o_ref, dg_ref = refs[2 * k:]
        dn = _dot_nt(dy_refs[0][...].astype(BF16), w_refs[0][...])
        for j in range(1, k):
            dn = dn + _dot_nt(dy_refs[j][...].astype(BF16), w_refs[j][...])
        xh, r = _rms(h_ref[...])
        o_ref[...] = dh_ref[...] + _rms_bwd(dn, xh, r, g_ref[...])

        @pl.when(pl.program_id(0) == 0)
        def _():
            dg_ref[...] = jnp.zeros_like(dg_ref)

        dg_ref[...] += jnp.sum(dn * xh, axis=0, keepdims=True)

    return pl.pallas_call(
        body, name=name, grid=(t // tm,),
        in_specs=[_rows(tm, dy.shape[1]) for dy in dys] + [_wspec(w) for w in ws] + [_rows(tm, d), _whole(g), _rows(tm, d)],
        out_specs=[_rows(tm, d), pl.BlockSpec((1, d), lambda i: (0, 0))],
        out_shape=[jax.ShapeDtypeStruct((t, d), F32), jax.ShapeDtypeStruct((1, d), F32)],
        compiler_params=_row_params("arbitrary"),
    )(*dys, *[w.arr for w in ws], h, g, dh)


def _wgrad(x, dy, name, like, into=None, layer=0, scale=1.0):
    t, kk = x.shape
    n = dy.shape[1]
    tm = _tile(t, WGRAD_ROW_TILE)
    steps = t // tm
    blocked = len(like) == 4
    if blocked:
        sc = like[3]
        nd = max(g for g in (1, 2, 4, 8) if g * sc * kk * 4 <= max(WGRAD_OUT_BYTES, sc * kk * 4))
        tn = nd * sc
        out_spec = pl.BlockSpec((nd, None, kk, sc), lambda j, i: (j, layer, 0, 0))
    else:
        tn = _tile(n, max(128, WGRAD_OUT_BYTES // (4 * kk)), 128)
        out_spec = pl.BlockSpec((None, kk, tn), lambda j, i: (layer, 0, j))

    def body(x_ref, dy_ref, *rest):
        o_ref = rest[-1]
        i = pl.program_id(1)

        @pl.when(i == 0)
        def _():
            o_ref[...] = jnp.zeros_like(o_ref)

        acc = _dot_tn(x_ref[...].astype(BF16), dy_ref[...].astype(BF16))
        if blocked:
            for dd in range(nd):
                o_ref[dd] += acc[:, dd * sc:(dd + 1) * sc]
        else:
            o_ref[...] += acc
        if scale != 1.0:
            @pl.when(i == steps - 1)
            def _():
                o_ref[...] = o_ref[...] * scale

    held = [] if into is None else [into]
    return pl.pallas_call(
        body, name=name, grid=(n // tn, steps),
        in_specs=[pl.BlockSpec((tm, kk), lambda j, i: (i, 0)), pl.BlockSpec((tm, tn), lambda j, i: (i, j))]
        + [pl.BlockSpec(memory_space=pl.ANY)] * len(held),
        out_specs=out_spec,
        out_shape=jax.ShapeDtypeStruct(like, F32),
        input_output_aliases={2: 0} if held else {},
        compiler_params=pltpu.CompilerParams(dimension_semantics=("parallel", "arbitrary"), vmem_limit_bytes=V7X_VMEM_LIMIT),
    )(x, dy, *held)


def _norm_lin(h, g, ws, out_dtypes, name):
    t, d = h.shape
    tm = _tile(t, ROW_TILE)
    k = len(ws)

    def body(*refs):
        h_ref, g_ref = refs[:2]
        w_refs = refs[2:2 + k]
        n_ref = refs[2 + k]
        o_refs = refs[3 + k:]
        xh, _ = _rms(h_ref[...])
        n = (xh * g_ref[...]).astype(BF16)
        n_ref[...] = n
        for w_ref, o_ref in zip(w_refs, o_refs):
            o_ref[...] = _dot(n, w_ref[...]).astype(o_ref.dtype)

    return pl.pallas_call(
        body, name=name, grid=(t // tm,),
        in_specs=[_rows(tm, d), _whole(g)] + [_wspec(w) for w in ws],
        out_specs=[_rows(tm, d)] + [_rows(tm, w.shape[1]) for w in ws],
        out_shape=[jax.ShapeDtypeStruct((t, d), BF16)] + [jax.ShapeDtypeStruct((t, w.shape[1]), dt) for w, dt in zip(ws, out_dtypes)],
        compiler_params=_row_params(),
    )(h, g, *[w.arr for w in ws])


def _lin_res(h, x, w, name):
    t, d = h.shape
    tm = _tile(t, ROW_TILE)

    def body(h_ref, x_ref, w_ref, o_ref):
        o_ref[...] = h_ref[...] + _dot(x_ref[...], w_ref[...])

    return pl.pallas_call(
        body, name=name, grid=(t // tm,),
        in_specs=[_rows(tm, d), _rows(tm, x.shape[1]), _wspec(w)],
        out_specs=_rows(tm, d), out_shape=jax.ShapeDtypeStruct((t, d), F32),
        compiler_params=_row_params(),
    )(h, x, w.arr)


def _lin_nt(dy, w, name):
    t = dy.shape[0]
    kk = w.shape[0]
    tm = _tile(t, ROW_TILE)

    def body(dy_ref, w_ref, o_ref):
        o_ref[...] = _dot_nt(dy_ref[...].astype(BF16), w_ref[...]).astype(BF16)

    return pl.pallas_call(
        body, name=name, grid=(t // tm,),
        in_specs=[_rows(tm, dy.shape[1]), _wspec(w)],
        out_specs=_rows(tm, kk), out_shape=jax.ShapeDtypeStruct((t, kk), BF16),
        compiler_params=_row_params(),
    )(dy, w.arr)


def _ple_fwd(h, g, wg, p, wp):
    t, d = h.shape
    tm = _tile(t, ROW_TILE)

    def body(h_ref, g_ref, wg_ref, p_ref, wp_ref, o_ref, n_ref):
        x = h_ref[...]
        xh, _ = _rms(x)
        n = (xh * g_ref[...]).astype(BF16)
        n_ref[...] = n
        gate = jax.nn.sigmoid(_dot(n, wg_ref[...]))
        o_ref[...] = x + _dot(p_ref[...].astype(BF16), wp_ref[...]) * gate

    return pl.pallas_call(
        body, name="ple_fwd", grid=(t // tm,),
        in_specs=[_rows(tm, d), _whole(g), _wspec(wg), _rows(tm, p.shape[1]), _wspec(wp)],
        out_specs=[_rows(tm, d), _rows(tm, d)],
        out_shape=[jax.ShapeDtypeStruct((t, d), F32), jax.ShapeDtypeStruct((t, d), BF16)],
        compiler_params=_row_params(),
    )(h, g, wg.arr, p, wp.arr)


def _ple_bwd(dh, h, g, n, p, wg, wp):
    t, d = h.shape
    tm = _tile(t, ROW_TILE)

    def body(dh_ref, h_ref, g_ref, n_ref, p_ref, wg_ref, wp_ref, o_ref, dgl_ref, dpp_ref, dg_ref):
        dh_v = dh_ref[...]
        gate = jax.nn.sigmoid(_dot(n_ref[...], wg_ref[...]))
        pp = _dot(p_ref[...].astype(BF16), wp_ref[...])
        dgl = (dh_v * pp * gate * (1.0 - gate)).astype(BF16)
        dgl_ref[...] = dgl
        dpp_ref[...] = (dh_v * gate).astype(BF16)
        dn = _dot_nt(dgl, wg_ref[...])
        xh, r = _rms(h_ref[...])
        o_ref[...] = dh_v + _rms_bwd(dn, xh, r, g_ref[...])

        @pl.when(pl.program_id(0) == 0)
        def _():
            dg_ref[...] = jnp.zeros_like(dg_ref)

        dg_ref[...] += jnp.sum(dn * xh, axis=0, keepdims=True)

    return pl.pallas_call(
        body, name="ple_bwd", grid=(t // tm,),
        in_specs=[_rows(tm, d), _rows(tm, d), _whole(g), _rows(tm, d), _rows(tm, p.shape[1]), _wspec(wg), _wspec(wp)],
        out_specs=[_rows(tm, d), _rows(tm, d), _rows(tm, d), pl.BlockSpec((1, d), lambda i: (0, 0))],
        out_shape=[jax.ShapeDtypeStruct((t, d), F32), jax.ShapeDtypeStruct((t, d), BF16),
                   jax.ShapeDtypeStruct((t, d), BF16), jax.ShapeDtypeStruct((1, d), F32)],
        compiler_params=_row_params("arbitrary"),
    )(dh, h, g, n, p, wg.arr, wp.arr)


def _loss_head(h, g, tgt):
    t, d = h.shape
    tm = _tile(t, ROW_TILE)

    def body(h_ref, g_ref, t_ref, l_ref, dh_ref, dg_ref):
        xh, r = _rms(h_ref[...])
        gg = g_ref[...]
        e = xh * gg - t_ref[...]
        dy = e * (1.0 / d)

        @pl.when(pl.program_id(0) == 0)
        def _():
            l_ref[...] = jnp.zeros_like(l_ref)
            dg_ref[...] = jnp.zeros_like(dg_ref)

        l_ref[...] += 0.5 * jnp.sum(jnp.mean(e * e, axis=-1, keepdims=True), axis=0, keepdims=True)
        dg_ref[...] += jnp.sum(dy * xh, axis=0, keepdims=True)
        dh_ref[...] = _rms_bwd(dy, xh, r, gg)

    return pl.pallas_call(
        body, name="loss_head", grid=(t // tm,),
        in_specs=[_rows(tm, d), _whole(g), _rows(tm, d)],
        out_specs=[pl.BlockSpec((1, 128), lambda i: (0, 0)), _rows(tm, d), pl.BlockSpec((1, d), lambda i: (0, 0))],
        out_shape=[jax.ShapeDtypeStruct((1, 128), F32), jax.ShapeDtypeStruct((t, d), F32), jax.ShapeDtypeStruct((1, d), F32)],
        compiler_params=_row_params("arbitrary"),
    )(h, g, tgt)


def _s5_disc_math(a_re, a_im, log_dt):
    lam_re = jnp.minimum(a_re, -1e-4)
    lam_im = a_im
    dt = jnp.exp(log_dt)
    mag = jnp.exp(lam_re * dt)
    abar_re = mag * jnp.cos(lam_im * dt)
    abar_im = mag * jnp.sin(lam_im * dt)
    den = lam_re * lam_re + lam_im * lam_im
    nr = abar_re - 1.0
    ni = abar_im
    return abar_re, abar_im, (nr * lam_re + ni * lam_im) / den, (ni * lam_re - nr * lam_im) / den


def _s5_disc(a_re, a_im, log_dt):
    gp = jax.ShapeDtypeStruct(a_re.shape, F32)

    def body(ar_ref, ai_ref, ld_ref, o0, o1, o2, o3):
        outs = _s5_disc_math(ar_ref[...], ai_ref[...], ld_ref[...])
        for o_ref, val in zip((o0, o1, o2, o3), outs):
            o_ref[...] = val

    return pl.pallas_call(body, name="s5_disc", out_shape=[gp] * 4)(a_re, a_im, log_dt)


def _s5_disc_bwd(a_re, a_im, log_dt, cts):
    def body(ar_ref, ai_ref, ld_ref, c0, c1, c2, c3, dar_ref, dai_ref, dld_ref):
        _, vjp = jax.vjp(_s5_disc_math, ar_ref[...], ai_ref[...], ld_ref[...])
        dar, dai, dld = vjp((c0[...], c1[...], c2[...], c3[...]))
        dar_ref[...] = dar
        dai_ref[...] = dai
        dld_ref[...] = dld

    return pl.pallas_call(
        body, name="s5_disc_bwd",
        out_shape=[jax.ShapeDtypeStruct(a_re.shape, F32), jax.ShapeDtypeStruct(a_im.shape, F32),
                   jax.ShapeDtypeStruct(log_dt.shape, F32)],
    )(a_re, a_im, log_dt, *cts)


def _blockdiag(w, gl):
    g, r, c = w.shape
    w = w.reshape(g // gl, gl, r, c)
    eye = jnp.eye(gl, dtype=w.dtype)
    return (w[:, :, :, None, :] * eye[None, :, None, :, None]).reshape(g // gl, gl * r, gl * c)


def _blockdiag_take(m, gl, r, c):
    nb = m.shape[0]
    own = jnp.eye(gl, dtype=bool)[None, :, None, :, None]
    return jnp.where(own, m.reshape(nb, gl, r, gl, c), 0.0).sum(axis=3).reshape(nb * gl, r, c)


def _cmul(ar, ai, br, bi):
    return ar * br - ai * bi, ar * bi + ai * br


def _powers(ar, ai, n):
    out = [(ar, ai)]
    for _ in range(n - 1):
        out.append(_cmul(ar, ai, *out[-1]))
    return out


def _interleave(a, lc):
    t, d = a.shape
    return a.reshape(t // lc, SUBLANES, lc // SUBLANES, d).transpose(0, 2, 1, 3).reshape(t, d)


def _deinterleave(a, lc):
    t, d = a.shape
    return a.reshape(t // lc, lc // SUBLANES, SUBLANES, d).transpose(0, 2, 1, 3).reshape(t, d)


def _seg_rows(k):
    return slice(k * SUBLANES, (k + 1) * SUBLANES)


def _seg_scan(re_ref, im_ref, ar, ai, seg, reverse=False):
    cr = ci = jnp.zeros((SUBLANES, re_ref.shape[1]), F32)
    for k in (reversed(range(seg)) if reverse else range(seg)):
        rows = _seg_rows(k)
        cr, ci = ar * cr - ai * ci + re_ref[rows, :], ar * ci + ai * cr + im_ref[rows, :]
        re_ref[rows, :] = cr
        im_ref[rows, :] = ci
    return cr, ci


def _seg_entering(er, ei, pr, pi, c0r, c0i, reverse=False):
    rows = lax.broadcasted_iota(jnp.int32, er.shape, 0)
    vr = vi = jnp.zeros_like(er)
    cr, ci = c0r, c0i
    for j in (reversed(range(SUBLANES)) if reverse else range(SUBLANES)):
        vr = jnp.where(rows == j, cr, vr)
        vi = jnp.where(rows == j, ci, vi)
        cr, ci = er[j:j + 1, :] + pr * cr - pi * ci, ei[j:j + 1, :] + pr * ci + pi * cr
    return vr, vi, cr, ci


def _s5_scan_fwd(u, bre, bim, cre, cim, par, dskip, bl, seq, lc):
    t, d = u.shape
    nb, gw, ns = bre.shape
    nc = seq // lc
    seg = lc // SUBLANES

    def body(u_ref, bre_ref, bim_ref, cre_ref, cim_ref, par_ref, d_ref, y_ref, z_ref, st_ref, carry, sre, sim):
        @pl.when(pl.program_id(2) == 0)
        def _():
            carry[...] = jnp.zeros_like(carry)

        st_ref[...] = carry[...]
        uu = u_ref[...]
        ug = uu.astype(BF16)
        wre = _dot(ug, bre_ref[...])
        wim = _dot(ug, bim_ref[...])
        ar, ai = par_ref[0:1, :], par_ref[1:2, :]
        fr, fi = par_ref[2:3, :], par_ref[3:4, :]
        sre[...] = fr * wre - fi * wim
        sim[...] = fr * wim + fi * wre
        pows = _powers(ar, ai, seg)
        er, ei = _seg_scan(sre, sim, ar, ai, seg)
        vr, vi, cr, ci = _seg_entering(er, ei, *pows[-1], carry[0:1, :], carry[1:2, :])
        carry[0:1, :] = cr
        carry[1:2, :] = ci
        for k in range(seg):
            rows = _seg_rows(k)
            fr_k, fi_k = _cmul(*pows[k], vr, vi)
            sre[rows, :] += fr_k
            sim[rows, :] += fi_k
        y = _dot(sre[...].astype(BF16), cre_ref[...]) - _dot(sim[...].astype(BF16), cim_ref[...]) + d_ref[...] * uu
        y_ref[...] = y
        z_ref[...] = _gelu(y).astype(BF16)

    tok = pl.BlockSpec((lc, gw), lambda g, b, c: (b * nc + c, g))
    mat_b = pl.BlockSpec((None, gw, ns), lambda g, b, c: (g, 0, 0))
    mat_c = pl.BlockSpec((None, ns, gw), lambda g, b, c: (g, 0, 0))
    return pl.pallas_call(
        body, name="s5_scan_fwd", grid=(nb, bl, nc),
        in_specs=[tok, mat_b, mat_b, mat_c, mat_c, pl.BlockSpec((None, 8, ns), lambda g, b, c: (g, 0, 0)),
                  pl.BlockSpec((1, gw), lambda g, b, c: (0, g))],
        out_specs=[tok, tok, pl.BlockSpec((None, None, 2, ns), lambda g, b, c: (g, b * nc + c, 0, 0))],
        out_shape=[jax.ShapeDtypeStruct((t, d), F32), jax.ShapeDtypeStruct((t, d), BF16),
                   jax.ShapeDtypeStruct((nb, bl * nc, 2, ns), F32)],
        scratch_shapes=[pltpu.VMEM((2, ns), F32), pltpu.VMEM((lc, ns), F32), pltpu.VMEM((lc, ns), F32)],
        compiler_params=pltpu.CompilerParams(dimension_semantics=("parallel", "arbitrary", "arbitrary"),
                                             vmem_limit_bytes=V7X_VMEM_LIMIT),
    )(u, bre, bim, cre, cim, par, dskip)


def _s5_scan_bwd(u, dy, st, bre, bim, cre, cim, par, dskip, bl, seq, lc):
    t, d = u.shape
    nb, gw, ns = bre.shape
    nc = seq // lc
    seg = lc // SUBLANES

    def body(u_ref, dy_ref, st_ref, bre_ref, bim_ref, cre_ref, cim_ref, par_ref, d_ref,
             du_ref, dbre_ref, dbim_ref, dcre_ref, dcim_ref, dpar_ref, dd_ref,
             lcarry, sre, sim, wre_s, wim_s, lre, lim):
        b, c = pl.program_id(1), pl.program_id(2)

        @pl.when((b == 0) & (c == 0))
        def _():
            for ref in (dbre_ref, dbim_ref, dcre_ref, dcim_ref, dpar_ref, dd_ref):
                ref[...] = jnp.zeros_like(ref)

        @pl.when(c == 0)
        def _():
            lcarry[...] = jnp.zeros_like(lcarry)

        uu = u_ref[...]
        ug = uu.astype(BF16)
        dyv = dy_ref[...]
        dyb = dyv.astype(BF16)
        ar, ai = par_ref[0:1, :], par_ref[1:2, :]
        fr, fi = par_ref[2:3, :], par_ref[3:4, :]
        wre = _dot(ug, bre_ref[...])
        wim = _dot(ug, bim_ref[...])
        wre_s[...] = wre
        wim_s[...] = wim
        sre[...] = fr * wre - fi * wim
        sim[...] = fr * wim + fi * wre

        pows = _powers(ar, ai, seg)
        er, ei = _seg_scan(sre, sim, ar, ai, seg)
        svr, svi, _, _ = _seg_entering(er, ei, *pows[-1], st_ref[0:1, :], st_ref[1:2, :])
        for k in range(seg):
            rows = _seg_rows(k)
            fr_k, fi_k = _cmul(*pows[k], svr, svi)
            sre[rows, :] += fr_k
            sim[rows, :] += fi_k

        lre[...] = _dot_nt(dyb, cre_ref[...])
        lim[...] = -_dot_nt(dyb, cim_ref[...])
        er, ei = _seg_scan(lre, lim, ar, -ai, seg, reverse=True)
        lvr, lvi, cr, ci = _seg_entering(er, ei, pows[-1][0], -pows[-1][1], lcarry[0:1, :], lcarry[1:2, :], reverse=True)
        lcarry[0:1, :] = cr
        lcarry[1:2, :] = ci
        dar = dai = jnp.zeros_like(er)
        for k in range(seg):
            rows = _seg_rows(k)
            pr, pi = pows[seg - 1 - k]
            fr_k, fi_k = _cmul(pr, -pi, lvr, lvi)
            lr = lre[rows, :] + fr_k
            li = lim[rows, :] + fi_k
            lre[rows, :] = lr
            lim[rows, :] = li
            spr, spi = (svr, svi) if k == 0 else (sre[_seg_rows(k - 1), :], sim[_seg_rows(k - 1), :])
            dar = dar + lr * spr + li * spi
            dai = dai + li * spr - lr * spi

        lr, li = lre[...], lim[...]
        wr, wi = wre_s[...], wim_s[...]
        dpar_ref[0:1, :] += jnp.sum(dar, axis=0, keepdims=True)
        dpar_ref[1:2, :] += jnp.sum(dai, axis=0, keepdims=True)
        dpar_ref[2:3, :] += jnp.sum(lr * wr + li * wi, axis=0, keepdims=True)
        dpar_ref[3:4, :] += jnp.sum(li * wr - lr * wi, axis=0, keepdims=True)
        dwr = (fr * lr + fi * li).astype(BF16)
        dwi = (fr * li - fi * lr).astype(BF16)
        dsk = d_ref[...]
        du_ref[...] = _dot_nt(dwr, bre_ref[...]) + _dot_nt(dwi, bim_ref[...]) + dsk * dyv
        dd_ref[...] += jnp.sum(dyv * uu, axis=0, keepdims=True)
        dbre_ref[...] += _dot_tn(ug, dwr)
        dbim_ref[...] += _dot_tn(ug, dwi)
        dcre_ref[...] += _dot_tn(sre[...].astype(BF16), dyb)
        dcim_ref[...] -= _dot_tn(sim[...].astype(BF16), dyb)

    tok = pl.BlockSpec((lc, gw), lambda g, b, c: (b * nc + nc - 1 - c, g))
    mat_b = pl.BlockSpec((None, gw, ns), lambda g, b, c: (g, 0, 0))
    mat_c = pl.BlockSpec((None, ns, gw), lambda g, b, c: (g, 0, 0))
    rows8 = pl.BlockSpec((None, 8, ns), lambda g, b, c: (g, 0, 0))
    dvec = pl.BlockSpec((1, gw), lambda g, b, c: (0, g))
    tile = pltpu.VMEM((lc, ns), F32)
    return pl.pallas_call(
        body, name="s5_scan_bwd", grid=(nb, bl, nc),
        in_specs=[tok, tok, pl.BlockSpec((None, None, 2, ns), lambda g, b, c: (g, b * nc + nc - 1 - c, 0, 0)),
                  mat_b, mat_b, mat_c, mat_c, rows8, dvec],
        out_specs=[tok, mat_b, mat_b, mat_c, mat_c, rows8, dvec],
        out_shape=[jax.ShapeDtypeStruct((t, d), F32),
                   jax.ShapeDtypeStruct((nb, gw, ns), F32), jax.ShapeDtypeStruct((nb, gw, ns), F32),
                   jax.ShapeDtypeStruct((nb, ns, gw), F32), jax.ShapeDtypeStruct((nb, ns, gw), F32),
                   jax.ShapeDtypeStruct((nb, 8, ns), F32), jax.ShapeDtypeStruct((1, d), F32)],
        scratch_shapes=[pltpu.VMEM((2, ns), F32)] + [tile] * 6,
        compiler_params=pltpu.CompilerParams(dimension_semantics=("arbitrary", "arbitrary", "arbitrary"),
                                             vmem_limit_bytes=V7X_VMEM_LIMIT),
    )(u, dy, st, bre, bim, cre, cim, par, dskip)


def _s5_out(h, z, wglu):
    t, d = h.shape
    tm = _tile(t, ROW_TILE)

    def body(h_ref, z_ref, w_ref, o_ref):
        zz = _dot(z_ref[...], w_ref[...])
        o_ref[...] = h_ref[...] + zz[:, :d] * jax.nn.sigmoid(zz[:, d:])

    return pl.pallas_call(
        body, name="s5_out", grid=(t // tm,),
        in_specs=[_rows(tm, d), _rows(tm, d), _wspec(wglu)],
        out_specs=_rows(tm, d), out_shape=jax.ShapeDtypeStruct((t, d), F32),
        compiler_params=_row_params(),
    )(h, z, wglu.arr)


def _s5_out_bwd(dh, z, y, wglu):
    t, d = dh.shape
    tm = _tile(t, ROW_TILE)

    def body(dh_ref, z_ref, y_ref, w_ref, dy_ref, dzz_ref):
        zz = _dot(z_ref[...], w_ref[...])
        out, sg = zz[:, :d], jax.nn.sigmoid(zz[:, d:])
        dh_v = dh_ref[...]
        dzz = jnp.concatenate([dh_v * sg, dh_v * out * sg * (1.0 - sg)], axis=1).astype(BF16)
        dzz_ref[...] = dzz
        dy_ref[...] = _dot_nt(dzz, w_ref[...]) * _gelu_grad(y_ref[...])

    return pl.pallas_call(
        body, name="s5_out_bwd", grid=(t // tm,),
        in_specs=[_rows(tm, d), _rows(tm, d), _rows(tm, d), _wspec(wglu)],
        out_specs=[_rows(tm, d), _rows(tm, 2 * d)],
        out_shape=[jax.ShapeDtypeStruct((t, d), F32), jax.ShapeDtypeStruct((t, 2 * d), BF16)],
        compiler_params=_row_params(),
    )(dh, z, y, wglu.arr)


def _head_spec(seq, w):
    return pl.BlockSpec((None, SB_HEADS_PER_STEP, seq, w), lambda b, h: (b, h, 0, 0))


def _tri_and_ones(kind):
    row = lax.broadcasted_iota(jnp.int32, (Q_BLOCK, Q_BLOCK), 0)
    col = lax.broadcasted_iota(jnp.int32, (Q_BLOCK, Q_BLOCK), 1)
    tri = {"after": row > col, "upto": row <= col, "before": row < col}[kind]
    return jnp.concatenate([tri.astype(BF16), jnp.ones((Q_BLOCK, Q_BLOCK), BF16)], axis=1)


def _sb_fwd(q, k, v):
    bsz, nh, seq, dh = q.shape
    tq = min(SB_Q_TILE, seq // 2)
    nq, nsub = seq // tq, tq // Q_BLOCK
    scale = dh ** -0.5
    hp = SB_HEADS_PER_STEP
    qb = Q_BLOCK

    def body(q_ref, k_ref, v_ref, o_ref, tot_ref):
        strict = lax.broadcasted_iota(jnp.int32, (qb, qb), 1) < lax.broadcasted_iota(jnp.int32, (qb, qb), 0)
        sums = _tri_and_ones("after")

        def sweep(qs, c0, units, carry):
            heads = sorted({u[0] for u in units})
            kbs = {hh: k_ref[hh, pl.ds(c0, qb), :] for hh in heads}
            vbs = {hh: v_ref[hh, pl.ds(c0, qb), :] for hh in heads}
            zs = [_dot_nt(qs[hh][s], kbs[hh]) * scale for hh, s, _ in units]
            lkrs = [_neg_softplus(z) for z in zs]
            lks = [jnp.where(strict, lkr, 0.0) if dg else lkr for lkr, (_, _, dg) in zip(lkrs, units)]
            css = [_dot2(lk, sums) for lk in lks]
            carry = dict(carry)
            for (hh, s, dg), z, lkr, cs in zip(units, zs, lkrs, css):
                acc, run = carry[hh, s]
                att = jnp.exp(z + lkr + cs[:, :qb] + run)
                if dg:
                    att = jnp.where(strict, att, 0.0)
                carry[hh, s] = (acc + _dot(att.astype(BF16), vbs[hh]), run + cs[:, qb:])
            return carry

        def q_loop(qi, _):
            r0 = qi * tq
            qs = {hh: [q_ref[hh, pl.ds(pl.multiple_of(r0 + s * qb, qb), qb), :] for s in range(nsub)] for hh in range(hp)}
            carry = {(hh, s): (jnp.zeros((qb, dh), F32), jnp.zeros((qb, qb), F32)) for hh in range(hp) for s in range(nsub)}
            for jj in reversed(range(nsub)):
                units = [(hh, s, s == jj) for hh in range(hp) for s in range(jj, nsub)]
                carry = sweep(qs, pl.multiple_of(r0 + jj * qb, qb), units, carry)
            units = [(hh, s, False) for hh in range(hp) for s in range(nsub)]
            carry = lax.fori_loop(
                0, nsub * qi, lambda t, c: sweep(qs, pl.multiple_of((nsub * qi - 1 - t) * qb, qb), units, c), carry)
            for (hh, s), (acc, run) in carry.items():
                rows = pl.ds(pl.multiple_of(r0 + s * qb, qb), qb)
                o_ref[hh, rows, :] = acc.astype(BF16)
                tot_ref[hh, rows, :] = run[:, 0:1]
            return 0

        lax.fori_loop(0, nq, q_loop, 0)

    return pl.pallas_call(
        body, name="sb_fwd", grid=(bsz, nh // hp),
        in_specs=[_head_spec(seq, dh)] * 3,
        out_specs=[_head_spec(seq, dh), _head_spec(seq, 1)],
        out_shape=[jax.ShapeDtypeStruct((bsz, nh, seq, dh), BF16), jax.ShapeDtypeStruct((bsz, nh, seq, 1), F32)],
        compiler_params=pltpu.CompilerParams(dimension_semantics=("parallel", "parallel"), vmem_limit_bytes=V7X_VMEM_LIMIT),
    )(q, k, v)


def _sb_bwd(q, k, v, do, tot):
    bsz, nh, seq, dh = q.shape
    tq = min(SB_Q_TILE, seq // 2)
    nq, nsub = seq // tq, tq // Q_BLOCK
    scale = dh ** -0.5
    hp = SB_HEADS_PER_STEP
    qb = Q_BLOCK

    def body(q_ref, k_ref, v_ref, do_ref, tot_ref, dq_ref, dk_ref, dv_ref, dka, dva):
        dka[...] = jnp.zeros_like(dka)
        dva[...] = jnp.zeros_like(dva)
        strict = lax.broadcasted_iota(jnp.int32, (qb, qb), 1) < lax.broadcasted_iota(jnp.int32, (qb, qb), 0)
        upto = _tri_and_ones("upto")
        before = _tri_and_ones("before")

        def sweep(qf, dof, tots, c0, first, units, carry):
            heads = sorted({u[0] for u in units})
            kbs = {hh: k_ref[hh, pl.ds(c0, qb), :] for hh in heads}
            vbs = {hh: v_ref[hh, pl.ds(c0, qb), :] for hh in heads}
            sub = lambda a, s: a[s * qb:(s + 1) * qb, :]
            zs = [_dot_nt(sub(qf[hh], s), kbs[hh]) * scale for hh, s, _ in units]
            das = [_dot_nt(sub(dof[hh], s), vbs[hh]) for hh, s, _ in units]
            lkrs = [_neg_softplus(z) for z in zs]
            lks = [jnp.where(strict, lkr, 0.0) if dg else lkr for lkr, (_, _, dg) in zip(lkrs, units)]
            css = [_dot2(lk, upto) for lk in lks]
            lsigs, atts, gls = [], [], []
            for (hh, s, dg), z, lkr, cs, da in zip(units, zs, lkrs, css, das):
                lsig = z + lkr
                att = jnp.exp(lsig + (tots[hh, s] - (cs[:, :qb] + carry[hh, s][1])))
                if dg:
                    att = jnp.where(strict, att, 0.0)
                lsigs.append(lsig)
                atts.append(att)
                gls.append(da * att)
            gss = [_dot2(gl, before) for gl in gls]
            carry = dict(carry)
            dzs = {}
            for (hh, s, dg), lsig, gl, cs, gs in zip(units, lsigs, gls, css, gss):
                dqa, pre, gpre = carry[hh, s]
                sig = jnp.exp(lsig)
                dz = gl * (1.0 - sig) - (gs[:, :qb] + gpre) * sig
                if dg:
                    dz = jnp.where(strict, dz, 0.0)
                dz = (dz * scale).astype(BF16)
                dzs[hh, s] = dz
                carry[hh, s] = (dqa + _dot(dz, kbs[hh]), pre + cs[:, qb:], gpre + gs[:, qb:])
            att_of = {(hh, s): a for (hh, s, _), a in zip(units, atts)}
            for hh in heads:
                dzc = jnp.concatenate([dzs[hh, s] for s in range(first, nsub)], axis=0)
                attc = jnp.concatenate([att_of[hh, s].astype(BF16) for s in range(first, nsub)], axis=0)
                dka[hh, pl.ds(c0, qb), :] += _dot_tn(dzc, qf[hh][first * qb:, :])
                dva[hh, pl.ds(c0, qb), :] += _dot_tn(attc, dof[hh][first * qb:, :])
            return carry

        def q_loop(qi, _):
            r0 = pl.multiple_of(qi * tq, tq)
            qf = [q_ref[hh, pl.ds(r0, tq), :] for hh in range(hp)]
            dof = [do_ref[hh, pl.ds(r0, tq), :] for hh in range(hp)]
            tots = {(hh, s): jnp.broadcast_to(tot_ref[hh, pl.ds(pl.multiple_of(r0 + s * qb, qb), qb), :], (qb, qb))
                    for hh in range(hp) for s in range(nsub)}
            carry = {(hh, s): (jnp.zeros((qb, dh), F32), jnp.zeros((qb, qb), F32), jnp.zeros((qb, qb), F32))
                     for hh in range(hp) for s in range(nsub)}
            units = [(hh, s, False) for hh in range(hp) for s in range(nsub)]
            carry = lax.fori_loop(
                0, nsub * qi, lambda kj, c: sweep(qf, dof, tots, pl.multiple_of(kj * qb, qb), 0, units, c), carry)
            for jj in range(nsub):
                units = [(hh, s, s == jj) for hh in range(hp) for s in range(jj, nsub)]
                carry = sweep(qf, dof, tots, pl.multiple_of(r0 + jj * qb, qb), jj, units, carry)
            for (hh, s), (dqa, _, _) in carry.items():
                dq_ref[hh, pl.ds(pl.multiple_of(r0 + s * qb, qb), qb), :] = dqa.astype(BF16)
            return 0

        lax.fori_loop(0, nq, q_loop, 0)
        dk_ref[...] = dka[...].astype(BF16)
        dv_ref[...] = dva[...].astype(BF16)

    hs = _head_spec(seq, dh)
    return pl.pallas_call(
        body, name="sb_bwd", grid=(bsz, nh // hp),
        in_specs=[hs, hs, hs, hs, _head_spec(seq, 1)],
        out_specs=[hs, hs, hs],
        out_shape=[jax.ShapeDtypeStruct((bsz, nh, seq, dh), BF16)] * 3,
        scratch_shapes=[pltpu.VMEM((hp, seq, dh), F32), pltpu.VMEM((hp, seq, dh), F32)],
        compiler_params=pltpu.CompilerParams(dimension_semantics=("parallel", "parallel"), vmem_limit_bytes=V7X_VMEM_LIMIT),
    )(q, k, v, do, tot)


def _to_heads(a, bsz, seq):
    return a.reshape(bsz, seq, -1, HEAD_DIM).transpose(0, 2, 1, 3)


def _from_heads(a):
    bsz, nh, seq, dh = a.shape
    return a.transpose(0, 2, 1, 3).reshape(bsz * seq, nh * dh)


def _coords():
    return lax.axis_index("x"), lax.axis_index("y"), lax.axis_index("c")


def _all_gather(x, name):
    r, c = x.shape

    def body(x_ref, out_ref, send_sems, recv_sems, local_sem):
        mx, my, mc = _coords()
        me, sibling = (mx, my, mc), (mx, my, 1 - mc)
        chips = [(1 - mx, my), (mx, 1 - my), (1 - mx, 1 - my)]

        def blk(px, py, pc):
            return out_ref.at[4 * px + 2 * py + pc]

        def copy(k, block, to, src=None):
            return pltpu.make_async_remote_copy(
                src_ref=blk(*block) if src is None else src, dst_ref=blk(*block),
                send_sem=send_sems.at[k], recv_sem=recv_sems.at[k], device_id=to, device_id_type=MESH)

        mine = pltpu.make_async_copy(x_ref, blk(*me), local_sem)
        mine.start()
        first = [copy(0, me, sibling, src=x_ref)]
        first += [copy(1 + j, me, (*chip, mc), src=x_ref) for j, chip in enumerate(chips)]
        for cp in first:
            cp.start()
        passed = [copy(4 + j, (*chip, mc), sibling) for j, chip in enumerate(chips)]
        for j, chip in enumerate(chips):
            copy(1 + j, (*chip, mc), me).wait_recv()
            passed[j].start()
        copy(0, sibling, me).wait_recv()
        for j, chip in enumerate(chips):
            copy(4 + j, (*chip, 1 - mc), me).wait_recv()
        for cp in first + passed:
            cp.wait_send()
        mine.wait()

    return pl.pallas_call(
        body, name=name,
        out_shape=jax.ShapeDtypeStruct((N_DEV, r, c), x.dtype),
        in_specs=[pl.BlockSpec(memory_space=pl.ANY)],
        out_specs=pl.BlockSpec(memory_space=pl.ANY),
        scratch_shapes=[pltpu.SemaphoreType.DMA((7,)), pltpu.SemaphoreType.DMA((7,)), pltpu.SemaphoreType.DMA],
    )(x)


_ANY = pl.BlockSpec(memory_space=pl.ANY)


def _window(ref, shard_shape, by_cols, dev):
    if by_cols:
        n = shard_shape[2]
        return ref.at[:, :, pl.ds(pl.multiple_of(dev * n, n), n)]
    k = shard_shape[1]
    return ref.at[:, pl.ds(pl.multiple_of(dev * k, k), k), :]


def _gather_weights(shards, by_cols):
    na = len(shards)
    fulls = [jax.ShapeDtypeStruct(
        (s.shape[0], s.shape[1], s.shape[2] * N_DEV) if c else (s.shape[0], s.shape[1] * N_DEV, s.shape[2]), s.dtype)
        for s, c in zip(shards, by_cols)]

    def body(*refs):
        x_refs, out_refs = refs[:na], refs[na:2 * na]
        send_sems, recv_sems, local_sems = refs[2 * na:]
        mx, my, mc = _coords()
        me, sibling = (mx, my, mc), (mx, my, 1 - mc)
        chips = [(1 - mx, my), (mx, 1 - my), (1 - mx, 1 - my)]

        def blk(a, dev):
            px, py, pc = dev
            return _window(out_refs[a], x_refs[a].shape, by_cols[a], 4 * px + 2 * py + pc)

        def copy(a, k, block, to, src=None):
            return pltpu.make_async_remote_copy(
                src_ref=blk(a, block) if src is None else src, dst_ref=blk(a, block),
                send_sem=send_sems.at[a, k], recv_sem=recv_sems.at[a, k], device_id=to, device_id_type=MESH)

        mines = [pltpu.make_async_copy(x_refs[a], blk(a, me), local_sems.at[a]) for a in range(na)]
        first = [[copy(a, 0, me, sibling, src=x_refs[a])]
                 + [copy(a, 1 + j, me, (*chip, mc), src=x_refs[a]) for j, chip in enumerate(chips)] for a in range(na)]
        passed = [[copy(a, 4 + j, (*chip, mc), sibling) for j, chip in enumerate(chips)] for a in range(na)]
        for a in range(na):
            mines[a].start()
            for cp in first[a]:
                cp.start()
        for a in range(na):
            for j, chip in enumerate(chips):
                copy(a, 1 + j, (*chip, mc), me).wait_recv()
                passed[a][j].start()
        for a in range(na):
            copy(a, 0, sibling, me).wait_recv()
            for j, chip in enumerate(chips):
                copy(a, 4 + j, (*chip, 1 - mc), me).wait_recv()
        for a in range(na):
            for cp in first[a] + passed[a]:
                cp.wait_send()
            mines[a].wait()

    return pl.pallas_call(
        body, name="gather_weights", out_shape=fulls, in_specs=[_ANY] * na, out_specs=[_ANY] * na,
        scratch_shapes=[pltpu.SemaphoreType.DMA((na, 7)), pltpu.SemaphoreType.DMA((na, 7)), pltpu.SemaphoreType.DMA((na,))],
    )(*shards)


def _rs_sibling(grads, shard_shapes, by_cols):
    na = len(grads)
    slabs = [jax.ShapeDtypeStruct((N_CHIP,) + tuple(s), F32) for s in shard_shapes]

    def body(*refs):
        g_refs, got_refs = refs[:na], refs[na:2 * na]
        send_sems, recv_sems = refs[2 * na:]
        mx, my, mc = _coords()
        copies = []
        for a in range(na):
            for q in range(N_CHIP):
                dev = 2 * q + 1 - mc
                theirs = g_refs[a].at[dev] if by_cols[a] else _window(g_refs[a], shard_shapes[a], False, dev)
                copies.append(pltpu.make_async_remote_copy(
                    src_ref=theirs, dst_ref=got_refs[a].at[q], send_sem=send_sems.at[a, q], recv_sem=recv_sems.at[a, q],
                    device_id=(mx, my, 1 - mc), device_id_type=MESH))
        for cp in copies:
            cp.start()
        for cp in copies:
            cp.wait()

    sems = pltpu.SemaphoreType.DMA((na, N_CHIP))
    return pl.pallas_call(
        body, name="rs_sibling", out_shape=slabs, in_specs=[_ANY] * na, out_specs=[_ANY] * na,
        scratch_shapes=[sems, sems],
    )(*grads)


def _rs_add(g, got, core, by_cols, name):
    nq, nl, r, c = got.shape
    tr = _tile(r, ELEMENTWISE_ROW_TILE, 2 * SUBLANES)

    def body(core_ref, a_ref, b_ref, o_ref):
        o_ref[...] = (a_ref[...] + b_ref[...]).astype(BF16)

    if by_cols:
        mine = pl.BlockSpec((None, None, tr, c), lambda q, l, i, core_ref: (2 * q + core_ref[0], l, i, 0))
    else:
        mine = pl.BlockSpec((None, tr, c), lambda q, l, i, core_ref: (l, (2 * q + core_ref[0]) * (r // tr) + i, 0))
    spec = pl.BlockSpec((None, None, tr, c), lambda q, l, i, core_ref: (q, l, i, 0))
    return pl.pallas_call(
        body, name=name, out_shape=jax.ShapeDtypeStruct(got.shape, BF16),
        grid_spec=pltpu.PrefetchScalarGridSpec(num_scalar_prefetch=1, grid=(nq, nl, r // tr),
                                               in_specs=[mine, spec], out_specs=spec),
        compiler_params=pltpu.CompilerParams(dimension_semantics=("parallel",) * 3),
    )(core, g, got)


def _rs_chips(parts):
    na = len(parts)

    def body(*refs):
        p_refs, out_refs = refs[:na], refs[na:2 * na]
        send_sems, recv_sems, local_sems = refs[2 * na:]
        mx, my, mc = _coords()
        here = 2 * mx + my
        chips = [(1 - mx, my), (mx, 1 - my), (1 - mx, 1 - my)]
        copies = []
        for a in range(na):
            copies.append(pltpu.make_async_copy(p_refs[a].at[here], out_refs[a].at[here], local_sems.at[a]))
            for j, (cx, cy) in enumerate(chips):
                copies.append(pltpu.make_async_remote_copy(
                    src_ref=p_refs[a].at[2 * cx + cy], dst_ref=out_refs[a].at[here],
                    send_sem=send_sems.at[a, j], recv_sem=recv_sems.at[a, j], device_id=(cx, cy, mc), device_id_type=MESH))
        for cp in copies:
            cp.start()
        for cp in copies:
            cp.wait()

    return pl.pallas_call(
        body, name="rs_chips", out_shape=[jax.ShapeDtypeStruct(p.shape, p.dtype) for p in parts],
        in_specs=[_ANY] * na, out_specs=[_ANY] * na,
        scratch_shapes=[pltpu.SemaphoreType.DMA((na, 3)), pltpu.SemaphoreType.DMA((na, 3)), pltpu.SemaphoreType.DMA((na,))],
    )(*parts)


def _adamw(parts, w, m, v, name):
    n, r, c = parts.shape
    tr = _tile(r, 320, 2 * SUBLANES)
    bc1 = 1.0 - ADAM_B1 ** ADAM_STEP
    bc2 = 1.0 - ADAM_B2 ** ADAM_STEP

    def body(p_ref, w_ref, m_ref, v_ref, g_ref, d_ref, mo_ref, vo_ref):
        g = p_ref[0].astype(F32)
        for j in range(1, n):
            g = g + p_ref[j].astype(F32)
        mn = ADAM_B1 * m_ref[...] + (1.0 - ADAM_B1) * g
        vn = ADAM_B2 * v_ref[...] + (1.0 - ADAM_B2) * (g * g)
        g_ref[...] = g
        mo_ref[...] = mn
        vo_ref[...] = vn
        d_ref[...] = -ADAM_LR * ((mn / bc1) / (jnp.sqrt(vn / bc2) + ADAM_EPS) + ADAM_WD * w_ref[...])

    flat = pl.BlockSpec((tr, c), lambda i: (i, 0))
    return pl.pallas_call(
        body, name=name, grid=(r // tr,),
        in_specs=[pl.BlockSpec((n, tr, c), lambda i: (0, i, 0)), flat, flat, flat],
        out_specs=[flat] * 4, out_shape=[jax.ShapeDtypeStruct((r, c), F32)] * 4,
        compiler_params=pltpu.CompilerParams(dimension_semantics=("parallel",)),
    )(parts, w, m, v)


def _adam_math(g, w, m, v):
    bc1 = 1.0 - ADAM_B1 ** ADAM_STEP
    bc2 = 1.0 - ADAM_B2 ** ADAM_STEP
    mn = ADAM_B1 * m + (1.0 - ADAM_B1) * g
    vn = ADAM_B2 * v + (1.0 - ADAM_B2) * (g * g)
    return -ADAM_LR * ((mn / bc1) / (jnp.sqrt(vn / bc2) + ADAM_EPS) + ADAM_WD * w), mn, vn


def _adamw_shard(parts, w, m, v, name):
    nl, r, c = w.shape
    tr = _tile(r, ELEMENTWISE_ROW_TILE, 2 * SUBLANES)

    def body(p_ref, w_ref, m_ref, v_ref, g_ref, d_ref, mo_ref, vo_ref):
        g = p_ref[0].astype(F32)
        for q in range(1, N_CHIP):
            g = g + p_ref[q].astype(F32)
        g = g[:, :c]
        g_ref[...] = g
        d_ref[...], mo_ref[...], vo_ref[...] = _adam_math(g, w_ref[...], m_ref[...], v_ref[...])

    native = pl.BlockSpec((None, tr, c), lambda l, i: (l, i, 0))
    return pl.pallas_call(
        body, name=name, grid=(nl, r // tr),
        in_specs=[pl.BlockSpec((N_CHIP, None, tr, parts.shape[3]), lambda l, i: (0, l, i, 0)), native, native, native],
        out_specs=[native] * 4, out_shape=[jax.ShapeDtypeStruct(w.shape, F32)] * 4,
        compiler_params=pltpu.CompilerParams(dimension_semantics=("parallel", "parallel"), vmem_limit_bytes=V7X_VMEM_LIMIT),
    )(parts, w, m, v)


def _pack(arrs, dtype=F32):
    cols = []
    for a in arrs:
        f = a.reshape(-1).astype(dtype)
        cols.append(jnp.pad(f, (0, -f.shape[0] % FLAT_W)))
    flat = jnp.concatenate(cols)
    flat = jnp.pad(flat, (0, -flat.shape[0] % (FLAT_W * SUBLANES)))
    return flat.reshape(-1, FLAT_W)


def _unpack(flat, shapes, lead=()):
    flat = flat.reshape(lead + (-1,))
    out, off = [], 0
    for s in shapes:
        n = math.prod(s)
        out.append(flat[..., off:off + n].reshape(lead + tuple(s)))
        off += n + (-n % FLAT_W)
    return out


def kernel(x, p, ffn1_norm, ffn1_w1, ffn1_w3, ffn1_w2, mix_norm, ffn2_norm, ffn2_w1, ffn2_w3, ffn2_w2, ple_norm, ple_proj, ple_gate, s5_w_in, s5_a_re, s5_a_im, s5_log_dt, s5_b_re, s5_b_im, s5_c_re, s5_c_im, s5_d, s5_w_glu, sb_w_qkv, sb_w_o, final_norm, loss_target, m_ffn1_norm, m_ffn1_w1, m_ffn1_w3, m_ffn1_w2, m_mix_norm, m_ffn2_norm, m_ffn2_w1, m_ffn2_w3, m_ffn2_w2, m_ple_norm, m_ple_proj, m_ple_gate, m_s5_w_in, m_s5_a_re, m_s5_a_im, m_s5_log_dt, m_s5_b_re, m_s5_b_im, m_s5_c_re, m_s5_c_im, m_s5_d, m_s5_w_glu, m_sb_w_qkv, m_sb_w_o, m_final_norm, v_ffn1_norm, v_ffn1_w1, v_ffn1_w3, v_ffn1_w2, v_mix_norm, v_ffn2_norm, v_ffn2_w1, v_ffn2_w3, v_ffn2_w2, v_ple_norm, v_ple_proj, v_ple_gate, v_s5_w_in, v_s5_a_re, v_s5_a_im, v_s5_log_dt, v_s5_b_re, v_s5_b_im, v_s5_c_re, v_s5_c_im, v_s5_d, v_s5_w_glu, v_sb_w_qkv, v_sb_w_o, v_final_norm):
    given = dict(locals())
    wts = {n: given[n] for n in WEIGHTS}
    mom = {n: given["m_" + n] for n in WEIGHTS}
    var = {n: given["v_" + n] for n in WEIGHTS}
    bl, seq, d = x.shape
    t = bl * seq
    depth = p.shape[0]

    hid_pad = -wts["ffn1_w1"].shape[-1] % LANES

    def padded(n):
        a = wts[n].astype(BF16)
        if n in FFN_COL:
            return jnp.pad(a, ((0, 0), (0, 0), (0, hid_pad)))
        if n in FFN_ROW:
            return jnp.pad(a, ((0, 0), (0, hid_pad), (0, 0)))
        return a

    sent = [padded(n) for n in SHARDED]
    shard_shape = {n: a.shape for n, a in zip(SHARDED, sent)}
    by_cols = [n in COL_SHARDED for n in SHARDED]
    full = dict(zip(SHARDED, _gather_weights(sent, by_cols)))

    def row(a):
        return a.reshape(1, -1)

    def wt(n, layer=0, col=0, width=None):
        return _W(full[n], layer, col, width)

    gbuf = {}

    def wgrad(n, x_act, dy, layer=0, scale=1.0):
        like = (N_DEV,) + shard_shape[n] if n in COL_SHARDED else full[n].shape
        gbuf[n] = _wgrad(x_act, dy, f"{n}_grad", like, gbuf.get(n), layer, scale)

    n_groups = d // S5_GROUP
    a_re, a_im = s5_a_re[0], s5_a_im[0]
    log_dt = s5_log_dt[0].reshape(n_groups, 1)
    disc = _s5_disc(a_re, a_im, log_dt)
    nb = n_groups // S5_BLOCK_GROUPS
    ns = S5_BLOCK_GROUPS * S5_STATE
    par = jnp.concatenate([jnp.stack([q.reshape(nb, ns) for q in disc], axis=1), jnp.zeros((nb, 4, ns), F32)], axis=1)
    bre = _blockdiag(s5_b_re[0].transpose(0, 2, 1), S5_BLOCK_GROUPS).astype(BF16)
    bim = _blockdiag(s5_b_im[0].transpose(0, 2, 1), S5_BLOCK_GROUPS).astype(BF16)
    cre = _blockdiag(s5_c_re[0].transpose(0, 2, 1), S5_BLOCK_GROUPS).astype(BF16)
    cim = _blockdiag(s5_c_im[0].transpose(0, 2, 1), S5_BLOCK_GROUPS).astype(BF16)
    lc = min(SCAN_CHUNK, seq)

    h = x.reshape(t, d)
    saved = []
    for i in range(depth):
        s = {"h0": h}
        h, s["n1"], s["a1"], s["b1"] = _ffn_fwd(h, row(ffn1_norm[i]), wt("ffn1_w1", i), wt("ffn1_w3", i), wt("ffn1_w2", i))
        s["h1"] = h
        j = i // 2
        if i % 2 == 0:
            s["hn"], u = _norm_lin(h, row(mix_norm[i]), [wt("s5_w_in", j)], [F32], "s5_in")
            s["u"] = _interleave(u, lc)
            y, z, s["st"] = _s5_scan_fwd(s["u"], bre, bim, cre, cim, par, row(s5_d[j]), bl, seq, lc)
            s["y"], s["z"] = _deinterleave(y, lc), _deinterleave(z, lc)
            h = _s5_out(h, s["z"], wt("s5_w_glu", j))
        else:
            wqkv = [wt("sb_w_qkv", j, col, d) for col in range(3)]
            s["hn"], q, k, v = _norm_lin(h, row(mix_norm[i]), wqkv, [BF16] * 3, "sb_in")
            s["q"], s["k"], s["v"] = (_to_heads(a, bl, seq) for a in (q, k, v))
            o, s["tot"] = _sb_fwd(s["q"], s["k"], s["v"])
            s["o"] = _from_heads(o)
            h = _lin_res(h, s["o"], wt("sb_w_o", j), "sb_out")
        s["h2"] = h
        h, s["n2"], s["a2"], s["b2"] = _ffn_fwd(h, row(ffn2_norm[i]), wt("ffn2_w1", i), wt("ffn2_w3", i), wt("ffn2_w2", i))
        s["h3"] = h
        s["p"] = p[i].reshape(t, -1)
        h, s["npl"] = _ple_fwd(h, row(ple_norm[i]), wt("ple_gate", i), s["p"], wt("ple_proj", i))
        saved.append(s)

    loss_part, dh, g_final = _loss_head(h, row(final_norm), loss_target.reshape(t, d))
    loss = lax.psum(loss_part[0, 0], ("x", "y", "c"))

    grads = {n: [None] * wts[n].shape[0] for n in REPLICATED if n != "final_norm"}
    grads["final_norm"] = g_final.reshape(-1)

    def ffn_bwd(dh, h_in, n, a, b, which, i):
        da, db, sact = _ffn_bwd_down(dh, a, b, wt(f"{which}_w2", i))
        wgrad(f"{which}_w2", sact, dh, i, scale=0.5)
        dh_in, dg = _lin_bwd_norm([da, db], [wt(f"{which}_w1", i), wt(f"{which}_w3", i)], h_in,
                                  row(wts[f"{which}_norm"][i]), dh, f"{which}_bwd_up")
        wgrad(f"{which}_w1", n, da, i)
        wgrad(f"{which}_w3", n, db, i)
        grads[f"{which}_norm"][i] = dg.reshape(-1)
        return dh_in

    for i in reversed(range(depth)):
        s = saved[i]
        j = i // 2
        dh, dgl, dpp, dg = _ple_bwd(dh, s["h3"], row(ple_norm[i]), s["npl"], s["p"], wt("ple_gate", i), wt("ple_proj", i))
        grads["ple_norm"][i] = dg.reshape(-1)
        wgrad("ple_gate", s["npl"], dgl, i)
        wgrad("ple_proj", s["p"], dpp, i)
        dh = ffn_bwd(dh, s["h2"], s["n2"], s["a2"], s["b2"], "ffn2", i)
        if i % 2 == 0:
            dy, dzz = _s5_out_bwd(dh, s["z"], s["y"], wt("s5_w_glu", j))
            wgrad("s5_w_glu", s["z"], dzz, j)
            du, dbre, dbim, dcre, dcim, dpar, dd = _s5_scan_bwd(
                s["u"], _interleave(dy, lc), s["st"], bre, bim, cre, cim, par, row(s5_d[j]), bl, seq, lc)
            du = _deinterleave(du, lc)
            cts = [dpar[:, r, :].reshape(n_groups, S5_STATE) for r in range(4)]
            g_are, g_aim, g_ldt = _s5_disc_bwd(a_re, a_im, log_dt, cts)
            grads["s5_a_re"][j], grads["s5_a_im"][j], grads["s5_log_dt"][j] = g_are, g_aim, g_ldt.reshape(-1)
            grads["s5_b_re"][j] = _blockdiag_take(dbre, S5_BLOCK_GROUPS, S5_GROUP, S5_STATE).transpose(0, 2, 1)
            grads["s5_b_im"][j] = _blockdiag_take(dbim, S5_BLOCK_GROUPS, S5_GROUP, S5_STATE).transpose(0, 2, 1)
            grads["s5_c_re"][j] = _blockdiag_take(dcre, S5_BLOCK_GROUPS, S5_STATE, S5_GROUP).transpose(0, 2, 1)
            grads["s5_c_im"][j] = _blockdiag_take(dcim, S5_BLOCK_GROUPS, S5_STATE, S5_GROUP).transpose(0, 2, 1)
            grads["s5_d"][j] = dd.reshape(-1)
            dh, dg = _lin_bwd_norm([du], [wt("s5_w_in", j)], s["h1"], row(mix_norm[i]), dh, "s5_in_bwd")
            wgrad("s5_w_in", s["hn"], du, j)
        else:
            do = _lin_nt(dh, wt("sb_w_o", j), "sb_out_bwd")
            wgrad("sb_w_o", s["o"], dh, j)
            dqkv = _sb_bwd(s["q"], s["k"], s["v"], _to_heads(do, bl, seq), s["tot"])
            dqkv = [_from_heads(a) for a in dqkv]
            dh, dg = _lin_bwd_norm(dqkv, [wt("sb_w_qkv", j, col, d) for col in range(3)], s["h1"], row(mix_norm[i]), dh, "sb_in_bwd")
            wgrad("sb_w_qkv", s["hn"], jnp.concatenate(dqkv, axis=1), j)
        grads["mix_norm"][i] = dg.reshape(-1)
        dh = ffn_bwd(dh, s["h0"], s["n1"], s["a1"], s["b1"], "ffn1", i)
    grad_x = dh.reshape(x.shape)
    for n in REPLICATED:
        if n != "final_norm":
            grads[n] = jnp.stack(grads[n])

    got = _rs_sibling([gbuf[n] for n in SHARDED], [shard_shape[n] for n in SHARDED], by_cols)
    core = lax.axis_index("c").astype(jnp.int32).reshape(1)
    parts = _rs_chips([_rs_add(gbuf[n], g, core, n in COL_SHARDED, f"rs_add_{n}") for n, g in zip(SHARDED, got)])
    res = {n: _adamw_shard(part, wts[n], mom[n], var[n], f"adamw_{n}") for n, part in zip(SHARDED, parts)}

    rep_shapes = [wts[n].shape for n in REPLICATED]
    rep_parts = _all_gather(_pack([grads[n].reshape(wts[n].shape) for n in REPLICATED]), "gather_small_grads")
    outs = _adamw(rep_parts, _pack([wts[n] for n in REPLICATED]), _pack([mom[n] for n in REPLICATED]),
                  _pack([var[n] for n in REPLICATED]), "adamw_replicated")
    res.update({n: vals for n, vals in zip(REPLICATED, zip(*[_unpack(o, rep_shapes) for o in outs]))})

    return (loss, grad_x, *[res[n][0] for n in WEIGHTS], *[res[n][1] for n in WEIGHTS],
            *[res[n][2] for n in WEIGHTS], *[res[n][3] for n in WEIGHTS])
```

```python
import math
from typing import NamedTuple, Optional

import jax
import jax.numpy as jnp
from jax import lax
from jax.experimental import pallas as pl
from jax.experimental.pallas import tpu as pltpu

F32 = jnp.float32
BF16 = jnp.bfloat16
MESH = pl.DeviceIdType.MESH

N_DEV = 8
N_CHIP = 4
EPS = 1e-6
S5_GROUP = 16
S5_STATE = 64
S5_BLOCK_GROUPS = 16
HEAD_DIM = 64
Q_BLOCK = 128
SB_Q_TILE = 256
SB_HEADS_PER_STEP = 4
SCAN_CHUNK = 512
SUBLANES = 8
LANES = 128
FLAT_W = 1024
ADAM_LR, ADAM_B1, ADAM_B2, ADAM_EPS, ADAM_WD, ADAM_STEP = 1e-3, 0.9, 0.999, 1e-8, 0.01, 10
V7X_VMEM_LIMIT = 56 * 1024 * 1024
ROW_TILE = 256
WGRAD_ROW_TILE = 1024
WGRAD_OUT_BYTES = 6 * 1024 * 1024
ELEMENTWISE_ROW_TILE = 512

SHARDED = ("ffn1_w1", "ffn1_w3", "ffn1_w2", "ffn2_w1", "ffn2_w3", "ffn2_w2", "ple_proj", "ple_gate",
           "s5_w_in", "s5_w_glu", "sb_w_qkv", "sb_w_o")
COL_SHARDED = ("ffn1_w1", "ffn1_w3", "ffn2_w1", "ffn2_w3", "ple_proj", "s5_w_glu", "sb_w_qkv")
FFN_COL = ("ffn1_w1", "ffn1_w3", "ffn2_w1", "ffn2_w3")
FFN_ROW = ("ffn1_w2", "ffn2_w2")
REPLICATED = ("ffn1_norm", "mix_norm", "ffn2_norm", "ple_norm", "s5_a_re", "s5_a_im", "s5_log_dt",
              "s5_b_re", "s5_b_im", "s5_c_re", "s5_c_im", "s5_d", "final_norm")
WEIGHTS = ("ffn1_norm", "ffn1_w1", "ffn1_w3", "ffn1_w2", "mix_norm", "ffn2_norm", "ffn2_w1", "ffn2_w3", "ffn2_w2",
           "ple_norm", "ple_proj", "ple_gate", "s5_w_in", "s5_a_re", "s5_a_im", "s5_log_dt", "s5_b_re", "s5_b_im",
           "s5_c_re", "s5_c_im", "s5_d", "s5_w_glu", "sb_w_qkv", "sb_w_o", "final_norm")


def _dot(a, b):
    return jnp.dot(a, b, preferred_element_type=F32)


def _dot_nt(a, b):
    return lax.dot_general(a, b, (((1,), (1,)), ((), ())), preferred_element_type=F32)


def _dot_tn(a, b):
    return lax.dot_general(a, b, (((0,), (0,)), ((), ())), preferred_element_type=F32)


def _dot2(x, m):
    hi = x.astype(BF16)
    lo = (x - hi.astype(F32)).astype(BF16)
    return _dot(hi, m) + _dot(lo, m)


def _rms(x):
    r = lax.rsqrt(jnp.mean(x * x, axis=-1, keepdims=True) + EPS)
    return x * r, r


def _rms_bwd(dn, xh, r, g):
    gy = dn * g
    return r * (gy - xh * jnp.mean(gy * xh, axis=-1, keepdims=True))


_GELU_C = math.sqrt(2.0 / math.pi)


def _gelu(x):
    return 0.5 * x * (1.0 + jnp.tanh(_GELU_C * (x + 0.044715 * x * x * x)))


def _gelu_grad(x):
    th = jnp.tanh(_GELU_C * (x + 0.044715 * x * x * x))
    return 0.5 * (1.0 + th) + 0.5 * x * (1.0 - th * th) * _GELU_C * (1.0 + 3.0 * 0.044715 * x * x)


def _neg_softplus(z):
    return -(jnp.maximum(z, 0.0) + jnp.log(1.0 + jnp.exp(-jnp.abs(z))))


def _tile(n, want, mult=SUBLANES):
    for t in range(min(want, n), 0, -1):
        if n % t == 0 and t % mult == 0:
            return t
    return n


def _rows(tm, c):
    return pl.BlockSpec((tm, c), lambda i: (i, 0))


def _whole(a):
    nd = a.ndim
    return pl.BlockSpec(a.shape, lambda i: (0,) * nd)


class _W(NamedTuple):
    arr: jax.Array
    layer: int = 0
    col: int = 0
    width: Optional[int] = None

    @property
    def shape(self):
        return self.arr.shape[1], self.width or self.arr.shape[2]


def _wspec(w):
    return pl.BlockSpec((None,) + w.shape, lambda *_: (w.layer, 0, w.col))


def _row_params(sem="parallel"):
    return pltpu.CompilerParams(dimension_semantics=(sem,), vmem_limit_bytes=V7X_VMEM_LIMIT)


def _ffn_fwd(h, g, w1, w3, w2):
    t, d = h.shape
    f = w1.shape[1]
    tm = _tile(t, ROW_TILE)

    def body(h_ref, g_ref, w1_ref, w3_ref, w2_ref, ho_ref, n_ref, a_ref, b_ref):
        x = h_ref[...]
        xh, _ = _rms(x)
        n = (xh * g_ref[...]).astype(BF16)
        a = _dot(n, w1_ref[...])
        b = _dot(n, w3_ref[...])
        s = (a * jax.nn.sigmoid(a) * b).astype(BF16)
        ho_ref[...] = x + 0.5 * _dot(s, w2_ref[...])
        n_ref[...] = n
        a_ref[...] = a.astype(BF16)
        b_ref[...] = b.astype(BF16)

    return pl.pallas_call(
        body, name="ffn_fwd", grid=(t // tm,),
        in_specs=[_rows(tm, d), _whole(g), _wspec(w1), _wspec(w3), _wspec(w2)],
        out_specs=[_rows(tm, d), _rows(tm, d), _rows(tm, f), _rows(tm, f)],
        out_shape=[jax.ShapeDtypeStruct((t, d), F32), jax.ShapeDtypeStruct((t, d), BF16),
                   jax.ShapeDtypeStruct((t, f), BF16), jax.ShapeDtypeStruct((t, f), BF16)],
        compiler_params=_row_params(),
    )(h, g, w1.arr, w3.arr, w2.arr)


def _ffn_bwd_down(dh, a, b, w2):
    t, d = dh.shape
    f = a.shape[1]
    tm = _tile(t, ROW_TILE)

    def body(dh_ref, a_ref, b_ref, w2_ref, da_ref, db_ref, s_ref):
        ds = _dot_nt((0.5 * dh_ref[...]).astype(BF16), w2_ref[...])
        a32 = a_ref[...].astype(F32)
        b32 = b_ref[...].astype(F32)
        sig = jax.nn.sigmoid(a32)
        sil = a32 * sig
        da_ref[...] = (ds * b32 * (sig * (1.0 + a32 * (1.0 - sig)))).astype(BF16)
        db_ref[...] = (ds * sil).astype(BF16)
        s_ref[...] = (sil * b32).astype(BF16)

    return pl.pallas_call(
        body, name="ffn_bwd_down", grid=(t // tm,),
        in_specs=[_rows(tm, d), _rows(tm, f), _rows(tm, f), _wspec(w2)],
        out_specs=[_rows(tm, f)] * 3,
        out_shape=[jax.ShapeDtypeStruct((t, f), BF16)] * 3,
        compiler_params=_row_params(),
    )(dh, a, b, w2.arr)


def _lin_bwd_norm(dys, ws, h, g, dh, name):
    t, d = h.shape
    tm = _tile(t, ROW_TILE)
    k = len(dys)

    def body(*refs):
        dy_refs, w_refs = refs[:k], refs[k:2 * k]
        h_ref, g_ref, dh_ref, o_ref, dg_ref = refs[2 * k:]
        dn = _dot_nt(dy_refs[0][...].astype(BF16), w_refs[0][...])
        for j in range(1, k):
            dn = dn + _dot_nt(dy_refs[j][...].astype(BF16), w_refs[j][...])
        xh, r = _rms(h_ref[...])
        o_ref[...] = dh_ref[...] + _rms_bwd(dn, xh, r, g_ref[...])

        @pl.when(pl.program_id(0) == 0)
        def _():
            dg_ref[...] = jnp.zeros_like(dg_ref)

        dg_ref[...] += jnp.sum(dn * xh, axis=0, keepdims=True)

    return pl.pallas_call(
        body, name=name, grid=(t // tm,),
        in_specs=[_rows(tm, dy.shape[1]) for dy in dys] + [_wspec(w) for w in ws] + [_rows(tm, d), _whole(g), _rows(tm, d)],
        out_specs=[_rows(tm, d), pl.BlockSpec((1, d), lambda i: (0, 0))],
        out_shape=[jax.ShapeDtypeStruct((t, d), F32), jax.ShapeDtypeStruct((1, d), F32)],
        compiler_params=_row_params("arbitrary"),
    )(*dys, *[w.arr for w in ws], h, g, dh)


def _wgrad(x, dy, name, like, into=None, layer=0, scale=1.0):
    t, kk = x.shape
    n = dy.shape[1]
    tm = _tile(t, WGRAD_ROW_TILE)
    steps = t // tm
    blocked = len(like) == 4
    if blocked:
        sc = like[3]
        nd = max(g for g in (1, 2, 4, 8) if g * sc * kk * 4 <= max(WGRAD_OUT_BYTES, sc * kk * 4))
        tn = nd * sc
        out_spec = pl.BlockSpec((nd, None, kk, sc), lambda j, i: (j, layer, 0, 0))
    else:
        tn = _tile(n, max(128, WGRAD_OUT_BYTES // (4 * kk)), 128)
        out_spec = pl.BlockSpec((None, kk, tn), lambda j, i: (layer, 0, j))

    def body(x_ref, dy_ref, *rest):
        o_ref = rest[-1]
        i = pl.program_id(1)

        @pl.when(i == 0)
        def _():
            o_ref[...] = jnp.zeros_like(o_ref)

        acc = _dot_tn(x_ref[...].astype(BF16), dy_ref[...].astype(BF16))
        if blocked:
            for dd in range(nd):
                o_ref[dd] += acc[:, dd * sc:(dd + 1) * sc]
        else:
            o_ref[...] += acc
        if scale != 1.0:
            @pl.when(i == steps - 1)
            def _():
                o_ref[...] = o_ref[...] * scale

    held = [] if into is None else [into]
    return pl.pallas_call(
        body, name=name, grid=(n // tn, steps),
        in_specs=[pl.BlockSpec((tm, kk), lambda j, i: (i, 0)), pl.BlockSpec((tm, tn), lambda j, i: (i, j))]
        + [pl.BlockSpec(memory_space=pl.ANY)] * len(held),
        out_specs=out_spec,
        out_shape=jax.ShapeDtypeStruct(like, F32),
        input_output_aliases={2: 0} if held else {},
        compiler_params=pltpu.CompilerParams(dimension_semantics=("parallel", "arbitrary"), vmem_limit_bytes=V7X_VMEM_LIMIT),
    )(x, dy, *held)


def _norm_lin(h, g, ws, out_dtypes, name):
    t, d = h.shape
    tm = _tile(t, ROW_TILE)
    k = len(ws)

    def body(*refs):
        h_ref, g_ref = refs[:2]
        w_refs = refs[2:2 + k]
        n_ref = refs[2 + k]
        o_refs = refs[3 + k:]
        xh, _ = _rms(h_ref[...])
        n = (xh * g_ref[...]).astype(BF16)
        n_ref[...] = n
        for w_ref, o_ref in zip(w_refs, o_refs):
            o_ref[...] = _dot(n, w_ref[...]).astype(o_ref.dtype)

    return pl.pallas_call(
        body, name=name, grid=(t // tm,),
        in_specs=[_rows(tm, d), _whole(g)] + [_wspec(w) for w in ws],
        out_specs=[_rows(tm, d)] + [_rows(tm, w.shape[1]) for w in ws],
        out_shape=[jax.ShapeDtypeStruct((t, d), BF16)] + [jax.ShapeDtypeStruct((t, w.shape[1]), dt) for w, dt in zip(ws, out_dtypes)],
        compiler_params=_row_params(),
    )(h, g, *[w.arr for w in ws])


def _lin_res(h, x, w, name):
    t, d = h.shape
    tm = _tile(t, ROW_TILE)

    def body(h_ref, x_ref, w_ref, o_ref):
        o_ref[...] = h_ref[...] + _dot(x_ref[...], w_ref[...])

    return pl.pallas_call(
        body, name=name, grid=(t // tm,),
        in_specs=[_rows(tm, d), _rows(tm, x.shape[1]), _wspec(w)],
        out_specs=_rows(tm, d), out_shape=jax.ShapeDtypeStruct((t, d), F32),
        compiler_params=_row_params(),
    )(h, x, w.arr)


def _lin_nt(dy, w, name):
    t = dy.shape[0]
    kk = w.shape[0]
    tm = _tile(t, ROW_TILE)

    def body(dy_ref, w_ref, o_ref):
        o_ref[...] = _dot_nt(dy_ref[...].astype(BF16), w_ref[...]).astype(BF16)

    return pl.pallas_call(
        body, name=name, grid=(t // tm,),
        in_specs=[_rows(tm, dy.shape[1]), _wspec(w)],
        out_specs=_rows(tm, kk), out_shape=jax.ShapeDtypeStruct((t, kk), BF16),
        compiler_params=_row_params(),
    )(dy, w.arr)


def _ple_fwd(h, g, wg, p, wp):
    t, d = h.shape
    tm = _tile(t, ROW_TILE)

    def body(h_ref, g_ref, wg_ref, p_ref, wp_ref, o_ref, n_ref):
        x = h_ref[...]
        xh, _ = _rms(x)
        n = (xh * g_ref[...]).astype(BF16)
        n_ref[...] = n
        gate = jax.nn.sigmoid(_dot(n, wg_ref[...]))
        o_ref[...] = x + _dot(p_ref[...].astype(BF16), wp_ref[...]) * gate

    return pl.pallas_call(
        body, name="ple_fwd", grid=(t // tm,),
        in_specs=[_rows(tm, d), _whole(g), _wspec(wg), _rows(tm, p.shape[1]), _wspec(wp)],
        out_specs=[_rows(tm, d), _rows(tm, d)],
        out_shape=[jax.ShapeDtypeStruct((t, d), F32), jax.ShapeDtypeStruct((t, d), BF16)],
        compiler_params=_row_params(),
    )(h, g, wg.arr, p, wp.arr)


def _ple_bwd(dh, h, g, n, p, wg, wp):
    t, d = h.shape
    tm = _tile(t, ROW_TILE)

    def body(dh_ref, h_ref, g_ref, n_ref, p_ref, wg_ref, wp_ref, o_ref, dgl_ref, dpp_ref, dg_ref):
        dh_v = dh_ref[...]
        gate = jax.nn.sigmoid(_dot(n_ref[...], wg_ref[...]))
        pp = _dot(p_ref[...].astype(BF16), wp_ref[...])
        dgl = (dh_v * pp * gate * (1.0 - gate)).astype(BF16)
        dgl_ref[...] = dgl
        dpp_ref[...] = (dh_v * gate).astype(BF16)
        dn = _dot_nt(dgl, wg_ref[...])
        xh, r = _rms(h_ref[...])
        o_ref[...] = dh_v + _rms_bwd(dn, xh, r, g_ref[...])

        @pl.when(pl.program_id(0) == 0)
        def _():
            dg_ref[...] = jnp.zeros_like(dg_ref)

        dg_ref[...] += jnp.sum(dn * xh, axis=0, keepdims=True)

    return pl.pallas_call(
        body, name="ple_bwd", grid=(t // tm,),
        in_specs=[_rows(tm, d), _rows(tm, d), _whole(g), _rows(tm, d), _rows(tm, p.shape[1]), _wspec(wg), _wspec(wp)],
        out_specs=[_rows(tm, d), _rows(tm, d), _rows(tm, d), pl.BlockSpec((1, d), lambda i: (0, 0))],
        out_shape=[jax.ShapeDtypeStruct((t, d), F32), jax.ShapeDtypeStruct((t, d), BF16),
                   jax.ShapeDtypeStruct((t, d), BF16), jax.ShapeDtypeStruct((1, d), F32)],
        compiler_params=_row_params("arbitrary"),
    )(dh, h, g, n, p, wg.arr, wp.arr)


def _loss_head(h, g, tgt):
    t, d = h.shape
    tm = _tile(t, ROW_TILE)

    def body(h_ref, g_ref, t_ref, l_ref, dh_ref, dg_ref):
        xh, r = _rms(h_ref[...])
        gg = g_ref[...]
        e = xh * gg - t_ref[...]
        dy = e * (1.0 / d)

        @pl.when(pl.program_id(0) == 0)
        def _():
            l_ref[...] = jnp.zeros_like(l_ref)
            dg_ref[...] = jnp.zeros_like(dg_ref)

        l_ref[...] += 0.5 * jnp.sum(jnp.mean(e * e, axis=-1, keepdims=True), axis=0, keepdims=True)
        dg_ref[...] += jnp.sum(dy * xh, axis=0, keepdims=True)
        dh_ref[...] = _rms_bwd(dy, xh, r, gg)

    return pl.pallas_call(
        body, name="loss_head", grid=(t // tm,),
        in_specs=[_rows(tm, d), _whole(g), _rows(tm, d)],
        out_specs=[pl.BlockSpec((1, 128), lambda i: (0, 0)), _rows(tm, d), pl.BlockSpec((1, d), lambda i: (0, 0))],
        out_shape=[jax.ShapeDtypeStruct((1, 128), F32), jax.ShapeDtypeStruct((t, d), F32), jax.ShapeDtypeStruct((1, d), F32)],
        compiler_params=_row_params("arbitrary"),
    )(h, g, tgt)


def _s5_disc_math(a_re, a_im, log_dt):
    lam_re = jnp.minimum(a_re, -1e-4)
    lam_im = a_im
    dt = jnp.exp(log_dt)
    mag = jnp.exp(lam_re * dt)
    abar_re = mag * jnp.cos(lam_im * dt)
    abar_im = mag * jnp.sin(lam_im * dt)
    den = lam_re * lam_re + lam_im * lam_im
    nr = abar_re - 1.0
    ni = abar_im
    return abar_re, abar_im, (nr * lam_re + ni * lam_im) / den, (ni * lam_re - nr * lam_im) / den


def _s5_disc(a_re, a_im, log_dt):
    gp = jax.ShapeDtypeStruct(a_re.shape, F32)

    def body(ar_ref, ai_ref, ld_ref, o0, o1, o2, o3):
        outs = _s5_disc_math(ar_ref[...], ai_ref[...], ld_ref[...])
        for o_ref, val in zip((o0, o1, o2, o3), outs):
            o_ref[...] = val

    return pl.pallas_call(body, name="s5_disc", out_shape=[gp] * 4)(a_re, a_im, log_dt)


def _s5_disc_bwd(a_re, a_im, log_dt, cts):
    def body(ar_ref, ai_ref, ld_ref, c0, c1, c2, c3, dar_ref, dai_ref, dld_ref):
        _, vjp = jax.vjp(_s5_disc_math, ar_ref[...], ai_ref[...], ld_ref[...])
        dar, dai, dld = vjp((c0[...], c1[...], c2[...], c3[...]))
        dar_ref[...] = dar
        dai_ref[...] = dai
        dld_ref[...] = dld

    return pl.pallas_call(
        body, name="s5_disc_bwd",
        out_shape=[jax.ShapeDtypeStruct(a_re.shape, F32), jax.ShapeDtypeStruct(a_im.shape, F32),
                   jax.ShapeDtypeStruct(log_dt.shape, F32)],
    )(a_re, a_im, log_dt, *cts)


def _blockdiag(w, gl):
    g, r, c = w.shape
    w = w.reshape(g // gl, gl, r, c)
    eye = jnp.eye(gl, dtype=w.dtype)
    return (w[:, :, :, None, :] * eye[None, :, None, :, None]).reshape(g // gl, gl * r, gl * c)


def _blockdiag_take(m, gl, r, c):
    nb = m.shape[0]
    own = jnp.eye(gl, dtype=bool)[None, :, None, :, None]
    return jnp.where(own, m.reshape(nb, gl, r, gl, c), 0.0).sum(axis=3).reshape(nb * gl, r, c)


def _cmul(ar, ai, br, bi):
    return ar * br - ai * bi, ar * bi + ai * br


def _powers(ar, ai, n):
    out = [(ar, ai)]
    for _ in range(n - 1):
        out.append(_cmul(ar, ai, *out[-1]))
    return out


def _interleave(a, lc):
    t, d = a.shape
    return a.reshape(t // lc, SUBLANES, lc // SUBLANES, d).transpose(0, 2, 1, 3).reshape(t, d)


def _deinterleave(a, lc):
    t, d = a.shape
    return a.reshape(t // lc, lc // SUBLANES, SUBLANES, d).transpose(0, 2, 1, 3).reshape(t, d)


def _seg_rows(k):
    return slice(k * SUBLANES, (k + 1) * SUBLANES)


def _seg_scan(re_ref, im_ref, ar, ai, seg, reverse=False):
    cr = ci = jnp.zeros((SUBLANES, re_ref.shape[1]), F32)
    for k in (reversed(range(seg)) if reverse else range(seg)):
        rows = _seg_rows(k)
        cr, ci = ar * cr - ai * ci + re_ref[rows, :], ar * ci + ai * cr + im_ref[rows, :]
        re_ref[rows, :] = cr
        im_ref[rows, :] = ci
    return cr, ci


def _seg_entering(er, ei, pr, pi, c0r, c0i, reverse=False):
    rows = lax.broadcasted_iota(jnp.int32, er.shape, 0)
    vr = vi = jnp.zeros_like(er)
    cr, ci = c0r, c0i
    for j in (reversed(range(SUBLANES)) if reverse else range(SUBLANES)):
        vr = jnp.where(rows == j, cr, vr)
        vi = jnp.where(rows == j, ci, vi)
        cr, ci = er[j:j + 1, :] + pr * cr - pi * ci, ei[j:j + 1, :] + pr * ci + pi * cr
    return vr, vi, cr, ci


def _s5_scan_fwd(u, bre, bim, cre, cim, par, dskip, bl, seq, lc):
    t, d = u.shape
    nb, gw, ns = bre.shape
    nc = seq // lc
    seg = lc // SUBLANES

    def body(u_ref, bre_ref, bim_ref, cre_ref, cim_ref, par_ref, d_ref, y_ref, z_ref, st_ref, carry, sre, sim):
        @pl.when(pl.program_id(2) == 0)
        def _():
            carry[...] = jnp.zeros_like(carry)

        st_ref[...] = carry[...]
        uu = u_ref[...]
        ug = uu.astype(BF16)
        wre = _dot(ug, bre_ref[...])
        wim = _dot(ug, bim_ref[...])
        ar, ai = par_ref[0:1, :], par_ref[1:2, :]
        fr, fi = par_ref[2:3, :], par_ref[3:4, :]
        sre[...] = fr * wre - fi * wim
        sim[...] = fr * wim + fi * wre
        pows = _powers(ar, ai, seg)
        er, ei = _seg_scan(sre, sim, ar, ai, seg)
        vr, vi, cr, ci = _seg_entering(er, ei, *pows[-1], carry[0:1, :], carry[1:2, :])
        carry[0:1, :] = cr
        carry[1:2, :] = ci
        for k in range(seg):
            rows = _seg_rows(k)
            fr_k, fi_k = _cmul(*pows[k], vr, vi)
            sre[rows, :] += fr_k
            sim[rows, :] += fi_k
        y = _dot(sre[...].astype(BF16), cre_ref[...]) - _dot(sim[...].astype(BF16), cim_ref[...]) + d_ref[...] * uu
        y_ref[...] = y
        z_ref[...] = _gelu(y).astype(BF16)

    tok = pl.BlockSpec((lc, gw), lambda g, b, c: (b * nc + c, g))
    mat_b = pl.BlockSpec((None, gw, ns), lambda g, b, c: (g, 0, 0))
    mat_c = pl.BlockSpec((None, ns, gw), lambda g, b, c: (g, 0, 0))
    return pl.pallas_call(
        body, name="s5_scan_fwd", grid=(nb, bl, nc),
        in_specs=[tok, mat_b, mat_b, mat_c, mat_c, pl.BlockSpec((None, 8, ns), lambda g, b, c: (g, 0, 0)),
                  pl.BlockSpec((1, gw), lambda g, b, c: (0, g))],
        out_specs=[tok, tok, pl.BlockSpec((None, None, 2, ns), lambda g, b, c: (g, b * nc + c, 0, 0))],
        out_shape=[jax.ShapeDtypeStruct((t, d), F32), jax.ShapeDtypeStruct((t, d), BF16),
                   jax.ShapeDtypeStruct((nb, bl * nc, 2, ns), F32)],
        scratch_shapes=[pltpu.VMEM((2, ns), F32), pltpu.VMEM((lc, ns), F32), pltpu.VMEM((lc, ns), F32)],
        compiler_params=pltpu.CompilerParams(dimension_semantics=("parallel", "arbitrary", "arbitrary"),
                                             vmem_limit_bytes=V7X_VMEM_LIMIT),
    )(u, bre, bim, cre, cim, par, dskip)


def _s5_scan_bwd(u, dy, st, bre, bim, cre, cim, par, dskip, bl, seq, lc):
    t, d = u.shape
    nb, gw, ns = bre.shape
    nc = seq // lc
    seg = lc // SUBLANES

    def body(u_ref, dy_ref, st_ref, bre_ref, bim_ref, cre_ref, cim_ref, par_ref, d_ref,
             du_ref, dbre_ref, dbim_ref, dcre_ref, dcim_ref, dpar_ref, dd_ref,
             lcarry, sre, sim, wre_s, wim_s, lre, lim):
        b, c = pl.program_id(1), pl.program_id(2)

        @pl.when((b == 0) & (c == 0))
        def _():
            for ref in (dbre_ref, dbim_ref, dcre_ref, dcim_ref, dpar_ref, dd_ref):
                ref[...] = jnp.zeros_like(ref)

        @pl.when(c == 0)
        def _():
            lcarry[...] = jnp.zeros_like(lcarry)

        uu = u_ref[...]
        ug = uu.astype(BF16)
        dyv = dy_ref[...]
        dyb = dyv.astype(BF16)
        ar, ai = par_ref[0:1, :], par_ref[1:2, :]
        fr, fi = par_ref[2:3, :], par_ref[3:4, :]
        wre = _dot(ug, bre_ref[...])
        wim = _dot(ug, bim_ref[...])
        wre_s[...] = wre
        wim_s[...] = wim
        sre[...] = fr * wre - fi * wim
        sim[...] = fr * wim + fi * wre

        pows = _powers(ar, ai, seg)
        er, ei = _seg_scan(sre, sim, ar, ai, seg)
        svr, svi, _, _ = _seg_entering(er, ei, *pows[-1], st_ref[0:1, :], st_ref[1:2, :])
        for k in range(seg):
            rows = _seg_rows(k)
            fr_k, fi_k = _cmul(*pows[k], svr, svi)
            sre[rows, :] += fr_k
            sim[rows, :] += fi_k

        lre[...] = _dot_nt(dyb, cre_ref[...])
        lim[...] = -_dot_nt(dyb, cim_ref[...])
        er, ei = _seg_scan(lre, lim, ar, -ai, seg, reverse=True)
        lvr, lvi, cr, ci = _seg_entering(er, ei, pows[-1][0], -pows[-1][1], lcarry[0:1, :], lcarry[1:2, :], reverse=True)
        lcarry[0:1, :] = cr
        lcarry[1:2, :] = ci
        dar = dai = jnp.zeros_like(er)
        for k in range(seg):
            rows = _seg_rows(k)
            pr, pi = pows[seg - 1 - k]
            fr_k, fi_k = _cmul(pr, -pi, lvr, lvi)
            lr = lre[rows, :] + fr_k
            li = lim[rows, :] + fi_k
            lre[rows, :] = lr
            lim[rows, :] = li
            spr, spi = (svr, svi) if k == 0 else (sre[_seg_rows(k - 1), :], sim[_seg_rows(k - 1), :])
            dar = dar + lr * spr + li * spi
            dai = dai + li * spr - lr * spi

        lr, li = lre[...], lim[...]
        wr, wi = wre_s[...], wim_s[...]
        dpar_ref[0:1, :] += jnp.sum(dar, axis=0, keepdims=True)
        dpar_ref[1:2, :] += jnp.sum(dai, axis=0, keepdims=True)
        dpar_ref[2:3, :] += jnp.sum(lr * wr + li * wi, axis=0, keepdims=True)
        dpar_ref[3:4, :] += jnp.sum(li * wr - lr * wi, axis=0, keepdims=True)
        dwr = (fr * lr + fi * li).astype(BF16)
        dwi = (fr * li - fi * lr).astype(BF16)
        dsk = d_ref[...]
        du_ref[...] = _dot_nt(dwr, bre_ref[...]) + _dot_nt(dwi, bim_ref[...]) + dsk * dyv
        dd_ref[...] += jnp.sum(dyv * uu, axis=0, keepdims=True)
        dbre_ref[...] += _dot_tn(ug, dwr)
        dbim_ref[...] += _dot_tn(ug, dwi)
        dcre_ref[...] += _dot_tn(sre[...].astype(BF16), dyb)
        dcim_ref[...] -= _dot_tn(sim[...].astype(BF16), dyb)

    tok = pl.BlockSpec((lc, gw), lambda g, b, c: (b * nc + nc - 1 - c, g))
    mat_b = pl.BlockSpec((None, gw, ns), lambda g, b, c: (g, 0, 0))
    mat_c = pl.BlockSpec((None, ns, gw), lambda g, b, c: (g, 0, 0))
    rows8 = pl.BlockSpec((None, 8, ns), lambda g, b, c: (g, 0, 0))
    dvec = pl.BlockSpec((1, gw), lambda g, b, c: (0, g))
    tile = pltpu.VMEM((lc, ns), F32)
    return pl.pallas_call(
        body, name="s5_scan_bwd", grid=(nb, bl, nc),
        in_specs=[tok, tok, pl.BlockSpec((None, None, 2, ns), lambda g, b, c: (g, b * nc + nc - 1 - c, 0, 0)),
                  mat_b, mat_b, mat_c, mat_c, rows8, dvec],
        out_specs=[tok, mat_b, mat_b, mat_c, mat_c, rows8, dvec],
        out_shape=[jax.ShapeDtypeStruct((t, d), F32),
                   jax.ShapeDtypeStruct((nb, gw, ns), F32), jax.ShapeDtypeStruct((nb, gw, ns), F32),
                   jax.ShapeDtypeStruct((nb, ns, gw), F32), jax.ShapeDtypeStruct((nb, ns, gw), F32),
                   jax.ShapeDtypeStruct((nb, 8, ns), F32), jax.ShapeDtypeStruct((1, d), F32)],
        scratch_shapes=[pltpu.VMEM((2, ns), F32)] + [tile] * 6,
        compiler_params=pltpu.CompilerParams(dimension_semantics=("arbitrary", "arbitrary", "arbitrary"),
                                             vmem_limit_bytes=V7X_VMEM_LIMIT),
    )(u, dy, st, bre, bim, cre, cim, par, dskip)


def _s5_out(h, z, wglu):
    t, d = h.shape
    tm = _tile(t, ROW_TILE)

    def body(h_ref, z_ref, w_ref, o_ref):
        zz = _dot(z_ref[...], w_ref[...])
        o_ref[...] = h_ref[...] + zz[:, :d] * jax.nn.sigmoid(zz[:, d:])

    return pl.pallas_call(
        body, name="s5_out", grid=(t // tm,),
        in_specs=[_rows(tm, d), _rows(tm, d), _wspec(wglu)],
        out_specs=_rows(tm, d), out_shape=jax.ShapeDtypeStruct((t, d), F32),
        compiler_params=_row_params(),
    )(h, z, wglu.arr)


def _s5_out_bwd(dh, z, y, wglu):
    t, d = dh.shape
    tm = _tile(t, ROW_TILE)

    def body(dh_ref, z_ref, y_ref, w_ref, dy_ref, dzz_ref):
        zz = _dot(z_ref[...], w_ref[...])
        out, sg = zz[:, :d], jax.nn.sigmoid(zz[:, d:])
        dh_v = dh_ref[...]
        dzz = jnp.concatenate([dh_v * sg, dh_v * out * sg * (1.0 - sg)], axis=1).astype(BF16)
        dzz_ref[...] = dzz
        dy_ref[...] = _dot_nt(dzz, w_ref[...]) * _gelu_grad(y_ref[...])

    return pl.pallas_call(
        body, name="s5_out_bwd", grid=(t // tm,),
        in_specs=[_rows(tm, d), _rows(tm, d), _rows(tm, d), _wspec(wglu)],
        out_specs=[_rows(tm, d), _rows(tm, 2 * d)],
        out_shape=[jax.ShapeDtypeStruct((t, d), F32), jax.ShapeDtypeStruct((t, 2 * d), BF16)],
        compiler_params=_row_params(),
    )(dh, z, y, wglu.arr)


def _head_spec(seq, w):
    return pl.BlockSpec((None, SB_HEADS_PER_STEP, seq, w), lambda b, h: (b, h, 0, 0))


def _tri_and_ones(kind):
    row = lax.broadcasted_iota(jnp.int32, (Q_BLOCK, Q_BLOCK), 0)
    col = lax.broadcasted_iota(jnp.int32, (Q_BLOCK, Q_BLOCK), 1)
    tri = {"after": row > col, "upto": row <= col, "before": row < col}[kind]
    return jnp.concatenate([tri.astype(BF16), jnp.ones((Q_BLOCK, Q_BLOCK), BF16)], axis=1)


def _sb_fwd(q, k, v):
    bsz, nh, seq, dh = q.shape
    tq = min(SB_Q_TILE, seq // 2)
    nq, nsub = seq // tq, tq // Q_BLOCK
    scale = dh ** -0.5
    hp = SB_HEADS_PER_STEP
    qb = Q_BLOCK

    def body(q_ref, k_ref, v_ref, o_ref, tot_ref):
        strict = lax.broadcasted_iota(jnp.int32, (qb, qb), 1) < lax.broadcasted_iota(jnp.int32, (qb, qb), 0)
        sums = _tri_and_ones("after")

        def sweep(qs, c0, units, carry):
            heads = sorted({u[0] for u in units})
            kbs = {hh: k_ref[hh, pl.ds(c0, qb), :] for hh in heads}
            vbs = {hh: v_ref[hh, pl.ds(c0, qb), :] for hh in heads}
            zs = [_dot_nt(qs[hh][s], kbs[hh]) * scale for hh, s, _ in units]
            lkrs = [_neg_softplus(z) for z in zs]
            lks = [jnp.where(strict, lkr, 0.0) if dg else lkr for lkr, (_, _, dg) in zip(lkrs, units)]
            css = [_dot2(lk, sums) for lk in lks]
            carry = dict(carry)
            for (hh, s, dg), z, lkr, cs in zip(units, zs, lkrs, css):
                acc, run = carry[hh, s]
                att = jnp.exp(z + lkr + cs[:, :qb] + run)
                if dg:
                    att = jnp.where(strict, att, 0.0)
                carry[hh, s] = (acc + _dot(att.astype(BF16), vbs[hh]), run + cs[:, qb:])
            return carry

        def q_loop(qi, _):
            r0 = qi * tq
            qs = {hh: [q_ref[hh, pl.ds(pl.multiple_of(r0 + s * qb, qb), qb), :] for s in range(nsub)] for hh in range(hp)}
            carry = {(hh, s): (jnp.zeros((qb, dh), F32), jnp.zeros((qb, qb), F32)) for hh in range(hp) for s in range(nsub)}
            for jj in reversed(range(nsub)):
                units = [(hh, s, s == jj) for hh in range(hp) for s in range(jj, nsub)]
                carry = sweep(qs, pl.multiple_of(r0 + jj * qb, qb), units, carry)
            units = [(hh, s, False) for hh in range(hp) for s in range(nsub)]
            carry = lax.fori_loop(
                0, nsub * qi, lambda t, c: sweep(qs, pl.multiple_of((nsub * qi - 1 - t) * qb, qb), units, c), carry)
            for (hh, s), (acc, run) in carry.items():
                rows = pl.ds(pl.multiple_of(r0 + s * qb, qb), qb)
                o_ref[hh, rows, :] = acc.astype(BF16)
                tot_ref[hh, rows, :] = run[:, 0:1]
            return 0

        lax.fori_loop(0, nq, q_loop, 0)

    return pl.pallas_call(
        body, name="sb_fwd", grid=(bsz, nh // hp),
        in_specs=[_head_spec(seq, dh)] * 3,
        out_specs=[_head_spec(seq, dh), _head_spec(seq, 1)],
        out_shape=[jax.ShapeDtypeStruct((bsz, nh, seq, dh), BF16), jax.ShapeDtypeStruct((bsz, nh, seq, 1), F32)],
        compiler_params=pltpu.CompilerParams(dimension_semantics=("parallel", "parallel"), vmem_limit_bytes=V7X_VMEM_LIMIT),
    )(q, k, v)


def _sb_bwd(q, k, v, do, tot):
    bsz, nh, seq, dh = q.shape
    tq = min(SB_Q_TILE, seq // 2)
    nq, nsub = seq // tq, tq // Q_BLOCK
    scale = dh ** -0.5
    hp = SB_HEADS_PER_STEP
    qb = Q_BLOCK

    def body(q_ref, k_ref, v_ref, do_ref, tot_ref, dq_ref, dk_ref, dv_ref, dka, dva):
        dka[...] = jnp.zeros_like(dka)
        dva[...] = jnp.zeros_like(dva)
        strict = lax.broadcasted_iota(jnp.int32, (qb, qb), 1) < lax.broadcasted_iota(jnp.int32, (qb, qb), 0)
        upto = _tri_and_ones("upto")
        before = _tri_and_ones("before")

        def sweep(qf, dof, tots, c0, first, units, carry):
            heads = sorted({u[0] for u in units})
            kbs = {hh: k_ref[hh, pl.ds(c0, qb), :] for hh in heads}
            vbs = {hh: v_ref[hh, pl.ds(c0, qb), :] for hh in heads}
            sub = lambda a, s: a[s * qb:(s + 1) * qb, :]
            zs = [_dot_nt(sub(qf[hh], s), kbs[hh]) * scale for hh, s, _ in units]
            das = [_dot_nt(sub(dof[hh], s), vbs[hh]) for hh, s, _ in units]
            lkrs = [_neg_softplus(z) for z in zs]
            lks = [jnp.where(strict, lkr, 0.0) if dg else lkr for lkr, (_, _, dg) in zip(lkrs, units)]
            css = [_dot2(lk, upto) for lk in lks]
            lsigs, atts, gls = [], [], []
            for (hh, s, dg), z, lkr, cs, da in zip(units, zs, lkrs, css, das):
                lsig = z + lkr
                att = jnp.exp(lsig + (tots[hh, s] - (cs[:, :qb] + carry[hh, s][1])))
                if dg:
                    att = jnp.where(strict, att, 0.0)
                lsigs.append(lsig)
                atts.append(att)
                gls.append(da * att)
            gss = [_dot2(gl, before) for gl in gls]
            carry = dict(carry)
            dzs = {}
            for (hh, s, dg), lsig, gl, cs, gs in zip(units, lsigs, gls, css, gss):
                dqa, pre, gpre = carry[hh, s]
                sig = jnp.exp(lsig)
                dz = gl * (1.0 - sig) - (gs[:, :qb] + gpre) * sig
                if dg:
                    dz = jnp.where(strict, dz, 0.0)
                dz = (dz * scale).astype(BF16)
                dzs[hh, s] = dz
                carry[hh, s] = (dqa + _dot(dz, kbs[hh]), pre + cs[:, qb:], gpre + gs[:, qb:])
            att_of = {(hh, s): a for (hh, s, _), a in zip(units, atts)}
            for hh in heads:
                dzc = jnp.concatenate([dzs[hh, s] for s in range(first, nsub)], axis=0)
                attc = jnp.concatenate([att_of[hh, s].astype(BF16) for s in range(first, nsub)], axis=0)
                dka[hh, pl.ds(c0, qb), :] += _dot_tn(dzc, qf[hh][first * qb:, :])
                dva[hh, pl.ds(c0, qb), :] += _dot_tn(attc, dof[hh][first * qb:, :])
            return carry

        def q_loop(qi, _):
            r0 = pl.multiple_of(qi * tq, tq)
            qf = [q_ref[hh, pl.ds(r0, tq), :] for hh in range(hp)]
            dof = [do_ref[hh, pl.ds(r0, tq), :] for hh in range(hp)]
            tots = {(hh, s): jnp.broadcast_to(tot_ref[hh, pl.ds(pl.multiple_of(r0 + s * qb, qb), qb), :], (qb, qb))
                    for hh in range(hp) for s in range(nsub)}
            carry = {(hh, s): (jnp.zeros((qb, dh), F32), jnp.zeros((qb, qb), F32), jnp.zeros((qb, qb), F32))
                     for hh in range(hp) for s in range(nsub)}
            units = [(hh, s, False) for hh in range(hp) for s in range(nsub)]
            carry = lax.fori_loop(
                0, nsub * qi, lambda kj, c: sweep(qf, dof, tots, pl.multiple_of(kj * qb, qb), 0, units, c), carry)
            for jj in range(nsub):
                units = [(hh, s, s == jj) for hh in range(hp) for s in range(jj, nsub)]
                carry = sweep(qf, dof, tots, pl.multiple_of(r0 + jj * qb, qb), jj, units, carry)
            for (hh, s), (dqa, _, _) in carry.items():
                dq_ref[hh, pl.ds(pl.multiple_of(r0 + s * qb, qb), qb), :] = dqa.astype(BF16)
            return 0

        lax.fori_loop(0, nq, q_loop, 0)
        dk_ref[...] = dka[...].astype(BF16)
        dv_ref[...] = dva[...].astype(BF16)

    hs = _head_spec(seq, dh)
    return pl.pallas_call(
        body, name="sb_bwd", grid=(bsz, nh // hp),
        in_specs=[hs, hs, hs, hs, _head_spec(seq, 1)],
        out_specs=[hs, hs, hs],
        out_shape=[jax.ShapeDtypeStruct((bsz, nh, seq, dh), BF16)] * 3,
        scratch_shapes=[pltpu.VMEM((hp, seq, dh), F32), pltpu.VMEM((hp, seq, dh), F32)],
        compiler_params=pltpu.CompilerParams(dimension_semantics=("parallel", "parallel"), vmem_limit_bytes=V7X_VMEM_LIMIT),
    )(q, k, v, do, tot)


def _to_heads(a, bsz, seq):
    return a.reshape(bsz, seq, -1, HEAD_DIM).transpose(0, 2, 1, 3)


def _from_heads(a):
    bsz, nh, seq, dh = a.shape
    return a.transpose(0, 2, 1, 3).reshape(bsz * seq, nh * dh)


def _coords():
    return lax.axis_index("x"), lax.axis_index("y"), lax.axis_index("c")


def _all_gather(x, name):
    r, c = x.shape

    def body(x_ref, out_ref, send_sems, recv_sems, local_sem):
        mx, my, mc = _coords()
        me, sibling = (mx, my, mc), (mx, my, 1 - mc)
        chips = [(1 - mx, my), (mx, 1 - my), (1 - mx, 1 - my)]

        def blk(px, py, pc):
            return out_ref.at[4 * px + 2 * py + pc]

        def copy(k, block, to, src=None):
            return pltpu.make_async_remote_copy(
                src_ref=blk(*block) if src is None else src, dst_ref=blk(*block),
                send_sem=send_sems.at[k], recv_sem=recv_sems.at[k], device_id=to, device_id_type=MESH)

        mine = pltpu.make_async_copy(x_ref, blk(*me), local_sem)
        mine.start()
        first = [copy(0, me, sibling, src=x_ref)]
        first += [copy(1 + j, me, (*chip, mc), src=x_ref) for j, chip in enumerate(chips)]
        for cp in first:
            cp.start()
        passed = [copy(4 + j, (*chip, mc), sibling) for j, chip in enumerate(chips)]
        for j, chip in enumerate(chips):
            copy(1 + j, (*chip, mc), me).wait_recv()
            passed[j].start()
        copy(0, sibling, me).wait_recv()
        for j, chip in enumerate(chips):
            copy(4 + j, (*chip, 1 - mc), me).wait_recv()
        for cp in first + passed:
            cp.wait_send()
        mine.wait()

    return pl.pallas_call(
        body, name=name,
        out_shape=jax.ShapeDtypeStruct((N_DEV, r, c), x.dtype),
        in_specs=[pl.BlockSpec(memory_space=pl.ANY)],
        out_specs=pl.BlockSpec(memory_space=pl.ANY),
        scratch_shapes=[pltpu.SemaphoreType.DMA((7,)), pltpu.SemaphoreType.DMA((7,)), pltpu.SemaphoreType.DMA],
    )(x)


_ANY = pl.BlockSpec(memory_space=pl.ANY)


def _window(ref, shard_shape, by_cols, dev, half=None):
    rows = shard_shape[1]
    r0, nr = (0, rows) if half is None else (half * (rows // 2), rows // 2)
    if by_cols:
        n = shard_shape[2]
        return ref.at[:, pl.ds(r0, nr), pl.ds(pl.multiple_of(dev * n, n), n)]
    return ref.at[:, pl.ds(pl.multiple_of(dev * rows + r0, 2 * SUBLANES), nr), :]


_TO_SIBLING, _TO_X, _TO_Y, _X_TO_SIBLING, _Y_TO_SIBLING, _DIAG0_TO_SIBLING, _DIAG1_TO_SIBLING, _X_HALF_TO_Y, _Y_HALF_TO_X = range(9)


def _gather_weights(shards, by_cols):
    na = len(shards)
    fulls = [jax.ShapeDtypeStruct(
        (s.shape[0], s.shape[1], s.shape[2] * N_DEV) if c else (s.shape[0], s.shape[1] * N_DEV, s.shape[2]), s.dtype)
        for s, c in zip(shards, by_cols)]

    def body(*refs):
        x_refs, out_refs = refs[:na], refs[na:2 * na]
        send_sems, recv_sems, local_sems = refs[2 * na:]
        mx, my, mc = _coords()
        me, sibling = (mx, my, mc), (mx, my, 1 - mc)
        xn, yn, diag = (1 - mx, my, mc), (mx, 1 - my, mc), (1 - mx, 1 - my, mc)

        def blk(a, dev, half=None):
            px, py, pc = dev
            return _window(out_refs[a], x_refs[a].shape, by_cols[a], 4 * px + 2 * py + pc, half)

        def copy(a, k, block, to, half=None, src=None):
            return pltpu.make_async_remote_copy(
                src_ref=blk(a, block, half) if src is None else src, dst_ref=blk(a, block, half),
                send_sem=send_sems.at[a, k], recv_sem=recv_sems.at[a, k], device_id=to, device_id_type=MESH)

        def other_core(dev):
            return dev[0], dev[1], 1 - dev[2]

        sends = []
        for a in range(na):
            mine = pltpu.make_async_copy(x_refs[a], blk(a, me), local_sems.at[a])
            mine.start()
            first = [copy(a, _TO_SIBLING, me, sibling, src=x_refs[a]), copy(a, _TO_X, me, xn, src=x_refs[a]),
                     copy(a, _TO_Y, me, yn, src=x_refs[a])]
            for cp in first:
                cp.start()
            sends += first
        for a in range(na):
            copy(a, _TO_X, xn, me).wait_recv()
            copy(a, _TO_Y, yn, me).wait_recv()
            passed = [copy(a, _X_TO_SIBLING, xn, sibling), copy(a, _X_HALF_TO_Y, xn, yn, half=0),
                      copy(a, _Y_TO_SIBLING, yn, sibling), copy(a, _Y_HALF_TO_X, yn, xn, half=1)]
            for cp in passed:
                cp.start()
            sends += passed
        for a in range(na):
            copy(a, _X_HALF_TO_Y, diag, me, half=0).wait_recv()
            copy(a, _Y_HALF_TO_X, diag, me, half=1).wait_recv()
            passed = [copy(a, _DIAG0_TO_SIBLING, diag, sibling, half=0), copy(a, _DIAG1_TO_SIBLING, diag, sibling, half=1)]
            for cp in passed:
                cp.start()
            sends += passed
        for a in range(na):
            copy(a, _TO_SIBLING, sibling, me).wait_recv()
            copy(a, _X_TO_SIBLING, other_core(xn), me).wait_recv()
            copy(a, _Y_TO_SIBLING, other_core(yn), me).wait_recv()
            copy(a, _DIAG0_TO_SIBLING, other_core(diag), me, half=0).wait_recv()
            copy(a, _DIAG1_TO_SIBLING, other_core(diag), me, half=1).wait_recv()
        for cp in sends:
            cp.wait_send()
        for a in range(na):
            pltpu.make_async_copy(x_refs[a], blk(a, me), local_sems.at[a]).wait()

    nsem = _Y_HALF_TO_X + 1
    return pl.pallas_call(
        body, name="gather_weights", out_shape=fulls, in_specs=[_ANY] * na, out_specs=[_ANY] * na,
        scratch_shapes=[pltpu.SemaphoreType.DMA((na, nsem)), pltpu.SemaphoreType.DMA((na, nsem)), pltpu.SemaphoreType.DMA((na,))],
    )(*shards)


def _rs_sibling(grads, shard_shapes, by_cols):
    na = len(grads)
    slabs = [jax.ShapeDtypeStruct((N_CHIP,) + tuple(s), F32) for s in shard_shapes]

    def body(*refs):
        g_refs, got_refs = refs[:na], refs[na:2 * na]
        send_sems, recv_sems = refs[2 * na:]
        mx, my, mc = _coords()
        copies = []
        for a in range(na):
            for q in range(N_CHIP):
                dev = 2 * q + 1 - mc
                theirs = g_refs[a].at[dev] if by_cols[a] else _window(g_refs[a], shard_shapes[a], False, dev)
                copies.append(pltpu.make_async_remote_copy(
                    src_ref=theirs, dst_ref=got_refs[a].at[q], send_sem=send_sems.at[a, q], recv_sem=recv_sems.at[a, q],
                    device_id=(mx, my, 1 - mc), device_id_type=MESH))
        for cp in copies:
            cp.start()
        for cp in copies:
            cp.wait()

    sems = pltpu.SemaphoreType.DMA((na, N_CHIP))
    return pl.pallas_call(
        body, name="rs_sibling", out_shape=slabs, in_specs=[_ANY] * na, out_specs=[_ANY] * na,
        scratch_shapes=[sems, sems],
    )(*grads)


def _rs_add(g, got, core, by_cols, cols, name):
    nq, nl, r, c = got.shape
    tr = _tile(r, ELEMENTWISE_ROW_TILE, 2 * SUBLANES)

    def body(core_ref, a_ref, b_ref, o_ref):
        o_ref[...] = (a_ref[...] + b_ref[...])[:, :cols].astype(BF16)

    if by_cols:
        mine = pl.BlockSpec((None, None, tr, c), lambda q, l, i, core_ref: (2 * q + core_ref[0], l, i, 0))
    else:
        mine = pl.BlockSpec((None, tr, c), lambda q, l, i, core_ref: (l, (2 * q + core_ref[0]) * (r // tr) + i, 0))
    spec = pl.BlockSpec((None, None, tr, c), lambda q, l, i, core_ref: (q, l, i, 0))
    return pl.pallas_call(
        body, name=name, out_shape=jax.ShapeDtypeStruct((nq, nl, r, cols), BF16),
        grid_spec=pltpu.PrefetchScalarGridSpec(
            num_scalar_prefetch=1, grid=(nq, nl, r // tr), in_specs=[mine, spec],
            out_specs=pl.BlockSpec((None, None, tr, cols), lambda q, l, i, core_ref: (q, l, i, 0))),
        compiler_params=pltpu.CompilerParams(dimension_semantics=("parallel",) * 3),
    )(core, g, got)


def _rs_chips(parts):
    na = len(parts)

    def body(*refs):
        p_refs, out_refs = refs[:na], refs[na:2 * na]
        send_sems, recv_sems, local_sems = refs[2 * na:]
        mx, my, mc = _coords()
        here = 2 * mx + my
        chips = [(1 - mx, my), (mx, 1 - my), (1 - mx, 1 - my)]
        copies = []
        for a in range(na):
            copies.append(pltpu.make_async_copy(p_refs[a].at[here], out_refs[a].at[here], local_sems.at[a]))
            for j, (cx, cy) in enumerate(chips):
                copies.append(pltpu.make_async_remote_copy(
                    src_ref=p_refs[a].at[2 * cx + cy], dst_ref=out_refs[a].at[here],
                    send_sem=send_sems.at[a, j], recv_sem=recv_sems.at[a, j], device_id=(cx, cy, mc), device_id_type=MESH))
        for cp in copies:
            cp.start()
        for cp in copies:
            cp.wait()

    return pl.pallas_call(
        body, name="rs_chips", out_shape=[jax.ShapeDtypeStruct(p.shape, p.dtype) for p in parts],
        in_specs=[_ANY] * na, out_specs=[_ANY] * na,
        scratch_shapes=[pltpu.SemaphoreType.DMA((na, 3)), pltpu.SemaphoreType.DMA((na, 3)), pltpu.SemaphoreType.DMA((na,))],
    )(*parts)


def _adamw(parts, w, m, v, name):
    n, r, c = parts.shape
    tr = _tile(r, 320, 2 * SUBLANES)
    bc1 = 1.0 - ADAM_B1 ** ADAM_STEP
    bc2 = 1.0 - ADAM_B2 ** ADAM_STEP

    def body(p_ref, w_ref, m_ref, v_ref, g_ref, d_ref, mo_ref, vo_ref):
        g = p_ref[0].astype(F32)
        for j in range(1, n):
            g = g + p_ref[j].astype(F32)
        mn = ADAM_B1 * m_ref[...] + (1.0 - ADAM_B1) * g
        vn = ADAM_B2 * v_ref[...] + (1.0 - ADAM_B2) * (g * g)
        g_ref[...] = g
        mo_ref[...] = mn
        vo_ref[...] = vn
        d_ref[...] = -ADAM_LR * ((mn / bc1) / (jnp.sqrt(vn / bc2) + ADAM_EPS) + ADAM_WD * w_ref[...])

    flat = pl.BlockSpec((tr, c), lambda i: (i, 0))
    return pl.pallas_call(
        body, name=name, grid=(r // tr,),
        in_specs=[pl.BlockSpec((n, tr, c), lambda i: (0, i, 0)), flat, flat, flat],
        out_specs=[flat] * 4, out_shape=[jax.ShapeDtypeStruct((r, c), F32)] * 4,
        compiler_params=pltpu.CompilerParams(dimension_semantics=("parallel",)),
    )(parts, w, m, v)


def _adam_math(g, w, m, v):
    bc1 = 1.0 - ADAM_B1 ** ADAM_STEP
    bc2 = 1.0 - ADAM_B2 ** ADAM_STEP
    mn = ADAM_B1 * m + (1.0 - ADAM_B1) * g
    vn = ADAM_B2 * v + (1.0 - ADAM_B2) * (g * g)
    return -ADAM_LR * ((mn / bc1) / (jnp.sqrt(vn / bc2) + ADAM_EPS) + ADAM_WD * w), mn, vn


def _adamw_shard(parts, w, m, v, name):
    nl, r, c = w.shape
    tr = _tile(r, ELEMENTWISE_ROW_TILE, 2 * SUBLANES)

    def body(p_ref, w_ref, m_ref, v_ref, g_ref, d_ref, mo_ref, vo_ref):
        g = p_ref[0].astype(F32)
        for q in range(1, N_CHIP):
            g = g + p_ref[q].astype(F32)
        g = g[:, :c]
        g_ref[...] = g
        d_ref[...], mo_ref[...], vo_ref[...] = _adam_math(g, w_ref[...], m_ref[...], v_ref[...])

    native = pl.BlockSpec((None, tr, c), lambda l, i: (l, i, 0))
    return pl.pallas_call(
        body, name=name, grid=(nl, r // tr),
        in_specs=[pl.BlockSpec((N_CHIP, None, tr, parts.shape[3]), lambda l, i: (0, l, i, 0)), native, native, native],
        out_specs=[native] * 4, out_shape=[jax.ShapeDtypeStruct(w.shape, F32)] * 4,
        compiler_params=pltpu.CompilerParams(dimension_semantics=("parallel", "parallel"), vmem_limit_bytes=V7X_VMEM_LIMIT),
    )(parts, w, m, v)


def _pack(arrs, dtype=F32):
    cols = []
    for a in arrs:
        f = a.reshape(-1).astype(dtype)
        cols.append(jnp.pad(f, (0, -f.shape[0] % FLAT_W)))
    flat = jnp.concatenate(cols)
    flat = jnp.pad(flat, (0, -flat.shape[0] % (FLAT_W * SUBLANES)))
    return flat.reshape(-1, FLAT_W)


def _unpack(flat, shapes, lead=()):
    flat = flat.reshape(lead + (-1,))
    out, off = [], 0
    for s in shapes:
        n = math.prod(s)
        out.append(flat[..., off:off + n].reshape(lead + tuple(s)))
        off += n + (-n % FLAT_W)
    return out


def kernel(x, p, ffn1_norm, ffn1_w1, ffn1_w3, ffn1_w2, mix_norm, ffn2_norm, ffn2_w1, ffn2_w3, ffn2_w2, ple_norm, ple_proj, ple_gate, s5_w_in, s5_a_re, s5_a_im, s5_log_dt, s5_b_re, s5_b_im, s5_c_re, s5_c_im, s5_d, s5_w_glu, sb_w_qkv, sb_w_o, final_norm, loss_target, m_ffn1_norm, m_ffn1_w1, m_ffn1_w3, m_ffn1_w2, m_mix_norm, m_ffn2_norm, m_ffn2_w1, m_ffn2_w3, m_ffn2_w2, m_ple_norm, m_ple_proj, m_ple_gate, m_s5_w_in, m_s5_a_re, m_s5_a_im, m_s5_log_dt, m_s5_b_re, m_s5_b_im, m_s5_c_re, m_s5_c_im, m_s5_d, m_s5_w_glu, m_sb_w_qkv, m_sb_w_o, m_final_norm, v_ffn1_norm, v_ffn1_w1, v_ffn1_w3, v_ffn1_w2, v_mix_norm, v_ffn2_norm, v_ffn2_w1, v_ffn2_w3, v_ffn2_w2, v_ple_norm, v_ple_proj, v_ple_gate, v_s5_w_in, v_s5_a_re, v_s5_a_im, v_s5_log_dt, v_s5_b_re, v_s5_b_im, v_s5_c_re, v_s5_c_im, v_s5_d, v_s5_w_glu, v_sb_w_qkv, v_sb_w_o, v_final_norm):
    given = dict(locals())
    wts = {n: given[n] for n in WEIGHTS}
    mom = {n: given["m_" + n] for n in WEIGHTS}
    var = {n: given["v_" + n] for n in WEIGHTS}
    bl, seq, d = x.shape
    t = bl * seq
    depth = p.shape[0]

    hid_pad = -wts["ffn1_w1"].shape[-1] % LANES

    def padded(n):
        a = wts[n].astype(BF16)
        if n in FFN_COL:
            return jnp.pad(a, ((0, 0), (0, 0), (0, hid_pad)))
        if n in FFN_ROW:
            return jnp.pad(a, ((0, 0), (0, hid_pad), (0, 0)))
        return a

    sent = [padded(n) for n in SHARDED]
    shard_shape = {n: a.shape for n, a in zip(SHARDED, sent)}
    by_cols = [n in COL_SHARDED for n in SHARDED]
    full = dict(zip(SHARDED, _gather_weights(sent, by_cols)))

    def row(a):
        return a.reshape(1, -1)

    def wt(n, layer=0, col=0, width=None):
        return _W(full[n], layer, col, width)

    gbuf = {}

    def wgrad(n, x_act, dy, layer=0, scale=1.0):
        like = (N_DEV,) + shard_shape[n] if n in COL_SHARDED else full[n].shape
        gbuf[n] = _wgrad(x_act, dy, f"{n}_grad", like, gbuf.get(n), layer, scale)

    n_groups = d // S5_GROUP
    a_re, a_im = s5_a_re[0], s5_a_im[0]
    log_dt = s5_log_dt[0].reshape(n_groups, 1)
    disc = _s5_disc(a_re, a_im, log_dt)
    nb = n_groups // S5_BLOCK_GROUPS
    ns = S5_BLOCK_GROUPS * S5_STATE
    par = jnp.concatenate([jnp.stack([q.reshape(nb, ns) for q in disc], axis=1), jnp.zeros((nb, 4, ns), F32)], axis=1)
    bre = _blockdiag(s5_b_re[0].transpose(0, 2, 1), S5_BLOCK_GROUPS).astype(BF16)
    bim = _blockdiag(s5_b_im[0].transpose(0, 2, 1), S5_BLOCK_GROUPS).astype(BF16)
    cre = _blockdiag(s5_c_re[0].transpose(0, 2, 1), S5_BLOCK_GROUPS).astype(BF16)
    cim = _blockdiag(s5_c_im[0].transpose(0, 2, 1), S5_BLOCK_GROUPS).astype(BF16)
    lc = min(SCAN_CHUNK, seq)

    h = x.reshape(t, d)
    saved = []
    for i in range(depth):
        s = {"h0": h}
        h, s["n1"], s["a1"], s["b1"] = _ffn_fwd(h, row(ffn1_norm[i]), wt("ffn1_w1", i), wt("ffn1_w3", i), wt("ffn1_w2", i))
        s["h1"] = h
        j = i // 2
        if i % 2 == 0:
            s["hn"], u = _norm_lin(h, row(mix_norm[i]), [wt("s5_w_in", j)], [F32], "s5_in")
            s["u"] = _interleave(u, lc)
            y, z, s["st"] = _s5_scan_fwd(s["u"], bre, bim, cre, cim, par, row(s5_d[j]), bl, seq, lc)
            s["y"], s["z"] = _deinterleave(y, lc), _deinterleave(z, lc)
            h = _s5_out(h, s["z"], wt("s5_w_glu", j))
        else:
            wqkv = [wt("sb_w_qkv", j, col, d) for col in range(3)]
            s["hn"], q, k, v = _norm_lin(h, row(mix_norm[i]), wqkv, [BF16] * 3, "sb_in")
            s["q"], s["k"], s["v"] = (_to_heads(a, bl, seq) for a in (q, k, v))
            o, s["tot"] = _sb_fwd(s["q"], s["k"], s["v"])
            s["o"] = _from_heads(o)
            h = _lin_res(h, s["o"], wt("sb_w_o", j), "sb_out")
        s["h2"] = h
        h, s["n2"], s["a2"], s["b2"] = _ffn_fwd(h, row(ffn2_norm[i]), wt("ffn2_w1", i), wt("ffn2_w3", i), wt("ffn2_w2", i))
        s["h3"] = h
        s["p"] = p[i].reshape(t, -1)
        h, s["npl"] = _ple_fwd(h, row(ple_norm[i]), wt("ple_gate", i), s["p"], wt("ple_proj", i))
        saved.append(s)

    loss_part, dh, g_final = _loss_head(h, row(final_norm), loss_target.reshape(t, d))
    loss = lax.psum(loss_part[0, 0], ("x", "y", "c"))

    grads = {n: [None] * wts[n].shape[0] for n in REPLICATED if n != "final_norm"}
    grads["final_norm"] = g_final.reshape(-1)

    def ffn_bwd(dh, h_in, n, a, b, which, i):
        da, db, sact = _ffn_bwd_down(dh, a, b, wt(f"{which}_w2", i))
        wgrad(f"{which}_w2", sact, dh, i, scale=0.5)
        dh_in, dg = _lin_bwd_norm([da, db], [wt(f"{which}_w1", i), wt(f"{which}_w3", i)], h_in,
                                  row(wts[f"{which}_norm"][i]), dh, f"{which}_bwd_up")
        wgrad(f"{which}_w1", n, da, i)
        wgrad(f"{which}_w3", n, db, i)
        grads[f"{which}_norm"][i] = dg.reshape(-1)
        return dh_in

    for i in reversed(range(depth)):
        s = saved[i]
        j = i // 2
        dh, dgl, dpp, dg = _ple_bwd(dh, s["h3"], row(ple_norm[i]), s["npl"], s["p"], wt("ple_gate", i), wt("ple_proj", i))
        grads["ple_norm"][i] = dg.reshape(-1)
        wgrad("ple_gate", s["npl"], dgl, i)
        wgrad("ple_proj", s["p"], dpp, i)
        dh = ffn_bwd(dh, s["h2"], s["n2"], s["a2"], s["b2"], "ffn2", i)
        if i % 2 == 0:
            dy, dzz = _s5_out_bwd(dh, s["z"], s["y"], wt("s5_w_glu", j))
            wgrad("s5_w_glu", s["z"], dzz, j)
            du, dbre, dbim, dcre, dcim, dpar, dd = _s5_scan_bwd(
                s["u"], _interleave(dy, lc), s["st"], bre, bim, cre, cim, par, row(s5_d[j]), bl, seq, lc)
            du = _deinterleave(du, lc)
            cts = [dpar[:, r, :].reshape(n_groups, S5_STATE) for r in range(4)]
            g_are, g_aim, g_ldt = _s5_disc_bwd(a_re, a_im, log_dt, cts)
            grads["s5_a_re"][j], grads["s5_a_im"][j], grads["s5_log_dt"][j] = g_are, g_aim, g_ldt.reshape(-1)
            grads["s5_b_re"][j] = _blockdiag_take(dbre, S5_BLOCK_GROUPS, S5_GROUP, S5_STATE).transpose(0, 2, 1)
            grads["s5_b_im"][j] = _blockdiag_take(dbim, S5_BLOCK_GROUPS, S5_GROUP, S5_STATE).transpose(0, 2, 1)
            grads["s5_c_re"][j] = _blockdiag_take(dcre, S5_BLOCK_GROUPS, S5_STATE, S5_GROUP).transpose(0, 2, 1)
            grads["s5_c_im"][j] = _blockdiag_take(dcim, S5_BLOCK_GROUPS, S5_STATE, S5_GROUP).transpose(0, 2, 1)
            grads["s5_d"][j] = dd.reshape(-1)
            dh, dg = _lin_bwd_norm([du], [wt("s5_w_in", j)], s["h1"], row(mix_norm[i]), dh, "s5_in_bwd")
            wgrad("s5_w_in", s["hn"], du, j)
        else:
            do = _lin_nt(dh, wt("sb_w_o", j), "sb_out_bwd")
            wgrad("sb_w_o", s["o"], dh, j)
            dqkv = _sb_bwd(s["q"], s["k"], s["v"], _to_heads(do, bl, seq), s["tot"])
            dqkv = [_from_heads(a) for a in dqkv]
            dh, dg = _lin_bwd_norm(dqkv, [wt("sb_w_qkv", j, col, d) for col in range(3)], s["h1"], row(mix_norm[i]), dh, "sb_in_bwd")
            wgrad("sb_w_qkv", s["hn"], jnp.concatenate(dqkv, axis=1), j)
        grads["mix_norm"][i] = dg.reshape(-1)
        dh = ffn_bwd(dh, s["h0"], s["n1"], s["a1"], s["b1"], "ffn1", i)
    grad_x = dh.reshape(x.shape)
    for n in REPLICATED:
        if n != "final_norm":
            grads[n] = jnp.stack(grads[n])

    got = _rs_sibling([gbuf[n] for n in SHARDED], [shard_shape[n] for n in SHARDED], by_cols)
    core = lax.axis_index("c").astype(jnp.int32).reshape(1)
    parts = _rs_chips([_rs_add(gbuf[n], g, core, n in COL_SHARDED, wts[n].shape[2], f"rs_add_{n}")
                       for n, g in zip(SHARDED, got)])
    res = {n: _adamw_shard(part, wts[n], mom[n], var[n], f"adamw_{n}") for n, part in zip(SHARDED, parts)}

    rep_shapes = [wts[n].shape for n in REPLICATED]
    rep_parts = _all_gather(_pack([grads[n].reshape(wts[n].shape) for n in REPLICATED]), "gather_small_grads")
    outs = _adamw(rep_parts, _pack([wts[n] for n in REPLICATED]), _pack([mom[n] for n in REPLICATED]),
                  _pack([var[n] for n in REPLICATED]), "adamw_replicated")
    res.update({n: vals for n, vals in zip(REPLICATED, zip(*[_unpack(o, rep_shapes) for o in outs]))})

    return (loss, grad_x, *[res[n][0] for n in WEIGHTS], *[res[n][1] for n in WEIGHTS],
            *[res[n][2] for n in WEIGHTS], *[res[n][3] for n in WEIGHTS])
```

```python
import math
from typing import NamedTuple, Optional

import jax
import jax.numpy as jnp
from jax import lax
from jax.experimental import pallas as pl
from jax.experimental.pallas import tpu as pltpu

F32 = jnp.float32
BF16 = jnp.bfloat16
MESH = pl.DeviceIdType.MESH

N_DEV = 8
N_CHIP = 4
EPS = 1e-6
S5_GROUP = 16
S5_STATE = 64
S5_BLOCK_GROUPS = 16
HEAD_DIM = 64
Q_BLOCK = 128
SB_Q_TILE = 256
SB_HEADS_PER_STEP = 4
SCAN_CHUNK = 512
SUBLANES = 8
LANES = 128
FLAT_W = 1024
ADAM_LR, ADAM_B1, ADAM_B2, ADAM_EPS, ADAM_WD, ADAM_STEP = 1e-3, 0.9, 0.999, 1e-8, 0.01, 10
V7X_VMEM_LIMIT = 56 * 1024 * 1024
ROW_TILE = 256
WGRAD_ROW_TILE = 1024
WGRAD_OUT_BYTES = 6 * 1024 * 1024
ELEMENTWISE_ROW_TILE = 512

SHARDED = ("ffn1_w1", "ffn1_w3", "ffn1_w2", "ffn2_w1", "ffn2_w3", "ffn2_w2", "ple_proj", "ple_gate",
           "s5_w_in", "s5_w_glu", "sb_w_qkv", "sb_w_o")
COL_SHARDED = ("ffn1_w1", "ffn1_w3", "ffn2_w1", "ffn2_w3", "ple_proj", "s5_w_glu", "sb_w_qkv")
FFN_COL = ("ffn1_w1", "ffn1_w3", "ffn2_w1", "ffn2_w3")
FFN_ROW = ("ffn1_w2", "ffn2_w2")
REPLICATED = ("ffn1_norm", "mix_norm", "ffn2_norm", "ple_norm", "s5_a_re", "s5_a_im", "s5_log_dt",
              "s5_b_re", "s5_b_im", "s5_c_re", "s5_c_im", "s5_d", "final_norm")
WEIGHTS = ("ffn1_norm", "ffn1_w1", "ffn1_w3", "ffn1_w2", "mix_norm", "ffn2_norm", "ffn2_w1", "ffn2_w3", "ffn2_w2",
           "ple_norm", "ple_proj", "ple_gate", "s5_w_in", "s5_a_re", "s5_a_im", "s5_log_dt", "s5_b_re", "s5_b_im",
           "s5_c_re", "s5_c_im", "s5_d", "s5_w_glu", "sb_w_qkv", "sb_w_o", "final_norm")


def _dot(a, b):
    return jnp.dot(a, b, preferred_element_type=F32)


def _dot_nt(a, b):
    return lax.dot_general(a, b, (((1,), (1,)), ((), ())), preferred_element_type=F32)


def _dot_tn(a, b):
    return lax.dot_general(a, b, (((0,), (0,)), ((), ())), preferred_element_type=F32)


def _dot2(x, m):
    hi = x.astype(BF16)
    lo = (x - hi.astype(F32)).astype(BF16)
    return _dot(hi, m) + _dot(lo, m)


def _rms(x):
    r = lax.rsqrt(jnp.mean(x * x, axis=-1, keepdims=True) + EPS)
    return x * r, r


def _rms_bwd(dn, xh, r, g):
    gy = dn * g
    return r * (gy - xh * jnp.mean(gy * xh, axis=-1, keepdims=True))


_GELU_C = math.sqrt(2.0 / math.pi)


def _gelu(x):
    return 0.5 * x * (1.0 + jnp.tanh(_GELU_C * (x + 0.044715 * x * x * x)))


def _gelu_grad(x):
    th = jnp.tanh(_GELU_C * (x + 0.044715 * x * x * x))
    return 0.5 * (1.0 + th) + 0.5 * x * (1.0 - th * th) * _GELU_C * (1.0 + 3.0 * 0.044715 * x * x)


def _neg_softplus(z):
    return -(jnp.maximum(z, 0.0) + jnp.log(1.0 + jnp.exp(-jnp.abs(z))))


def _tile(n, want, mult=SUBLANES):
    for t in range(min(want, n), 0, -1):
        if n % t == 0 and t % mult == 0:
            return t
    return n


def _rows(tm, c):
    return pl.BlockSpec((tm, c), lambda i: (i, 0))


def _whole(a):
    nd = a.ndim
    return pl.BlockSpec(a.shape, lambda i: (0,) * nd)


class _W(NamedTuple):
    arr: jax.Array
    layer: int = 0
    col: int = 0
    width: Optional[int] = None

    @property
    def shape(self):
        return self.arr.shape[1], self.width or self.arr.shape[2]


def _wspec(w):
    return pl.BlockSpec((None,) + w.shape, lambda *_: (w.layer, 0, w.col))


def _row_params(sem="parallel"):
    return pltpu.CompilerParams(dimension_semantics=(sem,), vmem_limit_bytes=V7X_VMEM_LIMIT)


def _ffn_fwd(h, g, w1, w3, w2):
    t, d = h.shape
    f = w1.shape[1]
    tm = _tile(t, ROW_TILE)

    def body(h_ref, g_ref, w1_ref, w3_ref, w2_ref, ho_ref, n_ref, a_ref, b_ref):
        x = h_ref[...]
        xh, _ = _rms(x)
        n = (xh * g_ref[...]).astype(BF16)
        a = _dot(n, w1_ref[...])
        b = _dot(n, w3_ref[...])
        s = (a * jax.nn.sigmoid(a) * b).astype(BF16)
        ho_ref[...] = x + 0.5 * _dot(s, w2_ref[...])
        n_ref[...] = n
        a_ref[...] = a.astype(BF16)
        b_ref[...] = b.astype(BF16)

    return pl.pallas_call(
        body, name="ffn_fwd", grid=(t // tm,),
        in_specs=[_rows(tm, d), _whole(g), _wspec(w1), _wspec(w3), _wspec(w2)],
        out_specs=[_rows(tm, d), _rows(tm, d), _rows(tm, f), _rows(tm, f)],
        out_shape=[jax.ShapeDtypeStruct((t, d), F32), jax.ShapeDtypeStruct((t, d), BF16),
                   jax.ShapeDtypeStruct((t, f), BF16), jax.ShapeDtypeStruct((t, f), BF16)],
        compiler_params=_row_params(),
    )(h, g, w1.arr, w3.arr, w2.arr)


def _ffn_bwd_down(dh, a, b, w2):
    t, d = dh.shape
    f = a.shape[1]
    tm = _tile(t, ROW_TILE)

    def body(dh_ref, a_ref, b_ref, w2_ref, da_ref, db_ref, s_ref):
        ds = _dot_nt((0.5 * dh_ref[...]).astype(BF16), w2_ref[...])
        a32 = a_ref[...].astype(F32)
        b32 = b_ref[...].astype(F32)
        sig = jax.nn.sigmoid(a32)
        sil = a32 * sig
        da_ref[...] = (ds * b32 * (sig * (1.0 + a32 * (1.0 - sig)))).astype(BF16)
        db_ref[...] = (ds * sil).astype(BF16)
        s_ref[...] = (sil * b32).astype(BF16)

    return pl.pallas_call(
        body, name="ffn_bwd_down", grid=(t // tm,),
        in_specs=[_rows(tm, d), _rows(tm, f), _rows(tm, f), _wspec(w2)],
        out_specs=[_rows(tm, f)] * 3,
        out_shape=[jax.ShapeDtypeStruct((t, f), BF16)] * 3,
        compiler_params=_row_params(),
    )(dh, a, b, w2.arr)


def _lin_bwd_norm(dys, ws, h, g, dh, name):
    t, d = h.shape
    tm = _tile(t, ROW_TILE)
    k = len(dys)

    def body(*refs):
        dy_refs, w_refs = refs[:k], refs[k:2 * k]
        h_ref, g_ref, dh_ref, o_ref, dg_ref = refs[2 * k:]
        dn = _dot_nt(dy_refs[0][...].astype(BF16), w_refs[0][...])
        for j in range(1, k):
            dn = dn + _dot_nt(dy_refs[j][...].astype(BF16), w_refs[j][...])
        xh, r = _rms(h_ref[...])
        o_ref[...] = dh_ref[...] + _rms_bwd(dn, xh, r, g_ref[...])

        @pl.when(pl.program_id(0) == 0)
        def _():
            dg_ref[...] = jnp.zeros_like(dg_ref)

        dg_ref[...] += jnp.sum(dn * xh, axis=0, keepdims=True)

    return pl.pallas_call(
        body, name=name, grid=(t // tm,),
        in_specs=[_rows(tm, dy.shape[1]) for dy in dys] + [_wspec(w) for w in ws] + [_rows(tm, d), _whole(g), _rows(tm, d)],
        out_specs=[_rows(tm, d), pl.BlockSpec((1, d), lambda i: (0, 0))],
        out_shape=[jax.ShapeDtypeStruct((t, d), F32), jax.ShapeDtypeStruct((1, d), F32)],
        compiler_params=_row_params("arbitrary"),
    )(*dys, *[w.arr for w in ws], h, g, dh)


def _wgrad(x, dy, name, like, into=None, layer=0, scale=1.0):
    t, kk = x.shape
    n = dy.shape[1]
    tm = _tile(t, WGRAD_ROW_TILE)
    steps = t // tm
    blocked = len(like) == 4
    if blocked:
        sc = like[3]
        nd = max(g for g in (1, 2, 4, 8) if g * sc * kk * 4 <= max(WGRAD_OUT_BYTES, sc * kk * 4))
        tn = nd * sc
        out_spec = pl.BlockSpec((nd, None, kk, sc), lambda j, i: (j, layer, 0, 0))
    else:
        tn = _tile(n, max(128, WGRAD_OUT_BYTES // (4 * kk)), 128)
        out_spec = pl.BlockSpec((None, kk, tn), lambda j, i: (layer, 0, j))

    def body(x_ref, dy_ref, *rest):
        o_ref = rest[-1]
        i = pl.program_id(1)

        @pl.when(i == 0)
        def _():
            o_ref[...] = jnp.zeros_like(o_ref)

        acc = _dot_tn(x_ref[...].astype(BF16), dy_ref[...].astype(BF16))
        if blocked:
            for dd in range(nd):
                o_ref[dd] += acc[:, dd * sc:(dd + 1) * sc]
        else:
            o_ref[...] += acc
        if scale != 1.0:
            @pl.when(i == steps - 1)
            def _():
                o_ref[...] = o_ref[...] * scale

    held = [] if into is None else [into]
    return pl.pallas_call(
        body, name=name, grid=(n // tn, steps),
        in_specs=[pl.BlockSpec((tm, kk), lambda j, i: (i, 0)), pl.BlockSpec((tm, tn), lambda j, i: (i, j))]
        + [pl.BlockSpec(memory_space=pl.ANY)] * len(held),
        out_specs=out_spec,
        out_shape=jax.ShapeDtypeStruct(like, F32),
        input_output_aliases={2: 0} if held else {},
        compiler_params=pltpu.CompilerParams(dimension_semantics=("parallel", "arbitrary"), vmem_limit_bytes=V7X_VMEM_LIMIT),
    )(x, dy, *held)


def _norm_lin(h, g, ws, out_dtypes, name):
    t, d = h.shape
    tm = _tile(t, ROW_TILE)
    k = len(ws)

    def body(*refs):
        h_ref, g_ref = refs[:2]
        w_refs = refs[2:2 + k]
        n_ref = refs[2 + k]
        o_refs = refs[3 + k:]
        xh, _ = _rms(h_ref[...])
        n = (xh * g_ref[...]).astype(BF16)
        n_ref[...] = n
        for w_ref, o_ref in zip(w_refs, o_refs):
            o_ref[...] = _dot(n, w_ref[...]).astype(o_ref.dtype)

    return pl.pallas_call(
        body, name=name, grid=(t // tm,),
        in_specs=[_rows(tm, d), _whole(g)] + [_wspec(w) for w in ws],
        out_specs=[_rows(tm, d)] + [_rows(tm, w.shape[1]) for w in ws],
        out_shape=[jax.ShapeDtypeStruct((t, d), BF16)] + [jax.ShapeDtypeStruct((t, w.shape[1]), dt) for w, dt in zip(ws, out_dtypes)],
        compiler_params=_row_params(),
    )(h, g, *[w.arr for w in ws])


def _lin_res(h, x, w, name):
    t, d = h.shape
    tm = _tile(t, ROW_TILE)

    def body(h_ref, x_ref, w_ref, o_ref):
        o_ref[...] = h_ref[...] + _dot(x_ref[...], w_ref[...])

    return pl.pallas_call(
        body, name=name, grid=(t // tm,),
        in_specs=[_rows(tm, d), _rows(tm, x.shape[1]), _wspec(w)],
        out_specs=_rows(tm, d), out_shape=jax.ShapeDtypeStruct((t, d), F32),
        compiler_params=_row_params(),
    )(h, x, w.arr)


def _lin_nt(dy, w, name):
    t = dy.shape[0]
    kk = w.shape[0]
    tm = _tile(t, ROW_TILE)

    def body(dy_ref, w_ref, o_ref):
        o_ref[...] = _dot_nt(dy_ref[...].astype(BF16), w_ref[...]).astype(BF16)

    return pl.pallas_call(
        body, name=name, grid=(t // tm,),
        in_specs=[_rows(tm, dy.shape[1]), _wspec(w)],
        out_specs=_rows(tm, kk), out_shape=jax.ShapeDtypeStruct((t, kk), BF16),
        compiler_params=_row_params(),
    )(dy, w.arr)


def _ple_fwd(h, g, wg, p, wp):
    t, d = h.shape
    tm = _tile(t, ROW_TILE)

    def body(h_ref, g_ref, wg_ref, p_ref, wp_ref, o_ref, n_ref):
        x = h_ref[...]
        xh, _ = _rms(x)
        n = (xh * g_ref[...]).astype(BF16)
        n_ref[...] = n
        gate = jax.nn.sigmoid(_dot(n, wg_ref[...]))
        o_ref[...] = x + _dot(p_ref[...].astype(BF16), wp_ref[...]) * gate

    return pl.pallas_call(
        body, name="ple_fwd", grid=(t // tm,),
        in_specs=[_rows(tm, d), _whole(g), _wspec(wg), _rows(tm, p.shape[1]), _wspec(wp)],
        out_specs=[_rows(tm, d), _rows(tm, d)],
        out_shape=[jax.ShapeDtypeStruct((t, d), F32), jax.ShapeDtypeStruct((t, d), BF16)],
        compiler_params=_row_params(),
    )(h, g, wg.arr, p, wp.arr)


def _ple_bwd(dh, h, g, n, p, wg, wp):
    t, d = h.shape
    tm = _tile(t, ROW_TILE)

    def body(dh_ref, h_ref, g_ref, n_ref, p_ref, wg_ref, wp_ref, o_ref, dgl_ref, dpp_ref, dg_ref):
        dh_v = dh_ref[...]
        gate = jax.nn.sigmoid(_dot(n_ref[...], wg_ref[...]))
        pp = _dot(p_ref[...].astype(BF16), wp_ref[...])
        dgl = (dh_v * pp * gate * (1.0 - gate)).astype(BF16)
        dgl_ref[...] = dgl
        dpp_ref[...] = (dh_v * gate).astype(BF16)
        dn = _dot_nt(dgl, wg_ref[...])
        xh, r = _rms(h_ref[...])
        o_ref[...] = dh_v + _rms_bwd(dn, xh, r, g_ref[...])

        @pl.when(pl.program_id(0) == 0)
        def _():
            dg_ref[...] = jnp.zeros_like(dg_ref)

        dg_ref[...] += jnp.sum(dn * xh, axis=0, keepdims=True)

    return pl.pallas_call(
        body, name="ple_bwd", grid=(t // tm,),
        in_specs=[_rows(tm, d), _rows(tm, d), _whole(g), _rows(tm, d), _rows(tm, p.shape[1]), _wspec(wg), _wspec(wp)],
        out_specs=[_rows(tm, d), _rows(tm, d), _rows(tm, d), pl.BlockSpec((1, d), lambda i: (0, 0))],
        out_shape=[jax.ShapeDtypeStruct((t, d), F32), jax.ShapeDtypeStruct((t, d), BF16),
                   jax.ShapeDtypeStruct((t, d), BF16), jax.ShapeDtypeStruct((1, d), F32)],
        compiler_params=_row_params("arbitrary"),
    )(dh, h, g, n, p, wg.arr, wp.arr)


def _loss_head(h, g, tgt):
    t, d = h.shape
    tm = _tile(t, ROW_TILE)

    def body(h_ref, g_ref, t_ref, l_ref, dh_ref, dg_ref):
        xh, r = _rms(h_ref[...])
        gg = g_ref[...]
        e = xh * gg - t_ref[...]
        dy = e * (1.0 / d)

        @pl.when(pl.program_id(0) == 0)
        def _():
            l_ref[...] = jnp.zeros_like(l_ref)
            dg_ref[...] = jnp.zeros_like(dg_ref)

        l_ref[...] += 0.5 * jnp.sum(jnp.mean(e * e, axis=-1, keepdims=True), axis=0, keepdims=True)
        dg_ref[...] += jnp.sum(dy * xh, axis=0, keepdims=True)
        dh_ref[...] = _rms_bwd(dy, xh, r, gg)

    return pl.pallas_call(
        body, name="loss_head", grid=(t // tm,),
        in_specs=[_rows(tm, d), _whole(g), _rows(tm, d)],
        out_specs=[pl.BlockSpec((1, 128), lambda i: (0, 0)), _rows(tm, d), pl.BlockSpec((1, d), lambda i: (0, 0))],
        out_shape=[jax.ShapeDtypeStruct((1, 128), F32), jax.ShapeDtypeStruct((t, d), F32), jax.ShapeDtypeStruct((1, d), F32)],
        compiler_params=_row_params("arbitrary"),
    )(h, g, tgt)


def _s5_disc_math(a_re, a_im, log_dt):
    lam_re = jnp.minimum(a_re, -1e-4)
    lam_im = a_im
    dt = jnp.exp(log_dt)
    mag = jnp.exp(lam_re * dt)
    abar_re = mag * jnp.cos(lam_im * dt)
    abar_im = mag * jnp.sin(lam_im * dt)
    den = lam_re * lam_re + lam_im * lam_im
    nr = abar_re - 1.0
    ni = abar_im
    return abar_re, abar_im, (nr * lam_re + ni * lam_im) / den, (ni * lam_re - nr * lam_im) / den


def _s5_disc(a_re, a_im, log_dt):
    gp = jax.ShapeDtypeStruct(a_re.shape, F32)

    def body(ar_ref, ai_ref, ld_ref, o0, o1, o2, o3):
        outs = _s5_disc_math(ar_ref[...], ai_ref[...], ld_ref[...])
        for o_ref, val in zip((o0, o1, o2, o3), outs):
            o_ref[...] = val

    return pl.pallas_call(body, name="s5_disc", out_shape=[gp] * 4)(a_re, a_im, log_dt)


def _s5_disc_bwd(a_re, a_im, log_dt, cts):
    def body(ar_ref, ai_ref, ld_ref, c0, c1, c2, c3, dar_ref, dai_ref, dld_ref):
        _, vjp = jax.vjp(_s5_disc_math, ar_ref[...], ai_ref[...], ld_ref[...])
        dar, dai, dld = vjp((c0[...], c1[...], c2[...], c3[...]))
        dar_ref[...] = dar
        dai_ref[...] = dai
        dld_ref[...] = dld

    return pl.pallas_call(
        body, name="s5_disc_bwd",
        out_shape=[jax.ShapeDtypeStruct(a_re.shape, F32), jax.ShapeDtypeStruct(a_im.shape, F32),
                   jax.ShapeDtypeStruct(log_dt.shape, F32)],
    )(a_re, a_im, log_dt, *cts)


def _blockdiag(w, gl):
    g, r, c = w.shape
    w = w.reshape(g // gl, gl, r, c)
    eye = jnp.eye(gl, dtype=w.dtype)
    return (w[:, :, :, None, :] * eye[None, :, None, :, None]).reshape(g // gl, gl * r, gl * c)


def _blockdiag_take(m, gl, r, c):
    nb = m.shape[0]
    own = jnp.eye(gl, dtype=bool)[None, :, None, :, None]
    return jnp.where(own, m.reshape(nb, gl, r, gl, c), 0.0).sum(axis=3).reshape(nb * gl, r, c)


def _cmul(ar, ai, br, bi):
    return ar * br - ai * bi, ar * bi + ai * br


def _powers(ar, ai, n):
    out = [(ar, ai)]
    for _ in range(n - 1):
        out.append(_cmul(ar, ai, *out[-1]))
    return out


def _interleave(a, lc):
    t, d = a.shape
    return a.reshape(t // lc, SUBLANES, lc // SUBLANES, d).transpose(0, 2, 1, 3).reshape(t, d)


def _deinterleave(a, lc):
    t, d = a.shape
    return a.reshape(t // lc, lc // SUBLANES, SUBLANES, d).transpose(0, 2, 1, 3).reshape(t, d)


def _seg_rows(k):
    return slice(k * SUBLANES, (k + 1) * SUBLANES)


def _seg_scan(re_ref, im_ref, ar, ai, seg, reverse=False):
    cr = ci = jnp.zeros((SUBLANES, re_ref.shape[1]), F32)
    for k in (reversed(range(seg)) if reverse else range(seg)):
        rows = _seg_rows(k)
        cr, ci = ar * cr - ai * ci + re_ref[rows, :], ar * ci + ai * cr + im_ref[rows, :]
        re_ref[rows, :] = cr
        im_ref[rows, :] = ci
    return cr, ci


def _seg_entering(er, ei, pr, pi, c0r, c0i, reverse=False):
    rows = lax.broadcasted_iota(jnp.int32, er.shape, 0)
    vr = vi = jnp.zeros_like(er)
    cr, ci = c0r, c0i
    for j in (reversed(range(SUBLANES)) if reverse else range(SUBLANES)):
        vr = jnp.where(rows == j, cr, vr)
        vi = jnp.where(rows == j, ci, vi)
        cr, ci = er[j:j + 1, :] + pr * cr - pi * ci, ei[j:j + 1, :] + pr * ci + pi * cr
    return vr, vi, cr, ci


def _s5_scan_fwd(u, bre, bim, cre, cim, par, dskip, bl, seq, lc):
    t, d = u.shape
    nb, gw, ns = bre.shape
    nc = seq // lc
    seg = lc // SUBLANES

    def body(u_ref, bre_ref, bim_ref, cre_ref, cim_ref, par_ref, d_ref, y_ref, z_ref, st_ref, carry, sre, sim):
        @pl.when(pl.program_id(2) == 0)
        def _():
            carry[...] = jnp.zeros_like(carry)

        st_ref[...] = carry[...]
        uu = u_ref[...]
        ug = uu.astype(BF16)
        wre = _dot(ug, bre_ref[...])
        wim = _dot(ug, bim_ref[...])
        ar, ai = par_ref[0:1, :], par_ref[1:2, :]
        fr, fi = par_ref[2:3, :], par_ref[3:4, :]
        sre[...] = fr * wre - fi * wim
        sim[...] = fr * wim + fi * wre
        pows = _powers(ar, ai, seg)
        er, ei = _seg_scan(sre, sim, ar, ai, seg)
        vr, vi, cr, ci = _seg_entering(er, ei, *pows[-1], carry[0:1, :], carry[1:2, :])
        carry[0:1, :] = cr
        carry[1:2, :] = ci
        for k in range(seg):
            rows = _seg_rows(k)
            fr_k, fi_k = _cmul(*pows[k], vr, vi)
            sre[rows, :] += fr_k
            sim[rows, :] += fi_k
        y = _dot(sre[...].astype(BF16), cre_ref[...]) - _dot(sim[...].astype(BF16), cim_ref[...]) + d_ref[...] * uu
        y_ref[...] = y
        z_ref[...] = _gelu(y).astype(BF16)

    tok = pl.BlockSpec((lc, gw), lambda g, b, c: (b * nc + c, g))
    mat_b = pl.BlockSpec((None, gw, ns), lambda g, b, c: (g, 0, 0))
    mat_c = pl.BlockSpec((None, ns, gw), lambda g, b, c: (g, 0, 0))
    return pl.pallas_call(
        body, name="s5_scan_fwd", grid=(nb, bl, nc),
        in_specs=[tok, mat_b, mat_b, mat_c, mat_c, pl.BlockSpec((None, 8, ns), lambda g, b, c: (g, 0, 0)),
                  pl.BlockSpec((1, gw), lambda g, b, c: (0, g))],
        out_specs=[tok, tok, pl.BlockSpec((None, None, 2, ns), lambda g, b, c: (g, b * nc + c, 0, 0))],
        out_shape=[jax.ShapeDtypeStruct((t, d), F32), jax.ShapeDtypeStruct((t, d), BF16),
                   jax.ShapeDtypeStruct((nb, bl * nc, 2, ns), F32)],
        scratch_shapes=[pltpu.VMEM((2, ns), F32), pltpu.VMEM((lc, ns), F32), pltpu.VMEM((lc, ns), F32)],
        compiler_params=pltpu.CompilerParams(dimension_semantics=("parallel", "arbitrary", "arbitrary"),
                                             vmem_limit_bytes=V7X_VMEM_LIMIT),
    )(u, bre, bim, cre, cim, par, dskip)


def _s5_scan_bwd(u, dy, st, bre, bim, cre, cim, par, dskip, bl, seq, lc):
    t, d = u.shape
    nb, gw, ns = bre.shape
    nc = seq // lc
    seg = lc // SUBLANES

    def body(u_ref, dy_ref, st_ref, bre_ref, bim_ref, cre_ref, cim_ref, par_ref, d_ref,
             du_ref, dbre_ref, dbim_ref, dcre_ref, dcim_ref, dpar_ref, dd_ref,
             lcarry, sre, sim, wre_s, wim_s, lre, lim):
        b, c = pl.program_id(1), pl.program_id(2)

        @pl.when((b == 0) & (c == 0))
        def _():
            for ref in (dbre_ref, dbim_ref, dcre_ref, dcim_ref, dpar_ref, dd_ref):
                ref[...] = jnp.zeros_like(ref)

        @pl.when(c == 0)
        def _():
            lcarry[...] = jnp.zeros_like(lcarry)

        uu = u_ref[...]
        ug = uu.astype(BF16)
        dyv = dy_ref[...]
        dyb = dyv.astype(BF16)
        ar, ai = par_ref[0:1, :], par_ref[1:2, :]
        fr, fi = par_ref[2:3, :], par_ref[3:4, :]
        wre = _dot(ug, bre_ref[...])
        wim = _dot(ug, bim_ref[...])
        wre_s[...] = wre
        wim_s[...] = wim
        sre[...] = fr * wre - fi * wim
        sim[...] = fr * wim + fi * wre

        pows = _powers(ar, ai, seg)
        er, ei = _seg_scan(sre, sim, ar, ai, seg)
        svr, svi, _, _ = _seg_entering(er, ei, *pows[-1], st_ref[0:1, :], st_ref[1:2, :])
        for k in range(seg):
            rows = _seg_rows(k)
            fr_k, fi_k = _cmul(*pows[k], svr, svi)
            sre[rows, :] += fr_k
            sim[rows, :] += fi_k

        lre[...] = _dot_nt(dyb, cre_ref[...])
        lim[...] = -_dot_nt(dyb, cim_ref[...])
        er, ei = _seg_scan(lre, lim, ar, -ai, seg, reverse=True)
        lvr, lvi, cr, ci = _seg_entering(er, ei, pows[-1][0], -pows[-1][1], lcarry[0:1, :], lcarry[1:2, :], reverse=True)
        lcarry[0:1, :] = cr
        lcarry[1:2, :] = ci
        dar = dai = jnp.zeros_like(er)
        for k in range(seg):
            rows = _seg_rows(k)
            pr, pi = pows[seg - 1 - k]
            fr_k, fi_k = _cmul(pr, -pi, lvr, lvi)
            lr = lre[rows, :] + fr_k
            li = lim[rows, :] + fi_k
            lre[rows, :] = lr
            lim[rows, :] = li
            spr, spi = (svr, svi) if k == 0 else (sre[_seg_rows(k - 1), :], sim[_seg_rows(k - 1), :])
            dar = dar + lr * spr + li * spi
            dai = dai + li * spr - lr * spi

        lr, li = lre[...], lim[...]
        wr, wi = wre_s[...], wim_s[...]
        dpar_ref[0:1, :] += jnp.sum(dar, axis=0, keepdims=True)
        dpar_ref[1:2, :] += jnp.sum(dai, axis=0, keepdims=True)
        dpar_ref[2:3, :] += jnp.sum(lr * wr + li * wi, axis=0, keepdims=True)
        dpar_ref[3:4, :] += jnp.sum(li * wr - lr * wi, axis=0, keepdims=True)
        dwr = (fr * lr + fi * li).astype(BF16)
        dwi = (fr * li - fi * lr).astype(BF16)
        dsk = d_ref[...]
        du_ref[...] = _dot_nt(dwr, bre_ref[...]) + _dot_nt(dwi, bim_ref[...]) + dsk * dyv
        dd_ref[...] += jnp.sum(dyv * uu, axis=0, keepdims=True)
        dbre_ref[...] += _dot_tn(ug, dwr)
        dbim_ref[...] += _dot_tn(ug, dwi)
        dcre_ref[...] += _dot_tn(sre[...].astype(BF16), dyb)
        dcim_ref[...] -= _dot_tn(sim[...].astype(BF16), dyb)

    tok = pl.BlockSpec((lc, gw), lambda g, b, c: (b * nc + nc - 1 - c, g))
    mat_b = pl.BlockSpec((None, gw, ns), lambda g, b, c: (g, 0, 0))
    mat_c = pl.BlockSpec((None, ns, gw), lambda g, b, c: (g, 0, 0))
    rows8 = pl.BlockSpec((None, 8, ns), lambda g, b, c: (g, 0, 0))
    dvec = pl.BlockSpec((1, gw), lambda g, b, c: (0, g))
    tile = pltpu.VMEM((lc, ns), F32)
    return pl.pallas_call(
        body, name="s5_scan_bwd", grid=(nb, bl, nc),
        in_specs=[tok, tok, pl.BlockSpec((None, None, 2, ns), lambda g, b, c: (g, b * nc + nc - 1 - c, 0, 0)),
                  mat_b, mat_b, mat_c, mat_c, rows8, dvec],
        out_specs=[tok, mat_b, mat_b, mat_c, mat_c, rows8, dvec],
        out_shape=[jax.ShapeDtypeStruct((t, d), F32),
                   jax.ShapeDtypeStruct((nb, gw, ns), F32), jax.ShapeDtypeStruct((nb, gw, ns), F32),
                   jax.ShapeDtypeStruct((nb, ns, gw), F32), jax.ShapeDtypeStruct((nb, ns, gw), F32),
                   jax.ShapeDtypeStruct((nb, 8, ns), F32), jax.ShapeDtypeStruct((1, d), F32)],
        scratch_shapes=[pltpu.VMEM((2, ns), F32)] + [tile] * 6,
        compiler_params=pltpu.CompilerParams(dimension_semantics=("arbitrary", "arbitrary", "arbitrary"),
                                             vmem_limit_bytes=V7X_VMEM_LIMIT),
    )(u, dy, st, bre, bim, cre, cim, par, dskip)


def _s5_out(h, z, wglu):
    t, d = h.shape
    tm = _tile(t, ROW_TILE)

    def body(h_ref, z_ref, w_ref, o_ref):
        zz = _dot(z_ref[...], w_ref[...])
        o_ref[...] = h_ref[...] + zz[:, :d] * jax.nn.sigmoid(zz[:, d:])

    return pl.pallas_call(
        body, name="s5_out", grid=(t // tm,),
        in_specs=[_rows(tm, d), _rows(tm, d), _wspec(wglu)],
        out_specs=_rows(tm, d), out_shape=jax.ShapeDtypeStruct((t, d), F32),
        compiler_params=_row_params(),
    )(h, z, wglu.arr)


def _s5_out_bwd(dh, z, y, wglu):
    t, d = dh.shape
    tm = _tile(t, ROW_TILE)

    def body(dh_ref, z_ref, y_ref, w_ref, dy_ref, dzz_ref):
        zz = _dot(z_ref[...], w_ref[...])
        out, sg = zz[:, :d], jax.nn.sigmoid(zz[:, d:])
        dh_v = dh_ref[...]
        dzz = jnp.concatenate([dh_v * sg, dh_v * out * sg * (1.0 - sg)], axis=1).astype(BF16)
        dzz_ref[...] = dzz
        dy_ref[...] = _dot_nt(dzz, w_ref[...]) * _gelu_grad(y_ref[...])

    return pl.pallas_call(
        body, name="s5_out_bwd", grid=(t // tm,),
        in_specs=[_rows(tm, d), _rows(tm, d), _rows(tm, d), _wspec(wglu)],
        out_specs=[_rows(tm, d), _rows(tm, 2 * d)],
        out_shape=[jax.ShapeDtypeStruct((t, d), F32), jax.ShapeDtypeStruct((t, 2 * d), BF16)],
        compiler_params=_row_params(),
    )(dh, z, y, wglu.arr)


def _head_spec(seq, w):
    return pl.BlockSpec((None, SB_HEADS_PER_STEP, seq, w), lambda b, h: (b, h, 0, 0))


def _pair_spec(seq):
    return pl.BlockSpec((seq, SB_HEADS_PER_STEP * HEAD_DIM), lambda b, g: (b, g))


def _own_lanes(a, hh):
    lane = lax.broadcasted_iota(jnp.int32, a.shape, 1)
    return jnp.where((lane < HEAD_DIM) == (hh % 2 == 0), a, jnp.zeros_like(a))


def _pick_lanes(even, odd):
    lane = lax.broadcasted_iota(jnp.int32, even.shape, 1)
    return jnp.where(lane < HEAD_DIM, even, odd)


def _pair_cols(hh):
    return slice((hh // 2) * LANES, (hh // 2 + 1) * LANES)


def _tri_and_ones(kind):
    row = lax.broadcasted_iota(jnp.int32, (Q_BLOCK, Q_BLOCK), 0)
    col = lax.broadcasted_iota(jnp.int32, (Q_BLOCK, Q_BLOCK), 1)
    tri = {"after": row > col, "upto": row <= col, "before": row < col}[kind]
    return jnp.concatenate([tri.astype(BF16), jnp.ones((Q_BLOCK, Q_BLOCK), BF16)], axis=1)


def _sb_fwd(q, k, v, bsz, seq):
    dh = HEAD_DIM
    nh = q.shape[1] // dh
    tq = min(SB_Q_TILE, seq // 2)
    nq, nsub = seq // tq, tq // Q_BLOCK
    scale = dh ** -0.5
    hp = SB_HEADS_PER_STEP
    qb = Q_BLOCK

    def body(q_ref, k_ref, v_ref, o_ref, tot_ref):
        strict = lax.broadcasted_iota(jnp.int32, (qb, qb), 1) < lax.broadcasted_iota(jnp.int32, (qb, qb), 0)
        sums = _tri_and_ones("after")

        def sweep(qs, c0, units, carry):
            heads = sorted({u[0] for u in units})
            kbs = {hh: k_ref[pl.ds(c0, qb), _pair_cols(hh)] for hh in heads}
            vbs = {hh: v_ref[pl.ds(c0, qb), _pair_cols(hh)] for hh in heads}
            zs = [_dot_nt(qs[hh][s], kbs[hh]) * scale for hh, s, _ in units]
            lkrs = [_neg_softplus(z) for z in zs]
            lks = [jnp.where(strict, lkr, 0.0) if dg else lkr for lkr, (_, _, dg) in zip(lkrs, units)]
            css = [_dot2(lk, sums) for lk in lks]
            carry = dict(carry)
            for (hh, s, dg), z, lkr, cs in zip(units, zs, lkrs, css):
                acc, run = carry[hh, s]
                att = jnp.exp(z + lkr + cs[:, :qb] + run)
                if dg:
                    att = jnp.where(strict, att, 0.0)
                carry[hh, s] = (acc + _dot(att.astype(BF16), vbs[hh]), run + cs[:, qb:])
            return carry

        def q_loop(qi, _):
            r0 = qi * tq
            qs = {hh: [_own_lanes(q_ref[pl.ds(pl.multiple_of(r0 + s * qb, qb), qb), _pair_cols(hh)], hh)
                       for s in range(nsub)] for hh in range(hp)}
            carry = {(hh, s): (jnp.zeros((qb, LANES), F32), jnp.zeros((qb, qb), F32)) for hh in range(hp) for s in range(nsub)}
            for jj in reversed(range(nsub)):
                units = [(hh, s, s == jj) for hh in range(hp) for s in range(jj, nsub)]
                carry = sweep(qs, pl.multiple_of(r0 + jj * qb, qb), units, carry)
            units = [(hh, s, False) for hh in range(hp) for s in range(nsub)]
            carry = lax.fori_loop(
                0, nsub * qi, lambda t, c: sweep(qs, pl.multiple_of((nsub * qi - 1 - t) * qb, qb), units, c), carry)
            for s in range(nsub):
                rows = pl.ds(pl.multiple_of(r0 + s * qb, qb), qb)
                for hh in range(0, hp, 2):
                    o_ref[rows, _pair_cols(hh)] = _pick_lanes(carry[hh, s][0], carry[hh + 1, s][0]).astype(BF16)
                for hh in range(hp):
                    tot_ref[hh, rows, :] = carry[hh, s][1][:, 0:1]
            return 0

        lax.fori_loop(0, nq, q_loop, 0)

    return pl.pallas_call(
        body, name="sb_fwd", grid=(bsz, nh // hp),
        in_specs=[_pair_spec(seq)] * 3,
        out_specs=[_pair_spec(seq), _head_spec(seq, 1)],
        out_shape=[jax.ShapeDtypeStruct(q.shape, BF16), jax.ShapeDtypeStruct((bsz, nh, seq, 1), F32)],
        compiler_params=pltpu.CompilerParams(dimension_semantics=("parallel", "parallel"), vmem_limit_bytes=V7X_VMEM_LIMIT),
    )(q, k, v)


def _sb_bwd(q, k, v, do, tot, bsz, seq):
    dh = HEAD_DIM
    nh = q.shape[1] // dh
    tq = min(SB_Q_TILE, seq // 2)
    nq, nsub = seq // tq, tq // Q_BLOCK
    scale = dh ** -0.5
    hp = SB_HEADS_PER_STEP
    qb = Q_BLOCK

    def body(q_ref, k_ref, v_ref, do_ref, tot_ref, dq_ref, dk_ref, dv_ref, dka, dva):
        dka[...] = jnp.zeros_like(dka)
        dva[...] = jnp.zeros_like(dva)
        strict = lax.broadcasted_iota(jnp.int32, (qb, qb), 1) < lax.broadcasted_iota(jnp.int32, (qb, qb), 0)
        upto = _tri_and_ones("upto")
        before = _tri_and_ones("before")

        def sweep(qf, dof, tots, c0, first, units, carry):
            heads = sorted({u[0] for u in units})
            kbs = {hh: k_ref[pl.ds(c0, qb), _pair_cols(hh)] for hh in heads}
            vbs = {hh: v_ref[pl.ds(c0, qb), _pair_cols(hh)] for hh in heads}
            sub = lambda a, s: a[s * qb:(s + 1) * qb, :]
            zs = [_dot_nt(sub(qf[hh], s), kbs[hh]) * scale for hh, s, _ in units]
            das = [_dot_nt(sub(dof[hh], s), vbs[hh]) for hh, s, _ in units]
            lkrs = [_neg_softplus(z) for z in zs]
            lks = [jnp.where(strict, lkr, 0.0) if dg else lkr for lkr, (_, _, dg) in zip(lkrs, units)]
            css = [_dot2(lk, upto) for lk in lks]
            lsigs, atts, gls = [], [], []
            for (hh, s, dg), z, lkr, cs, da in zip(units, zs, lkrs, css, das):
                lsig = z + lkr
                att = jnp.exp(lsig + (tots[hh, s] - (cs[:, :qb] + carry[hh, s][1])))
                if dg:
                    att = jnp.where(strict, att, 0.0)
                lsigs.append(lsig)
                atts.append(att)
                gls.append(da * att)
            gss = [_dot2(gl, before) for gl in gls]
            carry = dict(carry)
            dzs = {}
            for (hh, s, dg), lsig, gl, cs, gs in zip(units, lsigs, gls, css, gss):
                dqa, pre, gpre = carry[hh, s]
                sig = jnp.exp(lsig)
                dz = gl * (1.0 - sig) - (gs[:, :qb] + gpre) * sig
                if dg:
                    dz = jnp.where(strict, dz, 0.0)
                dz = (dz * scale).astype(BF16)
                dzs[hh, s] = dz
                carry[hh, s] = (dqa + _dot(dz, kbs[hh]), pre + cs[:, qb:], gpre + gs[:, qb:])
            att_of = {(hh, s): a for (hh, s, _), a in zip(units, atts)}
            for hh in heads:
                dzc = jnp.concatenate([dzs[hh, s] for s in range(first, nsub)], axis=0)
                attc = jnp.concatenate([att_of[hh, s].astype(BF16) for s in range(first, nsub)], axis=0)
                dka[pl.ds(c0, qb), _pair_cols(hh)] += _dot_tn(dzc, qf[hh][first * qb:, :])
                dva[pl.ds(c0, qb), _pair_cols(hh)] += _dot_tn(attc, dof[hh][first * qb:, :])
            return carry

        def q_loop(qi, _):
            r0 = pl.multiple_of(qi * tq, tq)
            qf = [_own_lanes(q_ref[pl.ds(r0, tq), _pair_cols(hh)], hh) for hh in range(hp)]
            dof = [_own_lanes(do_ref[pl.ds(r0, tq), _pair_cols(hh)], hh) for hh in range(hp)]
            tots = {(hh, s): jnp.broadcast_to(tot_ref[hh, pl.ds(pl.multiple_of(r0 + s * qb, qb), qb), :], (qb, qb))
                    for hh in range(hp) for s in range(nsub)}
            carry = {(hh, s): (jnp.zeros((qb, LANES), F32), jnp.zeros((qb, qb), F32), jnp.zeros((qb, qb), F32))
                     for hh in range(hp) for s in range(nsub)}
            units = [(hh, s, False) for hh in range(hp) for s in range(nsub)]
            carry = lax.fori_loop(
                0, nsub * qi, lambda kj, c: sweep(qf, dof, tots, pl.multiple_of(kj * qb, qb), 0, units, c), carry)
            for jj in range(nsub):
                units = [(hh, s, s == jj) for hh in range(hp) for s in range(jj, nsub)]
                carry = sweep(qf, dof, tots, pl.multiple_of(r0 + jj * qb, qb), jj, units, carry)
            for s in range(nsub):
                rows = pl.ds(pl.multiple_of(r0 + s * qb, qb), qb)
                for hh in range(0, hp, 2):
                    dq_ref[rows, _pair_cols(hh)] = _pick_lanes(carry[hh, s][0], carry[hh + 1, s][0]).astype(BF16)
            return 0

        lax.fori_loop(0, nq, q_loop, 0)
        dk_ref[...] = dka[...].astype(BF16)
        dv_ref[...] = dva[...].astype(BF16)

    ps = _pair_spec(seq)
    return pl.pallas_call(
        body, name="sb_bwd", grid=(bsz, nh // hp),
        in_specs=[ps, ps, ps, ps, _head_spec(seq, 1)],
        out_specs=[ps, ps, ps],
        out_shape=[jax.ShapeDtypeStruct(q.shape, BF16)] * 3,
        scratch_shapes=[pltpu.VMEM((seq, hp * dh), F32), pltpu.VMEM((seq, hp * dh), F32)],
        compiler_params=pltpu.CompilerParams(dimension_semantics=("parallel", "parallel"), vmem_limit_bytes=V7X_VMEM_LIMIT),
    )(q, k, v, do, tot)


def _coords():
    return lax.axis_index("x"), lax.axis_index("y"), lax.axis_index("c")


def _all_gather(x, name):
    r, c = x.shape

    def body(x_ref, out_ref, send_sems, recv_sems, local_sem):
        mx, my, mc = _coords()
        me, sibling = (mx, my, mc), (mx, my, 1 - mc)
        chips = [(1 - mx, my), (mx, 1 - my), (1 - mx, 1 - my)]

        def blk(px, py, pc):
            return out_ref.at[4 * px + 2 * py + pc]

        def copy(k, block, to, src=None):
            return pltpu.make_async_remote_copy(
                src_ref=blk(*block) if src is None else src, dst_ref=blk(*block),
                send_sem=send_sems.at[k], recv_sem=recv_sems.at[k], device_id=to, device_id_type=MESH)

        mine = pltpu.make_async_copy(x_ref, blk(*me), local_sem)
        mine.start()
        first = [copy(0, me, sibling, src=x_ref)]
        first += [copy(1 + j, me, (*chip, mc), src=x_ref) for j, chip in enumerate(chips)]
        for cp in first:
            cp.start()
        passed = [copy(4 + j, (*chip, mc), sibling) for j, chip in enumerate(chips)]
        for j, chip in enumerate(chips):
            copy(1 + j, (*chip, mc), me).wait_recv()
            passed[j].start()
        copy(0, sibling, me).wait_recv()
        for j, chip in enumerate(chips):
            copy(4 + j, (*chip, 1 - mc), me).wait_recv()
        for cp in first + passed:
            cp.wait_send()
        mine.wait()

    return pl.pallas_call(
        body, name=name,
        out_shape=jax.ShapeDtypeStruct((N_DEV, r, c), x.dtype),
        in_specs=[pl.BlockSpec(memory_space=pl.ANY)],
        out_specs=pl.BlockSpec(memory_space=pl.ANY),
        scratch_shapes=[pltpu.SemaphoreType.DMA((7,)), pltpu.SemaphoreType.DMA((7,)), pltpu.SemaphoreType.DMA],
    )(x)


_ANY = pl.BlockSpec(memory_space=pl.ANY)


def _window(ref, shard_shape, by_cols, dev, half=None):
    rows = shard_shape[1]
    r0, nr = (0, rows) if half is None else (half * (rows // 2), rows // 2)
    if by_cols:
        n = shard_shape[2]
        return ref.at[:, pl.ds(r0, nr), pl.ds(pl.multiple_of(dev * n, n), n)]
    return ref.at[:, pl.ds(pl.multiple_of(dev * rows + r0, 2 * SUBLANES), nr), :]


_TO_SIBLING, _TO_X, _TO_Y, _X_TO_SIBLING, _Y_TO_SIBLING, _DIAG0_TO_SIBLING, _DIAG1_TO_SIBLING, _X_HALF_TO_Y, _Y_HALF_TO_X = range(9)


def _gather_weights(shards, by_cols):
    na = len(shards)
    fulls = [jax.ShapeDtypeStruct(
        (s.shape[0], s.shape[1], s.shape[2] * N_DEV) if c else (s.shape[0], s.shape[1] * N_DEV, s.shape[2]), s.dtype)
        for s, c in zip(shards, by_cols)]

    def body(*refs):
        x_refs, out_refs = refs[:na], refs[na:2 * na]
        send_sems, recv_sems, local_sems = refs[2 * na:]
        mx, my, mc = _coords()
        me, sibling = (mx, my, mc), (mx, my, 1 - mc)
        xn, yn, diag = (1 - mx, my, mc), (mx, 1 - my, mc), (1 - mx, 1 - my, mc)

        def blk(a, dev, half=None):
            px, py, pc = dev
            return _window(out_refs[a], x_refs[a].shape, by_cols[a], 4 * px + 2 * py + pc, half)

        def copy(a, k, block, to, half=None, src=None):
            return pltpu.make_async_remote_copy(
                src_ref=blk(a, block, half) if src is None else src, dst_ref=blk(a, block, half),
                send_sem=send_sems.at[a, k], recv_sem=recv_sems.at[a, k], device_id=to, device_id_type=MESH)

        def other_core(dev):
            return dev[0], dev[1], 1 - dev[2]

        sends = []
        for a in range(na):
            mine = pltpu.make_async_copy(x_refs[a], blk(a, me), local_sems.at[a])
            mine.start()
            first = [copy(a, _TO_SIBLING, me, sibling, src=x_refs[a]), copy(a, _TO_X, me, xn, src=x_refs[a]),
                     copy(a, _TO_Y, me, yn, src=x_refs[a])]
            for cp in first:
                cp.start()
            sends += first
        for a in range(na):
            copy(a, _TO_X, xn, me).wait_recv()
            copy(a, _TO_Y, yn, me).wait_recv()
            passed = [copy(a, _X_TO_SIBLING, xn, sibling), copy(a, _X_HALF_TO_Y, xn, yn, half=0),
                      copy(a, _Y_TO_SIBLING, yn, sibling), copy(a, _Y_HALF_TO_X, yn, xn, half=1)]
            for cp in passed:
                cp.start()
            sends += passed
        for a in range(na):
            copy(a, _X_HALF_TO_Y, diag, me, half=0).wait_recv()
            copy(a, _Y_HALF_TO_X, diag, me, half=1).wait_recv()
            passed = [copy(a, _DIAG0_TO_SIBLING, diag, sibling, half=0), copy(a, _DIAG1_TO_SIBLING, diag, sibling, half=1)]
            for cp in passed:
                cp.start()
            sends += passed
        for a in range(na):
            copy(a, _TO_SIBLING, sibling, me).wait_recv()
            copy(a, _X_TO_SIBLING, other_core(xn), me).wait_recv()
            copy(a, _Y_TO_SIBLING, other_core(yn), me).wait_recv()
            copy(a, _DIAG0_TO_SIBLING, other_core(diag), me, half=0).wait_recv()
            copy(a, _DIAG1_TO_SIBLING, other_core(diag), me, half=1).wait_recv()
        for cp in sends:
            cp.wait_send()
        for a in range(na):
            pltpu.make_async_copy(x_refs[a], blk(a, me), local_sems.at[a]).wait()

    nsem = _Y_HALF_TO_X + 1
    return pl.pallas_call(
        body, name="gather_weights", out_shape=fulls, in_specs=[_ANY] * na, out_specs=[_ANY] * na,
        scratch_shapes=[pltpu.SemaphoreType.DMA((na, nsem)), pltpu.SemaphoreType.DMA((na, nsem)), pltpu.SemaphoreType.DMA((na,))],
    )(*shards)


def _rs_sibling(grads, shard_shapes, by_cols):
    na = len(grads)
    slabs = [jax.ShapeDtypeStruct((N_CHIP,) + tuple(s), F32) for s in shard_shapes]

    def body(*refs):
        g_refs, got_refs = refs[:na], refs[na:2 * na]
        send_sems, recv_sems = refs[2 * na:]
        mx, my, mc = _coords()
        copies = []
        for a in range(na):
            for q in range(N_CHIP):
                dev = 2 * q + 1 - mc
                theirs = g_refs[a].at[dev] if by_cols[a] else _window(g_refs[a], shard_shapes[a], False, dev)
                copies.append(pltpu.make_async_remote_copy(
                    src_ref=theirs, dst_ref=got_refs[a].at[q], send_sem=send_sems.at[a, q], recv_sem=recv_sems.at[a, q],
                    device_id=(mx, my, 1 - mc), device_id_type=MESH))
        for cp in copies:
            cp.start()
        for cp in copies:
            cp.wait()

    sems = pltpu.SemaphoreType.DMA((na, N_CHIP))
    return pl.pallas_call(
        body, name="rs_sibling", out_shape=slabs, in_specs=[_ANY] * na, out_specs=[_ANY] * na,
        scratch_shapes=[sems, sems],
    )(*grads)


def _rs_add(g, got, core, by_cols, cols, name):
    nq, nl, r, c = got.shape
    tr = _tile(r, ELEMENTWISE_ROW_TILE, 2 * SUBLANES)

    def body(core_ref, a_ref, b_ref, o_ref):
        o_ref[...] = (a_ref[...] + b_ref[...])[:, :cols].astype(BF16)

    if by_cols:
        mine = pl.BlockSpec((None, None, tr, c), lambda q, l, i, core_ref: (2 * q + core_ref[0], l, i, 0))
    else:
        mine = pl.BlockSpec((None, tr, c), lambda q, l, i, core_ref: (l, (2 * q + core_ref[0]) * (r // tr) + i, 0))
    spec = pl.BlockSpec((None, None, tr, c), lambda q, l, i, core_ref: (q, l, i, 0))
    return pl.pallas_call(
        body, name=name, out_shape=jax.ShapeDtypeStruct((nq, nl, r, cols), BF16),
        grid_spec=pltpu.PrefetchScalarGridSpec(
            num_scalar_prefetch=1, grid=(nq, nl, r // tr), in_specs=[mine, spec],
            out_specs=pl.BlockSpec((None, None, tr, cols), lambda q, l, i, core_ref: (q, l, i, 0))),
        compiler_params=pltpu.CompilerParams(dimension_semantics=("parallel",) * 3),
    )(core, g, got)


def _rs_chips(parts):
    na = len(parts)

    def body(*refs):
        p_refs, out_refs = refs[:na], refs[na:2 * na]
        send_sems, recv_sems, local_sems = refs[2 * na:]
        mx, my, mc = _coords()
        here = 2 * mx + my
        chips = [(1 - mx, my), (mx, 1 - my), (1 - mx, 1 - my)]
        copies = []
        for a in range(na):
            copies.append(pltpu.make_async_copy(p_refs[a].at[here], out_refs[a].at[here], local_sems.at[a]))
            for j, (cx, cy) in enumerate(chips):
                copies.append(pltpu.make_async_remote_copy(
                    src_ref=p_refs[a].at[2 * cx + cy], dst_ref=out_refs[a].at[here],
                    send_sem=send_sems.at[a, j], recv_sem=recv_sems.at[a, j], device_id=(cx, cy, mc), device_id_type=MESH))
        for cp in copies:
            cp.start()
        for cp in copies:
            cp.wait()

    return pl.pallas_call(
        body, name="rs_chips", out_shape=[jax.ShapeDtypeStruct(p.shape, p.dtype) for p in parts],
        in_specs=[_ANY] * na, out_specs=[_ANY] * na,
        scratch_shapes=[pltpu.SemaphoreType.DMA((na, 3)), pltpu.SemaphoreType.DMA((na, 3)), pltpu.SemaphoreType.DMA((na,))],
    )(*parts)


def _adamw(parts, w, m, v, name):
    n, r, c = parts.shape
    tr = _tile(r, 320, 2 * SUBLANES)
    bc1 = 1.0 - ADAM_B1 ** ADAM_STEP
    bc2 = 1.0 - ADAM_B2 ** ADAM_STEP

    def body(p_ref, w_ref, m_ref, v_ref, g_ref, d_ref, mo_ref, vo_ref):
        g = p_ref[0].astype(F32)
        for j in range(1, n):
            g = g + p_ref[j].astype(F32)
        mn = ADAM_B1 * m_ref[...] + (1.0 - ADAM_B1) * g
        vn = ADAM_B2 * v_ref[...] + (1.0 - ADAM_B2) * (g * g)
        g_ref[...] = g
        mo_ref[...] = mn
        vo_ref[...] = vn
        d_ref[...] = -ADAM_LR * ((mn / bc1) / (jnp.sqrt(vn / bc2) + ADAM_EPS) + ADAM_WD * w_ref[...])

    flat = pl.BlockSpec((tr, c), lambda i: (i, 0))
    return pl.pallas_call(
        body, name=name, grid=(r // tr,),
        in_specs=[pl.BlockSpec((n, tr, c), lambda i: (0, i, 0)), flat, flat, flat],
        out_specs=[flat] * 4, out_shape=[jax.ShapeDtypeStruct((r, c), F32)] * 4,
        compiler_params=pltpu.CompilerParams(dimension_semantics=("parallel",)),
    )(parts, w, m, v)


def _adam_math(g, w, m, v):
    bc1 = 1.0 - ADAM_B1 ** ADAM_STEP
    bc2 = 1.0 - ADAM_B2 ** ADAM_STEP
    mn = ADAM_B1 * m + (1.0 - ADAM_B1) * g
    vn = ADAM_B2 * v + (1.0 - ADAM_B2) * (g * g)
    return -ADAM_LR * ((mn / bc1) / (jnp.sqrt(vn / bc2) + ADAM_EPS) + ADAM_WD * w), mn, vn


def _adamw_shard(parts, w, m, v, name):
    nl, r, c = w.shape
    tr = _tile(r, ELEMENTWISE_ROW_TILE, 2 * SUBLANES)

    def body(p_ref, w_ref, m_ref, v_ref, g_ref, d_ref, mo_ref, vo_ref):
        g = p_ref[0].astype(F32)
        for q in range(1, N_CHIP):
            g = g + p_ref[q].astype(F32)
        g = g[:, :c]
        g_ref[...] = g
        d_ref[...], mo_ref[...], vo_ref[...] = _adam_math(g, w_ref[...], m_ref[...], v_ref[...])

    native = pl.BlockSpec((None, tr, c), lambda l, i: (l, i, 0))
    return pl.pallas_call(
        body, name=name, grid=(nl, r // tr),
        in_specs=[pl.BlockSpec((N_CHIP, None, tr, parts.shape[3]), lambda l, i: (0, l, i, 0)), native, native, native],
        out_specs=[native] * 4, out_shape=[jax.ShapeDtypeStruct(w.shape, F32)] * 4,
        compiler_params=pltpu.CompilerParams(dimension_semantics=("parallel", "parallel"), vmem_limit_bytes=V7X_VMEM_LIMIT),
    )(parts, w, m, v)


def _pack(arrs, dtype=F32):
    cols = []
    for a in arrs:
        f = a.reshape(-1).astype(dtype)
        cols.append(jnp.pad(f, (0, -f.shape[0] % FLAT_W)))
    flat = jnp.concatenate(cols)
    flat = jnp.pad(flat, (0, -flat.shape[0] % (FLAT_W * SUBLANES)))
    return flat.reshape(-1, FLAT_W)


def _unpack(flat, shapes, lead=()):
    flat = flat.reshape(lead + (-1,))
    out, off = [], 0
    for s in shapes:
        n = math.prod(s)
        out.append(flat[..., off:off + n].reshape(lead + tuple(s)))
        off += n + (-n % FLAT_W)
    return out


def kernel(x, p, ffn1_norm, ffn1_w1, ffn1_w3, ffn1_w2, mix_norm, ffn2_norm, ffn2_w1, ffn2_w3, ffn2_w2, ple_norm, ple_proj, ple_gate, s5_w_in, s5_a_re, s5_a_im, s5_log_dt, s5_b_re, s5_b_im, s5_c_re, s5_c_im, s5_d, s5_w_glu, sb_w_qkv, sb_w_o, final_norm, loss_target, m_ffn1_norm, m_ffn1_w1, m_ffn1_w3, m_ffn1_w2, m_mix_norm, m_ffn2_norm, m_ffn2_w1, m_ffn2_w3, m_ffn2_w2, m_ple_norm, m_ple_proj, m_ple_gate, m_s5_w_in, m_s5_a_re, m_s5_a_im, m_s5_log_dt, m_s5_b_re, m_s5_b_im, m_s5_c_re, m_s5_c_im, m_s5_d, m_s5_w_glu, m_sb_w_qkv, m_sb_w_o, m_final_norm, v_ffn1_norm, v_ffn1_w1, v_ffn1_w3, v_ffn1_w2, v_mix_norm, v_ffn2_norm, v_ffn2_w1, v_ffn2_w3, v_ffn2_w2, v_ple_norm, v_ple_proj, v_ple_gate, v_s5_w_in, v_s5_a_re, v_s5_a_im, v_s5_log_dt, v_s5_b_re, v_s5_b_im, v_s5_c_re, v_s5_c_im, v_s5_d, v_s5_w_glu, v_sb_w_qkv, v_sb_w_o, v_final_norm):
    given = dict(locals())
    wts = {n: given[n] for n in WEIGHTS}
    mom = {n: given["m_" + n] for n in WEIGHTS}
    var = {n: given["v_" + n] for n in WEIGHTS}
    bl, seq, d = x.shape
    t = bl * seq
    depth = p.shape[0]

    hid_pad = -wts["ffn1_w1"].shape[-1] % LANES

    def padded(n):
        a = wts[n].astype(BF16)
        if n in FFN_COL:
            return jnp.pad(a, ((0, 0), (0, 0), (0, hid_pad)))
        if n in FFN_ROW:
            return jnp.pad(a, ((0, 0), (0, hid_pad), (0, 0)))
        return a

    sent = [padded(n) for n in SHARDED]
    shard_shape = {n: a.shape for n, a in zip(SHARDED, sent)}
    by_cols = [n in COL_SHARDED for n in SHARDED]
    full = dict(zip(SHARDED, _gather_weights(sent, by_cols)))

    def row(a):
        return a.reshape(1, -1)

    def wt(n, layer=0, col=0, width=None):
        return _W(full[n], layer, col, width)

    gbuf = {}

    def wgrad(n, x_act, dy, layer=0, scale=1.0):
        like = (N_DEV,) + shard_shape[n] if n in COL_SHARDED else full[n].shape
        gbuf[n] = _wgrad(x_act, dy, f"{n}_grad", like, gbuf.get(n), layer, scale)

    n_groups = d // S5_GROUP
    a_re, a_im = s5_a_re[0], s5_a_im[0]
    log_dt = s5_log_dt[0].reshape(n_groups, 1)
    disc = _s5_disc(a_re, a_im, log_dt)
    nb = n_groups // S5_BLOCK_GROUPS
    ns = S5_BLOCK_GROUPS * S5_STATE
    par = jnp.concatenate([jnp.stack([q.reshape(nb, ns) for q in disc], axis=1), jnp.zeros((nb, 4, ns), F32)], axis=1)
    bre = _blockdiag(s5_b_re[0].transpose(0, 2, 1), S5_BLOCK_GROUPS).astype(BF16)
    bim = _blockdiag(s5_b_im[0].transpose(0, 2, 1), S5_BLOCK_GROUPS).astype(BF16)
    cre = _blockdiag(s5_c_re[0].transpose(0, 2, 1), S5_BLOCK_GROUPS).astype(BF16)
    cim = _blockdiag(s5_c_im[0].transpose(0, 2, 1), S5_BLOCK_GROUPS).astype(BF16)
    lc = min(SCAN_CHUNK, seq)

    h = x.reshape(t, d)
    saved = []
    for i in range(depth):
        s = {"h0": h}
        h, s["n1"], s["a1"], s["b1"] = _ffn_fwd(h, row(ffn1_norm[i]), wt("ffn1_w1", i), wt("ffn1_w3", i), wt("ffn1_w2", i))
        s["h1"] = h
        j = i // 2
        if i % 2 == 0:
            s["hn"], u = _norm_lin(h, row(mix_norm[i]), [wt("s5_w_in", j)], [F32], "s5_in")
            s["u"] = _interleave(u, lc)
            y, z, s["st"] = _s5_scan_fwd(s["u"], bre, bim, cre, cim, par, row(s5_d[j]), bl, seq, lc)
            s["y"], s["z"] = _deinterleave(y, lc), _deinterleave(z, lc)
            h = _s5_out(h, s["z"], wt("s5_w_glu", j))
        else:
            wqkv = [wt("sb_w_qkv", j, col, d) for col in range(3)]
            s["hn"], s["q"], s["k"], s["v"] = _norm_lin(h, row(mix_norm[i]), wqkv, [BF16] * 3, "sb_in")
            s["o"], s["tot"] = _sb_fwd(s["q"], s["k"], s["v"], bl, seq)
            h = _lin_res(h, s["o"], wt("sb_w_o", j), "sb_out")
        s["h2"] = h
        h, s["n2"], s["a2"], s["b2"] = _ffn_fwd(h, row(ffn2_norm[i]), wt("ffn2_w1", i), wt("ffn2_w3", i), wt("ffn2_w2", i))
        s["h3"] = h
        s["p"] = p[i].reshape(t, -1)
        h, s["npl"] = _ple_fwd(h, row(ple_norm[i]), wt("ple_gate", i), s["p"], wt("ple_proj", i))
        saved.append(s)

    loss_part, dh, g_final = _loss_head(h, row(final_norm), loss_target.reshape(t, d))
    loss = lax.psum(loss_part[0, 0], ("x", "y", "c"))

    grads = {n: [None] * wts[n].shape[0] for n in REPLICATED if n != "final_norm"}
    grads["final_norm"] = g_final.reshape(-1)

    def ffn_bwd(dh, h_in, n, a, b, which, i):
        da, db, sact = _ffn_bwd_down(dh, a, b, wt(f"{which}_w2", i))
        wgrad(f"{which}_w2", sact, dh, i, scale=0.5)
        dh_in, dg = _lin_bwd_norm([da, db], [wt(f"{which}_w1", i), wt(f"{which}_w3", i)], h_in,
                                  row(wts[f"{which}_norm"][i]), dh, f"{which}_bwd_up")
        wgrad(f"{which}_w1", n, da, i)
        wgrad(f"{which}_w3", n, db, i)
        grads[f"{which}_norm"][i] = dg.reshape(-1)
        return dh_in

    for i in reversed(range(depth)):
        s = saved[i]
        j = i // 2
        dh, dgl, dpp, dg = _ple_bwd(dh, s["h3"], row(ple_norm[i]), s["npl"], s["p"], wt("ple_gate", i), wt("ple_proj", i))
        grads["ple_norm"][i] = dg.reshape(-1)
        wgrad("ple_gate", s["npl"], dgl, i)
        wgrad("ple_proj", s["p"], dpp, i)
        dh = ffn_bwd(dh, s["h2"], s["n2"], s["a2"], s["b2"], "ffn2", i)
        if i % 2 == 0:
            dy, dzz = _s5_out_bwd(dh, s["z"], s["y"], wt("s5_w_glu", j))
            wgrad("s5_w_glu", s["z"], dzz, j)
            du, dbre, dbim, dcre, dcim, dpar, dd = _s5_scan_bwd(
                s["u"], _interleave(dy, lc), s["st"], bre, bim, cre, cim, par, row(s5_d[j]), bl, seq, lc)
            du = _deinterleave(du, lc)
            cts = [dpar[:, r, :].reshape(n_groups, S5_STATE) for r in range(4)]
            g_are, g_aim, g_ldt = _s5_disc_bwd(a_re, a_im, log_dt, cts)
            grads["s5_a_re"][j], grads["s5_a_im"][j], grads["s5_log_dt"][j] = g_are, g_aim, g_ldt.reshape(-1)
            grads["s5_b_re"][j] = _blockdiag_take(dbre, S5_BLOCK_GROUPS, S5_GROUP, S5_STATE).transpose(0, 2, 1)
            grads["s5_b_im"][j] = _blockdiag_take(dbim, S5_BLOCK_GROUPS, S5_GROUP, S5_STATE).transpose(0, 2, 1)
            grads["s5_c_re"][j] = _blockdiag_take(dcre, S5_BLOCK_GROUPS, S5_STATE, S5_GROUP).transpose(0, 2, 1)
            grads["s5_c_im"][j] = _blockdiag_take(dcim, S5_BLOCK_GROUPS, S5_STATE, S5_GROUP).transpose(0, 2, 1)
            grads["s5_d"][j] = dd.reshape(-1)
            dh, dg = _lin_bwd_norm([du], [wt("s5_w_in", j)], s["h1"], row(mix_norm[i]), dh, "s5_in_bwd")
            wgrad("s5_w_in", s["hn"], du, j)
        else:
            do = _lin_nt(dh, wt("sb_w_o", j), "sb_out_bwd")
            wgrad("sb_w_o", s["o"], dh, j)
            dqkv = _sb_bwd(s["q"], s["k"], s["v"], do, s["tot"], bl, seq)
            dh, dg = _lin_bwd_norm(dqkv, [wt("sb_w_qkv", j, col, d) for col in range(3)], s["h1"], row(mix_norm[i]), dh, "sb_in_bwd")
            wgrad("sb_w_qkv", s["hn"], jnp.concatenate(dqkv, axis=1), j)
        grads["mix_norm"][i] = dg.reshape(-1)
        dh = ffn_bwd(dh, s["h0"], s["n1"], s["a1"], s["b1"], "ffn1", i)
    grad_x = dh.reshape(x.shape)
    for n in REPLICATED:
        if n != "final_norm":
            grads[n] = jnp.stack(grads[n])

    got = _rs_sibling([gbuf[n] for n in SHARDED], [shard_shape[n] for n in SHARDED], by_cols)
    core = lax.axis_index("c").astype(jnp.int32).reshape(1)
    parts = _rs_chips([_rs_add(gbuf[n], g, core, n in COL_SHARDED, wts[n].shape[2], f"rs_add_{n}")
                       for n, g in zip(SHARDED, got)])
    res = {n: _adamw_shard(part, wts[n], mom[n], var[n], f"adamw_{n}") for n, part in zip(SHARDED, parts)}

    rep_shapes = [wts[n].shape for n in REPLICATED]
    rep_parts = _all_gather(_pack([grads[n].reshape(wts[n].shape) for n in REPLICATED]), "gather_small_grads")
    outs = _adamw(rep_parts, _pack([wts[n] for n in REPLICATED]), _pack([mom[n] for n in REPLICATED]),
                  _pack([var[n] for n in REPLICATED]), "adamw_replicated")
    res.update({n: vals for n, vals in zip(REPLICATED, zip(*[_unpack(o, rep_shapes) for o in outs]))})

    return (loss, grad_x, *[res[n][0] for n in WEIGHTS], *[res[n][1] for n in WEIGHTS],
            *[res[n][2] for n in WEIGHTS], *[res[n][3] for n in WEIGHTS])
```

```python
import math
from typing import NamedTuple, Optional

import jax
import jax.numpy as jnp
from jax import lax
from jax.experimental import pallas as pl
from jax.experimental.pallas import tpu as pltpu

F32 = jnp.float32
BF16 = jnp.bfloat16
MESH = pl.DeviceIdType.MESH

N_DEV = 8
N_CHIP = 4
EPS = 1e-6
S5_GROUP = 16
S5_STATE = 64
S5_BLOCK_GROUPS = 16
HEAD_DIM = 64
Q_BLOCK = 128
SB_Q_TILE = 256
SB_HEADS_PER_STEP = 4
SCAN_CHUNK = 512
SUBLANES = 8
LANES = 128
FLAT_W = 1024
ADAM_LR, ADAM_B1, ADAM_B2, ADAM_EPS, ADAM_WD, ADAM_STEP = 1e-3, 0.9, 0.999, 1e-8, 0.01, 10
V7X_VMEM_LIMIT = 56 * 1024 * 1024
ROW_TILE = 256
LIGHT_ROW_TILE = 512
LIGHT_WEIGHT_BYTES = 8 * 1024 * 1024
WGRAD_ROW_TILE = 1024
WGRAD_OUT_BYTES = 6 * 1024 * 1024
ELEMENTWISE_ROW_TILE = 512

SHARDED = ("ffn1_w1", "ffn1_w3", "ffn1_w2", "ffn2_w1", "ffn2_w3", "ffn2_w2", "ple_proj", "ple_gate",
           "s5_w_in", "s5_w_glu", "sb_w_qkv", "sb_w_o")
COL_SHARDED = ("ffn1_w1", "ffn1_w3", "ffn2_w1", "ffn2_w3", "ple_proj", "s5_w_glu", "sb_w_qkv")
FFN_COL = ("ffn1_w1", "ffn1_w3", "ffn2_w1", "ffn2_w3")
FFN_ROW = ("ffn1_w2", "ffn2_w2")
REPLICATED = ("ffn1_norm", "mix_norm", "ffn2_norm", "ple_norm", "s5_a_re", "s5_a_im", "s5_log_dt",
              "s5_b_re", "s5_b_im", "s5_c_re", "s5_c_im", "s5_d", "final_norm")
WEIGHTS = ("ffn1_norm", "ffn1_w1", "ffn1_w3", "ffn1_w2", "mix_norm", "ffn2_norm", "ffn2_w1", "ffn2_w3", "ffn2_w2",
           "ple_norm", "ple_proj", "ple_gate", "s5_w_in", "s5_a_re", "s5_a_im", "s5_log_dt", "s5_b_re", "s5_b_im",
           "s5_c_re", "s5_c_im", "s5_d", "s5_w_glu", "sb_w_qkv", "sb_w_o", "final_norm")


def _dot(a, b):
    return jnp.dot(a, b, preferred_element_type=F32)


def _dot_nt(a, b):
    return lax.dot_general(a, b, (((1,), (1,)), ((), ())), preferred_element_type=F32)


def _dot_tn(a, b):
    return lax.dot_general(a, b, (((0,), (0,)), ((), ())), preferred_element_type=F32)


def _dot2(x, m):
    hi = x.astype(BF16)
    lo = (x - hi.astype(F32)).astype(BF16)
    return _dot(hi, m) + _dot(lo, m)


def _rms(x):
    r = lax.rsqrt(jnp.mean(x * x, axis=-1, keepdims=True) + EPS)
    return x * r, r


def _rms_bwd(dn, xh, r, g):
    gy = dn * g
    return r * (gy - xh * jnp.mean(gy * xh, axis=-1, keepdims=True))


_GELU_C = math.sqrt(2.0 / math.pi)


def _gelu(x):
    return 0.5 * x * (1.0 + jnp.tanh(_GELU_C * (x + 0.044715 * x * x * x)))


def _gelu_grad(x):
    th = jnp.tanh(_GELU_C * (x + 0.044715 * x * x * x))
    return 0.5 * (1.0 + th) + 0.5 * x * (1.0 - th * th) * _GELU_C * (1.0 + 3.0 * 0.044715 * x * x)


def _neg_softplus(z):
    return -(jnp.maximum(z, 0.0) + jnp.log(1.0 + jnp.exp(-jnp.abs(z))))


def _tile(n, want, mult=SUBLANES):
    for t in range(min(want, n), 0, -1):
        if n % t == 0 and t % mult == 0:
            return t
    return n


def _rows(tm, c):
    return pl.BlockSpec((tm, c), lambda i: (i, 0))


def _whole(a):
    nd = a.ndim
    return pl.BlockSpec(a.shape, lambda i: (0,) * nd)


class _W(NamedTuple):
    arr: jax.Array
    layer: int = 0
    col: int = 0
    width: Optional[int] = None

    @property
    def shape(self):
        return self.arr.shape[1], self.width or self.arr.shape[2]


def _wspec(w):
    return pl.BlockSpec((None,) + w.shape, lambda *_: (w.layer, 0, w.col))


def _row_params(sem="parallel"):
    return pltpu.CompilerParams(dimension_semantics=(sem,), vmem_limit_bytes=V7X_VMEM_LIMIT)


def _ffn_fwd(h, g, w1, w3, w2):
    t, d = h.shape
    f = w1.shape[1]
    tm = _tile(t, ROW_TILE)

    def body(h_ref, g_ref, w1_ref, w3_ref, w2_ref, ho_ref, n_ref, a_ref, b_ref):
        x = h_ref[...]
        xh, _ = _rms(x)
        n = (xh * g_ref[...]).astype(BF16)
        a = _dot(n, w1_ref[...])
        b = _dot(n, w3_ref[...])
        s = (a * jax.nn.sigmoid(a) * b).astype(BF16)
        ho_ref[...] = x + 0.5 * _dot(s, w2_ref[...])
        n_ref[...] = n
        a_ref[...] = a.astype(BF16)
        b_ref[...] = b.astype(BF16)

    return pl.pallas_call(
        body, name="ffn_fwd", grid=(t // tm,),
        in_specs=[_rows(tm, d), _whole(g), _wspec(w1), _wspec(w3), _wspec(w2)],
        out_specs=[_rows(tm, d), _rows(tm, d), _rows(tm, f), _rows(tm, f)],
        out_shape=[jax.ShapeDtypeStruct((t, d), F32), jax.ShapeDtypeStruct((t, d), BF16),
                   jax.ShapeDtypeStruct((t, f), BF16), jax.ShapeDtypeStruct((t, f), BF16)],
        compiler_params=_row_params(),
    )(h, g, w1.arr, w3.arr, w2.arr)


def _ffn_bwd_down(dh, a, b, w2):
    t, d = dh.shape
    f = a.shape[1]
    tm = _tile(t, ROW_TILE)

    def body(dh_ref, a_ref, b_ref, w2_ref, da_ref, db_ref, s_ref):
        ds = _dot_nt((0.5 * dh_ref[...]).astype(BF16), w2_ref[...])
        a32 = a_ref[...].astype(F32)
        b32 = b_ref[...].astype(F32)
        sig = jax.nn.sigmoid(a32)
        sil = a32 * sig
        da_ref[...] = (ds * b32 * (sig * (1.0 + a32 * (1.0 - sig)))).astype(BF16)
        db_ref[...] = (ds * sil).astype(BF16)
        s_ref[...] = (sil * b32).astype(BF16)

    return pl.pallas_call(
        body, name="ffn_bwd_down", grid=(t // tm,),
        in_specs=[_rows(tm, d), _rows(tm, f), _rows(tm, f), _wspec(w2)],
        out_specs=[_rows(tm, f)] * 3,
        out_shape=[jax.ShapeDtypeStruct((t, f), BF16)] * 3,
        compiler_params=_row_params(),
    )(dh, a, b, w2.arr)


def _lin_bwd_norm(dys, ws, h, g, dh, name):
    t, d = h.shape
    heavy = sum(w.shape[0] * w.shape[1] for w in ws) * 2 > LIGHT_WEIGHT_BYTES
    tm = _tile(t, ROW_TILE if heavy else LIGHT_ROW_TILE)
    k = len(dys)

    def body(*refs):
        dy_refs, w_refs = refs[:k], refs[k:2 * k]
        h_ref, g_ref, dh_ref, o_ref, dg_ref = refs[2 * k:]
        dn = _dot_nt(dy_refs[0][...].astype(BF16), w_refs[0][...])
        for j in range(1, k):
            dn = dn + _dot_nt(dy_refs[j][...].astype(BF16), w_refs[j][...])
        xh, r = _rms(h_ref[...])
        o_ref[...] = dh_ref[...] + _rms_bwd(dn, xh, r, g_ref[...])

        @pl.when(pl.program_id(0) == 0)
        def _():
            dg_ref[...] = jnp.zeros_like(dg_ref)

        dg_ref[...] += jnp.sum(dn * xh, axis=0, keepdims=True)

    return pl.pallas_call(
        body, name=name, grid=(t // tm,),
        in_specs=[_rows(tm, dy.shape[1]) for dy in dys] + [_wspec(w) for w in ws] + [_rows(tm, d), _whole(g), _rows(tm, d)],
        out_specs=[_rows(tm, d), pl.BlockSpec((1, d), lambda i: (0, 0))],
        out_shape=[jax.ShapeDtypeStruct((t, d), F32), jax.ShapeDtypeStruct((1, d), F32)],
        compiler_params=_row_params("arbitrary"),
    )(*dys, *[w.arr for w in ws], h, g, dh)


def _wgrad(x, dy, name, like, into=None, layer=0, scale=1.0):
    t, kk = x.shape
    n = dy.shape[1]
    tm = _tile(t, WGRAD_ROW_TILE)
    steps = t // tm
    blocked = len(like) == 4
    if blocked:
        sc = like[3]
        nd = max(g for g in (1, 2, 4, 8) if g * sc * kk * 4 <= max(WGRAD_OUT_BYTES, sc * kk * 4))
        tn = nd * sc
        out_spec = pl.BlockSpec((nd, None, kk, sc), lambda j, i: (j, layer, 0, 0))
    else:
        tn = _tile(n, max(128, WGRAD_OUT_BYTES // (4 * kk)), 128)
        out_spec = pl.BlockSpec((None, kk, tn), lambda j, i: (layer, 0, j))

    def body(x_ref, dy_ref, *rest):
        o_ref = rest[-1]
        i = pl.program_id(1)

        @pl.when(i == 0)
        def _():
            o_ref[...] = jnp.zeros_like(o_ref)

        acc = _dot_tn(x_ref[...].astype(BF16), dy_ref[...].astype(BF16))
        if blocked:
            for dd in range(nd):
                o_ref[dd] += acc[:, dd * sc:(dd + 1) * sc]
        else:
            o_ref[...] += acc
        if scale != 1.0:
            @pl.when(i == steps - 1)
            def _():
                o_ref[...] = o_ref[...] * scale

    held = [] if into is None else [into]
    return pl.pallas_call(
        body, name=name, grid=(n // tn, steps),
        in_specs=[pl.BlockSpec((tm, kk), lambda j, i: (i, 0)), pl.BlockSpec((tm, tn), lambda j, i: (i, j))]
        + [pl.BlockSpec(memory_space=pl.ANY)] * len(held),
        out_specs=out_spec,
        out_shape=jax.ShapeDtypeStruct(like, F32),
        input_output_aliases={2: 0} if held else {},
        compiler_params=pltpu.CompilerParams(dimension_semantics=("parallel", "arbitrary"), vmem_limit_bytes=V7X_VMEM_LIMIT),
    )(x, dy, *held)


def _norm_lin(h, g, ws, out_dtypes, name):
    t, d = h.shape
    tm = _tile(t, LIGHT_ROW_TILE)
    k = len(ws)

    def body(*refs):
        h_ref, g_ref = refs[:2]
        w_refs = refs[2:2 + k]
        n_ref = refs[2 + k]
        o_refs = refs[3 + k:]
        xh, _ = _rms(h_ref[...])
        n = (xh * g_ref[...]).astype(BF16)
        n_ref[...] = n
        for w_ref, o_ref in zip(w_refs, o_refs):
            o_ref[...] = _dot(n, w_ref[...]).astype(o_ref.dtype)

    return pl.pallas_call(
        body, name=name, grid=(t // tm,),
        in_specs=[_rows(tm, d), _whole(g)] + [_wspec(w) for w in ws],
        out_specs=[_rows(tm, d)] + [_rows(tm, w.shape[1]) for w in ws],
        out_shape=[jax.ShapeDtypeStruct((t, d), BF16)] + [jax.ShapeDtypeStruct((t, w.shape[1]), dt) for w, dt in zip(ws, out_dtypes)],
        compiler_params=_row_params(),
    )(h, g, *[w.arr for w in ws])


def _lin_res(h, x, w, name):
    t, d = h.shape
    tm = _tile(t, LIGHT_ROW_TILE)

    def body(h_ref, x_ref, w_ref, o_ref):
        o_ref[...] = h_ref[...] + _dot(x_ref[...], w_ref[...])

    return pl.pallas_call(
        body, name=name, grid=(t // tm,),
        in_specs=[_rows(tm, d), _rows(tm, x.shape[1]), _wspec(w)],
        out_specs=_rows(tm, d), out_shape=jax.ShapeDtypeStruct((t, d), F32),
        compiler_params=_row_params(),
    )(h, x, w.arr)


def _lin_nt(dy, w, name):
    t = dy.shape[0]
    kk = w.shape[0]
    tm = _tile(t, LIGHT_ROW_TILE)

    def body(dy_ref, w_ref, o_ref):
        o_ref[...] = _dot_nt(dy_ref[...].astype(BF16), w_ref[...]).astype(BF16)

    return pl.pallas_call(
        body, name=name, grid=(t // tm,),
        in_specs=[_rows(tm, dy.shape[1]), _wspec(w)],
        out_specs=_rows(tm, kk), out_shape=jax.ShapeDtypeStruct((t, kk), BF16),
        compiler_params=_row_params(),
    )(dy, w.arr)


def _ple_fwd(h, g, wg, p, wp):
    t, d = h.shape
    tm = _tile(t, LIGHT_ROW_TILE)

    def body(h_ref, g_ref, wg_ref, p_ref, wp_ref, o_ref, n_ref):
        x = h_ref[...]
        xh, _ = _rms(x)
        n = (xh * g_ref[...]).astype(BF16)
        n_ref[...] = n
        gate = jax.nn.sigmoid(_dot(n, wg_ref[...]))
        o_ref[...] = x + _dot(p_ref[...].astype(BF16), wp_ref[...]) * gate

    return pl.pallas_call(
        body, name="ple_fwd", grid=(t // tm,),
        in_specs=[_rows(tm, d), _whole(g), _wspec(wg), _rows(tm, p.shape[1]), _wspec(wp)],
        out_specs=[_rows(tm, d), _rows(tm, d)],
        out_shape=[jax.ShapeDtypeStruct((t, d), F32), jax.ShapeDtypeStruct((t, d), BF16)],
        compiler_params=_row_params(),
    )(h, g, wg.arr, p, wp.arr)


def _ple_bwd(dh, h, g, n, p, wg, wp):
    t, d = h.shape
    tm = _tile(t, LIGHT_ROW_TILE)

    def body(dh_ref, h_ref, g_ref, n_ref, p_ref, wg_ref, wp_ref, o_ref, dgl_ref, dpp_ref, dg_ref):
        dh_v = dh_ref[...]
        gate = jax.nn.sigmoid(_dot(n_ref[...], wg_ref[...]))
        pp = _dot(p_ref[...].astype(BF16), wp_ref[...])
        dgl = (dh_v * pp * gate * (1.0 - gate)).astype(BF16)
        dgl_ref[...] = dgl
        dpp_ref[...] = (dh_v * gate).astype(BF16)
        dn = _dot_nt(dgl, wg_ref[...])
        xh, r = _rms(h_ref[...])
        o_ref[...] = dh_v + _rms_bwd(dn, xh, r, g_ref[...])

        @pl.when(pl.program_id(0) == 0)
        def _():
            dg_ref[...] = jnp.zeros_like(dg_ref)

        dg_ref[...] += jnp.sum(dn * xh, axis=0, keepdims=True)

    return pl.pallas_call(
        body, name="ple_bwd", grid=(t // tm,),
        in_specs=[_rows(tm, d), _rows(tm, d), _whole(g), _rows(tm, d), _rows(tm, p.shape[1]), _wspec(wg), _wspec(wp)],
        out_specs=[_rows(tm, d), _rows(tm, d), _rows(tm, d), pl.BlockSpec((1, d), lambda i: (0, 0))],
        out_shape=[jax.ShapeDtypeStruct((t, d), F32), jax.ShapeDtypeStruct((t, d), BF16),
                   jax.ShapeDtypeStruct((t, d), BF16), jax.ShapeDtypeStruct((1, d), F32)],
        compiler_params=_row_params("arbitrary"),
    )(dh, h, g, n, p, wg.arr, wp.arr)


def _loss_head(h, g, tgt):
    t, d = h.shape
    tm = _tile(t, LIGHT_ROW_TILE)

    def body(h_ref, g_ref, t_ref, l_ref, dh_ref, dg_ref):
        xh, r = _rms(h_ref[...])
        gg = g_ref[...]
        e = xh * gg - t_ref[...]
        dy = e * (1.0 / d)

        @pl.when(pl.program_id(0) == 0)
        def _():
            l_ref[...] = jnp.zeros_like(l_ref)
            dg_ref[...] = jnp.zeros_like(dg_ref)

        l_ref[...] += 0.5 * jnp.sum(jnp.mean(e * e, axis=-1, keepdims=True), axis=0, keepdims=True)
        dg_ref[...] += jnp.sum(dy * xh, axis=0, keepdims=True)
        dh_ref[...] = _rms_bwd(dy, xh, r, gg)

    return pl.pallas_call(
        body, name="loss_head", grid=(t // tm,),
        in_specs=[_rows(tm, d), _whole(g), _rows(tm, d)],
        out_specs=[pl.BlockSpec((1, 128), lambda i: (0, 0)), _rows(tm, d), pl.BlockSpec((1, d), lambda i: (0, 0))],
        out_shape=[jax.ShapeDtypeStruct((1, 128), F32), jax.ShapeDtypeStruct((t, d), F32), jax.ShapeDtypeStruct((1, d), F32)],
        compiler_params=_row_params("arbitrary"),
    )(h, g, tgt)


def _s5_disc_math(a_re, a_im, log_dt):
    lam_re = jnp.minimum(a_re, -1e-4)
    lam_im = a_im
    dt = jnp.exp(log_dt)
    mag = jnp.exp(lam_re * dt)
    abar_re = mag * jnp.cos(lam_im * dt)
    abar_im = mag * jnp.sin(lam_im * dt)
    den = lam_re * lam_re + lam_im * lam_im
    nr = abar_re - 1.0
    ni = abar_im
    return abar_re, abar_im, (nr * lam_re + ni * lam_im) / den, (ni * lam_re - nr * lam_im) / den


def _s5_disc(a_re, a_im, log_dt):
    gp = jax.ShapeDtypeStruct(a_re.shape, F32)

    def body(ar_ref, ai_ref, ld_ref, o0, o1, o2, o3):
        outs = _s5_disc_math(ar_ref[...], ai_ref[...], ld_ref[...])
        for o_ref, val in zip((o0, o1, o2, o3), outs):
            o_ref[...] = val

    return pl.pallas_call(body, name="s5_disc", out_shape=[gp] * 4)(a_re, a_im, log_dt)


def _s5_disc_bwd(a_re, a_im, log_dt, cts):
    def body(ar_ref, ai_ref, ld_ref, c0, c1, c2, c3, dar_ref, dai_ref, dld_ref):
        _, vjp = jax.vjp(_s5_disc_math, ar_ref[...], ai_ref[...], ld_ref[...])
        dar, dai, dld = vjp((c0[...], c1[...], c2[...], c3[...]))
        dar_ref[...] = dar
        dai_ref[...] = dai
        dld_ref[...] = dld

    return pl.pallas_call(
        body, name="s5_disc_bwd",
        out_shape=[jax.ShapeDtypeStruct(a_re.shape, F32), jax.ShapeDtypeStruct(a_im.shape, F32),
                   jax.ShapeDtypeStruct(log_dt.shape, F32)],
    )(a_re, a_im, log_dt, *cts)


def _blockdiag(w, gl):
    g, r, c = w.shape
    w = w.reshape(g // gl, gl, r, c)
    eye = jnp.eye(gl, dtype=w.dtype)
    return (w[:, :, :, None, :] * eye[None, :, None, :, None]).reshape(g // gl, gl * r, gl * c)


def _blockdiag_take(m, gl, r, c):
    nb = m.shape[0]
    own = jnp.eye(gl, dtype=bool)[None, :, None, :, None]
    return jnp.where(own, m.reshape(nb, gl, r, gl, c), 0.0).sum(axis=3).reshape(nb * gl, r, c)


def _cmul(ar, ai, br, bi):
    return ar * br - ai * bi, ar * bi + ai * br


def _powers(ar, ai, n):
    out = [(ar, ai)]
    for _ in range(n - 1):
        out.append(_cmul(ar, ai, *out[-1]))
    return out


def _interleave(a, lc):
    t, d = a.shape
    return a.reshape(t // lc, SUBLANES, lc // SUBLANES, d).transpose(0, 2, 1, 3).reshape(t, d)


def _deinterleave(a, lc):
    t, d = a.shape
    return a.reshape(t // lc, lc // SUBLANES, SUBLANES, d).transpose(0, 2, 1, 3).reshape(t, d)


def _seg_rows(k):
    return slice(k * SUBLANES, (k + 1) * SUBLANES)


def _seg_scan(re_ref, im_ref, ar, ai, seg, reverse=False):
    cr = ci = jnp.zeros((SUBLANES, re_ref.shape[1]), F32)
    for k in (reversed(range(seg)) if reverse else range(seg)):
        rows = _seg_rows(k)
        cr, ci = ar * cr - ai * ci + re_ref[rows, :], ar * ci + ai * cr + im_ref[rows, :]
        re_ref[rows, :] = cr
        im_ref[rows, :] = ci
    return cr, ci


def _seg_entering(er, ei, pr, pi, c0r, c0i, reverse=False):
    rows = lax.broadcasted_iota(jnp.int32, er.shape, 0)
    vr = vi = jnp.zeros_like(er)
    cr, ci = c0r, c0i
    for j in (reversed(range(SUBLANES)) if reverse else range(SUBLANES)):
        vr = jnp.where(rows == j, cr, vr)
        vi = jnp.where(rows == j, ci, vi)
        cr, ci = er[j:j + 1, :] + pr * cr - pi * ci, ei[j:j + 1, :] + pr * ci + pi * cr
    return vr, vi, cr, ci


def _s5_scan_fwd(u, bre, bim, cre, cim, par, dskip, bl, seq, lc):
    t, d = u.shape
    nb, gw, ns = bre.shape
    nc = seq // lc
    seg = lc // SUBLANES

    def body(u_ref, bre_ref, bim_ref, cre_ref, cim_ref, par_ref, d_ref, y_ref, z_ref, st_ref, carry, sre, sim):
        @pl.when(pl.program_id(2) == 0)
        def _():
            carry[...] = jnp.zeros_like(carry)

        st_ref[...] = carry[...]
        uu = u_ref[...]
        ug = uu.astype(BF16)
        wre = _dot(ug, bre_ref[...])
        wim = _dot(ug, bim_ref[...])
        ar, ai = par_ref[0:1, :], par_ref[1:2, :]
        fr, fi = par_ref[2:3, :], par_ref[3:4, :]
        sre[...] = fr * wre - fi * wim
        sim[...] = fr * wim + fi * wre
        pows = _powers(ar, ai, seg)
        er, ei = _seg_scan(sre, sim, ar, ai, seg)
        vr, vi, cr, ci = _seg_entering(er, ei, *pows[-1], carry[0:1, :], carry[1:2, :])
        carry[0:1, :] = cr
        carry[1:2, :] = ci
        for k in range(seg):
            rows = _seg_rows(k)
            fr_k, fi_k = _cmul(*pows[k], vr, vi)
            sre[rows, :] += fr_k
            sim[rows, :] += fi_k
        y = _dot(sre[...].astype(BF16), cre_ref[...]) - _dot(sim[...].astype(BF16), cim_ref[...]) + d_ref[...] * uu
        y_ref[...] = y
        z_ref[...] = _gelu(y).astype(BF16)

    tok = pl.BlockSpec((lc, gw), lambda g, b, c: (b * nc + c, g))
    mat_b = pl.BlockSpec((None, gw, ns), lambda g, b, c: (g, 0, 0))
    mat_c = pl.BlockSpec((None, ns, gw), lambda g, b, c: (g, 0, 0))
    return pl.pallas_call(
        body, name="s5_scan_fwd", grid=(nb, bl, nc),
        in_specs=[tok, mat_b, mat_b, mat_c, mat_c, pl.BlockSpec((None, 8, ns), lambda g, b, c: (g, 0, 0)),
                  pl.BlockSpec((1, gw), lambda g, b, c: (0, g))],
        out_specs=[tok, tok, pl.BlockSpec((None, None, 2, ns), lambda g, b, c: (g, b * nc + c, 0, 0))],
        out_shape=[jax.ShapeDtypeStruct((t, d), F32), jax.ShapeDtypeStruct((t, d), BF16),
                   jax.ShapeDtypeStruct((nb, bl * nc, 2, ns), F32)],
        scratch_shapes=[pltpu.VMEM((2, ns), F32), pltpu.VMEM((lc, ns), F32), pltpu.VMEM((lc, ns), F32)],
        compiler_params=pltpu.CompilerParams(dimension_semantics=("parallel", "arbitrary", "arbitrary"),
                                             vmem_limit_bytes=V7X_VMEM_LIMIT),
    )(u, bre, bim, cre, cim, par, dskip)


def _s5_scan_bwd(u, dy, st, bre, bim, cre, cim, par, dskip, bl, seq, lc):
    t, d = u.shape
    nb, gw, ns = bre.shape
    nc = seq // lc
    seg = lc // SUBLANES

    def body(u_ref, dy_ref, st_ref, bre_ref, bim_ref, cre_ref, cim_ref, par_ref, d_ref,
             du_ref, dbre_ref, dbim_ref, dcre_ref, dcim_ref, dpar_ref, dd_ref,
             lcarry, sre, sim, wre_s, wim_s, lre, lim):
        b, c = pl.program_id(1), pl.program_id(2)

        @pl.when((b == 0) & (c == 0))
        def _():
            for ref in (dbre_ref, dbim_ref, dcre_ref, dcim_ref, dpar_ref, dd_ref):
                ref[...] = jnp.zeros_like(ref)

        @pl.when(c == 0)
        def _():
            lcarry[...] = jnp.zeros_like(lcarry)

        uu = u_ref[...]
        ug = uu.astype(BF16)
        dyv = dy_ref[...]
        dyb = dyv.astype(BF16)
        ar, ai = par_ref[0:1, :], par_ref[1:2, :]
        fr, fi = par_ref[2:3, :], par_ref[3:4, :]
        wre = _dot(ug, bre_ref[...])
        wim = _dot(ug, bim_ref[...])
        wre_s[...] = wre
        wim_s[...] = wim
        sre[...] = fr * wre - fi * wim
        sim[...] = fr * wim + fi * wre

        pows = _powers(ar, ai, seg)
        er, ei = _seg_scan(sre, sim, ar, ai, seg)
        svr, svi, _, _ = _seg_entering(er, ei, *pows[-1], st_ref[0:1, :], st_ref[1:2, :])
        for k in range(seg):
            rows = _seg_rows(k)
            fr_k, fi_k = _cmul(*pows[k], svr, svi)
            sre[rows, :] += fr_k
            sim[rows, :] += fi_k

        lre[...] = _dot_nt(dyb, cre_ref[...])
        lim[...] = -_dot_nt(dyb, cim_ref[...])
        er, ei = _seg_scan(lre, lim, ar, -ai, seg, reverse=True)
        lvr, lvi, cr, ci = _seg_entering(er, ei, pows[-1][0], -pows[-1][1], lcarry[0:1, :], lcarry[1:2, :], reverse=True)
        lcarry[0:1, :] = cr
        lcarry[1:2, :] = ci
        dar = dai = jnp.zeros_like(er)
        for k in range(seg):
            rows = _seg_rows(k)
            pr, pi = pows[seg - 1 - k]
            fr_k, fi_k = _cmul(pr, -pi, lvr, lvi)
            lr = lre[rows, :] + fr_k
            li = lim[rows, :] + fi_k
            lre[rows, :] = lr
            lim[rows, :] = li
            spr, spi = (svr, svi) if k == 0 else (sre[_seg_rows(k - 1), :], sim[_seg_rows(k - 1), :])
            dar = dar + lr * spr + li * spi
            dai = dai + li * spr - lr * spi

        lr, li = lre[...], lim[...]
        wr, wi = wre_s[...], wim_s[...]
        dpar_ref[0:1, :] += jnp.sum(dar, axis=0, keepdims=True)
        dpar_ref[1:2, :] += jnp.sum(dai, axis=0, keepdims=True)
        dpar_ref[2:3, :] += jnp.sum(lr * wr + li * wi, axis=0, keepdims=True)
        dpar_ref[3:4, :] += jnp.sum(li * wr - lr * wi, axis=0, keepdims=True)
        dwr = (fr * lr + fi * li).astype(BF16)
        dwi = (fr * li - fi * lr).astype(BF16)
        dsk = d_ref[...]
        du_ref[...] = _dot_nt(dwr, bre_ref[...]) + _dot_nt(dwi, bim_ref[...]) + dsk * dyv
        dd_ref[...] += jnp.sum(dyv * uu, axis=0, keepdims=True)
        dbre_ref[...] += _dot_tn(ug, dwr)
        dbim_ref[...] += _dot_tn(ug, dwi)
        dcre_ref[...] += _dot_tn(sre[...].astype(BF16), dyb)
        dcim_ref[...] -= _dot_tn(sim[...].astype(BF16), dyb)

    tok = pl.BlockSpec((lc, gw), lambda g, b, c: (b * nc + nc - 1 - c, g))
    mat_b = pl.BlockSpec((None, gw, ns), lambda g, b, c: (g, 0, 0))
    mat_c = pl.BlockSpec((None, ns, gw), lambda g, b, c: (g, 0, 0))
    rows8 = pl.BlockSpec((None, 8, ns), lambda g, b, c: (g, 0, 0))
    dvec = pl.BlockSpec((1, gw), lambda g, b, c: (0, g))
    tile = pltpu.VMEM((lc, ns), F32)
    return pl.pallas_call(
        body, name="s5_scan_bwd", grid=(nb, bl, nc),
        in_specs=[tok, tok, pl.BlockSpec((None, None, 2, ns), lambda g, b, c: (g, b * nc + nc - 1 - c, 0, 0)),
                  mat_b, mat_b, mat_c, mat_c, rows8, dvec],
        out_specs=[tok, mat_b, mat_b, mat_c, mat_c, rows8, dvec],
        out_shape=[jax.ShapeDtypeStruct((t, d), F32),
                   jax.ShapeDtypeStruct((nb, gw, ns), F32), jax.ShapeDtypeStruct((nb, gw, ns), F32),
                   jax.ShapeDtypeStruct((nb, ns, gw), F32), jax.ShapeDtypeStruct((nb, ns, gw), F32),
                   jax.ShapeDtypeStruct((nb, 8, ns), F32), jax.ShapeDtypeStruct((1, d), F32)],
        scratch_shapes=[pltpu.VMEM((2, ns), F32)] + [tile] * 6,
        compiler_params=pltpu.CompilerParams(dimension_semantics=("arbitrary", "arbitrary", "arbitrary"),
                                             vmem_limit_bytes=V7X_VMEM_LIMIT),
    )(u, dy, st, bre, bim, cre, cim, par, dskip)


def _s5_out(h, z, wglu):
    t, d = h.shape
    tm = _tile(t, LIGHT_ROW_TILE)

    def body(h_ref, z_ref, w_ref, o_ref):
        zz = _dot(z_ref[...], w_ref[...])
        o_ref[...] = h_ref[...] + zz[:, :d] * jax.nn.sigmoid(zz[:, d:])

    return pl.pallas_call(
        body, name="s5_out", grid=(t // tm,),
        in_specs=[_rows(tm, d), _rows(tm, d), _wspec(wglu)],
        out_specs=_rows(tm, d), out_shape=jax.ShapeDtypeStruct((t, d), F32),
        compiler_params=_row_params(),
    )(h, z, wglu.arr)


def _s5_out_bwd(dh, z, y, wglu):
    t, d = dh.shape
    tm = _tile(t, LIGHT_ROW_TILE)

    def body(dh_ref, z_ref, y_ref, w_ref, dy_ref, dzz_ref):
        zz = _dot(z_ref[...], w_ref[...])
        out, sg = zz[:, :d], jax.nn.sigmoid(zz[:, d:])
        dh_v = dh_ref[...]
        dzz = jnp.concatenate([dh_v * sg, dh_v * out * sg * (1.0 - sg)], axis=1).astype(BF16)
        dzz_ref[...] = dzz
        dy_ref[...] = _dot_nt(dzz, w_ref[...]) * _gelu_grad(y_ref[...])

    return pl.pallas_call(
        body, name="s5_out_bwd", grid=(t // tm,),
        in_specs=[_rows(tm, d), _rows(tm, d), _rows(tm, d), _wspec(wglu)],
        out_specs=[_rows(tm, d), _rows(tm, 2 * d)],
        out_shape=[jax.ShapeDtypeStruct((t, d), F32), jax.ShapeDtypeStruct((t, 2 * d), BF16)],
        compiler_params=_row_params(),
    )(dh, z, y, wglu.arr)


def _head_spec(seq, w):
    return pl.BlockSpec((None, SB_HEADS_PER_STEP, seq, w), lambda b, h: (b, h, 0, 0))


def _pair_spec(seq):
    return pl.BlockSpec((seq, SB_HEADS_PER_STEP * HEAD_DIM), lambda b, g: (b, g))


def _own_lanes(a, hh):
    lane = lax.broadcasted_iota(jnp.int32, a.shape, 1)
    return jnp.where((lane < HEAD_DIM) == (hh % 2 == 0), a, jnp.zeros_like(a))


def _pick_lanes(even, odd):
    lane = lax.broadcasted_iota(jnp.int32, even.shape, 1)
    return jnp.where(lane < HEAD_DIM, even, odd)


def _pair_cols(hh):
    return slice((hh // 2) * LANES, (hh // 2 + 1) * LANES)


def _tri_and_ones(kind):
    row = lax.broadcasted_iota(jnp.int32, (Q_BLOCK, Q_BLOCK), 0)
    col = lax.broadcasted_iota(jnp.int32, (Q_BLOCK, Q_BLOCK), 1)
    tri = {"after": row > col, "upto": row <= col, "before": row < col}[kind]
    return jnp.concatenate([tri.astype(BF16), jnp.ones((Q_BLOCK, Q_BLOCK), BF16)], axis=1)


def _sb_fwd(q, k, v, bsz, seq):
    dh = HEAD_DIM
    nh = q.shape[1] // dh
    tq = min(SB_Q_TILE, seq // 2)
    nq, nsub = seq // tq, tq // Q_BLOCK
    scale = dh ** -0.5
    hp = SB_HEADS_PER_STEP
    qb = Q_BLOCK

    def body(q_ref, k_ref, v_ref, o_ref, tot_ref):
        strict = lax.broadcasted_iota(jnp.int32, (qb, qb), 1) < lax.broadcasted_iota(jnp.int32, (qb, qb), 0)
        sums = _tri_and_ones("after")

        def sweep(qs, c0, units, carry):
            heads = sorted({u[0] for u in units})
            kbs = {hh: k_ref[pl.ds(c0, qb), _pair_cols(hh)] for hh in heads}
            vbs = {hh: v_ref[pl.ds(c0, qb), _pair_cols(hh)] for hh in heads}
            zs = [_dot_nt(qs[hh][s], kbs[hh]) * scale for hh, s, _ in units]
            lkrs = [_neg_softplus(z) for z in zs]
            lks = [jnp.where(strict, lkr, 0.0) if dg else lkr for lkr, (_, _, dg) in zip(lkrs, units)]
            css = [_dot2(lk, sums) for lk in lks]
            carry = dict(carry)
            for (hh, s, dg), z, lkr, cs in zip(units, zs, lkrs, css):
                acc, run = carry[hh, s]
                att = jnp.exp(z + lkr + cs[:, :qb] + run)
                if dg:
                    att = jnp.where(strict, att, 0.0)
                carry[hh, s] = (acc + _dot(att.astype(BF16), vbs[hh]), run + cs[:, qb:])
            return carry

        def q_loop(qi, _):
            r0 = qi * tq
            qs = {hh: [_own_lanes(q_ref[pl.ds(pl.multiple_of(r0 + s * qb, qb), qb), _pair_cols(hh)], hh)
                       for s in range(nsub)] for hh in range(hp)}
            carry = {(hh, s): (jnp.zeros((qb, LANES), F32), jnp.zeros((qb, qb), F32)) for hh in range(hp) for s in range(nsub)}
            for jj in reversed(range(nsub)):
                units = [(hh, s, s == jj) for hh in range(hp) for s in range(jj, nsub)]
                carry = sweep(qs, pl.multiple_of(r0 + jj * qb, qb), units, carry)
            units = [(hh, s, False) for hh in range(hp) for s in range(nsub)]
            carry = lax.fori_loop(
                0, nsub * qi, lambda t, c: sweep(qs, pl.multiple_of((nsub * qi - 1 - t) * qb, qb), units, c), carry)
            for s in range(nsub):
                rows = pl.ds(pl.multiple_of(r0 + s * qb, qb), qb)
                for hh in range(0, hp, 2):
                    o_ref[rows, _pair_cols(hh)] = _pick_lanes(carry[hh, s][0], carry[hh + 1, s][0]).astype(BF16)
                for hh in range(hp):
                    tot_ref[hh, rows, :] = carry[hh, s][1][:, 0:1]
            return 0

        lax.fori_loop(0, nq, q_loop, 0)

    return pl.pallas_call(
        body, name="sb_fwd", grid=(bsz, nh // hp),
        in_specs=[_pair_spec(seq)] * 3,
        out_specs=[_pair_spec(seq), _head_spec(seq, 1)],
        out_shape=[jax.ShapeDtypeStruct(q.shape, BF16), jax.ShapeDtypeStruct((bsz, nh, seq, 1), F32)],
        compiler_params=pltpu.CompilerParams(dimension_semantics=("parallel", "parallel"), vmem_limit_bytes=V7X_VMEM_LIMIT),
    )(q, k, v)


def _sb_bwd(q, k, v, do, tot, bsz, seq):
    dh = HEAD_DIM
    nh = q.shape[1] // dh
    tq = min(SB_Q_TILE, seq // 2)
    nq, nsub = seq // tq, tq // Q_BLOCK
    scale = dh ** -0.5
    hp = SB_HEADS_PER_STEP
    qb = Q_BLOCK

    def body(q_ref, k_ref, v_ref, do_ref, tot_ref, dq_ref, dk_ref, dv_ref, dka, dva):
        dka[...] = jnp.zeros_like(dka)
        dva[...] = jnp.zeros_like(dva)
        strict = lax.broadcasted_iota(jnp.int32, (qb, qb), 1) < lax.broadcasted_iota(jnp.int32, (qb, qb), 0)
        upto = _tri_and_ones("upto")
        before = _tri_and_ones("before")

        def sweep(qf, dof, tots, c0, first, units, carry):
            heads = sorted({u[0] for u in units})
            kbs = {hh: k_ref[pl.ds(c0, qb), _pair_cols(hh)] for hh in heads}
            vbs = {hh: v_ref[pl.ds(c0, qb), _pair_cols(hh)] for hh in heads}
            sub = lambda a, s: a[s * qb:(s + 1) * qb, :]
            zs = [_dot_nt(sub(qf[hh], s), kbs[hh]) * scale for hh, s, _ in units]
            das = [_dot_nt(sub(dof[hh], s), vbs[hh]) for hh, s, _ in units]
            lkrs = [_neg_softplus(z) for z in zs]
            lks = [jnp.where(strict, lkr, 0.0) if dg else lkr for lkr, (_, _, dg) in zip(lkrs, units)]
            css = [_dot2(lk, upto) for lk in lks]
            lsigs, atts, gls = [], [], []
            for (hh, s, dg), z, lkr, cs, da in zip(units, zs, lkrs, css, das):
                lsig = z + lkr
                att = jnp.exp(lsig + (tots[hh, s] - (cs[:, :qb] + carry[hh, s][1])))
                if dg:
                    att = jnp.where(strict, att, 0.0)
                lsigs.append(lsig)
                atts.append(att)
                gls.append(da * att)
            gss = [_dot2(gl, before) for gl in gls]
            carry = dict(carry)
            dzs = {}
            for (hh, s, dg), lsig, gl, cs, gs in zip(units, lsigs, gls, css, gss):
                dqa, pre, gpre = carry[hh, s]
                sig = jnp.exp(lsig)
                dz = gl * (1.0 - sig) - (gs[:, :qb] + gpre) * sig
                if dg:
                    dz = jnp.where(strict, dz, 0.0)
                dz = (dz * scale).astype(BF16)
                dzs[hh, s] = dz
                carry[hh, s] = (dqa + _dot(dz, kbs[hh]), pre + cs[:, qb:], gpre + gs[:, qb:])
            att_of = {(hh, s): a for (hh, s, _), a in zip(units, atts)}
            for hh in heads:
                dzc = jnp.concatenate([dzs[hh, s] for s in range(first, nsub)], axis=0)
                attc = jnp.concatenate([att_of[hh, s].astype(BF16) for s in range(first, nsub)], axis=0)
                dka[pl.ds(c0, qb), _pair_cols(hh)] += _dot_tn(dzc, qf[hh][first * qb:, :])
                dva[pl.ds(c0, qb), _pair_cols(hh)] += _dot_tn(attc, dof[hh][first * qb:, :])
            return carry

        def q_loop(qi, _):
            r0 = pl.multiple_of(qi * tq, tq)
            qf = [_own_lanes(q_ref[pl.ds(r0, tq), _pair_cols(hh)], hh) for hh in range(hp)]
            dof = [_own_lanes(do_ref[pl.ds(r0, tq), _pair_cols(hh)], hh) for hh in range(hp)]
            tots = {(hh, s): jnp.broadcast_to(tot_ref[hh, pl.ds(pl.multiple_of(r0 + s * qb, qb), qb), :], (qb, qb))
                    for hh in range(hp) for s in range(nsub)}
            carry = {(hh, s): (jnp.zeros((qb, LANES), F32), jnp.zeros((qb, qb), F32), jnp.zeros((qb, qb), F32))
                     for hh in range(hp) for s in range(nsub)}
            units = [(hh, s, False) for hh in range(hp) for s in range(nsub)]
            carry = lax.fori_loop(
                0, nsub * qi, lambda kj, c: sweep(qf, dof, tots, pl.multiple_of(kj * qb, qb), 0, units, c), carry)
            for jj in range(nsub):
                units = [(hh, s, s == jj) for hh in range(hp) for s in range(jj, nsub)]
                carry = sweep(qf, dof, tots, pl.multiple_of(r0 + jj * qb, qb), jj, units, carry)
            for s in range(nsub):
                rows = pl.ds(pl.multiple_of(r0 + s * qb, qb), qb)
                for hh in range(0, hp, 2):
                    dq_ref[rows, _pair_cols(hh)] = _pick_lanes(carry[hh, s][0], carry[hh + 1, s][0]).astype(BF16)
            return 0

        lax.fori_loop(0, nq, q_loop, 0)
        dk_ref[...] = dka[...].astype(BF16)
        dv_ref[...] = dva[...].astype(BF16)

    ps = _pair_spec(seq)
    return pl.pallas_call(
        body, name="sb_bwd", grid=(bsz, nh // hp),
        in_specs=[ps, ps, ps, ps, _head_spec(seq, 1)],
        out_specs=[ps, ps, ps],
        out_shape=[jax.ShapeDtypeStruct(q.shape, BF16)] * 3,
        scratch_shapes=[pltpu.VMEM((seq, hp * dh), F32), pltpu.VMEM((seq, hp * dh), F32)],
        compiler_params=pltpu.CompilerParams(dimension_semantics=("parallel", "parallel"), vmem_limit_bytes=V7X_VMEM_LIMIT),
    )(q, k, v, do, tot)


def _coords():
    return lax.axis_index("x"), lax.axis_index("y"), lax.axis_index("c")


def _all_gather(x, name):
    r, c = x.shape

    def body(x_ref, out_ref, send_sems, recv_sems, local_sem):
        mx, my, mc = _coords()
        me, sibling = (mx, my, mc), (mx, my, 1 - mc)
        chips = [(1 - mx, my), (mx, 1 - my), (1 - mx, 1 - my)]

        def blk(px, py, pc):
            return out_ref.at[4 * px + 2 * py + pc]

        def copy(k, block, to, src=None):
            return pltpu.make_async_remote_copy(
                src_ref=blk(*block) if src is None else src, dst_ref=blk(*block),
                send_sem=send_sems.at[k], recv_sem=recv_sems.at[k], device_id=to, device_id_type=MESH)

        mine = pltpu.make_async_copy(x_ref, blk(*me), local_sem)
        mine.start()
        first = [copy(0, me, sibling, src=x_ref)]
        first += [copy(1 + j, me, (*chip, mc), src=x_ref) for j, chip in enumerate(chips)]
        for cp in first:
            cp.start()
        passed = [copy(4 + j, (*chip, mc), sibling) for j, chip in enumerate(chips)]
        for j, chip in enumerate(chips):
            copy(1 + j, (*chip, mc), me).wait_recv()
            passed[j].start()
        copy(0, sibling, me).wait_recv()
        for j, chip in enumerate(chips):
            copy(4 + j, (*chip, 1 - mc), me).wait_recv()
        for cp in first + passed:
            cp.wait_send()
        mine.wait()

    return pl.pallas_call(
        body, name=name,
        out_shape=jax.ShapeDtypeStruct((N_DEV, r, c), x.dtype),
        in_specs=[pl.BlockSpec(memory_space=pl.ANY)],
        out_specs=pl.BlockSpec(memory_space=pl.ANY),
        scratch_shapes=[pltpu.SemaphoreType.DMA((7,)), pltpu.SemaphoreType.DMA((7,)), pltpu.SemaphoreType.DMA],
    )(x)


_ANY = pl.BlockSpec(memory_space=pl.ANY)


def _window(ref, shard_shape, by_cols, dev, half=None):
    rows = shard_shape[1]
    r0, nr = (0, rows) if half is None else (half * (rows // 2), rows // 2)
    if by_cols:
        n = shard_shape[2]
        return ref.at[:, pl.ds(r0, nr), pl.ds(pl.multiple_of(dev * n, n), n)]
    return ref.at[:, pl.ds(pl.multiple_of(dev * rows + r0, 2 * SUBLANES), nr), :]


_TO_SIBLING, _TO_X, _TO_Y, _X_TO_SIBLING, _Y_TO_SIBLING, _DIAG0_TO_SIBLING, _DIAG1_TO_SIBLING, _X_HALF_TO_Y, _Y_HALF_TO_X = range(9)


def _gather_weights(shards, by_cols):
    na = len(shards)
    fulls = [jax.ShapeDtypeStruct(
        (s.shape[0], s.shape[1], s.shape[2] * N_DEV) if c else (s.shape[0], s.shape[1] * N_DEV, s.shape[2]), s.dtype)
        for s, c in zip(shards, by_cols)]

    def body(*refs):
        x_refs, out_refs = refs[:na], refs[na:2 * na]
        send_sems, recv_sems, local_sems = refs[2 * na:]
        mx, my, mc = _coords()
        me, sibling = (mx, my, mc), (mx, my, 1 - mc)
        xn, yn, diag = (1 - mx, my, mc), (mx, 1 - my, mc), (1 - mx, 1 - my, mc)

        def blk(a, dev, half=None):
            px, py, pc = dev
            return _window(out_refs[a], x_refs[a].shape, by_cols[a], 4 * px + 2 * py + pc, half)

        def copy(a, k, block, to, half=None, src=None):
            return pltpu.make_async_remote_copy(
                src_ref=blk(a, block, half) if src is None else src, dst_ref=blk(a, block, half),
                send_sem=send_sems.at[a, k], recv_sem=recv_sems.at[a, k], device_id=to, device_id_type=MESH)

        def other_core(dev):
            return dev[0], dev[1], 1 - dev[2]

        sends = []
        for a in range(na):
            mine = pltpu.make_async_copy(x_refs[a], blk(a, me), local_sems.at[a])
            mine.start()
            first = [copy(a, _TO_SIBLING, me, sibling, src=x_refs[a]), copy(a, _TO_X, me, xn, src=x_refs[a]),
                     copy(a, _TO_Y, me, yn, src=x_refs[a])]
            for cp in first:
                cp.start()
            sends += first
        for a in range(na):
            copy(a, _TO_X, xn, me).wait_recv()
            copy(a, _TO_Y, yn, me).wait_recv()
            passed = [copy(a, _X_TO_SIBLING, xn, sibling), copy(a, _X_HALF_TO_Y, xn, yn, half=0),
                      copy(a, _Y_TO_SIBLING, yn, sibling), copy(a, _Y_HALF_TO_X, yn, xn, half=1)]
            for cp in passed:
                cp.start()
            sends += passed
        for a in range(na):
            copy(a, _X_HALF_TO_Y, diag, me, half=0).wait_recv()
            copy(a, _Y_HALF_TO_X, diag, me, half=1).wait_recv()
            passed = [copy(a, _DIAG0_TO_SIBLING, diag, sibling, half=0), copy(a, _DIAG1_TO_SIBLING, diag, sibling, half=1)]
            for cp in passed:
                cp.start()
            sends += passed
        for a in range(na):
            copy(a, _TO_SIBLING, sibling, me).wait_recv()
            copy(a, _X_TO_SIBLING, other_core(xn), me).wait_recv()
            copy(a, _Y_TO_SIBLING, other_core(yn), me).wait_recv()
            copy(a, _DIAG0_TO_SIBLING, other_core(diag), me, half=0).wait_recv()
            copy(a, _DIAG1_TO_SIBLING, other_core(diag), me, half=1).wait_recv()
        for cp in sends:
            cp.wait_send()
        for a in range(na):
            pltpu.make_async_copy(x_refs[a], blk(a, me), local_sems.at[a]).wait()

    nsem = _Y_HALF_TO_X + 1
    return pl.pallas_call(
        body, name="gather_weights", out_shape=fulls, in_specs=[_ANY] * na, out_specs=[_ANY] * na,
        scratch_shapes=[pltpu.SemaphoreType.DMA((na, nsem)), pltpu.SemaphoreType.DMA((na, nsem)), pltpu.SemaphoreType.DMA((na,))],
    )(*shards)


def _rs_sibling(grads, shard_shapes, by_cols):
    na = len(grads)
    slabs = [jax.ShapeDtypeStruct((N_CHIP,) + tuple(s), F32) for s in shard_shapes]

    def body(*refs):
        g_refs, got_refs = refs[:na], refs[na:2 * na]
        send_sems, recv_sems = refs[2 * na:]
        mx, my, mc = _coords()
        copies = []
        for a in range(na):
            for q in range(N_CHIP):
                dev = 2 * q + 1 - mc
                theirs = g_refs[a].at[dev] if by_cols[a] else _window(g_refs[a], shard_shapes[a], False, dev)
                copies.append(pltpu.make_async_remote_copy(
                    src_ref=theirs, dst_ref=got_refs[a].at[q], send_sem=send_sems.at[a, q], recv_sem=recv_sems.at[a, q],
                    device_id=(mx, my, 1 - mc), device_id_type=MESH))
        for cp in copies:
            cp.start()
        for cp in copies:
            cp.wait()

    sems = pltpu.SemaphoreType.DMA((na, N_CHIP))
    return pl.pallas_call(
        body, name="rs_sibling", out_shape=slabs, in_specs=[_ANY] * na, out_specs=[_ANY] * na,
        scratch_shapes=[sems, sems],
    )(*grads)


def _rs_add(g, got, core, by_cols, cols, name):
    nq, nl, r, c = got.shape
    tr = _tile(r, ELEMENTWISE_ROW_TILE, 2 * SUBLANES)

    def body(core_ref, a_ref, b_ref, o_ref):
        o_ref[...] = (a_ref[...] + b_ref[...])[:, :cols].astype(BF16)

    if by_cols:
        mine = pl.BlockSpec((None, None, tr, c), lambda q, l, i, core_ref: (2 * q + core_ref[0], l, i, 0))
    else:
        mine = pl.BlockSpec((None, tr, c), lambda q, l, i, core_ref: (l, (2 * q + core_ref[0]) * (r // tr) + i, 0))
    spec = pl.BlockSpec((None, None, tr, c), lambda q, l, i, core_ref: (q, l, i, 0))
    return pl.pallas_call(
        body, name=name, out_shape=jax.ShapeDtypeStruct((nq, nl, r, cols), BF16),
        grid_spec=pltpu.PrefetchScalarGridSpec(
            num_scalar_prefetch=1, grid=(nq, nl, r // tr), in_specs=[mine, spec],
            out_specs=pl.BlockSpec((None, None, tr, cols), lambda q, l, i, core_ref: (q, l, i, 0))),
        compiler_params=pltpu.CompilerParams(dimension_semantics=("parallel",) * 3),
    )(core, g, got)


def _rs_chips(parts):
    na = len(parts)

    def body(*refs):
        p_refs, out_refs = refs[:na], refs[na:2 * na]
        send_sems, recv_sems, local_sems = refs[2 * na:]
        mx, my, mc = _coords()
        here = 2 * mx + my
        chips = [(1 - mx, my), (mx, 1 - my), (1 - mx, 1 - my)]
        copies = []
        for a in range(na):
            copies.append(pltpu.make_async_copy(p_refs[a].at[here], out_refs[a].at[here], local_sems.at[a]))
            for j, (cx, cy) in enumerate(chips):
                copies.append(pltpu.make_async_remote_copy(
                    src_ref=p_refs[a].at[2 * cx + cy], dst_ref=out_refs[a].at[here],
                    send_sem=send_sems.at[a, j], recv_sem=recv_sems.at[a, j], device_id=(cx, cy, mc), device_id_type=MESH))
        for cp in copies:
            cp.start()
        for cp in copies:
            cp.wait()

    return pl.pallas_call(
        body, name="rs_chips", out_shape=[jax.ShapeDtypeStruct(p.shape, p.dtype) for p in parts],
        in_specs=[_ANY] * na, out_specs=[_ANY] * na,
        scratch_shapes=[pltpu.SemaphoreType.DMA((na, 3)), pltpu.SemaphoreType.DMA((na, 3)), pltpu.SemaphoreType.DMA((na,))],
    )(*parts)


def _adamw(parts, w, m, v, name):
    n, r, c = parts.shape
    tr = _tile(r, 320, 2 * SUBLANES)
    bc1 = 1.0 - ADAM_B1 ** ADAM_STEP
    bc2 = 1.0 - ADAM_B2 ** ADAM_STEP

    def body(p_ref, w_ref, m_ref, v_ref, g_ref, d_ref, mo_ref, vo_ref):
        g = p_ref[0].astype(F32)
        for j in range(1, n):
            g = g + p_ref[j].astype(F32)
        mn = ADAM_B1 * m_ref[...] + (1.0 - ADAM_B1) * g
        vn = ADAM_B2 * v_ref[...] + (1.0 - ADAM_B2) * (g * g)
        g_ref[...] = g
        mo_ref[...] = mn
        vo_ref[...] = vn
        d_ref[...] = -ADAM_LR * ((mn / bc1) / (jnp.sqrt(vn / bc2) + ADAM_EPS) + ADAM_WD * w_ref[...])

    flat = pl.BlockSpec((tr, c), lambda i: (i, 0))
    return pl.pallas_call(
        body, name=name, grid=(r // tr,),
        in_specs=[pl.BlockSpec((n, tr, c), lambda i: (0, i, 0)), flat, flat, flat],
        out_specs=[flat] * 4, out_shape=[jax.ShapeDtypeStruct((r, c), F32)] * 4,
        compiler_params=pltpu.CompilerParams(dimension_semantics=("parallel",)),
    )(parts, w, m, v)


def _adam_math(g, w, m, v):
    bc1 = 1.0 - ADAM_B1 ** ADAM_STEP
    bc2 = 1.0 - ADAM_B2 ** ADAM_STEP
    mn = ADAM_B1 * m + (1.0 - ADAM_B1) * g
    vn = ADAM_B2 * v + (1.0 - ADAM_B2) * (g * g)
    return -ADAM_LR * ((mn / bc1) / (jnp.sqrt(vn / bc2) + ADAM_EPS) + ADAM_WD * w), mn, vn


def _adamw_shard(parts, w, m, v, name):
    nl, r, c = w.shape
    tr = _tile(r, ELEMENTWISE_ROW_TILE, 2 * SUBLANES)

    def body(p_ref, w_ref, m_ref, v_ref, g_ref, d_ref, mo_ref, vo_ref):
        g = p_ref[0].astype(F32)
        for q in range(1, N_CHIP):
            g = g + p_ref[q].astype(F32)
        g = g[:, :c]
        g_ref[...] = g
        d_ref[...], mo_ref[...], vo_ref[...] = _adam_math(g, w_ref[...], m_ref[...], v_ref[...])

    native = pl.BlockSpec((None, tr, c), lambda l, i: (l, i, 0))
    return pl.pallas_call(
        body, name=name, grid=(nl, r // tr),
        in_specs=[pl.BlockSpec((N_CHIP, None, tr, parts.shape[3]), lambda l, i: (0, l, i, 0)), native, native, native],
        out_specs=[native] * 4, out_shape=[jax.ShapeDtypeStruct(w.shape, F32)] * 4,
        compiler_params=pltpu.CompilerParams(dimension_semantics=("parallel", "parallel"), vmem_limit_bytes=V7X_VMEM_LIMIT),
    )(parts, w, m, v)


def _pack(arrs, dtype=F32):
    cols = []
    for a in arrs:
        f = a.reshape(-1).astype(dtype)
        cols.append(jnp.pad(f, (0, -f.shape[0] % FLAT_W)))
    flat = jnp.concatenate(cols)
    flat = jnp.pad(flat, (0, -flat.shape[0] % (FLAT_W * SUBLANES)))
    return flat.reshape(-1, FLAT_W)


def _unpack(flat, shapes, lead=()):
    flat = flat.reshape(lead + (-1,))
    out, off = [], 0
    for s in shapes:
        n = math.prod(s)
        out.append(flat[..., off:off + n].reshape(lead + tuple(s)))
        off += n + (-n % FLAT_W)
    return out


def kernel(x, p, ffn1_norm, ffn1_w1, ffn1_w3, ffn1_w2, mix_norm, ffn2_norm, ffn2_w1, ffn2_w3, ffn2_w2, ple_norm, ple_proj, ple_gate, s5_w_in, s5_a_re, s5_a_im, s5_log_dt, s5_b_re, s5_b_im, s5_c_re, s5_c_im, s5_d, s5_w_glu, sb_w_qkv, sb_w_o, final_norm, loss_target, m_ffn1_norm, m_ffn1_w1, m_ffn1_w3, m_ffn1_w2, m_mix_norm, m_ffn2_norm, m_ffn2_w1, m_ffn2_w3, m_ffn2_w2, m_ple_norm, m_ple_proj, m_ple_gate, m_s5_w_in, m_s5_a_re, m_s5_a_im, m_s5_log_dt, m_s5_b_re, m_s5_b_im, m_s5_c_re, m_s5_c_im, m_s5_d, m_s5_w_glu, m_sb_w_qkv, m_sb_w_o, m_final_norm, v_ffn1_norm, v_ffn1_w1, v_ffn1_w3, v_ffn1_w2, v_mix_norm, v_ffn2_norm, v_ffn2_w1, v_ffn2_w3, v_ffn2_w2, v_ple_norm, v_ple_proj, v_ple_gate, v_s5_w_in, v_s5_a_re, v_s5_a_im, v_s5_log_dt, v_s5_b_re, v_s5_b_im, v_s5_c_re, v_s5_c_im, v_s5_d, v_s5_w_glu, v_sb_w_qkv, v_sb_w_o, v_final_norm):
    given = dict(locals())
    wts = {n: given[n] for n in WEIGHTS}
    mom = {n: given["m_" + n] for n in WEIGHTS}
    var = {n: given["v_" + n] for n in WEIGHTS}
    bl, seq, d = x.shape
    t = bl * seq
    depth = p.shape[0]

    hid_pad = -wts["ffn1_w1"].shape[-1] % LANES

    def padded(n):
        a = wts[n].astype(BF16)
        if n in FFN_COL:
            return jnp.pad(a, ((0, 0), (0, 0), (0, hid_pad)))
        if n in FFN_ROW:
            return jnp.pad(a, ((0, 0), (0, hid_pad), (0, 0)))
        return a

    sent = [padded(n) for n in SHARDED]
    shard_shape = {n: a.shape for n, a in zip(SHARDED, sent)}
    by_cols = [n in COL_SHARDED for n in SHARDED]
    full = dict(zip(SHARDED, _gather_weights(sent, by_cols)))

    def row(a):
        return a.reshape(1, -1)

    def wt(n, layer=0, col=0, width=None):
        return _W(full[n], layer, col, width)

    gbuf = {}

    def wgrad(n, x_act, dy, layer=0, scale=1.0):
        like = (N_DEV,) + shard_shape[n] if n in COL_SHARDED else full[n].shape
        gbuf[n] = _wgrad(x_act, dy, f"{n}_grad", like, gbuf.get(n), layer, scale)

    n_groups = d // S5_GROUP
    a_re, a_im = s5_a_re[0], s5_a_im[0]
    log_dt = s5_log_dt[0].reshape(n_groups, 1)
    disc = _s5_disc(a_re, a_im, log_dt)
    nb = n_groups // S5_BLOCK_GROUPS
    ns = S5_BLOCK_GROUPS * S5_STATE
    par = jnp.concatenate([jnp.stack([q.reshape(nb, ns) for q in disc], axis=1), jnp.zeros((nb, 4, ns), F32)], axis=1)
    bre = _blockdiag(s5_b_re[0].transpose(0, 2, 1), S5_BLOCK_GROUPS).astype(BF16)
    bim = _blockdiag(s5_b_im[0].transpose(0, 2, 1), S5_BLOCK_GROUPS).astype(BF16)
    cre = _blockdiag(s5_c_re[0].transpose(0, 2, 1), S5_BLOCK_GROUPS).astype(BF16)
    cim = _blockdiag(s5_c_im[0].transpose(0, 2, 1), S5_BLOCK_GROUPS).astype(BF16)
    lc = min(SCAN_CHUNK, seq)

    h = x.reshape(t, d)
    saved = []
    for i in range(depth):
        s = {"h0": h}
        h, s["n1"], s["a1"], s["b1"] = _ffn_fwd(h, row(ffn1_norm[i]), wt("ffn1_w1", i), wt("ffn1_w3", i), wt("ffn1_w2", i))
        s["h1"] = h
        j = i // 2
        if i % 2 == 0:
            s["hn"], u = _norm_lin(h, row(mix_norm[i]), [wt("s5_w_in", j)], [F32], "s5_in")
            s["u"] = _interleave(u, lc)
            y, z, s["st"] = _s5_scan_fwd(s["u"], bre, bim, cre, cim, par, row(s5_d[j]), bl, seq, lc)
            s["y"], s["z"] = _deinterleave(y, lc), _deinterleave(z, lc)
            h = _s5_out(h, s["z"], wt("s5_w_glu", j))
        else:
            wqkv = [wt("sb_w_qkv", j, col, d) for col in range(3)]
            s["hn"], s["q"], s["k"], s["v"] = _norm_lin(h, row(mix_norm[i]), wqkv, [BF16] * 3, "sb_in")
            s["o"], s["tot"] = _sb_fwd(s["q"], s["k"], s["v"], bl, seq)
            h = _lin_res(h, s["o"], wt("sb_w_o", j), "sb_out")
        s["h2"] = h
        h, s["n2"], s["a2"], s["b2"] = _ffn_fwd(h, row(ffn2_norm[i]), wt("ffn2_w1", i), wt("ffn2_w3", i), wt("ffn2_w2", i))
        s["h3"] = h
        s["p"] = p[i].reshape(t, -1)
        h, s["npl"] = _ple_fwd(h, row(ple_norm[i]), wt("ple_gate", i), s["p"], wt("ple_proj", i))
        saved.append(s)

    loss_part, dh, g_final = _loss_head(h, row(final_norm), loss_target.reshape(t, d))
    loss = lax.psum(loss_part[0, 0], ("x", "y", "c"))

    grads = {n: [None] * wts[n].shape[0] for n in REPLICATED if n != "final_norm"}
    grads["final_norm"] = g_final.reshape(-1)

    def ffn_bwd(dh, h_in, n, a, b, which, i):
        da, db, sact = _ffn_bwd_down(dh, a, b, wt(f"{which}_w2", i))
        wgrad(f"{which}_w2", sact, dh, i, scale=0.5)
        dh_in, dg = _lin_bwd_norm([da, db], [wt(f"{which}_w1", i), wt(f"{which}_w3", i)], h_in,
                                  row(wts[f"{which}_norm"][i]), dh, f"{which}_bwd_up")
        wgrad(f"{which}_w1", n, da, i)
        wgrad(f"{which}_w3", n, db, i)
        grads[f"{which}_norm"][i] = dg.reshape(-1)
        return dh_in

    for i in reversed(range(depth)):
        s = saved[i]
        j = i // 2
        dh, dgl, dpp, dg = _ple_bwd(dh, s["h3"], row(ple_norm[i]), s["npl"], s["p"], wt("ple_gate", i), wt("ple_proj", i))
        grads["ple_norm"][i] = dg.reshape(-1)
        wgrad("ple_gate", s["npl"], dgl, i)
        wgrad("ple_proj", s["p"], dpp, i)
        dh = ffn_bwd(dh, s["h2"], s["n2"], s["a2"], s["b2"], "ffn2", i)
        if i % 2 == 0:
            dy, dzz = _s5_out_bwd(dh, s["z"], s["y"], wt("s5_w_glu", j))
            wgrad("s5_w_glu", s["z"], dzz, j)
            du, dbre, dbim, dcre, dcim, dpar, dd = _s5_scan_bwd(
                s["u"], _interleave(dy, lc), s["st"], bre, bim, cre, cim, par, row(s5_d[j]), bl, seq, lc)
            du = _deinterleave(du, lc)
            cts = [dpar[:, r, :].reshape(n_groups, S5_STATE) for r in range(4)]
            g_are, g_aim, g_ldt = _s5_disc_bwd(a_re, a_im, log_dt, cts)
            grads["s5_a_re"][j], grads["s5_a_im"][j], grads["s5_log_dt"][j] = g_are, g_aim, g_ldt.reshape(-1)
            grads["s5_b_re"][j] = _blockdiag_take(dbre, S5_BLOCK_GROUPS, S5_GROUP, S5_STATE).transpose(0, 2, 1)
            grads["s5_b_im"][j] = _blockdiag_take(dbim, S5_BLOCK_GROUPS, S5_GROUP, S5_STATE).transpose(0, 2, 1)
            grads["s5_c_re"][j] = _blockdiag_take(dcre, S5_BLOCK_GROUPS, S5_STATE, S5_GROUP).transpose(0, 2, 1)
            grads["s5_c_im"][j] = _blockdiag_take(dcim, S5_BLOCK_GROUPS, S5_STATE, S5_GROUP).transpose(0, 2, 1)
            grads["s5_d"][j] = dd.reshape(-1)
            dh, dg = _lin_bwd_norm([du], [wt("s5_w_in", j)], s["h1"], row(mix_norm[i]), dh, "s5_in_bwd")
            wgrad("s5_w_in", s["hn"], du, j)
        else:
            do = _lin_nt(dh, wt("sb_w_o", j), "sb_out_bwd")
            wgrad("sb_w_o", s["o"], dh, j)
            dqkv = _sb_bwd(s["q"], s["k"], s["v"], do, s["tot"], bl, seq)
            dh, dg = _lin_bwd_norm(dqkv, [wt("sb_w_qkv", j, col, d) for col in range(3)], s["h1"], row(mix_norm[i]), dh, "sb_in_bwd")
            wgrad("sb_w_qkv", s["hn"], jnp.concatenate(dqkv, axis=1), j)
        grads["mix_norm"][i] = dg.reshape(-1)
        dh = ffn_bwd(dh, s["h0"], s["n1"], s["a1"], s["b1"], "ffn1", i)
    grad_x = dh.reshape(x.shape)
    for n in REPLICATED:
        if n != "final_norm":
            grads[n] = jnp.stack(grads[n])

    got = _rs_sibling([gbuf[n] for n in SHARDED], [shard_shape[n] for n in SHARDED], by_cols)
    core = lax.axis_index("c").astype(jnp.int32).reshape(1)
    parts = _rs_chips([_rs_add(gbuf[n], g, core, n in COL_SHARDED, wts[n].shape[2], f"rs_add_{n}")
                       for n, g in zip(SHARDED, got)])
    res = {n: _adamw_shard(part, wts[n], mom[n], var[n], f"adamw_{n}") for n, part in zip(SHARDED, parts)}

    rep_shapes = [wts[n].shape for n in REPLICATED]
    rep_parts = _all_gather(_pack([grads[n].reshape(wts[n].shape) for n in REPLICATED]), "gather_small_grads")
    outs = _adamw(rep_parts, _pack([wts[n] for n in REPLICATED]), _pack([mom[n] for n in REPLICATED]),
                  _pack([var[n] for n in REPLICATED]), "adamw_replicated")
    res.update({n: vals for n, vals in zip(REPLICATED, zip(*[_unpack(o, rep_shapes) for o in outs]))})

    return (loss, grad_x, *[res[n][0] for n in WEIGHTS], *[res[n][1] for n in WEIGHTS],
            *[res[n][2] for n in WEIGHTS], *[res[n][3] for n in WEIGHTS])
```

```python
import math
from typing import NamedTuple, Optional

import jax
import jax.numpy as jnp
from jax import lax
from jax.experimental import pallas as pl
from jax.experimental.pallas import tpu as pltpu

F32 = jnp.float32
BF16 = jnp.bfloat16
MESH = pl.DeviceIdType.MESH

N_DEV = 8
N_CHIP = 4
EPS = 1e-6
S5_GROUP = 16
S5_STATE = 64
S5_BLOCK_GROUPS = 16
HEAD_DIM = 64
Q_BLOCK = 128
SB_Q_TILE = 512
SB_HEADS_PER_STEP = 4
SCAN_CHUNK = 512
SUBLANES = 8
LANES = 128
FLAT_W = 1024
ADAM_LR, ADAM_B1, ADAM_B2, ADAM_EPS, ADAM_WD, ADAM_STEP = 1e-3, 0.9, 0.999, 1e-8, 0.01, 10
V7X_VMEM_LIMIT = 56 * 1024 * 1024
ROW_TILE = 256
LIGHT_ROW_TILE = 512
LIGHT_WEIGHT_BYTES = 8 * 1024 * 1024
WGRAD_ROW_TILE = 1024
WGRAD_OUT_BYTES = 6 * 1024 * 1024
ELEMENTWISE_ROW_TILE = 512

SHARDED = ("ffn1_w1", "ffn1_w3", "ffn1_w2", "ffn2_w1", "ffn2_w3", "ffn2_w2", "ple_proj", "ple_gate",
           "s5_w_in", "s5_w_glu", "sb_w_qkv", "sb_w_o")
COL_SHARDED = ("ffn1_w1", "ffn1_w3", "ffn2_w1", "ffn2_w3", "ple_proj", "s5_w_glu", "sb_w_qkv")
FFN_COL = ("ffn1_w1", "ffn1_w3", "ffn2_w1", "ffn2_w3")
FFN_ROW = ("ffn1_w2", "ffn2_w2")
REPLICATED = ("ffn1_norm", "mix_norm", "ffn2_norm", "ple_norm", "s5_a_re", "s5_a_im", "s5_log_dt",
              "s5_b_re", "s5_b_im", "s5_c_re", "s5_c_im", "s5_d", "final_norm")
WEIGHTS = ("ffn1_norm", "ffn1_w1", "ffn1_w3", "ffn1_w2", "mix_norm", "ffn2_norm", "ffn2_w1", "ffn2_w3", "ffn2_w2",
           "ple_norm", "ple_proj", "ple_gate", "s5_w_in", "s5_a_re", "s5_a_im", "s5_log_dt", "s5_b_re", "s5_b_im",
           "s5_c_re", "s5_c_im", "s5_d", "s5_w_glu", "sb_w_qkv", "sb_w_o", "final_norm")


def _dot(a, b):
    return jnp.dot(a, b, preferred_element_type=F32)


def _dot_nt(a, b):
    return lax.dot_general(a, b, (((1,), (1,)), ((), ())), preferred_element_type=F32)


def _dot_tn(a, b):
    return lax.dot_general(a, b, (((0,), (0,)), ((), ())), preferred_element_type=F32)


def _dot2(x, m):
    hi = x.astype(BF16)
    lo = (x - hi.astype(F32)).astype(BF16)
    return _dot(hi, m) + _dot(lo, m)


def _rms(x):
    r = lax.rsqrt(jnp.mean(x * x, axis=-1, keepdims=True) + EPS)
    return x * r, r


def _rms_bwd(dn, xh, r, g):
    gy = dn * g
    return r * (gy - xh * jnp.mean(gy * xh, axis=-1, keepdims=True))


_GELU_C = math.sqrt(2.0 / math.pi)


def _gelu(x):
    return 0.5 * x * (1.0 + jnp.tanh(_GELU_C * (x + 0.044715 * x * x * x)))


def _gelu_grad(x):
    th = jnp.tanh(_GELU_C * (x + 0.044715 * x * x * x))
    return 0.5 * (1.0 + th) + 0.5 * x * (1.0 - th * th) * _GELU_C * (1.0 + 3.0 * 0.044715 * x * x)


def _neg_softplus(z):
    return -(jnp.maximum(z, 0.0) + jnp.log(1.0 + jnp.exp(-jnp.abs(z))))


def _tile(n, want, mult=SUBLANES):
    for t in range(min(want, n), 0, -1):
        if n % t == 0 and t % mult == 0:
            return t
    return n


def _rows(tm, c):
    return pl.BlockSpec((tm, c), lambda i: (i, 0))


def _whole(a):
    nd = a.ndim
    return pl.BlockSpec(a.shape, lambda i: (0,) * nd)


class _W(NamedTuple):
    arr: jax.Array
    layer: int = 0
    col: int = 0
    width: Optional[int] = None

    @property
    def shape(self):
        return self.arr.shape[1], self.width or self.arr.shape[2]


def _wspec(w):
    return pl.BlockSpec((None,) + w.shape, lambda *_: (w.layer, 0, w.col))


def _row_params(sem="parallel"):
    return pltpu.CompilerParams(dimension_semantics=(sem,), vmem_limit_bytes=V7X_VMEM_LIMIT)


def _ffn_fwd(h, g, w1, w3, w2):
    t, d = h.shape
    f = w1.shape[1]
    tm = _tile(t, ROW_TILE)

    def body(h_ref, g_ref, w1_ref, w3_ref, w2_ref, ho_ref, n_ref, a_ref, b_ref):
        x = h_ref[...]
        xh, _ = _rms(x)
        n = (xh * g_ref[...]).astype(BF16)
        a = _dot(n, w1_ref[...])
        b = _dot(n, w3_ref[...])
        s = (a * jax.nn.sigmoid(a) * b).astype(BF16)
        ho_ref[...] = x + 0.5 * _dot(s, w2_ref[...])
        n_ref[...] = n
        a_ref[...] = a.astype(BF16)
        b_ref[...] = b.astype(BF16)

    return pl.pallas_call(
        body, name="ffn_fwd", grid=(t // tm,),
        in_specs=[_rows(tm, d), _whole(g), _wspec(w1), _wspec(w3), _wspec(w2)],
        out_specs=[_rows(tm, d), _rows(tm, d), _rows(tm, f), _rows(tm, f)],
        out_shape=[jax.ShapeDtypeStruct((t, d), F32), jax.ShapeDtypeStruct((t, d), BF16),
                   jax.ShapeDtypeStruct((t, f), BF16), jax.ShapeDtypeStruct((t, f), BF16)],
        compiler_params=_row_params(),
    )(h, g, w1.arr, w3.arr, w2.arr)


def _ffn_bwd_down(dh, a, b, w2):
    t, d = dh.shape
    f = a.shape[1]
    tm = _tile(t, ROW_TILE)

    def body(dh_ref, a_ref, b_ref, w2_ref, da_ref, db_ref, s_ref):
        ds = _dot_nt((0.5 * dh_ref[...]).astype(BF16), w2_ref[...])
        a32 = a_ref[...].astype(F32)
        b32 = b_ref[...].astype(F32)
        sig = jax.nn.sigmoid(a32)
        sil = a32 * sig
        da_ref[...] = (ds * b32 * (sig * (1.0 + a32 * (1.0 - sig)))).astype(BF16)
        db_ref[...] = (ds * sil).astype(BF16)
        s_ref[...] = (sil * b32).astype(BF16)

    return pl.pallas_call(
        body, name="ffn_bwd_down", grid=(t // tm,),
        in_specs=[_rows(tm, d), _rows(tm, f), _rows(tm, f), _wspec(w2)],
        out_specs=[_rows(tm, f)] * 3,
        out_shape=[jax.ShapeDtypeStruct((t, f), BF16)] * 3,
        compiler_params=_row_params(),
    )(dh, a, b, w2.arr)


def _lin_bwd_norm(dys, ws, h, g, dh, name):
    t, d = h.shape
    heavy = sum(w.shape[0] * w.shape[1] for w in ws) * 2 > LIGHT_WEIGHT_BYTES
    tm = _tile(t, ROW_TILE if heavy else LIGHT_ROW_TILE)
    k = len(dys)

    def body(*refs):
        dy_refs, w_refs = refs[:k], refs[k:2 * k]
        h_ref, g_ref, dh_ref, o_ref, dg_ref = refs[2 * k:]
        dn = _dot_nt(dy_refs[0][...].astype(BF16), w_refs[0][...])
        for j in range(1, k):
            dn = dn + _dot_nt(dy_refs[j][...].astype(BF16), w_refs[j][...])
        xh, r = _rms(h_ref[...])
        o_ref[...] = dh_ref[...] + _rms_bwd(dn, xh, r, g_ref[...])

        @pl.when(pl.program_id(0) == 0)
        def _():
            dg_ref[...] = jnp.zeros_like(dg_ref)

        dg_ref[...] += jnp.sum(dn * xh, axis=0, keepdims=True)

    return pl.pallas_call(
        body, name=name, grid=(t // tm,),
        in_specs=[_rows(tm, dy.shape[1]) for dy in dys] + [_wspec(w) for w in ws] + [_rows(tm, d), _whole(g), _rows(tm, d)],
        out_specs=[_rows(tm, d), pl.BlockSpec((1, d), lambda i: (0, 0))],
        out_shape=[jax.ShapeDtypeStruct((t, d), F32), jax.ShapeDtypeStruct((1, d), F32)],
        compiler_params=_row_params("arbitrary"),
    )(*dys, *[w.arr for w in ws], h, g, dh)


def _wgrad(x, dy, name, like, into=None, layer=0, scale=1.0):
    t, kk = x.shape
    n = dy.shape[1]
    tm = _tile(t, WGRAD_ROW_TILE)
    steps = t // tm
    blocked = len(like) == 4
    if blocked:
        sc = like[3]
        nd = max(g for g in (1, 2, 4, 8) if g * sc * kk * 4 <= max(WGRAD_OUT_BYTES, sc * kk * 4))
        tn = nd * sc
        out_spec = pl.BlockSpec((nd, None, kk, sc), lambda j, i: (j, layer, 0, 0))
    else:
        tn = _tile(n, max(128, WGRAD_OUT_BYTES // (4 * kk)), 128)
        out_spec = pl.BlockSpec((None, kk, tn), lambda j, i: (layer, 0, j))

    def body(x_ref, dy_ref, *rest):
        o_ref = rest[-1]
        i = pl.program_id(1)

        @pl.when(i == 0)
        def _():
            o_ref[...] = jnp.zeros_like(o_ref)

        acc = _dot_tn(x_ref[...].astype(BF16), dy_ref[...].astype(BF16))
        if blocked:
            for dd in range(nd):
                o_ref[dd] += acc[:, dd * sc:(dd + 1) * sc]
        else:
            o_ref[...] += acc
        if scale != 1.0:
            @pl.when(i == steps - 1)
            def _():
                o_ref[...] = o_ref[...] * scale

    held = [] if into is None else [into]
    return pl.pallas_call(
        body, name=name, grid=(n // tn, steps),
        in_specs=[pl.BlockSpec((tm, kk), lambda j, i: (i, 0)), pl.BlockSpec((tm, tn), lambda j, i: (i, j))]
        + [pl.BlockSpec(memory_space=pl.ANY)] * len(held),
        out_specs=out_spec,
        out_shape=jax.ShapeDtypeStruct(like, F32),
        input_output_aliases={2: 0} if held else {},
        compiler_params=pltpu.CompilerParams(dimension_semantics=("parallel", "arbitrary"), vmem_limit_bytes=V7X_VMEM_LIMIT),
    )(x, dy, *held)


def _norm_lin(h, g, ws, out_dtypes, name):
    t, d = h.shape
    tm = _tile(t, LIGHT_ROW_TILE)
    k = len(ws)

    def body(*refs):
        h_ref, g_ref = refs[:2]
        w_refs = refs[2:2 + k]
        n_ref = refs[2 + k]
        o_refs = refs[3 + k:]
        xh, _ = _rms(h_ref[...])
        n = (xh * g_ref[...]).astype(BF16)
        n_ref[...] = n
        for w_ref, o_ref in zip(w_refs, o_refs):
            o_ref[...] = _dot(n, w_ref[...]).astype(o_ref.dtype)

    return pl.pallas_call(
        body, name=name, grid=(t // tm,),
        in_specs=[_rows(tm, d), _whole(g)] + [_wspec(w) for w in ws],
        out_specs=[_rows(tm, d)] + [_rows(tm, w.shape[1]) for w in ws],
        out_shape=[jax.ShapeDtypeStruct((t, d), BF16)] + [jax.ShapeDtypeStruct((t, w.shape[1]), dt) for w, dt in zip(ws, out_dtypes)],
        compiler_params=_row_params(),
    )(h, g, *[w.arr for w in ws])


def _lin_res(h, x, w, name):
    t, d = h.shape
    tm = _tile(t, LIGHT_ROW_TILE)

    def body(h_ref, x_ref, w_ref, o_ref):
        o_ref[...] = h_ref[...] + _dot(x_ref[...], w_ref[...])

    return pl.pallas_call(
        body, name=name, grid=(t // tm,),
        in_specs=[_rows(tm, d), _rows(tm, x.shape[1]), _wspec(w)],
        out_specs=_rows(tm, d), out_shape=jax.ShapeDtypeStruct((t, d), F32),
        compiler_params=_row_params(),
    )(h, x, w.arr)


def _lin_nt(dy, w, name):
    t = dy.shape[0]
    kk = w.shape[0]
    tm = _tile(t, LIGHT_ROW_TILE)

    def body(dy_ref, w_ref, o_ref):
        o_ref[...] = _dot_nt(dy_ref[...].astype(BF16), w_ref[...]).astype(BF16)

    return pl.pallas_call(
        body, name=name, grid=(t // tm,),
        in_specs=[_rows(tm, dy.shape[1]), _wspec(w)],
        out_specs=_rows(tm, kk), out_shape=jax.ShapeDtypeStruct((t, kk), BF16),
        compiler_params=_row_params(),
    )(dy, w.arr)


def _ple_fwd(h, g, wg, p, wp):
    t, d = h.shape
    tm = _tile(t, LIGHT_ROW_TILE)

    def body(h_ref, g_ref, wg_ref, p_ref, wp_ref, o_ref, n_ref):
        x = h_ref[...]
        xh, _ = _rms(x)
        n = (xh * g_ref[...]).astype(BF16)
        n_ref[...] = n
        gate = jax.nn.sigmoid(_dot(n, wg_ref[...]))
        o_ref[...] = x + _dot(p_ref[...].astype(BF16), wp_ref[...]) * gate

    return pl.pallas_call(
        body, name="ple_fwd", grid=(t // tm,),
        in_specs=[_rows(tm, d), _whole(g), _wspec(wg), _rows(tm, p.shape[1]), _wspec(wp)],
        out_specs=[_rows(tm, d), _rows(tm, d)],
        out_shape=[jax.ShapeDtypeStruct((t, d), F32), jax.ShapeDtypeStruct((t, d), BF16)],
        compiler_params=_row_params(),
    )(h, g, wg.arr, p, wp.arr)


def _ple_bwd(dh, h, g, n, p, wg, wp):
    t, d = h.shape
    tm = _tile(t, LIGHT_ROW_TILE)

    def body(dh_ref, h_ref, g_ref, n_ref, p_ref, wg_ref, wp_ref, o_ref, dgl_ref, dpp_ref, dg_ref):
        dh_v = dh_ref[...]
        gate = jax.nn.sigmoid(_dot(n_ref[...], wg_ref[...]))
        pp = _dot(p_ref[...].astype(BF16), wp_ref[...])
        dgl = (dh_v * pp * gate * (1.0 - gate)).astype(BF16)
        dgl_ref[...] = dgl
        dpp_ref[...] = (dh_v * gate).astype(BF16)
        dn = _dot_nt(dgl, wg_ref[...])
        xh, r = _rms(h_ref[...])
        o_ref[...] = dh_v + _rms_bwd(dn, xh, r, g_ref[...])

        @pl.when(pl.program_id(0) == 0)
        def _():
            dg_ref[...] = jnp.zeros_like(dg_ref)

        dg_ref[...] += jnp.sum(dn * xh, axis=0, keepdims=True)

    return pl.pallas_call(
        body, name="ple_bwd", grid=(t // tm,),
        in_specs=[_rows(tm, d), _rows(tm, d), _whole(g), _rows(tm, d), _rows(tm, p.shape[1]), _wspec(wg), _wspec(wp)],
        out_specs=[_rows(tm, d), _rows(tm, d), _rows(tm, d), pl.BlockSpec((1, d), lambda i: (0, 0))],
        out_shape=[jax.ShapeDtypeStruct((t, d), F32), jax.ShapeDtypeStruct((t, d), BF16),
                   jax.ShapeDtypeStruct((t, d), BF16), jax.ShapeDtypeStruct((1, d), F32)],
        compiler_params=_row_params("arbitrary"),
    )(dh, h, g, n, p, wg.arr, wp.arr)


def _loss_head(h, g, tgt):
    t, d = h.shape
    tm = _tile(t, LIGHT_ROW_TILE)

    def body(h_ref, g_ref, t_ref, l_ref, dh_ref, dg_ref):
        xh, r = _rms(h_ref[...])
        gg = g_ref[...]
        e = xh * gg - t_ref[...]
        dy = e * (1.0 / d)

        @pl.when(pl.program_id(0) == 0)
        def _():
            l_ref[...] = jnp.zeros_like(l_ref)
            dg_ref[...] = jnp.zeros_like(dg_ref)

        l_ref[...] += 0.5 * jnp.sum(jnp.mean(e * e, axis=-1, keepdims=True), axis=0, keepdims=True)
        dg_ref[...] += jnp.sum(dy * xh, axis=0, keepdims=True)
        dh_ref[...] = _rms_bwd(dy, xh, r, gg)

    return pl.pallas_call(
        body, name="loss_head", grid=(t // tm,),
        in_specs=[_rows(tm, d), _whole(g), _rows(tm, d)],
        out_specs=[pl.BlockSpec((1, 128), lambda i: (0, 0)), _rows(tm, d), pl.BlockSpec((1, d), lambda i: (0, 0))],
        out_shape=[jax.ShapeDtypeStruct((1, 128), F32), jax.ShapeDtypeStruct((t, d), F32), jax.ShapeDtypeStruct((1, d), F32)],
        compiler_params=_row_params("arbitrary"),
    )(h, g, tgt)


def _s5_disc_math(a_re, a_im, log_dt):
    lam_re = jnp.minimum(a_re, -1e-4)
    lam_im = a_im
    dt = jnp.exp(log_dt)
    mag = jnp.exp(lam_re * dt)
    abar_re = mag * jnp.cos(lam_im * dt)
    abar_im = mag * jnp.sin(lam_im * dt)
    den = lam_re * lam_re + lam_im * lam_im
    nr = abar_re - 1.0
    ni = abar_im
    return abar_re, abar_im, (nr * lam_re + ni * lam_im) / den, (ni * lam_re - nr * lam_im) / den


def _s5_disc(a_re, a_im, log_dt):
    gp = jax.ShapeDtypeStruct(a_re.shape, F32)

    def body(ar_ref, ai_ref, ld_ref, o0, o1, o2, o3):
        outs = _s5_disc_math(ar_ref[...], ai_ref[...], ld_ref[...])
        for o_ref, val in zip((o0, o1, o2, o3), outs):
            o_ref[...] = val

    return pl.pallas_call(body, name="s5_disc", out_shape=[gp] * 4)(a_re, a_im, log_dt)


def _s5_disc_bwd(a_re, a_im, log_dt, cts):
    def body(ar_ref, ai_ref, ld_ref, c0, c1, c2, c3, dar_ref, dai_ref, dld_ref):
        _, vjp = jax.vjp(_s5_disc_math, ar_ref[...], ai_ref[...], ld_ref[...])
        dar, dai, dld = vjp((c0[...], c1[...], c2[...], c3[...]))
        dar_ref[...] = dar
        dai_ref[...] = dai
        dld_ref[...] = dld

    return pl.pallas_call(
        body, name="s5_disc_bwd",
        out_shape=[jax.ShapeDtypeStruct(a_re.shape, F32), jax.ShapeDtypeStruct(a_im.shape, F32),
                   jax.ShapeDtypeStruct(log_dt.shape, F32)],
    )(a_re, a_im, log_dt, *cts)


def _blockdiag(w, gl):
    g, r, c = w.shape
    w = w.reshape(g // gl, gl, r, c)
    eye = jnp.eye(gl, dtype=w.dtype)
    return (w[:, :, :, None, :] * eye[None, :, None, :, None]).reshape(g // gl, gl * r, gl * c)


def _blockdiag_take(m, gl, r, c):
    nb = m.shape[0]
    own = jnp.eye(gl, dtype=bool)[None, :, None, :, None]
    return jnp.where(own, m.reshape(nb, gl, r, gl, c), 0.0).sum(axis=3).reshape(nb * gl, r, c)


def _cmul(ar, ai, br, bi):
    return ar * br - ai * bi, ar * bi + ai * br


def _powers(ar, ai, n):
    out = [(ar, ai)]
    for _ in range(n - 1):
        out.append(_cmul(ar, ai, *out[-1]))
    return out


def _interleave(a, lc):
    t, d = a.shape
    return a.reshape(t // lc, SUBLANES, lc // SUBLANES, d).transpose(0, 2, 1, 3).reshape(t, d)


def _deinterleave(a, lc):
    t, d = a.shape
    return a.reshape(t // lc, lc // SUBLANES, SUBLANES, d).transpose(0, 2, 1, 3).reshape(t, d)


def _seg_rows(k):
    return slice(k * SUBLANES, (k + 1) * SUBLANES)


def _seg_scan(re_ref, im_ref, ar, ai, seg, reverse=False):
    cr = ci = jnp.zeros((SUBLANES, re_ref.shape[1]), F32)
    for k in (reversed(range(seg)) if reverse else range(seg)):
        rows = _seg_rows(k)
        cr, ci = ar * cr - ai * ci + re_ref[rows, :], ar * ci + ai * cr + im_ref[rows, :]
        re_ref[rows, :] = cr
        im_ref[rows, :] = ci
    return cr, ci


def _seg_entering(er, ei, pr, pi, c0r, c0i, reverse=False):
    rows = lax.broadcasted_iota(jnp.int32, er.shape, 0)
    vr = vi = jnp.zeros_like(er)
    cr, ci = c0r, c0i
    for j in (reversed(range(SUBLANES)) if reverse else range(SUBLANES)):
        vr = jnp.where(rows == j, cr, vr)
        vi = jnp.where(rows == j, ci, vi)
        cr, ci = er[j:j + 1, :] + pr * cr - pi * ci, ei[j:j + 1, :] + pr * ci + pi * cr
    return vr, vi, cr, ci


def _s5_scan_fwd(u, bre, bim, cre, cim, par, dskip, bl, seq, lc):
    t, d = u.shape
    nb, gw, ns = bre.shape
    nc = seq // lc
    seg = lc // SUBLANES

    def body(u_ref, bre_ref, bim_ref, cre_ref, cim_ref, par_ref, d_ref, y_ref, z_ref, st_ref, carry, sre, sim):
        @pl.when(pl.program_id(2) == 0)
        def _():
            carry[...] = jnp.zeros_like(carry)

        st_ref[...] = carry[...]
        uu = u_ref[...]
        ug = uu.astype(BF16)
        wre = _dot(ug, bre_ref[...])
        wim = _dot(ug, bim_ref[...])
        ar, ai = par_ref[0:1, :], par_ref[1:2, :]
        fr, fi = par_ref[2:3, :], par_ref[3:4, :]
        sre[...] = fr * wre - fi * wim
        sim[...] = fr * wim + fi * wre
        pows = _powers(ar, ai, seg)
        er, ei = _seg_scan(sre, sim, ar, ai, seg)
        vr, vi, cr, ci = _seg_entering(er, ei, *pows[-1], carry[0:1, :], carry[1:2, :])
        carry[0:1, :] = cr
        carry[1:2, :] = ci
        for k in range(seg):
            rows = _seg_rows(k)
            fr_k, fi_k = _cmul(*pows[k], vr, vi)
            sre[rows, :] += fr_k
            sim[rows, :] += fi_k
        y = _dot(sre[...].astype(BF16), cre_ref[...]) - _dot(sim[...].astype(BF16), cim_ref[...]) + d_ref[...] * uu
        y_ref[...] = y
        z_ref[...] = _gelu(y).astype(BF16)

    tok = pl.BlockSpec((lc, gw), lambda g, b, c: (b * nc + c, g))
    mat_b = pl.BlockSpec((None, gw, ns), lambda g, b, c: (g, 0, 0))
    mat_c = pl.BlockSpec((None, ns, gw), lambda g, b, c: (g, 0, 0))
    return pl.pallas_call(
        body, name="s5_scan_fwd", grid=(nb, bl, nc),
        in_specs=[tok, mat_b, mat_b, mat_c, mat_c, pl.BlockSpec((None, 8, ns), lambda g, b, c: (g, 0, 0)),
                  pl.BlockSpec((1, gw), lambda g, b, c: (0, g))],
        out_specs=[tok, tok, pl.BlockSpec((None, None, 2, ns), lambda g, b, c: (g, b * nc + c, 0, 0))],
        out_shape=[jax.ShapeDtypeStruct((t, d), F32), jax.ShapeDtypeStruct((t, d), BF16),
                   jax.ShapeDtypeStruct((nb, bl * nc, 2, ns), F32)],
        scratch_shapes=[pltpu.VMEM((2, ns), F32), pltpu.VMEM((lc, ns), F32), pltpu.VMEM((lc, ns), F32)],
        compiler_params=pltpu.CompilerParams(dimension_semantics=("parallel", "arbitrary", "arbitrary"),
                                             vmem_limit_bytes=V7X_VMEM_LIMIT),
    )(u, bre, bim, cre, cim, par, dskip)


def _s5_scan_bwd(u, dy, st, bre, bim, cre, cim, par, dskip, bl, seq, lc):
    t, d = u.shape
    nb, gw, ns = bre.shape
    nc = seq // lc
    seg = lc // SUBLANES

    def body(u_ref, dy_ref, st_ref, bre_ref, bim_ref, cre_ref, cim_ref, par_ref, d_ref,
             du_ref, dbre_ref, dbim_ref, dcre_ref, dcim_ref, dpar_ref, dd_ref,
             lcarry, sre, sim, wre_s, wim_s, lre, lim):
        b, c = pl.program_id(1), pl.program_id(2)

        @pl.when((b == 0) & (c == 0))
        def _():
            for ref in (dbre_ref, dbim_ref, dcre_ref, dcim_ref, dpar_ref, dd_ref):
                ref[...] = jnp.zeros_like(ref)

        @pl.when(c == 0)
        def _():
            lcarry[...] = jnp.zeros_like(lcarry)

        uu = u_ref[...]
        ug = uu.astype(BF16)
        dyv = dy_ref[...]
        dyb = dyv.astype(BF16)
        ar, ai = par_ref[0:1, :], par_ref[1:2, :]
        fr, fi = par_ref[2:3, :], par_ref[3:4, :]
        wre = _dot(ug, bre_ref[...])
        wim = _dot(ug, bim_ref[...])
        wre_s[...] = wre
        wim_s[...] = wim
        sre[...] = fr * wre - fi * wim
        sim[...] = fr * wim + fi * wre

        pows = _powers(ar, ai, seg)
        er, ei = _seg_scan(sre, sim, ar, ai, seg)
        svr, svi, _, _ = _seg_entering(er, ei, *pows[-1], st_ref[0:1, :], st_ref[1:2, :])
        for k in range(seg):
            rows = _seg_rows(k)
            fr_k, fi_k = _cmul(*pows[k], svr, svi)
            sre[rows, :] += fr_k
            sim[rows, :] += fi_k

        lre[...] = _dot_nt(dyb, cre_ref[...])
        lim[...] = -_dot_nt(dyb, cim_ref[...])
        er, ei = _seg_scan(lre, lim, ar, -ai, seg, reverse=True)
        lvr, lvi, cr, ci = _seg_entering(er, ei, pows[-1][0], -pows[-1][1], lcarry[0:1, :], lcarry[1:2, :], reverse=True)
        lcarry[0:1, :] = cr
        lcarry[1:2, :] = ci
        dar = dai = jnp.zeros_like(er)
        for k in range(seg):
            rows = _seg_rows(k)
            pr, pi = pows[seg - 1 - k]
            fr_k, fi_k = _cmul(pr, -pi, lvr, lvi)
            lr = lre[rows, :] + fr_k
            li = lim[rows, :] + fi_k
            lre[rows, :] = lr
            lim[rows, :] = li
            spr, spi = (svr, svi) if k == 0 else (sre[_seg_rows(k - 1), :], sim[_seg_rows(k - 1), :])
            dar = dar + lr * spr + li * spi
            dai = dai + li * spr - lr * spi

        lr, li = lre[...], lim[...]
        wr, wi = wre_s[...], wim_s[...]
        dpar_ref[0:1, :] += jnp.sum(dar, axis=0, keepdims=True)
        dpar_ref[1:2, :] += jnp.sum(dai, axis=0, keepdims=True)
        dpar_ref[2:3, :] += jnp.sum(lr * wr + li * wi, axis=0, keepdims=True)
        dpar_ref[3:4, :] += jnp.sum(li * wr - lr * wi, axis=0, keepdims=True)
        dwr = (fr * lr + fi * li).astype(BF16)
        dwi = (fr * li - fi * lr).astype(BF16)
        dsk = d_ref[...]
        du_ref[...] = _dot_nt(dwr, bre_ref[...]) + _dot_nt(dwi, bim_ref[...]) + dsk * dyv
        dd_ref[...] += jnp.sum(dyv * uu, axis=0, keepdims=True)
        dbre_ref[...] += _dot_tn(ug, dwr)
        dbim_ref[...] += _dot_tn(ug, dwi)
        dcre_ref[...] += _dot_tn(sre[...].astype(BF16), dyb)
        dcim_ref[...] -= _dot_tn(sim[...].astype(BF16), dyb)

    tok = pl.BlockSpec((lc, gw), lambda g, b, c: (b * nc + nc - 1 - c, g))
    mat_b = pl.BlockSpec((None, gw, ns), lambda g, b, c: (g, 0, 0))
    mat_c = pl.BlockSpec((None, ns, gw), lambda g, b, c: (g, 0, 0))
    rows8 = pl.BlockSpec((None, 8, ns), lambda g, b, c: (g, 0, 0))
    dvec = pl.BlockSpec((1, gw), lambda g, b, c: (0, g))
    tile = pltpu.VMEM((lc, ns), F32)
    return pl.pallas_call(
        body, name="s5_scan_bwd", grid=(nb, bl, nc),
        in_specs=[tok, tok, pl.BlockSpec((None, None, 2, ns), lambda g, b, c: (g, b * nc + nc - 1 - c, 0, 0)),
                  mat_b, mat_b, mat_c, mat_c, rows8, dvec],
        out_specs=[tok, mat_b, mat_b, mat_c, mat_c, rows8, dvec],
        out_shape=[jax.ShapeDtypeStruct((t, d), F32),
                   jax.ShapeDtypeStruct((nb, gw, ns), F32), jax.ShapeDtypeStruct((nb, gw, ns), F32),
                   jax.ShapeDtypeStruct((nb, ns, gw), F32), jax.ShapeDtypeStruct((nb, ns, gw), F32),
                   jax.ShapeDtypeStruct((nb, 8, ns), F32), jax.ShapeDtypeStruct((1, d), F32)],
        scratch_shapes=[pltpu.VMEM((2, ns), F32)] + [tile] * 6,
        compiler_params=pltpu.CompilerParams(dimension_semantics=("arbitrary", "arbitrary", "arbitrary"),
                                             vmem_limit_bytes=V7X_VMEM_LIMIT),
    )(u, dy, st, bre, bim, cre, cim, par, dskip)


def _s5_out(h, z, wglu):
    t, d = h.shape
    tm = _tile(t, LIGHT_ROW_TILE)

    def body(h_ref, z_ref, w_ref, o_ref):
        zz = _dot(z_ref[...], w_ref[...])
        o_ref[...] = h_ref[...] + zz[:, :d] * jax.nn.sigmoid(zz[:, d:])

    return pl.pallas_call(
        body, name="s5_out", grid=(t // tm,),
        in_specs=[_rows(tm, d), _rows(tm, d), _wspec(wglu)],
        out_specs=_rows(tm, d), out_shape=jax.ShapeDtypeStruct((t, d), F32),
        compiler_params=_row_params(),
    )(h, z, wglu.arr)


def _s5_out_bwd(dh, z, y, wglu):
    t, d = dh.shape
    tm = _tile(t, LIGHT_ROW_TILE)

    def body(dh_ref, z_ref, y_ref, w_ref, dy_ref, dzz_ref):
        zz = _dot(z_ref[...], w_ref[...])
        out, sg = zz[:, :d], jax.nn.sigmoid(zz[:, d:])
        dh_v = dh_ref[...]
        dzz = jnp.concatenate([dh_v * sg, dh_v * out * sg * (1.0 - sg)], axis=1).astype(BF16)
        dzz_ref[...] = dzz
        dy_ref[...] = _dot_nt(dzz, w_ref[...]) * _gelu_grad(y_ref[...])

    return pl.pallas_call(
        body, name="s5_out_bwd", grid=(t // tm,),
        in_specs=[_rows(tm, d), _rows(tm, d), _rows(tm, d), _wspec(wglu)],
        out_specs=[_rows(tm, d), _rows(tm, 2 * d)],
        out_shape=[jax.ShapeDtypeStruct((t, d), F32), jax.ShapeDtypeStruct((t, 2 * d), BF16)],
        compiler_params=_row_params(),
    )(dh, z, y, wglu.arr)


def _head_spec(seq, w):
    return pl.BlockSpec((None, SB_HEADS_PER_STEP, seq, w), lambda b, h: (b, h, 0, 0))


def _pair_spec(seq):
    return pl.BlockSpec((seq, SB_HEADS_PER_STEP * HEAD_DIM), lambda b, g: (b, g))


def _own_lanes(a, hh):
    lane = lax.broadcasted_iota(jnp.int32, a.shape, 1)
    return jnp.where((lane < HEAD_DIM) == (hh % 2 == 0), a, jnp.zeros_like(a))


def _pick_lanes(even, odd):
    lane = lax.broadcasted_iota(jnp.int32, even.shape, 1)
    return jnp.where(lane < HEAD_DIM, even, odd)


def _pair_cols(hh):
    return slice((hh // 2) * LANES, (hh // 2 + 1) * LANES)


def _tri_and_ones(kind):
    row = lax.broadcasted_iota(jnp.int32, (Q_BLOCK, Q_BLOCK), 0)
    col = lax.broadcasted_iota(jnp.int32, (Q_BLOCK, Q_BLOCK), 1)
    tri = {"after": row > col, "upto": row <= col, "before": row < col}[kind]
    return jnp.concatenate([tri.astype(BF16), jnp.ones((Q_BLOCK, Q_BLOCK), BF16)], axis=1)


def _sb_fwd(q, k, v, bsz, seq):
    dh = HEAD_DIM
    nh = q.shape[1] // dh
    tq = min(SB_Q_TILE, seq // 2)
    nq, nsub = seq // tq, tq // Q_BLOCK
    scale = dh ** -0.5
    hp = SB_HEADS_PER_STEP
    qb = Q_BLOCK

    def body(q_ref, k_ref, v_ref, o_ref, tot_ref):
        strict = lax.broadcasted_iota(jnp.int32, (qb, qb), 1) < lax.broadcasted_iota(jnp.int32, (qb, qb), 0)
        sums = _tri_and_ones("after")

        def sweep(qs, c0, units, carry):
            heads = sorted({u[0] for u in units})
            kbs = {hh: k_ref[pl.ds(c0, qb), _pair_cols(hh)] for hh in heads}
            vbs = {hh: v_ref[pl.ds(c0, qb), _pair_cols(hh)] for hh in heads}
            zs = [_dot_nt(qs[hh][s], kbs[hh]) * scale for hh, s, _ in units]
            lkrs = [_neg_softplus(z) for z in zs]
            lks = [jnp.where(strict, lkr, 0.0) if dg else lkr for lkr, (_, _, dg) in zip(lkrs, units)]
            css = [_dot2(lk, sums) for lk in lks]
            carry = dict(carry)
            for (hh, s, dg), z, lkr, cs in zip(units, zs, lkrs, css):
                acc, run = carry[hh, s]
                att = jnp.exp(z + lkr + cs[:, :qb] + run)
                if dg:
                    att = jnp.where(strict, att, 0.0)
                carry[hh, s] = (acc + _dot(att.astype(BF16), vbs[hh]), run + cs[:, qb:])
            return carry

        def q_loop(qi, _):
            r0 = qi * tq
            qs = {hh: [_own_lanes(q_ref[pl.ds(pl.multiple_of(r0 + s * qb, qb), qb), _pair_cols(hh)], hh)
                       for s in range(nsub)] for hh in range(hp)}
            carry = {(hh, s): (jnp.zeros((qb, LANES), F32), jnp.zeros((qb, qb), F32)) for hh in range(hp) for s in range(nsub)}
            for jj in reversed(range(nsub)):
                units = [(hh, s, s == jj) for hh in range(hp) for s in range(jj, nsub)]
                carry = sweep(qs, pl.multiple_of(r0 + jj * qb, qb), units, carry)
            units = [(hh, s, False) for hh in range(hp) for s in range(nsub)]
            carry = lax.fori_loop(
                0, nsub * qi, lambda t, c: sweep(qs, pl.multiple_of((nsub * qi - 1 - t) * qb, qb), units, c), carry)
            for s in range(nsub):
                rows = pl.ds(pl.multiple_of(r0 + s * qb, qb), qb)
                for hh in range(0, hp, 2):
                    o_ref[rows, _pair_cols(hh)] = _pick_lanes(carry[hh, s][0], carry[hh + 1, s][0]).astype(BF16)
                for hh in range(hp):
                    tot_ref[hh, rows, :] = carry[hh, s][1][:, 0:1]
            return 0

        lax.fori_loop(0, nq, q_loop, 0)

    return pl.pallas_call(
        body, name="sb_fwd", grid=(bsz, nh // hp),
        in_specs=[_pair_spec(seq)] * 3,
        out_specs=[_pair_spec(seq), _head_spec(seq, 1)],
        out_shape=[jax.ShapeDtypeStruct(q.shape, BF16), jax.ShapeDtypeStruct((bsz, nh, seq, 1), F32)],
        compiler_params=pltpu.CompilerParams(dimension_semantics=("parallel", "parallel"), vmem_limit_bytes=V7X_VMEM_LIMIT),
    )(q, k, v)


def _sb_bwd(q, k, v, do, tot, bsz, seq):
    dh = HEAD_DIM
    nh = q.shape[1] // dh
    tq = min(SB_Q_TILE, seq // 2)
    nq, nsub = seq // tq, tq // Q_BLOCK
    scale = dh ** -0.5
    hp = SB_HEADS_PER_STEP
    qb = Q_BLOCK

    def body(q_ref, k_ref, v_ref, do_ref, tot_ref, dq_ref, dk_ref, dv_ref, dka, dva):
        dka[...] = jnp.zeros_like(dka)
        dva[...] = jnp.zeros_like(dva)
        strict = lax.broadcasted_iota(jnp.int32, (qb, qb), 1) < lax.broadcasted_iota(jnp.int32, (qb, qb), 0)
        upto = _tri_and_ones("upto")
        before = _tri_and_ones("before")

        def sweep(qf, dof, tots, c0, first, units, carry):
            heads = sorted({u[0] for u in units})
            kbs = {hh: k_ref[pl.ds(c0, qb), _pair_cols(hh)] for hh in heads}
            vbs = {hh: v_ref[pl.ds(c0, qb), _pair_cols(hh)] for hh in heads}
            sub = lambda a, s: a[s * qb:(s + 1) * qb, :]
            zs = [_dot_nt(sub(qf[hh], s), kbs[hh]) * scale for hh, s, _ in units]
            das = [_dot_nt(sub(dof[hh], s), vbs[hh]) for hh, s, _ in units]
            lkrs = [_neg_softplus(z) for z in zs]
            lks = [jnp.where(strict, lkr, 0.0) if dg else lkr for lkr, (_, _, dg) in zip(lkrs, units)]
            css = [_dot2(lk, upto) for lk in lks]
            lsigs, atts, gls = [], [], []
            for (hh, s, dg), z, lkr, cs, da in zip(units, zs, lkrs, css, das):
                lsig = z + lkr
                att = jnp.exp(lsig + (tots[hh, s] - (cs[:, :qb] + carry[hh, s][1])))
                if dg:
                    att = jnp.where(strict, att, 0.0)
                lsigs.append(lsig)
                atts.append(att)
                gls.append(da * att)
            gss = [_dot2(gl, before) for gl in gls]
            carry = dict(carry)
            dzs = {}
            for (hh, s, dg), lsig, gl, cs, gs in zip(units, lsigs, gls, css, gss):
                dqa, pre, gpre = carry[hh, s]
                sig = jnp.exp(lsig)
                dz = gl * (1.0 - sig) - (gs[:, :qb] + gpre) * sig
                if dg:
                    dz = jnp.where(strict, dz, 0.0)
                dz = (dz * scale).astype(BF16)
                dzs[hh, s] = dz
                carry[hh, s] = (dqa + _dot(dz, kbs[hh]), pre + cs[:, qb:], gpre + gs[:, qb:])
            att_of = {(hh, s): a for (hh, s, _), a in zip(units, atts)}
            for hh in heads:
                dzc = jnp.concatenate([dzs[hh, s] for s in range(first, nsub)], axis=0)
                attc = jnp.concatenate([att_of[hh, s].astype(BF16) for s in range(first, nsub)], axis=0)
                dka[pl.ds(c0, qb), _pair_cols(hh)] += _dot_tn(dzc, qf[hh][first * qb:, :])
                dva[pl.ds(c0, qb), _pair_cols(hh)] += _dot_tn(attc, dof[hh][first * qb:, :])
            return carry

        def q_loop(qi, _):
            r0 = pl.multiple_of(qi * tq, tq)
            qf = [_own_lanes(q_ref[pl.ds(r0, tq), _pair_cols(hh)], hh) for hh in range(hp)]
            dof = [_own_lanes(do_ref[pl.ds(r0, tq), _pair_cols(hh)], hh) for hh in range(hp)]
            tots = {(hh, s): jnp.broadcast_to(tot_ref[hh, pl.ds(pl.multiple_of(r0 + s * qb, qb), qb), :], (qb, qb))
                    for hh in range(hp) for s in range(nsub)}
            carry = {(hh, s): (jnp.zeros((qb, LANES), F32), jnp.zeros((qb, qb), F32), jnp.zeros((qb, qb), F32))
                     for hh in range(hp) for s in range(nsub)}
            units = [(hh, s, False) for hh in range(hp) for s in range(nsub)]
            carry = lax.fori_loop(
                0, nsub * qi, lambda kj, c: sweep(qf, dof, tots, pl.multiple_of(kj * qb, qb), 0, units, c), carry)
            for jj in range(nsub):
                units = [(hh, s, s == jj) for hh in range(hp) for s in range(jj, nsub)]
                carry = sweep(qf, dof, tots, pl.multiple_of(r0 + jj * qb, qb), jj, units, carry)
            for s in range(nsub):
                rows = pl.ds(pl.multiple_of(r0 + s * qb, qb), qb)
                for hh in range(0, hp, 2):
                    dq_ref[rows, _pair_cols(hh)] = _pick_lanes(carry[hh, s][0], carry[hh + 1, s][0]).astype(BF16)
            return 0

        lax.fori_loop(0, nq, q_loop, 0)
        dk_ref[...] = dka[...].astype(BF16)
        dv_ref[...] = dva[...].astype(BF16)

    ps = _pair_spec(seq)
    return pl.pallas_call(
        body, name="sb_bwd", grid=(bsz, nh // hp),
        in_specs=[ps, ps, ps, ps, _head_spec(seq, 1)],
        out_specs=[ps, ps, ps],
        out_shape=[jax.ShapeDtypeStruct(q.shape, BF16)] * 3,
        scratch_shapes=[pltpu.VMEM((seq, hp * dh), F32), pltpu.VMEM((seq, hp * dh), F32)],
        compiler_params=pltpu.CompilerParams(dimension_semantics=("parallel", "parallel"), vmem_limit_bytes=V7X_VMEM_LIMIT),
    )(q, k, v, do, tot)


def _coords():
    return lax.axis_index("x"), lax.axis_index("y"), lax.axis_index("c")


def _all_gather(x, name):
    r, c = x.shape

    def body(x_ref, out_ref, send_sems, recv_sems, local_sem):
        mx, my, mc = _coords()
        me, sibling = (mx, my, mc), (mx, my, 1 - mc)
        chips = [(1 - mx, my), (mx, 1 - my), (1 - mx, 1 - my)]

        def blk(px, py, pc):
            return out_ref.at[4 * px + 2 * py + pc]

        def copy(k, block, to, src=None):
            return pltpu.make_async_remote_copy(
                src_ref=blk(*block) if src is None else src, dst_ref=blk(*block),
                send_sem=send_sems.at[k], recv_sem=recv_sems.at[k], device_id=to, device_id_type=MESH)

        mine = pltpu.make_async_copy(x_ref, blk(*me), local_sem)
        mine.start()
        first = [copy(0, me, sibling, src=x_ref)]
        first += [copy(1 + j, me, (*chip, mc), src=x_ref) for j, chip in enumerate(chips)]
        for cp in first:
            cp.start()
        passed = [copy(4 + j, (*chip, mc), sibling) for j, chip in enumerate(chips)]
        for j, chip in enumerate(chips):
            copy(1 + j, (*chip, mc), me).wait_recv()
            passed[j].start()
        copy(0, sibling, me).wait_recv()
        for j, chip in enumerate(chips):
            copy(4 + j, (*chip, 1 - mc), me).wait_recv()
        for cp in first + passed:
            cp.wait_send()
        mine.wait()

    return pl.pallas_call(
        body, name=name,
        out_shape=jax.ShapeDtypeStruct((N_DEV, r, c), x.dtype),
        in_specs=[pl.BlockSpec(memory_space=pl.ANY)],
        out_specs=pl.BlockSpec(memory_space=pl.ANY),
        scratch_shapes=[pltpu.SemaphoreType.DMA((7,)), pltpu.SemaphoreType.DMA((7,)), pltpu.SemaphoreType.DMA],
    )(x)


_ANY = pl.BlockSpec(memory_space=pl.ANY)


def _window(ref, shard_shape, by_cols, dev, half=None):
    rows = shard_shape[1]
    r0, nr = (0, rows) if half is None else (half * (rows // 2), rows // 2)
    if by_cols:
        n = shard_shape[2]
        return ref.at[:, pl.ds(r0, nr), pl.ds(pl.multiple_of(dev * n, n), n)]
    return ref.at[:, pl.ds(pl.multiple_of(dev * rows + r0, 2 * SUBLANES), nr), :]


_TO_SIBLING, _TO_X, _TO_Y, _X_TO_SIBLING, _Y_TO_SIBLING, _DIAG0_TO_SIBLING, _DIAG1_TO_SIBLING, _X_HALF_TO_Y, _Y_HALF_TO_X = range(9)


def _gather_weights(shards, by_cols):
    na = len(shards)
    fulls = [jax.ShapeDtypeStruct(
        (s.shape[0], s.shape[1], s.shape[2] * N_DEV) if c else (s.shape[0], s.shape[1] * N_DEV, s.shape[2]), s.dtype)
        for s, c in zip(shards, by_cols)]

    def body(*refs):
        x_refs, out_refs = refs[:na], refs[na:2 * na]
        send_sems, recv_sems, local_sems = refs[2 * na:]
        mx, my, mc = _coords()
        me, sibling = (mx, my, mc), (mx, my, 1 - mc)
        xn, yn, diag = (1 - mx, my, mc), (mx, 1 - my, mc), (1 - mx, 1 - my, mc)

        def blk(a, dev, half=None):
            px, py, pc = dev
            return _window(out_refs[a], x_refs[a].shape, by_cols[a], 4 * px + 2 * py + pc, half)

        def copy(a, k, block, to, half=None, src=None):
            return pltpu.make_async_remote_copy(
                src_ref=blk(a, block, half) if src is None else src, dst_ref=blk(a, block, half),
                send_sem=send_sems.at[a, k], recv_sem=recv_sems.at[a, k], device_id=to, device_id_type=MESH)

        def other_core(dev):
            return dev[0], dev[1], 1 - dev[2]

        sends = []
        for a in range(na):
            mine = pltpu.make_async_copy(x_refs[a], blk(a, me), local_sems.at[a])
            mine.start()
            first = [copy(a, _TO_SIBLING, me, sibling, src=x_refs[a]), copy(a, _TO_X, me, xn, src=x_refs[a]),
                     copy(a, _TO_Y, me, yn, src=x_refs[a])]
            for cp in first:
                cp.start()
            sends += first
        for a in range(na):
            copy(a, _TO_X, xn, me).wait_recv()
            copy(a, _TO_Y, yn, me).wait_recv()
            passed = [copy(a, _X_TO_SIBLING, xn, sibling), copy(a, _X_HALF_TO_Y, xn, yn, half=0),
                      copy(a, _Y_TO_SIBLING, yn, sibling), copy(a, _Y_HALF_TO_X, yn, xn, half=1)]
            for cp in passed:
                cp.start()
            sends += passed
        for a in range(na):
            copy(a, _X_HALF_TO_Y, diag, me, half=0).wait_recv()
            copy(a, _Y_HALF_TO_X, diag, me, half=1).wait_recv()
            passed = [copy(a, _DIAG0_TO_SIBLING, diag, sibling, half=0), copy(a, _DIAG1_TO_SIBLING, diag, sibling, half=1)]
            for cp in passed:
                cp.start()
            sends += passed
        for a in range(na):
            copy(a, _TO_SIBLING, sibling, me).wait_recv()
            copy(a, _X_TO_SIBLING, other_core(xn), me).wait_recv()
            copy(a, _Y_TO_SIBLING, other_core(yn), me).wait_recv()
            copy(a, _DIAG0_TO_SIBLING, other_core(diag), me, half=0).wait_recv()
            copy(a, _DIAG1_TO_SIBLING, other_core(diag), me, half=1).wait_recv()
        for cp in sends:
            cp.wait_send()
        for a in range(na):
            pltpu.make_async_copy(x_refs[a], blk(a, me), local_sems.at[a]).wait()

    nsem = _Y_HALF_TO_X + 1
    return pl.pallas_call(
        body, name="gather_weights", out_shape=fulls, in_specs=[_ANY] * na, out_specs=[_ANY] * na,
        scratch_shapes=[pltpu.SemaphoreType.DMA((na, nsem)), pltpu.SemaphoreType.DMA((na, nsem)), pltpu.SemaphoreType.DMA((na,))],
    )(*shards)


def _rs_sibling(grads, shard_shapes, by_cols):
    na = len(grads)
    slabs = [jax.ShapeDtypeStruct((N_CHIP,) + tuple(s), F32) for s in shard_shapes]

    def body(*refs):
        g_refs, got_refs = refs[:na], refs[na:2 * na]
        send_sems, recv_sems = refs[2 * na:]
        mx, my, mc = _coords()
        copies = []
        for a in range(na):
            for q in range(N_CHIP):
                dev = 2 * q + 1 - mc
                theirs = g_refs[a].at[dev] if by_cols[a] else _window(g_refs[a], shard_shapes[a], False, dev)
                copies.append(pltpu.make_async_remote_copy(
                    src_ref=theirs, dst_ref=got_refs[a].at[q], send_sem=send_sems.at[a, q], recv_sem=recv_sems.at[a, q],
                    device_id=(mx, my, 1 - mc), device_id_type=MESH))
        for cp in copies:
            cp.start()
        for cp in copies:
            cp.wait()

    sems = pltpu.SemaphoreType.DMA((na, N_CHIP))
    return pl.pallas_call(
        body, name="rs_sibling", out_shape=slabs, in_specs=[_ANY] * na, out_specs=[_ANY] * na,
        scratch_shapes=[sems, sems],
    )(*grads)


def _rs_add(g, got, core, by_cols, cols, name):
    nq, nl, r, c = got.shape
    tr = _tile(r, ELEMENTWISE_ROW_TILE, 2 * SUBLANES)

    def body(core_ref, a_ref, b_ref, o_ref):
        o_ref[...] = (a_ref[...] + b_ref[...])[:, :cols].astype(BF16)

    if by_cols:
        mine = pl.BlockSpec((None, None, tr, c), lambda q, l, i, core_ref: (2 * q + core_ref[0], l, i, 0))
    else:
        mine = pl.BlockSpec((None, tr, c), lambda q, l, i, core_ref: (l, (2 * q + core_ref[0]) * (r // tr) + i, 0))
    spec = pl.BlockSpec((None, None, tr, c), lambda q, l, i, core_ref: (q, l, i, 0))
    return pl.pallas_call(
        body, name=name, out_shape=jax.ShapeDtypeStruct((nq, nl, r, cols), BF16),
        grid_spec=pltpu.PrefetchScalarGridSpec(
            num_scalar_prefetch=1, grid=(nq, nl, r // tr), in_specs=[mine, spec],
            out_specs=pl.BlockSpec((None, None, tr, cols), lambda q, l, i, core_ref: (q, l, i, 0))),
        compiler_params=pltpu.CompilerParams(dimension_semantics=("parallel",) * 3),
    )(core, g, got)


def _rs_chips(parts):
    na = len(parts)

    def body(*refs):
        p_refs, out_refs = refs[:na], refs[na:2 * na]
        send_sems, recv_sems, local_sems = refs[2 * na:]
        mx, my, mc = _coords()
        here = 2 * mx + my
        chips = [(1 - mx, my), (mx, 1 - my), (1 - mx, 1 - my)]
        copies = []
        for a in range(na):
            copies.append(pltpu.make_async_copy(p_refs[a].at[here], out_refs[a].at[here], local_sems.at[a]))
            for j, (cx, cy) in enumerate(chips):
                copies.append(pltpu.make_async_remote_copy(
                    src_ref=p_refs[a].at[2 * cx + cy], dst_ref=out_refs[a].at[here],
                    send_sem=send_sems.at[a, j], recv_sem=recv_sems.at[a, j], device_id=(cx, cy, mc), device_id_type=MESH))
        for cp in copies:
            cp.start()
        for cp in copies:
            cp.wait()

    return pl.pallas_call(
        body, name="rs_chips", out_shape=[jax.ShapeDtypeStruct(p.shape, p.dtype) for p in parts],
        in_specs=[_ANY] * na, out_specs=[_ANY] * na,
        scratch_shapes=[pltpu.SemaphoreType.DMA((na, 3)), pltpu.SemaphoreType.DMA((na, 3)), pltpu.SemaphoreType.DMA((na,))],
    )(*parts)


def _adamw(parts, w, m, v, name):
    n, r, c = parts.shape
    tr = _tile(r, 320, 2 * SUBLANES)
    bc1 = 1.0 - ADAM_B1 ** ADAM_STEP
    bc2 = 1.0 - ADAM_B2 ** ADAM_STEP

    def body(p_ref, w_ref, m_ref, v_ref, g_ref, d_ref, mo_ref, vo_ref):
        g = p_ref[0].astype(F32)
        for j in range(1, n):
            g = g + p_ref[j].astype(F32)
        mn = ADAM_B1 * m_ref[...] + (1.0 - ADAM_B1) * g
        vn = ADAM_B2 * v_ref[...] + (1.0 - ADAM_B2) * (g * g)
        g_ref[...] = g
        mo_ref[...] = mn
        vo_ref[...] = vn
        d_ref[...] = -ADAM_LR * ((mn / bc1) / (jnp.sqrt(vn / bc2) + ADAM_EPS) + ADAM_WD * w_ref[...])

    flat = pl.BlockSpec((tr, c), lambda i: (i, 0))
    return pl.pallas_call(
        body, name=name, grid=(r // tr,),
        in_specs=[pl.BlockSpec((n, tr, c), lambda i: (0, i, 0)), flat, flat, flat],
        out_specs=[flat] * 4, out_shape=[jax.ShapeDtypeStruct((r, c), F32)] * 4,
        compiler_params=pltpu.CompilerParams(dimension_semantics=("parallel",)),
    )(parts, w, m, v)


def _adam_math(g, w, m, v):
    bc1 = 1.0 - ADAM_B1 ** ADAM_STEP
    bc2 = 1.0 - ADAM_B2 ** ADAM_STEP
    mn = ADAM_B1 * m + (1.0 - ADAM_B1) * g
    vn = ADAM_B2 * v + (1.0 - ADAM_B2) * (g * g)
    return -ADAM_LR * ((mn / bc1) / (jnp.sqrt(vn / bc2) + ADAM_EPS) + ADAM_WD * w), mn, vn


def _adamw_shard(parts, w, m, v, name):
    nl, r, c = w.shape
    tr = _tile(r, ELEMENTWISE_ROW_TILE, 2 * SUBLANES)

    def body(p_ref, w_ref, m_ref, v_ref, g_ref, d_ref, mo_ref, vo_ref):
        g = p_ref[0].astype(F32)
        for q in range(1, N_CHIP):
            g = g + p_ref[q].astype(F32)
        g = g[:, :c]
        g_ref[...] = g
        d_ref[...], mo_ref[...], vo_ref[...] = _adam_math(g, w_ref[...], m_ref[...], v_ref[...])

    native = pl.BlockSpec((None, tr, c), lambda l, i: (l, i, 0))
    return pl.pallas_call(
        body, name=name, grid=(nl, r // tr),
        in_specs=[pl.BlockSpec((N_CHIP, None, tr, parts.shape[3]), lambda l, i: (0, l, i, 0)), native, native, native],
        out_specs=[native] * 4, out_shape=[jax.ShapeDtypeStruct(w.shape, F32)] * 4,
        compiler_params=pltpu.CompilerParams(dimension_semantics=("parallel", "parallel"), vmem_limit_bytes=V7X_VMEM_LIMIT),
    )(parts, w, m, v)


def _pack(arrs, dtype=F32):
    cols = []
    for a in arrs:
        f = a.reshape(-1).astype(dtype)
        cols.append(jnp.pad(f, (0, -f.shape[0] % FLAT_W)))
    flat = jnp.concatenate(cols)
    flat = jnp.pad(flat, (0, -flat.shape[0] % (FLAT_W * SUBLANES)))
    return flat.reshape(-1, FLAT_W)


def _unpack(flat, shapes, lead=()):
    flat = flat.reshape(lead + (-1,))
    out, off = [], 0
    for s in shapes:
        n = math.prod(s)
        out.append(flat[..., off:off + n].reshape(lead + tuple(s)))
        off += n + (-n % FLAT_W)
    return out


def kernel(x, p, ffn1_norm, ffn1_w1, ffn1_w3, ffn1_w2, mix_norm, ffn2_norm, ffn2_w1, ffn2_w3, ffn2_w2, ple_norm, ple_proj, ple_gate, s5_w_in, s5_a_re, s5_a_im, s5_log_dt, s5_b_re, s5_b_im, s5_c_re, s5_c_im, s5_d, s5_w_glu, sb_w_qkv, sb_w_o, final_norm, loss_target, m_ffn1_norm, m_ffn1_w1, m_ffn1_w3, m_ffn1_w2, m_mix_norm, m_ffn2_norm, m_ffn2_w1, m_ffn2_w3, m_ffn2_w2, m_ple_norm, m_ple_proj, m_ple_gate, m_s5_w_in, m_s5_a_re, m_s5_a_im, m_s5_log_dt, m_s5_b_re, m_s5_b_im, m_s5_c_re, m_s5_c_im, m_s5_d, m_s5_w_glu, m_sb_w_qkv, m_sb_w_o, m_final_norm, v_ffn1_norm, v_ffn1_w1, v_ffn1_w3, v_ffn1_w2, v_mix_norm, v_ffn2_norm, v_ffn2_w1, v_ffn2_w3, v_ffn2_w2, v_ple_norm, v_ple_proj, v_ple_gate, v_s5_w_in, v_s5_a_re, v_s5_a_im, v_s5_log_dt, v_s5_b_re, v_s5_b_im, v_s5_c_re, v_s5_c_im, v_s5_d, v_s5_w_glu, v_sb_w_qkv, v_sb_w_o, v_final_norm):
    given = dict(locals())
    wts = {n: given[n] for n in WEIGHTS}
    mom = {n: given["m_" + n] for n in WEIGHTS}
    var = {n: given["v_" + n] for n in WEIGHTS}
    bl, seq, d = x.shape
    t = bl * seq
    depth = p.shape[0]

    hid_pad = -wts["ffn1_w1"].shape[-1] % LANES

    def padded(n):
        a = wts[n].astype(BF16)
        if n in FFN_COL:
            return jnp.pad(a, ((0, 0), (0, 0), (0, hid_pad)))
        if n in FFN_ROW:
            return jnp.pad(a, ((0, 0), (0, hid_pad), (0, 0)))
        return a

    sent = [padded(n) for n in SHARDED]
    shard_shape = {n: a.shape for n, a in zip(SHARDED, sent)}
    by_cols = [n in COL_SHARDED for n in SHARDED]
    full = dict(zip(SHARDED, _gather_weights(sent, by_cols)))

    def row(a):
        return a.reshape(1, -1)

    def wt(n, layer=0, col=0, width=None):
        return _W(full[n], layer, col, width)

    gbuf = {}

    def wgrad(n, x_act, dy, layer=0, scale=1.0):
        like = (N_DEV,) + shard_shape[n] if n in COL_SHARDED else full[n].shape
        gbuf[n] = _wgrad(x_act, dy, f"{n}_grad", like, gbuf.get(n), layer, scale)

    n_groups = d // S5_GROUP
    a_re, a_im = s5_a_re[0], s5_a_im[0]
    log_dt = s5_log_dt[0].reshape(n_groups, 1)
    disc = _s5_disc(a_re, a_im, log_dt)
    nb = n_groups // S5_BLOCK_GROUPS
    ns = S5_BLOCK_GROUPS * S5_STATE
    par = jnp.concatenate([jnp.stack([q.reshape(nb, ns) for q in disc], axis=1), jnp.zeros((nb, 4, ns), F32)], axis=1)
    bre = _blockdiag(s5_b_re[0].transpose(0, 2, 1), S5_BLOCK_GROUPS).astype(BF16)
    bim = _blockdiag(s5_b_im[0].transpose(0, 2, 1), S5_BLOCK_GROUPS).astype(BF16)
    cre = _blockdiag(s5_c_re[0].transpose(0, 2, 1), S5_BLOCK_GROUPS).astype(BF16)
    cim = _blockdiag(s5_c_im[0].transpose(0, 2, 1), S5_BLOCK_GROUPS).astype(BF16)
    lc = min(SCAN_CHUNK, seq)

    h = x.reshape(t, d)
    saved = []
    for i in range(depth):
        s = {"h0": h}
        h, s["n1"], s["a1"], s["b1"] = _ffn_fwd(h, row(ffn1_norm[i]), wt("ffn1_w1", i), wt("ffn1_w3", i), wt("ffn1_w2", i))
        s["h1"] = h
        j = i // 2
        if i % 2 == 0:
            s["hn"], u = _norm_lin(h, row(mix_norm[i]), [wt("s5_w_in", j)], [F32], "s5_in")
            s["u"] = _interleave(u, lc)
            y, z, s["st"] = _s5_scan_fwd(s["u"], bre, bim, cre, cim, par, row(s5_d[j]), bl, seq, lc)
            s["y"], s["z"] = _deinterleave(y, lc), _deinterleave(z, lc)
            h = _s5_out(h, s["z"], wt("s5_w_glu", j))
        else:
            wqkv = [wt("sb_w_qkv", j, col, d) for col in range(3)]
            s["hn"], s["q"], s["k"], s["v"] = _norm_lin(h, row(mix_norm[i]), wqkv, [BF16] * 3, "sb_in")
            s["o"], s["tot"] = _sb_fwd(s["q"], s["k"], s["v"], bl, seq)
            h = _lin_res(h, s["o"], wt("sb_w_o", j), "sb_out")
        s["h2"] = h
        h, s["n2"], s["a2"], s["b2"] = _ffn_fwd(h, row(ffn2_norm[i]), wt("ffn2_w1", i), wt("ffn2_w3", i), wt("ffn2_w2", i))
        s["h3"] = h
        s["p"] = p[i].reshape(t, -1)
        h, s["npl"] = _ple_fwd(h, row(ple_norm[i]), wt("ple_gate", i), s["p"], wt("ple_proj", i))
        saved.append(s)

    loss_part, dh, g_final = _loss_head(h, row(final_norm), loss_target.reshape(t, d))
    loss = lax.psum(loss_part[0, 0], ("x", "y", "c"))

    grads = {n: [None] * wts[n].shape[0] for n in REPLICATED if n != "final_norm"}
    grads["final_norm"] = g_final.reshape(-1)

    def ffn_bwd(dh, h_in, n, a, b, which, i):
        da, db, sact = _ffn_bwd_down(dh, a, b, wt(f"{which}_w2", i))
        wgrad(f"{which}_w2", sact, dh, i, scale=0.5)
        dh_in, dg = _lin_bwd_norm([da, db], [wt(f"{which}_w1", i), wt(f"{which}_w3", i)], h_in,
                                  row(wts[f"{which}_norm"][i]), dh, f"{which}_bwd_up")
        wgrad(f"{which}_w1", n, da, i)
        wgrad(f"{which}_w3", n, db, i)
        grads[f"{which}_norm"][i] = dg.reshape(-1)
        return dh_in

    for i in reversed(range(depth)):
        s = saved[i]
        j = i // 2
        dh, dgl, dpp, dg = _ple_bwd(dh, s["h3"], row(ple_norm[i]), s["npl"], s["p"], wt("ple_gate", i), wt("ple_proj", i))
        grads["ple_norm"][i] = dg.reshape(-1)
        wgrad("ple_gate", s["npl"], dgl, i)
        wgrad("ple_proj", s["p"], dpp, i)
        dh = ffn_bwd(dh, s["h2"], s["n2"], s["a2"], s["b2"], "ffn2", i)
        if i % 2 == 0:
            dy, dzz = _s5_out_bwd(dh, s["z"], s["y"], wt("s5_w_glu", j))
            wgrad("s5_w_glu", s["z"], dzz, j)
            du, dbre, dbim, dcre, dcim, dpar, dd = _s5_scan_bwd(
                s["u"], _interleave(dy, lc), s["st"], bre, bim, cre, cim, par, row(s5_d[j]), bl, seq, lc)
            du = _deinterleave(du, lc)
            cts = [dpar[:, r, :].reshape(n_groups, S5_STATE) for r in range(4)]
            g_are, g_aim, g_ldt = _s5_disc_bwd(a_re, a_im, log_dt, cts)
            grads["s5_a_re"][j], grads["s5_a_im"][j], grads["s5_log_dt"][j] = g_are, g_aim, g_ldt.reshape(-1)
            grads["s5_b_re"][j] = _blockdiag_take(dbre, S5_BLOCK_GROUPS, S5_GROUP, S5_STATE).transpose(0, 2, 1)
            grads["s5_b_im"][j] = _blockdiag_take(dbim, S5_BLOCK_GROUPS, S5_GROUP, S5_STATE).transpose(0, 2, 1)
            grads["s5_c_re"][j] = _blockdiag_take(dcre, S5_BLOCK_GROUPS, S5_STATE, S5_GROUP).transpose(0, 2, 1)
            grads["s5_c_im"][j] = _blockdiag_take(dcim, S5_BLOCK_GROUPS, S5_STATE, S5_GROUP).transpose(0, 2, 1)
            grads["s5_d"][j] = dd.reshape(-1)
            dh, dg = _lin_bwd_norm([du], [wt("s5_w_in", j)], s["h1"], row(mix_norm[i]), dh, "s5_in_bwd")
            wgrad("s5_w_in", s["hn"], du, j)
        else:
            do = _lin_nt(dh, wt("sb_w_o", j), "sb_out_bwd")
            wgrad("sb_w_o", s["o"], dh, j)
            dqkv = _sb_bwd(s["q"], s["k"], s["v"], do, s["tot"], bl, seq)
            dh, dg = _lin_bwd_norm(dqkv, [wt("sb_w_qkv", j, col, d) for col in range(3)], s["h1"], row(mix_norm[i]), dh, "sb_in_bwd")
            wgrad("sb_w_qkv", s["hn"], jnp.concatenate(dqkv, axis=1), j)
        grads["mix_norm"][i] = dg.reshape(-1)
        dh = ffn_bwd(dh, s["h0"], s["n1"], s["a1"], s["b1"], "ffn1", i)
    grad_x = dh.reshape(x.shape)
    for n in REPLICATED:
        if n != "final_norm":
            grads[n] = jnp.stack(grads[n])

    got = _rs_sibling([gbuf[n] for n in SHARDED], [shard_shape[n] for n in SHARDED], by_cols)
    core = lax.axis_index("c").astype(jnp.int32).reshape(1)
    parts = _rs_chips([_rs_add(gbuf[n], g, core, n in COL_SHARDED, wts[n].shape[2], f"rs_add_{n}")
                       for n, g in zip(SHARDED, got)])
    res = {n: _adamw_shard(part, wts[n], mom[n], var[n], f"adamw_{n}") for n, part in zip(SHARDED, parts)}

    rep_shapes = [wts[n].shape for n in REPLICATED]
    rep_parts = _all_gather(_pack([grads[n].reshape(wts[n].shape) for n in REPLICATED]), "gather_small_grads")
    outs = _adamw(rep_parts, _pack([wts[n] for n in REPLICATED]), _pack([mom[n] for n in REPLICATED]),
                  _pack([var[n] for n in REPLICATED]), "adamw_replicated")
    res.update({n: vals for n, vals in zip(REPLICATED, zip(*[_unpack(o, rep_shapes) for o in outs]))})

    return (loss, grad_x, *[res[n][0] for n in WEIGHTS], *[res[n][1] for n in WEIGHTS],
            *[res[n][2] for n in WEIGHTS], *[res[n][3] for n in WEIGHTS])
```
